```python
import jax, jax.numpy as jnp
from jax import lax
import numpy as np

D_MODEL = 1024
BATCH = 8
SEQ = 4096
DEPTH = 2

ROPE_THETA = 10000.0
LN_EPS = 1e-5
RMS_EPS = 1e-6
DEEPNORM_ALPHA = (2 * DEPTH) ** 0.25
DEEPNORM_BETA = (8 * DEPTH) ** -0.25
NEG_BIG = -1e30
FORCE_SCORE = 1e4

HEAD_DIM = 64
SWA_Q_HEADS = 8
SWA_KV_HEADS = 2
SWA_WINDOW = 128
NSA_Q_HEADS = 8
NSA_KV_HEADS = 2
NSA_CMP_LEN = 32
NSA_CMP_STRIDE = 16
NSA_CMP_HIDDEN = 128
NSA_SEL_LEN = 64
NSA_N_SEL = 16
NSA_WINDOW = 512
NSA_QUERY_BLOCK = 64
BAND_QUERY_BLOCK = 128
MIX0_WIDTH = (SWA_Q_HEADS + NSA_Q_HEADS) * HEAD_DIM
MIX0_SPLITS = (
    SWA_Q_HEADS * HEAD_DIM,
    SWA_KV_HEADS * HEAD_DIM,
    SWA_KV_HEADS * HEAD_DIM,
    NSA_Q_HEADS * HEAD_DIM,
    NSA_KV_HEADS * HEAD_DIM,
    NSA_KV_HEADS * HEAD_DIM,
    NSA_KV_HEADS * HEAD_DIM,
    NSA_KV_HEADS * HEAD_DIM,
    NSA_KV_HEADS * HEAD_DIM,
    NSA_KV_HEADS * HEAD_DIM,
    3 * NSA_Q_HEADS,
)
MIX0_IN = sum(MIX0_SPLITS)

MLA_HEADS = 16
MLA_NOPE = 64
MLA_ROPE = 32
MLA_V = 64
MLA_Q_LORA = 384
MLA_KV_LORA = 256
MLA_QUERY_BLOCK = 128
MLA_IN = MLA_Q_LORA + MLA_KV_LORA + MLA_ROPE

N_EXPERTS = 64
MOE_GROUPS = 8
MOE_TOPK_GROUPS = 4
MOE_TOP_K = 6
D_EXPERT = 256
D_SHARED = 256
MOE_ROUTED_SCALE = 2.5
MOE_ROW_BLOCK = 256

N_EVEN = (DEPTH + 1) // 2
N_ODD = DEPTH // 2

kernel_name = "hybrid_swa_nsa_mla_moe_deepnorm"


def _rope(x, pos):
    d = x.shape[-1]
    half = d // 2
    inv_freq = ROPE_THETA ** (-2.0 * jnp.arange(half, dtype=jnp.float32) / d)
    ang = pos.astype(jnp.float32)[:, None] * inv_freq[None, :]
    cos = jnp.cos(ang)[:, None, :].astype(x.dtype)
    sin = jnp.sin(ang)[:, None, :].astype(x.dtype)
    x1, x2 = x[..., :half], x[..., half:]
    return jnp.concatenate([x1 * cos - x2 * sin, x1 * sin + x2 * cos], axis=-1)


def _layer_norm(x, g, b):
    xf = x.astype(jnp.float32)
    mu = xf.mean(-1, keepdims=True)
    xc = xf - mu
    var = (xc * xc).mean(-1, keepdims=True)
    return (xc * lax.rsqrt(var + LN_EPS) * g.astype(jnp.float32) + b.astype(jnp.float32)).astype(x.dtype)


def _rms_norm(x, g):
    xf = x.astype(jnp.float32)
    y = xf * lax.rsqrt((xf * xf).mean(-1, keepdims=True) + RMS_EPS)
    return (y * g.astype(jnp.float32)).astype(x.dtype)


def _banded_attention(q, k, v, window, sinks):
    B, S, G, R, d = q.shape
    C = BAND_QUERY_BLOCK
    n_blk = S // C
    L = window + C
    scale = d ** -0.5
    kp = jnp.pad(k, ((0, 0), (window, 0), (0, 0), (0, 0)))
    vp = jnp.pad(v, ((0, 0), (window, 0), (0, 0), (0, 0)))
    q_blocks = q.reshape(B, n_blk, C, G, R, d).swapaxes(0, 1)

    def step(args):
        i, qb = args
        start = i * C
        kb = lax.dynamic_slice_in_dim(kp, start, L, axis=1)
        vb = lax.dynamic_slice_in_dim(vp, start, L, axis=1)
        s = jnp.einsum('bqgrd,bkgd->bgrqk', qb, kb).astype(jnp.float32) * scale
        tq = start + jnp.arange(C)
        tk = start - window + jnp.arange(L)
        diff = tq[:, None] - tk[None, :]
        mask = (diff >= 0) & (diff < window) & (tk[None, :] >= 0)
        s = jnp.where(mask, s, -jnp.inf)
        if sinks is not None:
            sink = jnp.broadcast_to(sinks.astype(jnp.float32).reshape(1, G, R, 1, 1), s.shape[:-1] + (1,))
            p = jax.nn.softmax(jnp.concatenate([sink, s], axis=-1), axis=-1)[..., 1:]
        else:
            p = jax.nn.softmax(s, axis=-1)
        return jnp.einsum('bgrqk,bkgd->bqgrd', p.astype(vb.dtype), vb)

    o = lax.map(step, (jnp.arange(n_blk), q_blocks))
    return o.swapaxes(0, 1).reshape(B, S, G, R, d)


def _causal_attention(q, k, v):
    B, S, H, dq = q.shape
    dv = v.shape[-1]
    C = MLA_QUERY_BLOCK
    n_blk = S // C
    scale = dq ** -0.5
    q_blocks = q.reshape(B, n_blk, C, H, dq).swapaxes(0, 1)
    tk = jnp.arange(S)

    def step(args):
        i, qb = args
        tq = i * C + jnp.arange(C)
        s = jnp.einsum('bqhd,bkhd->bhqk', qb, k).astype(jnp.float32) * scale
        s = jnp.where(tk[None, :] <= tq[:, None], s, -jnp.inf)
        p = jax.nn.softmax(s, axis=-1)
        return jnp.einsum('bhqk,bkhd->bqhd', p.astype(v.dtype), v)

    o = lax.map(step, (jnp.arange(n_blk), q_blocks))
    return o.swapaxes(0, 1).reshape(B, S, H, dv)


def _compress(t, pe, w1, w2):
    B, S, G, d = t.shape
    n_cmp = (S - NSA_CMP_LEN) // NSA_CMP_STRIDE + 1
    idx = np.arange(n_cmp)[:, None] * NSA_CMP_STRIDE + np.arange(NSA_CMP_LEN)[None, :]
    blk = t[:, idx] + pe[:, None, :]
    blk = blk.transpose(0, 1, 3, 2, 4).reshape(B, n_cmp, G, NSA_CMP_LEN * d)
    return jax.nn.gelu(blk @ w1) @ w2


def _nsa_cmp_sel(q, kc, vc, ks, vs):
    B, S, G, R, d = q.shape
    n_cmp = kc.shape[1]
    n_sb = S // NSA_SEL_LEN
    n_sel = min(NSA_N_SEL, n_sb)
    C = NSA_QUERY_BLOCK
    n_qb = S // C
    scale = d ** -0.5
    cs = np.arange(n_cmp) * NSA_CMP_STRIDE
    bs = np.arange(n_sb) * NSA_SEL_LEN
    ov = np.clip(np.minimum(cs[:, None] + NSA_CMP_LEN, bs[None, :] + NSA_SEL_LEN)
                 - np.maximum(cs[:, None], bs[None, :]), 0, None) / NSA_CMP_LEN
    overlap = jnp.asarray(ov, dtype=jnp.float32)
    cmp_end = jnp.asarray(cs + NSA_CMP_LEN - 1)
    ks_blk = ks.reshape(B, n_sb, NSA_SEL_LEN, G, d).transpose(0, 3, 1, 2, 4)
    vs_blk = vs.reshape(B, n_sb, NSA_SEL_LEN, G, d).transpose(0, 3, 1, 2, 4)
    bi = jnp.arange(B)[:, None, None, None]
    gi = jnp.arange(G)[None, :, None, None]
    blk_ids = jnp.arange(n_sb)
    q_blocks = q.reshape(B, n_qb, C, G, R, d).swapaxes(0, 1)

    def step(args):
        i, qb = args
        tq = i * C + jnp.arange(C)
        s_c = jnp.einsum('bqgrd,bngd->bgrqn', qb, kc).astype(jnp.float32) * scale
        mask_c = cmp_end[None, :] <= tq[:, None]
        s_c = jnp.where(mask_c, s_c, NEG_BIG)
        p_c = jnp.exp(s_c - s_c.max(-1, keepdims=True)) * mask_c
        p_c = p_c / jnp.maximum(p_c.sum(-1, keepdims=True), 1e-30)
        o_c = jnp.einsum('bgrqn,bngd->bqgrd', p_c.astype(vc.dtype), vc)
        imp = jnp.einsum('bgrqn,nm->bgqm', p_c, overlap)
        cur = tq // NSA_SEL_LEN
        forced = ((blk_ids[None, :] == 0) | (blk_ids[None, :] == cur[:, None])
                  | (blk_ids[None, :] == cur[:, None] - 1))
        future = blk_ids[None, :] * NSA_SEL_LEN > tq[:, None]
        score = jnp.where(forced, FORCE_SCORE, jnp.where(future, -1.0, imp))
        _, idx = lax.top_k(score, n_sel)
        kg = ks_blk[bi, gi, idx]
        vg = vs_blk[bi, gi, idx]
        tk = idx[..., None] * NSA_SEL_LEN + jnp.arange(NSA_SEL_LEN)
        mask_s = (tk <= tq[None, None, :, None, None])[:, :, None]
        s_s = jnp.einsum('bqgrd,bgqnld->bgrqnl', qb, kg).astype(jnp.float32) * scale
        s_s = jnp.where(mask_s, s_s, -jnp.inf)
        p_s = jax.nn.softmax(s_s.reshape(B, G, R, C, n_sel * NSA_SEL_LEN), axis=-1).reshape(s_s.shape)
        o_s = jnp.einsum('bgrqnl,bgqnld->bqgrd', p_s.astype(vg.dtype), vg)
        return o_c, o_s

    o_c, o_s = lax.map(step, (jnp.arange(n_qb), q_blocks))
    o_c = o_c.swapaxes(0, 1).reshape(B, S, G, R, d)
    o_s = o_s.swapaxes(0, 1).reshape(B, S, G, R, d)
    return o_c, o_s


def _swa_nsa_mixer(x, w_in, sinks, pe_k, w_ck1, w_ck2, pe_v, w_cv1, w_cv2, w_out):
    B, S, _ = x.shape
    pos = jnp.arange(S)
    parts = jnp.split(x @ w_in, [int(c) for c in np.cumsum(MIX0_SPLITS)[:-1]], axis=-1)
    qa, ka, va, qb, kc, vc, ksl, vsl, kw, vw, gates = parts
    heads = lambda t, h: t.reshape(B, S, h, HEAD_DIM)
    ra = SWA_Q_HEADS // SWA_KV_HEADS
    rb = NSA_Q_HEADS // NSA_KV_HEADS
    qa = _rope(heads(qa, SWA_Q_HEADS), pos).reshape(B, S, SWA_KV_HEADS, ra, HEAD_DIM)
    ka = _rope(heads(ka, SWA_KV_HEADS), pos)
    o_swa = _banded_attention(qa, ka, heads(va, SWA_KV_HEADS), SWA_WINDOW, sinks)
    qb = _rope(heads(qb, NSA_Q_HEADS), pos).reshape(B, S, NSA_KV_HEADS, rb, HEAD_DIM)
    k_cmp = _compress(heads(kc, NSA_KV_HEADS), pe_k, w_ck1, w_ck2)
    n_cmp = k_cmp.shape[1]
    k_cmp = _rope(k_cmp, jnp.arange(n_cmp) * NSA_CMP_STRIDE + NSA_CMP_LEN - 1)
    v_cmp = _compress(heads(vc, NSA_KV_HEADS), pe_v, w_cv1, w_cv2)
    k_sel = _rope(heads(ksl, NSA_KV_HEADS), pos)
    o_cmp, o_sel = _nsa_cmp_sel(qb, k_cmp, v_cmp, k_sel, heads(vsl, NSA_KV_HEADS))
    o_win = _banded_attention(qb, _rope(heads(kw, NSA_KV_HEADS), pos), heads(vw, NSA_KV_HEADS),
                              NSA_WINDOW, None)
    g = jax.nn.sigmoid(gates.astype(jnp.float32)).astype(x.dtype).reshape(B, S, 3, NSA_KV_HEADS, rb, 1)
    o_nsa = g[:, :, 0] * o_cmp + g[:, :, 1] * o_sel + g[:, :, 2] * o_win
    o = jnp.concatenate([o_swa.reshape(B, S, -1), o_nsa.reshape(B, S, -1)], axis=-1)
    return o @ w_out


def _mla_mixer(x, w_in, q_norm, kv_norm, w_uq, w_ukv, w_out):
    B, S, _ = x.shape
    pos = jnp.arange(S)
    c_q, c_kv, k_r = jnp.split(x @ w_in, [MLA_Q_LORA, MLA_Q_LORA + MLA_KV_LORA], axis=-1)
    q = (_rms_norm(c_q, q_norm) @ w_uq).reshape(B, S, MLA_HEADS, MLA_NOPE + MLA_ROPE)
    kv = (_rms_norm(c_kv, kv_norm) @ w_ukv).reshape(B, S, MLA_HEADS, MLA_NOPE + MLA_V)
    q = jnp.concatenate([q[..., :MLA_NOPE], _rope(q[..., MLA_NOPE:], pos)], axis=-1)
    k_rope = _rope(k_r.reshape(B, S, 1, MLA_ROPE), pos)
    k = jnp.concatenate([kv[..., :MLA_NOPE], jnp.broadcast_to(k_rope, (B, S, MLA_HEADS, MLA_ROPE))], axis=-1)
    o = _causal_attention(q, k, kv[..., MLA_NOPE:])
    return o.reshape(B, S, MLA_HEADS * MLA_V) @ w_out


def _moe(x, router_w, router_bias, w_gate, w_up, w_down, sh_gate, sh_up, sh_down):
    B, S, D = x.shape
    n_tok = B * S
    xf = x.reshape(n_tok, D)
    s = jax.nn.sigmoid((xf @ router_w).astype(jnp.float32))
    sb = s + router_bias.astype(jnp.float32)
    grp = sb.reshape(n_tok, MOE_GROUPS, N_EXPERTS // MOE_GROUPS)
    grp_score = lax.top_k(grp, 2)[0].sum(-1)
    _, gidx = lax.top_k(grp_score, MOE_TOPK_GROUPS)
    gmask = jnp.any(gidx[:, :, None] == jnp.arange(MOE_GROUPS)[None, None, :], axis=1)
    emask = jnp.repeat(gmask, N_EXPERTS // MOE_GROUPS, axis=1)
    _, eidx = lax.top_k(jnp.where(emask, sb, -jnp.inf), MOE_TOP_K)
    gate = jnp.take_along_axis(s, eidx, axis=1)
    gate = gate / gate.sum(-1, keepdims=True) * MOE_ROUTED_SCALE
    n_rows = n_tok * MOE_TOP_K
    flat_e = eidx.reshape(-1)
    flat_t = jnp.repeat(jnp.arange(n_tok), MOE_TOP_K)
    order = jnp.argsort(flat_e)
    e_s = flat_e[order]
    counts = jnp.bincount(flat_e, length=N_EXPERTS)
    padded = (counts + MOE_ROW_BLOCK - 1) // MOE_ROW_BLOCK * MOE_ROW_BLOCK
    starts = jnp.cumsum(counts) - counts
    pad_ends = jnp.cumsum(padded)
    pad_starts = pad_ends - padded
    dest = pad_starts[e_s] + jnp.arange(n_rows) - starts[e_s]
    n_blk = -(-(n_rows + N_EXPERTS * (MOE_ROW_BLOCK - 1)) // MOE_ROW_BLOCK)
    n_pad_rows = n_blk * MOE_ROW_BLOCK
    row_tok = jnp.zeros((n_pad_rows,), jnp.int32).at[dest].set(flat_t[order])
    row_w = jnp.zeros((n_pad_rows,), x.dtype).at[dest].set(gate.reshape(-1)[order].astype(x.dtype))
    blk_exp = jnp.minimum(jnp.searchsorted(pad_ends, jnp.arange(n_blk) * MOE_ROW_BLOCK, side='right'),
                          N_EXPERTS - 1)

    def body(acc, blk):
        tok, w, e = blk
        xb = xf[tok]
        h = jax.nn.silu(xb @ w_gate[e]) * (xb @ w_up[e])
        return acc.at[tok].add((h @ w_down[e]) * w[:, None]), None

    routed, _ = lax.scan(body, jnp.zeros_like(xf),
                         (row_tok.reshape(n_blk, MOE_ROW_BLOCK), row_w.reshape(n_blk, MOE_ROW_BLOCK), blk_exp))
    shared = (jax.nn.silu(xf @ sh_gate) * (xf @ sh_up)) @ sh_down
    return (routed + shared).reshape(B, S, D)


def setup_inputs(seed: int = 0) -> dict:
    key = jax.random.key(seed)
    keys = iter(jax.random.split(key, 40))

    def nrm(shape, scale):
        return jax.random.normal(next(keys), shape, jnp.float32) * scale

    d = D_MODEL
    return {
        'x': nrm((BATCH, SEQ, d), 1.0),
        'swa_nsa_w_in': nrm((N_EVEN, d, MIX0_IN), d ** -0.5),
        'swa_sinks': nrm((N_EVEN, SWA_Q_HEADS), 1.0),
        'nsa_cmp_pe_k': nrm((N_EVEN, NSA_CMP_LEN, HEAD_DIM), 0.5),
        'nsa_cmp_k_w1': nrm((N_EVEN, NSA_CMP_LEN * HEAD_DIM, NSA_CMP_HIDDEN), (NSA_CMP_LEN * HEAD_DIM) ** -0.5),
        'nsa_cmp_k_w2': nrm((N_EVEN, NSA_CMP_HIDDEN, HEAD_DIM), NSA_CMP_HIDDEN ** -0.5),
        'nsa_cmp_pe_v': nrm((N_EVEN, NSA_CMP_LEN, HEAD_DIM), 0.5),
        'nsa_cmp_v_w1': nrm((N_EVEN, NSA_CMP_LEN * HEAD_DIM, NSA_CMP_HIDDEN), (NSA_CMP_LEN * HEAD_DIM) ** -0.5),
        'nsa_cmp_v_w2': nrm((N_EVEN, NSA_CMP_HIDDEN, HEAD_DIM), NSA_CMP_HIDDEN ** -0.5),
        'swa_nsa_w_out': nrm((N_EVEN, MIX0_WIDTH, d), MIX0_WIDTH ** -0.5 * DEEPNORM_BETA),
        'mla_w_in': nrm((N_ODD, d, MLA_IN), d ** -0.5),
        'mla_q_norm': 1.0 + nrm((N_ODD, MLA_Q_LORA), 0.01),
        'mla_kv_norm': 1.0 + nrm((N_ODD, MLA_KV_LORA), 0.01),
        'mla_w_uq': nrm((N_ODD, MLA_Q_LORA, MLA_HEADS * (MLA_NOPE + MLA_ROPE)), MLA_Q_LORA ** -0.5),
        'mla_w_ukv': nrm((N_ODD, MLA_KV_LORA, MLA_HEADS * (MLA_NOPE + MLA_V)), MLA_KV_LORA ** -0.5),
        'mla_w_out': nrm((N_ODD, MLA_HEADS * MLA_V, d), (MLA_HEADS * MLA_V) ** -0.5 * DEEPNORM_BETA),
        'ln_mix_g': 1.0 + nrm((DEPTH, d), 0.01),
        'ln_mix_b': nrm((DEPTH, d), 0.01),
        'ln_ffn_g': 1.0 + nrm((DEPTH, d), 0.01),
        'ln_ffn_b': nrm((DEPTH, d), 0.01),
        'router_w': nrm((DEPTH, d, N_EXPERTS), d ** -0.5),
        'router_bias': nrm((DEPTH, N_EXPERTS), 0.01),
        'expert_w_gate': nrm((DEPTH, N_EXPERTS, d, D_EXPERT), d ** -0.5),
        'expert_w_up': nrm((DEPTH, N_EXPERTS, d, D_EXPERT), d ** -0.5),
        'expert_w_down': nrm((DEPTH, N_EXPERTS, D_EXPERT, d), D_EXPERT ** -0.5 * DEEPNORM_BETA),
        'shared_w_gate': nrm((DEPTH, d, D_SHARED), d ** -0.5),
        'shared_w_up': nrm((DEPTH, d, D_SHARED), d ** -0.5),
        'shared_w_down': nrm((DEPTH, D_SHARED, d), D_SHARED ** -0.5 * DEEPNORM_BETA),
    }


def reference(x, swa_nsa_w_in, swa_sinks, nsa_cmp_pe_k, nsa_cmp_k_w1, nsa_cmp_k_w2, nsa_cmp_pe_v,
              nsa_cmp_v_w1, nsa_cmp_v_w2, swa_nsa_w_out, mla_w_in, mla_q_norm, mla_kv_norm, mla_w_uq,
              mla_w_ukv, mla_w_out, ln_mix_g, ln_mix_b, ln_ffn_g, ln_ffn_b, router_w, router_bias,
              expert_w_gate, expert_w_up, expert_w_down, shared_w_gate, shared_w_up, shared_w_down):
    for layer in range(DEPTH):
        j = layer // 2
        if layer % 2 == 0:
            h = _swa_nsa_mixer(x, swa_nsa_w_in[j], swa_sinks[j], nsa_cmp_pe_k[j], nsa_cmp_k_w1[j],
                               nsa_cmp_k_w2[j], nsa_cmp_pe_v[j], nsa_cmp_v_w1[j], nsa_cmp_v_w2[j],
                               swa_nsa_w_out[j])
        else:
            h = _mla_mixer(x, mla_w_in[j], mla_q_norm[j], mla_kv_norm[j], mla_w_uq[j], mla_w_ukv[j],
                           mla_w_out[j])
        x = _layer_norm(DEEPNORM_ALPHA * x + h, ln_mix_g[layer], ln_mix_b[layer])
        f = _moe(x, router_w[layer], router_bias[layer], expert_w_gate[layer], expert_w_up[layer],
                 expert_w_down[layer], shared_w_gate[layer], shared_w_up[layer], shared_w_down[layer])
        x = _layer_norm(DEEPNORM_ALPHA * x + f, ln_ffn_g[layer], ln_ffn_b[layer])
    return x
```

```python
import functools
import math

import numpy as np
import jax
import jax.numpy as jnp
from jax import lax
from jax.experimental import pallas as pl
from jax.experimental.pallas import tpu as pltpu

F32 = jnp.float32
BF16 = jnp.bfloat16
I32 = jnp.int32

LANES = 128
VMEM_LIMIT = 48 * 1024 * 1024

ROPE_THETA = 10000.0
LN_EPS = 1e-5
RMS_EPS = 1e-6
NEG_BIG = -1e30
FORCE_SCORE = 1e4

HEAD_DIM = 64
SWA_Q_HEADS = 8
SWA_KV_HEADS = 2
SWA_WINDOW = 128
NSA_Q_HEADS = 8
NSA_KV_HEADS = 2
NSA_CMP_LEN = 32
NSA_CMP_STRIDE = 16
NSA_CMP_HIDDEN = 128
NSA_SEL_LEN = 64
NSA_N_SEL = 16
NSA_WINDOW = 512
GQA_REP = 4

MLA_HEADS = 16
MLA_NOPE = 64
MLA_ROPE = 32
MLA_V = 64
MLA_Q_LORA = 384
MLA_KV_LORA = 256

N_EXPERTS = 64
MOE_GROUPS = 8
MOE_TOPK_GROUPS = 4
MOE_TOP_K = 6
D_EXPERT = 256
MOE_ROUTED_SCALE = 2.5
MOE_ROW_BLOCK = 256

NT_DIMS = (((1,), (1,)), ((), ()))


def _params(*sem):
    return pltpu.CompilerParams(dimension_semantics=sem, vmem_limit_bytes=VMEM_LIMIT)


def _full(shape):
    nd = len(shape)
    return pl.BlockSpec(shape, lambda *_: (0,) * nd)


def _rope_slab(y, cos, sin, half):
    lane = lax.broadcasted_iota(I32, y.shape, 1)
    first = (lane % (2 * half)) < half
    rot = jnp.where(first, pltpu.roll(y, LANES - half, 1), pltpu.roll(y, half, 1))
    return y * cos + rot * sin


def _layer_norm_rows(v, g, b):
    mu = jnp.mean(v, axis=-1, keepdims=True)
    vc = v - mu
    var = jnp.mean(vc * vc, axis=-1, keepdims=True)
    return vc * lax.rsqrt(var + LN_EPS) * g + b


def _silu(v):
    return v * (1.0 / (1.0 + jnp.exp(-v)))


def _sigmoid(v):
    return 1.0 / (1.0 + jnp.exp(-v))


def _stack_group_queries(q_ref, group, rep):
    lane = lax.broadcasted_iota(I32, (q_ref.shape[0], LANES), 1)
    keep_lo = jnp.where(lane < HEAD_DIM, 1.0, 0.0).astype(q_ref.dtype)
    keep_hi = jnp.where(lane < HEAD_DIM, 0.0, 1.0).astype(q_ref.dtype)
    parts = []
    for r in range(rep):
        h = group * rep + r
        slab = q_ref[:, (h // 2) * LANES:(h // 2 + 1) * LANES]
        parts.append(slab * (keep_lo if h % 2 == 0 else keep_hi))
    return jnp.concatenate(parts, axis=0)


def _rank_desc(score, n_valid, axis):
    idx = lax.broadcasted_iota(I32, score.shape, axis)
    rank = jnp.zeros(score.shape, I32)
    for j in range(n_valid):
        other = lax.slice_in_dim(score, j, j + 1, axis=axis)
        ahead = jnp.where(other > score, 1, jnp.where(other == score, jnp.where(idx > j, 1, 0), 0))
        rank = rank + ahead
    return rank


_MIX0_OUTS = (
    ("qa", 512, "rope_q", BF16), ("ka", 256, "rope", BF16), ("va", 256, "plain", BF16),
    ("qb", 512, "rope_q", BF16), ("kc", 128, "plain", BF16), ("vc", 128, "plain", BF16),
    ("ks", 256, "rope", BF16), ("vs", 256, "plain", BF16),
    ("kw", 256, "rope", BF16), ("vw", 256, "plain", BF16),
    ("gates", 128, "sigmoid", F32),
)


def _mix0_proj_kernel(x_ref, w_ref, cos_ref, sin_ref, *out_refs):
    xb = x_ref[...].astype(BF16)
    cos = cos_ref[...]
    sin = sin_ref[...]
    off = 0
    for (_, width, kind, _), o_ref in zip(_MIX0_OUTS, out_refs):
        y = jnp.dot(xb, w_ref[:, off:off + width], preferred_element_type=F32)
        for c in range(width // LANES):
            yc = y[:, c * LANES:(c + 1) * LANES]
            if kind in ("rope", "rope_q"):
                yc = _rope_slab(yc, cos, sin, HEAD_DIM // 2)
            if kind == "rope_q":
                yc = yc * (HEAD_DIM ** -0.5)
            if kind == "sigmoid":
                yc = _sigmoid(yc)
            o_ref[:, c * LANES:(c + 1) * LANES] = yc.astype(o_ref.dtype)
        off += width


def _mix0_proj(x2, w_all, cos, sin, seq, tm):
    n = x2.shape[0]
    d = x2.shape[1]
    wtot = w_all.shape[1]
    sblk = seq // tm
    return pl.pallas_call(
        _mix0_proj_kernel,
        grid=(n // tm,),
        in_specs=[pl.BlockSpec((tm, d), lambda i: (i, 0)),
                  _full((d, wtot)),
                  pl.BlockSpec((tm, LANES), lambda i: (i % sblk, 0)),
                  pl.BlockSpec((tm, LANES), lambda i: (i % sblk, 0))],
        out_specs=[pl.BlockSpec((tm, w), lambda i: (i, 0)) for _, w, _, _ in _MIX0_OUTS],
        out_shape=[jax.ShapeDtypeStruct((n, w), dt) for _, w, _, dt in _MIX0_OUTS],
        compiler_params=_params("parallel"),
        name="mix0_proj",
    )(x2, w_all, cos, sin)


def _banded_kernel(*refs, window, blk, seq, has_sink, gate_col):
    if has_sink:
        sink_ref, refs = refs[0], refs[1:]
    q_ref, k_ref, v_ref = refs[:3]
    g_ref = refs[3] if gate_col is not None else None
    o_ref = refs[-1]
    i = pl.program_id(1)
    start = i * blk
    span = min(window + blk, seq)
    k0 = pl.multiple_of(jnp.maximum(start - window, 0), LANES)
    tq = start + lax.broadcasted_iota(I32, (blk, span), 0)
    tk = k0 + lax.broadcasted_iota(I32, (blk, span), 1)
    diff = tq - tk
    bias = jnp.where(diff >= 0, jnp.where(diff < window, 0.0, -jnp.inf), -jnp.inf).astype(F32)
    bias = jnp.concatenate([bias] * GQA_REP, axis=0)
    lane = lax.broadcasted_iota(I32, (blk, LANES), 1)
    lo = lane < HEAD_DIM
    n_groups = q_ref.shape[1] // (GQA_REP * HEAD_DIM)
    for g in range(n_groups):
        k = k_ref[pl.ds(k0, span), g * LANES:(g + 1) * LANES]
        v = v_ref[pl.ds(k0, span), g * LANES:(g + 1) * LANES]
        q = _stack_group_queries(q_ref, g, GQA_REP)
        s = lax.dot_general(q, k, NT_DIMS, preferred_element_type=F32) + bias
        m = jnp.max(s, axis=-1, keepdims=True)
        if has_sink:
            sink = jnp.concatenate(
                [jnp.full((blk, 1), sink_ref[g * GQA_REP + r], F32) for r in range(GQA_REP)], axis=0)
            m = jnp.maximum(m, sink)
        p = jnp.exp(s - m)
        l = jnp.sum(p, axis=-1, keepdims=True)
        if has_sink:
            l = l + jnp.exp(sink - m)
        o = jnp.dot(p.astype(BF16), v, preferred_element_type=F32) / l
        for j in range(GQA_REP // 2):
            slab = jnp.where(lo, o[(2 * j) * blk:(2 * j + 1) * blk], o[(2 * j + 1) * blk:(2 * j + 2) * blk])
            if gate_col is not None:
                c = gate_col + g * GQA_REP + 2 * j
                gate = jnp.where(lo, g_ref[:, c:c + 1], g_ref[:, c + 1:c + 2])
                slab = slab * gate
            col = (g * (GQA_REP // 2) + j) * LANES
            o_ref[:, col:col + LANES] = slab.astype(o_ref.dtype)


def _banded_attention(q, k, v, batch, seq, window, blk, sinks=None, gates=None, gate_col=None,
                      out_dtype=None, name="banded"):
    n, qw = q.shape
    kw = k.shape[1]
    nblk = seq // blk
    kern = functools.partial(_banded_kernel, window=window, blk=blk, seq=seq,
                             has_sink=sinks is not None, gate_col=gate_col)
    in_specs = []
    args = []
    if sinks is not None:
        in_specs.append(pl.BlockSpec(memory_space=pltpu.SMEM))
        args.append(sinks)
    in_specs += [pl.BlockSpec((blk, qw), lambda b, i: (b * nblk + i, 0)),
                 pl.BlockSpec((seq, kw), lambda b, i: (b, 0)),
                 pl.BlockSpec((seq, kw), lambda b, i: (b, 0))]
    args += [q, k, v]
    if gates is not None:
        in_specs.append(pl.BlockSpec((blk, LANES), lambda b, i: (b * nblk + i, 0)))
        args.append(gates)
    return pl.pallas_call(
        kern,
        grid=(batch, nblk),
        in_specs=in_specs,
        out_specs=pl.BlockSpec((blk, qw), lambda b, i: (b * nblk + i, 0)),
        out_shape=jax.ShapeDtypeStruct((n, qw), BF16 if out_dtype is None else out_dtype),
        compiler_params=_params("parallel", "arbitrary"),
        name=name,
    )(*args)


def _gelu_tanh(v):
    return 0.5 * v * (1.0 + jnp.tanh(math.sqrt(2.0 / math.pi) * (v + 0.044715 * (v * v * v))))


def _compress_kernel(tk_ref, tv_ref, w1k_ref, w1v_ref, pek_ref, pev_ref, w1ko_ref, w1vo_ref,
                     w2k_ref, w2v_ref, cos_ref, sin_ref, kc_ref, vc_ref):
    n_chunk = tk_ref.shape[0]
    hid = NSA_CMP_HIDDEN
    for t_ref, w1_ref, pe_ref, w1o_ref, w2_ref, o_ref, rope in (
            (tk_ref, w1k_ref, pek_ref, w1ko_ref, w2k_ref, kc_ref, True),
            (tv_ref, w1v_ref, pev_ref, w1vo_ref, w2v_ref, vc_ref, False)):
        pe_term = jnp.dot(pe_ref[...], w1o_ref[...], preferred_element_type=F32)[0:1, :]
        t = t_ref[...]
        for g in range(NSA_KV_HEADS):
            uv = jnp.dot(t, w1_ref[g], preferred_element_type=F32)
            nxt = pltpu.roll(uv[:, hid:], n_chunk - 1, 0)
            h = _gelu_tanh(uv[:, :hid] + nxt + pe_term)
            c = jnp.dot(h.astype(BF16), w2_ref[...], preferred_element_type=F32)
            if rope:
                c = _rope_slab(c, cos_ref[...], sin_ref[...], HEAD_DIM // 2)
            o_ref[0, g] = c.astype(o_ref.dtype)


def _compress(tk, tv, w1k, w1v, pek, pev, w1ko, w1vo, w2k, w2v, cos, sin, batch, n_chunk):
    width = tk.shape[1]
    out = jax.ShapeDtypeStruct((batch, NSA_KV_HEADS, n_chunk, LANES), BF16)
    ospec = pl.BlockSpec((1, NSA_KV_HEADS, n_chunk, LANES), lambda b: (b, 0, 0, 0))
    return pl.pallas_call(
        _compress_kernel,
        grid=(batch,),
        in_specs=[pl.BlockSpec((n_chunk, width), lambda b: (b, 0)),
                  pl.BlockSpec((n_chunk, width), lambda b: (b, 0)),
                  _full(w1k.shape), _full(w1v.shape), _full(pek.shape), _full(pev.shape),
                  _full(w1ko.shape), _full(w1vo.shape), _full(w2k.shape), _full(w2v.shape),
                  _full(cos.shape), _full(sin.shape)],
        out_specs=[ospec, ospec],
        out_shape=[out, out],
        compiler_params=_params("parallel"),
        name="nsa_compress",
    )(tk, tv, w1k, w1v, pek, pev, w1ko, w1vo, w2k, w2v, cos, sin)


def _nsa_kernel(q_ref, kc_ref, vc_ref, ks_ref, vs_ref, g_ref, win_ref, ov_ref, e_ref, o_ref,
                *, qblk, seq, n_sb, n_sel, tk_sel):
    grp = pl.program_id(1)
    i = pl.program_id(2)
    start = i * qblk
    n_cmp = kc_ref.shape[2]
    rows = GQA_REP * qblk
    lane = lax.broadcasted_iota(I32, (qblk, LANES), 1)
    lo = lane < HEAD_DIM
    q = _stack_group_queries(q_ref, 0, GQA_REP)
    tq = start + lax.broadcasted_iota(I32, (qblk, 1), 0)

    kc = kc_ref[0, 0]
    vc = vc_ref[0, 0]
    cmp_end = lax.broadcasted_iota(I32, (qblk, n_cmp), 1) * NSA_CMP_STRIDE + (NSA_CMP_LEN - 1)
    vis = jnp.where(cmp_end <= tq, 1.0, 0.0).astype(F32)
    vis4 = jnp.concatenate([vis] * GQA_REP, axis=0)
    s_c = lax.dot_general(q, kc, NT_DIMS, preferred_element_type=F32)
    s_c = jnp.where(vis4 > 0.5, s_c, NEG_BIG)
    p_c = jnp.exp(s_c - jnp.max(s_c, axis=-1, keepdims=True)) * vis4
    p_c = p_c / jnp.maximum(jnp.sum(p_c, axis=-1, keepdims=True), 1e-30)
    o_c = jnp.dot(p_c.astype(BF16), vc, preferred_element_type=F32)

    p_sum = p_c[0:qblk]
    for r in range(1, GQA_REP):
        p_sum = p_sum + p_c[r * qblk:(r + 1) * qblk]
    imp = jnp.dot(p_sum, ov_ref[...], preferred_element_type=F32, precision=lax.Precision.HIGHEST)
    cur = tq // NSA_SEL_LEN
    forced = jnp.where(lane == 0, 1, jnp.where(lane == cur, 1, jnp.where(lane == cur - 1, 1, 0)))
    score = jnp.where(forced == 1, FORCE_SCORE, jnp.where(lane * NSA_SEL_LEN > tq, -1.0, imp))
    score = jnp.where(lane < n_sb, score, -2.0)
    sel = jnp.where(_rank_desc(score, n_sb, 1) < n_sel, 1.0, 0.0).astype(BF16)

    def body(j, carry):
        m, l, acc = carry
        k0 = pl.multiple_of(j * tk_sel, tk_sel)
        k = ks_ref[pl.ds(k0, tk_sel), :]
        v = vs_ref[pl.ds(k0, tk_sel), :]
        picked = jnp.dot(sel, e_ref[:, pl.ds(k0, tk_sel)], preferred_element_type=F32)
        tk = k0 + lax.broadcasted_iota(I32, (qblk, tk_sel), 1)
        bias = jnp.where(picked > 0.5, jnp.where(tk <= tq, 0.0, -jnp.inf), -jnp.inf).astype(F32)
        bias = jnp.concatenate([bias] * GQA_REP, axis=0)
        s = lax.dot_general(q, k, NT_DIMS, preferred_element_type=F32) + bias
        m_new = jnp.maximum(m, jnp.max(s, axis=-1, keepdims=True))
        a = jnp.exp(m - m_new)
        p = jnp.exp(s - m_new)
        l = a * l + jnp.sum(p, axis=-1, keepdims=True)
        acc = a * acc + jnp.dot(p.astype(BF16), v, preferred_element_type=F32)
        return m_new, l, acc

    n_tiles = (start + qblk + tk_sel - 1) // tk_sel
    init = (jnp.full((rows, 1), -jnp.inf, F32), jnp.zeros((rows, 1), F32), jnp.zeros((rows, LANES), F32))
    _, l_s, acc_s = lax.fori_loop(0, n_tiles, body, init)
    o_s = acc_s / l_s

    nh = NSA_Q_HEADS
    for j in range(GQA_REP // 2):
        h_even = grp * GQA_REP + 2 * j

        def gate(branch, h0=h_even):
            col = branch * nh + h0
            ids = lax.broadcasted_iota(I32, (qblk, LANES), 1)
            g_all = g_ref[...]
            ge = jnp.sum(jnp.where(ids == col, g_all, 0.0), axis=-1, keepdims=True)
            go = jnp.sum(jnp.where(ids == col + 1, g_all, 0.0), axis=-1, keepdims=True)
            return jnp.where(lo, ge, go)

        rows_e = slice((2 * j) * qblk, (2 * j + 1) * qblk)
        rows_o = slice((2 * j + 1) * qblk, (2 * j + 2) * qblk)
        oc = jnp.where(lo, o_c[rows_e], o_c[rows_o])
        os_ = jnp.where(lo, o_s[rows_e], o_s[rows_o])
        out = gate(0) * oc + gate(1) * os_ + win_ref[:, j * LANES:(j + 1) * LANES]
        o_ref[:, j * LANES:(j + 1) * LANES] = out.astype(o_ref.dtype)


def _nsa_cmp_sel(qb, kcmp, vcmp, ks, vs, gates, win, overlap, expand, batch, seq, qblk, n_sb, n_sel, tk_sel):
    n = qb.shape[0]
    nq = seq // qblk
    n_cmp = kcmp.shape[2]
    gw = GQA_REP * HEAD_DIM
    kern = functools.partial(_nsa_kernel, qblk=qblk, seq=seq, n_sb=n_sb, n_sel=n_sel, tk_sel=tk_sel)
    return pl.pallas_call(
        kern,
        grid=(batch, NSA_KV_HEADS, nq),
        in_specs=[pl.BlockSpec((qblk, gw), lambda b, g, i: (b * nq + i, g)),
                  pl.BlockSpec((1, 1, n_cmp, LANES), lambda b, g, i: (b, g, 0, 0)),
                  pl.BlockSpec((1, 1, n_cmp, LANES), lambda b, g, i: (b, g, 0, 0)),
                  pl.BlockSpec((seq, LANES), lambda b, g, i: (b, g)),
                  pl.BlockSpec((seq, LANES), lambda b, g, i: (b, g)),
                  pl.BlockSpec((qblk, LANES), lambda b, g, i: (b * nq + i, 0)),
                  pl.BlockSpec((qblk, gw), lambda b, g, i: (b * nq + i, g)),
                  _full(overlap.shape), _full(expand.shape)],
        out_specs=pl.BlockSpec((qblk, gw), lambda b, g, i: (b * nq + i, g)),
        out_shape=jax.ShapeDtypeStruct((n, NSA_Q_HEADS * HEAD_DIM), BF16),
        compiler_params=_params("parallel", "parallel", "arbitrary"),
        name="nsa_cmp_sel",
    )(qb, kcmp, vcmp, ks, vs, gates, win, overlap, expand)


def _oproj_ln_kernel(*refs, n_parts, alpha):
    o_refs = refs[:n_parts]
    w_refs = refs[n_parts:2 * n_parts]
    x_ref, g_ref, b_ref, y_ref = refs[2 * n_parts:]
    h = jnp.dot(o_refs[0][...], w_refs[0][...], preferred_element_type=F32)
    for o_ref, w_ref in zip(o_refs[1:], w_refs[1:]):
        h = h + jnp.dot(o_ref[...], w_ref[...], preferred_element_type=F32)
    y_ref[...] = _layer_norm_rows(alpha * x_ref[...] + h, g_ref[...], b_ref[...])


def _oproj_ln(o_parts, w_parts, x2, g, b, alpha, tm):
    n, d = x2.shape
    kern = functools.partial(_oproj_ln_kernel, n_parts=len(o_parts), alpha=alpha)
    in_specs = ([pl.BlockSpec((tm, o.shape[1]), lambda i: (i, 0)) for o in o_parts]
                + [_full(w.shape) for w in w_parts]
                + [pl.BlockSpec((tm, d), lambda i: (i, 0)), _full((1, d)), _full((1, d))])
    return pl.pallas_call(
        kern,
        grid=(n // tm,),
        in_specs=in_specs,
        out_specs=pl.BlockSpec((tm, d), lambda i: (i, 0)),
        out_shape=jax.ShapeDtypeStruct((n, d), F32),
        compiler_params=_params("parallel"),
        name="oproj_ln",
    )(*o_parts, *w_parts, x2, g, b)


def _rms_rows(v, g):
    return v * lax.rsqrt(jnp.mean(v * v, axis=-1, keepdims=True) + RMS_EPS) * g


def _mla_proj_kernel(x_ref, win_ref, qn_ref, kvn_ref, wuq_ref, wuk_ref, wuv_ref, cos_ref, sin_ref,
                     q_ref, k_ref, v_ref):
    xb = x_ref[...].astype(BF16)
    cos = cos_ref[...]
    sin = sin_ref[...]
    lat = jnp.dot(xb, win_ref[...], preferred_element_type=F32)
    cq = _rms_rows(lat[:, :MLA_Q_LORA], qn_ref[...]).astype(BF16)
    ckv = _rms_rows(lat[:, MLA_Q_LORA:MLA_Q_LORA + MLA_KV_LORA], kvn_ref[...]).astype(BF16)
    kr = _rope_slab(lat[:, MLA_Q_LORA + MLA_KV_LORA:], cos, sin, MLA_ROPE // 2)
    q = jnp.dot(cq, wuq_ref[...], preferred_element_type=F32)
    k = jnp.dot(ckv, wuk_ref[...], preferred_element_type=F32)
    for h in range(MLA_HEADS):
        sl = slice(h * LANES, (h + 1) * LANES)
        q_ref[:, sl] = _rope_slab(q[:, sl], cos, sin, MLA_ROPE // 2).astype(BF16)
        k_ref[:, sl] = (k[:, sl] + kr).astype(BF16)
    v_ref[...] = jnp.dot(ckv, wuv_ref[...], preferred_element_type=F32).astype(BF16)


def _mla_proj(x2, w_in, qn, kvn, wuq, wuk, wuv, cos, sin, seq, tm):
    n, d = x2.shape
    sblk = seq // tm
    hw = MLA_HEADS * LANES
    vw = MLA_HEADS * MLA_V
    return pl.pallas_call(
        _mla_proj_kernel,
        grid=(n // tm,),
        in_specs=[pl.BlockSpec((tm, d), lambda i: (i, 0)),
                  _full(w_in.shape), _full(qn.shape), _full(kvn.shape),
                  _full(wuq.shape), _full(wuk.shape), _full(wuv.shape),
                  pl.BlockSpec((tm, LANES), lambda i: (i % sblk, 0)),
                  pl.BlockSpec((tm, LANES), lambda i: (i % sblk, 0))],
        out_specs=[pl.BlockSpec((tm, hw), lambda i: (i, 0)),
                   pl.BlockSpec((tm, hw), lambda i: (i, 0)),
                   pl.BlockSpec((tm, vw), lambda i: (i, 0))],
        out_shape=[jax.ShapeDtypeStruct((n, hw), BF16), jax.ShapeDtypeStruct((n, hw), BF16),
                   jax.ShapeDtypeStruct((n, vw), BF16)],
        compiler_params=_params("parallel"),
        name="mla_proj",
    )(x2, w_in, qn, kvn, wuq, wuk, wuv, cos, sin)


def _mla_attn_kernel(q_ref, k_ref, v_ref, o_ref, *, tq, scale):
    i = pl.program_id(2)
    q_start = i * tq
    lane = lax.broadcasted_iota(I32, (tq, LANES), 1)
    lo = lane < MLA_V
    qs = [q_ref[:, e * LANES:(e + 1) * LANES] for e in range(2)]

    def body(j, carry):
        k0 = pl.multiple_of(j * tq, tq)
        v = v_ref[pl.ds(k0, tq), :]
        t_q = q_start + lax.broadcasted_iota(I32, (tq, tq), 0)
        t_k = k0 + lax.broadcasted_iota(I32, (tq, tq), 1)
        bias = jnp.where(t_k <= t_q, 0.0, -jnp.inf).astype(F32)
        out = []
        for e in range(2):
            m, l, acc = carry[e]
            k = k_ref[pl.ds(k0, tq), e * LANES:(e + 1) * LANES]
            s = lax.dot_general(qs[e], k, NT_DIMS, preferred_element_type=F32) * scale + bias
            m_new = jnp.maximum(m, jnp.max(s, axis=-1, keepdims=True))
            a = jnp.exp(m - m_new)
            p = jnp.exp(s - m_new)
            l = a * l + jnp.sum(p, axis=-1, keepdims=True)
            acc = a * acc + jnp.dot(p.astype(BF16), v, preferred_element_type=F32)
            out.append((m_new, l, acc))
        return tuple(out)

    one = (jnp.full((tq, 1), -jnp.inf, F32), jnp.zeros((tq, 1), F32), jnp.zeros((tq, LANES), F32))
    res = lax.fori_loop(0, i + 1, body, (one, one))
    o0 = res[0][2] / res[0][1]
    o1 = res[1][2] / res[1][1]
    o_ref[...] = jnp.where(lo, o0, o1).astype(o_ref.dtype)


def _mla_attention(q, k, v, batch, seq, tq):
    n = q.shape[0]
    nq = seq // tq
    kern = functools.partial(_mla_attn_kernel, tq=tq, scale=(MLA_NOPE + MLA_ROPE) ** -0.5)
    return pl.pallas_call(
        kern,
        grid=(batch, MLA_HEADS // 2, nq),
        in_specs=[pl.BlockSpec((tq, 2 * LANES), lambda b, p, i: (b * nq + i, p)),
                  pl.BlockSpec((seq, 2 * LANES), lambda b, p, i: (b, p)),
                  pl.BlockSpec((seq, LANES), lambda b, p, i: (b, p))],
        out_specs=pl.BlockSpec((tq, LANES), lambda b, p, i: (b * nq + i, p)),
        out_shape=jax.ShapeDtypeStruct((n, MLA_HEADS * MLA_V), BF16),
        compiler_params=_params("parallel", "parallel", "arbitrary"),
        name="mla_attn",
    )(q, k, v)


def _router_kernel(x_ref, rw_ref, rb_ref, eidx_ref, gate_ref, pos_ref, cnt_ref, carry_ref, *, steps_per_chunk):
    i = pl.program_id(0)
    tm = x_ref.shape[0]
    per_group = N_EXPERTS // MOE_GROUPS

    @pl.when(i % steps_per_chunk == 0)
    def _():
        carry_ref[...] = jnp.zeros_like(carry_ref)

    logits = jnp.dot(x_ref[...], rw_ref[...], preferred_element_type=F32, precision=lax.Precision.HIGHEST)
    lt = jnp.transpose(logits)[:N_EXPERTS]
    s = _sigmoid(lt)
    sb = s + rb_ref[...]

    g3 = sb.reshape(MOE_GROUPS, per_group, tm)
    idx3 = lax.broadcasted_iota(I32, g3.shape, 1).astype(F32)
    m1 = jnp.max(g3, axis=1, keepdims=True)
    first = jnp.min(jnp.where(g3 == m1, idx3, float(per_group)), axis=1, keepdims=True)
    m2 = jnp.max(jnp.where(idx3 == first, -jnp.inf, g3), axis=1, keepdims=True)
    gscore = (m1 + m2).reshape(MOE_GROUPS, tm)
    gsel = _rank_desc(gscore, MOE_GROUPS, 0) < MOE_TOPK_GROUPS
    gsel3 = jnp.where(gsel, 1.0, 0.0).astype(F32).reshape(MOE_GROUPS, 1, tm)
    masked = jnp.where(gsel3 > 0.5, g3, -jnp.inf).reshape(N_EXPERTS, tm)
    rank = _rank_desc(masked, N_EXPERTS, 0)
    sel = rank < MOE_TOP_K
    gate = jnp.where(sel, s, 0.0)
    gate = gate / jnp.sum(gate, axis=0, keepdims=True) * MOE_ROUTED_SCALE

    sel_b = jnp.where(sel, 1.0, 0.0).astype(BF16)
    r_i = lax.broadcasted_iota(I32, (tm, tm), 0)
    c_i = lax.broadcasted_iota(I32, (tm, tm), 1)
    tri = jnp.where(r_i < c_i, 1.0, 0.0).astype(BF16)
    carry = carry_ref[:, 0:1]
    before = jnp.dot(sel_b, tri, preferred_element_type=F32) + carry
    carry_new = carry + jnp.sum(sel_b.astype(F32), axis=1, keepdims=True)
    carry_ref[...] = jnp.broadcast_to(carry_new, carry_ref.shape)
    cnt_ref[0] = jnp.broadcast_to(carry_new, carry_ref.shape).astype(I32)

    e_iota = lax.broadcasted_iota(I32, (N_EXPERTS, tm), 0).astype(F32)
    rows_e, rows_g, rows_p = [], [], []
    for r in range(MOE_TOP_K):
        hit = rank == r
        rows_e.append(jnp.sum(jnp.where(hit, e_iota, 0.0), axis=0, keepdims=True))
        rows_g.append(jnp.sum(jnp.where(hit, gate, 0.0), axis=0, keepdims=True))
        rows_p.append(jnp.sum(jnp.where(hit, before, 0.0), axis=0, keepdims=True))
    pad = 8 - MOE_TOP_K
    eidx_ref[...] = jnp.concatenate(rows_e + [jnp.zeros((pad, tm), F32)], axis=0).astype(I32)
    gate_ref[...] = jnp.concatenate(rows_g + [jnp.zeros((pad, tm), F32)], axis=0)
    pos_ref[...] = jnp.concatenate(rows_p + [jnp.zeros((pad, tm), F32)], axis=0).astype(I32)


def _router(x2, rw, rb, n_chunk, tm):
    n, d = x2.shape
    steps = n // tm
    spc = steps // n_chunk
    kern = functools.partial(_router_kernel, steps_per_chunk=spc)
    row8 = pl.BlockSpec((8, tm), lambda i: (0, i))
    return pl.pallas_call(
        kern,
        grid=(steps,),
        in_specs=[pl.BlockSpec((tm, d), lambda i: (i, 0)), _full(rw.shape), _full(rb.shape)],
        out_specs=[row8, row8, row8, pl.BlockSpec((1, N_EXPERTS, LANES), lambda i: (i // spc, 0, 0))],
        out_shape=[jax.ShapeDtypeStruct((8, n), I32), jax.ShapeDtypeStruct((8, n), F32),
                   jax.ShapeDtypeStruct((8, n), I32),
                   jax.ShapeDtypeStruct((n_chunk, N_EXPERTS, LANES), I32)],
        scratch_shapes=[pltpu.VMEM((N_EXPERTS, LANES), F32)],
        compiler_params=_params("arbitrary"),
        name="moe_router",
    )(x2, rw, rb)


def _experts_kernel(be_ref, nv_ref, xs_ref, wgu_ref, wd_ref, y_ref):
    i = pl.program_id(0)

    @pl.when(i < nv_ref[0])
    def _():
        gu = jnp.dot(xs_ref[...], wgu_ref[0], preferred_element_type=F32)
        h = _silu(gu[:, :D_EXPERT]) * gu[:, D_EXPERT:]
        y_ref[...] = jnp.dot(h.astype(BF16), wd_ref[0], preferred_element_type=F32).astype(y_ref.dtype)

    @pl.when(i >= nv_ref[0])
    def _():
        y_ref[...] = jnp.zeros_like(y_ref)


def _experts(blk_exp, n_valid, xs, wgu, wd, tb):
    rows, d = xs.shape
    grid_spec = pltpu.PrefetchScalarGridSpec(
        num_scalar_prefetch=2,
        grid=(rows // tb,),
        in_specs=[pl.BlockSpec((tb, d), lambda i, be, nv: (i, 0)),
                  pl.BlockSpec((1, d, 2 * D_EXPERT), lambda i, be, nv: (be[i], 0, 0)),
                  pl.BlockSpec((1, D_EXPERT, d), lambda i, be, nv: (be[i], 0, 0))],
        out_specs=pl.BlockSpec((tb, d), lambda i, be, nv: (i, 0)),
    )
    return pl.pallas_call(
        _experts_kernel,
        grid_spec=grid_spec,
        out_shape=jax.ShapeDtypeStruct((rows, d), BF16),
        compiler_params=_params("arbitrary"),
        name="moe_experts",
    )(blk_exp, n_valid, xs, wgu, wd)


def _combine_ln_kernel(x_ref, yg_ref, gt_ref, wgu_ref, wd_ref, g_ref, b_ref, o_ref, *, alpha):
    x = x_ref[...]
    gu = jnp.dot(x.astype(BF16), wgu_ref[...], preferred_element_type=F32)
    dsh = gu.shape[1] // 2
    h = _silu(gu[:, :dsh]) * gu[:, dsh:]
    f = jnp.dot(h.astype(BF16), wd_ref[...], preferred_element_type=F32)
    for k in range(MOE_TOP_K):
        f = f + gt_ref[:, k:k + 1] * yg_ref[k].astype(F32)
    o_ref[...] = _layer_norm_rows(alpha * x + f, g_ref[...], b_ref[...])


def _combine_ln(x2, yg, gates_t, wgu, wd, g, b, alpha, tm):
    n, d = x2.shape
    kern = functools.partial(_combine_ln_kernel, alpha=alpha)
    return pl.pallas_call(
        kern,
        grid=(n // tm,),
        in_specs=[pl.BlockSpec((tm, d), lambda i: (i, 0)),
                  pl.BlockSpec((MOE_TOP_K, tm, d), lambda i: (0, i, 0)),
                  pl.BlockSpec((tm, 8), lambda i: (i, 0)),
                  _full(wgu.shape), _full(wd.shape), _full((1, d)), _full((1, d))],
        out_specs=pl.BlockSpec((tm, d), lambda i: (i, 0)),
        out_shape=jax.ShapeDtypeStruct((n, d), F32),
        compiler_params=_params("parallel"),
        name="moe_combine_ln",
    )(x2, yg, gates_t, wgu, wd, g, b)


def _rope_tables(positions, dim, lane_offset, period):
    half = dim // 2
    inv_freq = ROPE_THETA ** (-2.0 * jnp.arange(half, dtype=jnp.float32) / dim)
    ang = positions.astype(jnp.float32)[:, None] * inv_freq[None, :]
    cos_h, sin_h = jnp.cos(ang), jnp.sin(ang)
    lanes = np.arange(LANES)
    rel = (lanes - lane_offset) % period
    active = (lanes >= lane_offset) & (rel < dim)
    fidx = np.where(active, rel % half, 0)
    sign = np.where(rel < half, -1.0, 1.0)
    cos = jnp.where(active[None, :], cos_h[:, fidx], 1.0)
    sin = jnp.where(active[None, :], sin_h[:, fidx] * sign[None, :], 0.0)
    return cos.astype(F32), sin.astype(F32)


def _dup_heads(w):
    a, b = w[:, :HEAD_DIM], w[:, HEAD_DIM:]
    return jnp.concatenate([a, a, b, b], axis=1)


def _moe_layer(x1, rw, rb, wg, wu, wd, shg, shu, shd, ln_g, ln_b, alpha, tm):
    n, d = x1.shape
    n_chunk = 1
    tb = MOE_ROW_BLOCK
    rw_p = jnp.pad(rw, ((0, 0), (0, LANES - N_EXPERTS)))
    eidx, gate, pos, cnt = _router(x1, rw_p, rb.reshape(N_EXPERTS, 1), n_chunk, tm)
    eidx, gate, pos = eidx[:MOE_TOP_K], gate[:MOE_TOP_K], pos[:MOE_TOP_K]
    counts = cnt[:, :, 0].reshape(-1)
    padded = (counts + tb - 1) // tb * tb
    pad_ends = jnp.cumsum(padded)
    pad_starts = pad_ends - padded
    chunk_of = (jnp.arange(n, dtype=I32) // (n // n_chunk))[None, :]
    dest = pad_starts[chunk_of * N_EXPERTS + eidx] + pos
    n_rows = n * MOE_TOP_K
    n_blk = -(-(n_rows + n_chunk * N_EXPERTS * (tb - 1)) // tb)
    tok = jnp.broadcast_to(jnp.arange(n, dtype=I32)[None, :], dest.shape)
    row_tok = jnp.zeros((n_blk * tb,), I32).at[dest.reshape(-1)].set(tok.reshape(-1))
    blk_first_row = jnp.arange(n_blk, dtype=I32) * tb
    owner = jnp.sum((pad_ends[None, :] <= blk_first_row[:, None]).astype(I32), axis=1)
    blk_exp = jnp.minimum(owner, n_chunk * N_EXPERTS - 1).astype(I32) % N_EXPERTS
    n_valid = (pad_ends[-1] // tb).astype(I32).reshape(1)

    xs = jnp.take(x1.astype(BF16), row_tok, axis=0)
    wgu = jnp.concatenate([wg, wu], axis=-1).astype(BF16)
    y = _experts(blk_exp, n_valid, xs, wgu, wd.astype(BF16), tb)
    yg = jnp.take(y, dest.reshape(-1), axis=0).reshape(MOE_TOP_K, n, d)
    gates_t = jnp.pad(gate, ((0, 8 - MOE_TOP_K), (0, 0))).T
    sh_gu = jnp.concatenate([shg, shu], axis=-1).astype(BF16)
    return _combine_ln(x1, yg, gates_t, sh_gu, shd.astype(BF16), ln_g.reshape(1, d), ln_b.reshape(1, d), alpha, tm)


def _swa_nsa_layer(x2, batch, seq, w_in, sinks, pe_k, w_ck1, w_ck2, pe_v, w_cv1, w_cv2, w_out,
                   ln_g, ln_b, alpha, tm):
    n, d = x2.shape
    splits = np.cumsum([512, 128, 128, 512, 128, 128, 128, 128, 128, 128, 24])[:-1]
    qa, ka, va, qb, kc, vc, ksl, vsl, kw, vw, gts = jnp.split(w_in, [int(c) for c in splits], axis=1)
    w_all = jnp.concatenate(
        [qa, _dup_heads(ka), _dup_heads(va), qb, kc, vc, _dup_heads(ksl), _dup_heads(vsl),
         _dup_heads(kw), _dup_heads(vw), jnp.pad(gts, ((0, 0), (0, LANES - gts.shape[1])))], axis=1).astype(BF16)
    cos, sin = _rope_tables(jnp.arange(seq), HEAD_DIM, 0, HEAD_DIM)
    (q_a, k_a, v_a, q_b, k_c, v_c, k_s, v_s, k_w, v_w, gates) = _mix0_proj(x2, w_all, cos, sin, seq, tm)

    o_swa = _banded_attention(q_a, k_a, v_a, batch, seq, SWA_WINDOW, 128, sinks=sinks, name="swa")
    o_win = _banded_attention(q_b, k_w, v_w, batch, seq, NSA_WINDOW, 128, gates=gates,
                              gate_col=2 * NSA_Q_HEADS, out_dtype=F32, name="nsa_win")

    n_chunk = seq // NSA_CMP_STRIDE
    cw = NSA_CMP_STRIDE * NSA_KV_HEADS * HEAD_DIM
    hid = NSA_CMP_HIDDEN

    def expand_w1(w1):
        halves = w1.reshape(2, NSA_CMP_STRIDE, HEAD_DIM, hid)
        out = jnp.zeros((NSA_KV_HEADS, NSA_CMP_STRIDE, NSA_KV_HEADS, HEAD_DIM, 2 * hid), w1.dtype)
        for g in range(NSA_KV_HEADS):
            out = out.at[g, :, g, :, :hid].set(halves[0]).at[g, :, g, :, hid:].set(halves[1])
        return out.reshape(NSA_KV_HEADS, cw, 2 * hid).astype(BF16)

    def pe_rows(pe):
        return jnp.pad(pe.reshape(1, NSA_CMP_LEN * HEAD_DIM), ((0, 15), (0, 0))).astype(BF16)

    cos_c, sin_c = _rope_tables(jnp.arange(n_chunk) * NSA_CMP_STRIDE + NSA_CMP_LEN - 1, HEAD_DIM, 0, HEAD_DIM)
    k_cmp, v_cmp = _compress(
        k_c.reshape(batch * n_chunk, cw), v_c.reshape(batch * n_chunk, cw),
        expand_w1(w_ck1), expand_w1(w_cv1), pe_rows(pe_k), pe_rows(pe_v),
        w_ck1.astype(BF16), w_cv1.astype(BF16),
        jnp.concatenate([w_ck2, w_ck2], axis=1).astype(BF16),
        jnp.concatenate([w_cv2, w_cv2], axis=1).astype(BF16),
        cos_c, sin_c, batch, n_chunk)

    n_sb = seq // NSA_SEL_LEN
    n_sel = min(NSA_N_SEL, n_sb)
    cs = np.arange(n_chunk) * NSA_CMP_STRIDE
    bs = np.arange(n_sb) * NSA_SEL_LEN
    ov = np.clip(np.minimum(cs[:, None] + NSA_CMP_LEN, bs[None, :] + NSA_SEL_LEN)
                 - np.maximum(cs[:, None], bs[None, :]), 0, None) / NSA_CMP_LEN
    overlap = jnp.asarray(np.pad(ov, ((0, 0), (0, LANES - n_sb))), F32)
    expand = jnp.asarray(np.arange(LANES)[:, None] == (np.arange(seq)[None, :] // NSA_SEL_LEN), BF16)
    tk_sel = min(512, seq)
    o_nsa = _nsa_cmp_sel(q_b, k_cmp, v_cmp, k_s, v_s, gates, o_win, overlap, expand,
                         batch, seq, 128, n_sb, n_sel, tk_sel)
    w_o = w_out.astype(BF16)
    half = SWA_Q_HEADS * HEAD_DIM
    return _oproj_ln([o_swa, o_nsa], [w_o[:half], w_o[half:]], x2, ln_g.reshape(1, d), ln_b.reshape(1, d),
                     alpha, tm)


def _mla_layer(x2, batch, seq, w_in, q_norm, kv_norm, w_uq, w_ukv, w_out, ln_g, ln_b, alpha, tm):
    n, d = x2.shape
    dq = MLA_NOPE + MLA_ROPE
    w_kr = jnp.zeros((d, LANES), w_in.dtype).at[:, MLA_NOPE:dq].set(w_in[:, MLA_Q_LORA + MLA_KV_LORA:])
    w_in_p = jnp.concatenate([w_in[:, :MLA_Q_LORA + MLA_KV_LORA], w_kr], axis=1).astype(BF16)
    wuq = jnp.pad(w_uq.reshape(MLA_Q_LORA, MLA_HEADS, dq), ((0, 0), (0, 0), (0, LANES - dq)))
    wuq = wuq.reshape(MLA_Q_LORA, MLA_HEADS * LANES).astype(BF16)
    wukv = w_ukv.reshape(MLA_KV_LORA, MLA_HEADS, MLA_NOPE + MLA_V)
    wuk = jnp.pad(wukv[:, :, :MLA_NOPE], ((0, 0), (0, 0), (0, LANES - MLA_NOPE)))
    wuk = wuk.reshape(MLA_KV_LORA, MLA_HEADS * LANES).astype(BF16)
    wuv = wukv[:, :, MLA_NOPE:].reshape(MLA_KV_LORA, MLA_HEADS * MLA_V).astype(BF16)
    cos, sin = _rope_tables(jnp.arange(seq), MLA_ROPE, MLA_NOPE, LANES)
    q, k, v = _mla_proj(x2, w_in_p, q_norm.reshape(1, -1), kv_norm.reshape(1, -1), wuq, wuk, wuv, cos, sin, seq, tm)
    o = _mla_attention(q, k, v, batch, seq, min(512, seq))
    return _oproj_ln([o], [w_out.astype(BF16)], x2, ln_g.reshape(1, d), ln_b.reshape(1, d), alpha, tm)


def kernel(x, swa_nsa_w_in, swa_sinks, nsa_cmp_pe_k, nsa_cmp_k_w1, nsa_cmp_k_w2, nsa_cmp_pe_v, nsa_cmp_v_w1, nsa_cmp_v_w2, swa_nsa_w_out, mla_w_in, mla_q_norm, mla_kv_norm, mla_w_uq, mla_w_ukv, mla_w_out, ln_mix_g, ln_mix_b, ln_ffn_g, ln_ffn_b, router_w, router_bias, expert_w_gate, expert_w_up, expert_w_down, shared_w_gate, shared_w_up, shared_w_down):
    batch, seq, d = x.shape
    depth = ln_mix_g.shape[0]
    alpha = (2 * depth) ** 0.25
    tm = min(512, seq)
    x2 = x.reshape(batch * seq, d)
    for layer in range(depth):
        j = layer // 2
        if layer % 2 == 0:
            x2 = _swa_nsa_layer(x2, batch, seq, swa_nsa_w_in[j], swa_sinks[j], nsa_cmp_pe_k[j],
                                nsa_cmp_k_w1[j], nsa_cmp_k_w2[j], nsa_cmp_pe_v[j], nsa_cmp_v_w1[j],
                                nsa_cmp_v_w2[j], swa_nsa_w_out[j], ln_mix_g[layer], ln_mix_b[layer], alpha, tm)
        else:
            x2 = _mla_layer(x2, batch, seq, mla_w_in[j], mla_q_norm[j], mla_kv_norm[j], mla_w_uq[j],
                            mla_w_ukv[j], mla_w_out[j], ln_mix_g[layer], ln_mix_b[layer], alpha, tm)
        x2 = _moe_layer(x2, router_w[layer], router_bias[layer], expert_w_gate[layer], expert_w_up[layer],
                        expert_w_down[layer], shared_w_gate[layer], shared_w_up[layer], shared_w_down[layer],
                        ln_ffn_g[layer], ln_ffn_b[layer], alpha, tm)
    return x2.reshape(batch, seq, d)
```

```python
import functools
import math

import numpy as np
import jax
import jax.numpy as jnp
from jax import lax
from jax.experimental import pallas as pl
from jax.experimental.pallas import tpu as pltpu
from jax.experimental.pallas import tpu_sc as plsc

F32 = jnp.float32
BF16 = jnp.bfloat16
I32 = jnp.int32

LANES = 128
VMEM_LIMIT = 48 * 1024 * 1024

ROPE_THETA = 10000.0
LN_EPS = 1e-5
RMS_EPS = 1e-6
NEG_BIG = -1e30
FORCE_SCORE = 1e4

HEAD_DIM = 64
SWA_Q_HEADS = 8
SWA_KV_HEADS = 2
SWA_WINDOW = 128
NSA_Q_HEADS = 8
NSA_KV_HEADS = 2
NSA_CMP_LEN = 32
NSA_CMP_STRIDE = 16
NSA_CMP_HIDDEN = 128
NSA_SEL_LEN = 64
NSA_N_SEL = 16
NSA_WINDOW = 512
GQA_REP = 4

MLA_HEADS = 16
MLA_NOPE = 64
MLA_ROPE = 32
MLA_V = 64
MLA_Q_LORA = 384
MLA_KV_LORA = 256

N_EXPERTS = 64
MOE_GROUPS = 8
MOE_TOPK_GROUPS = 4
MOE_TOP_K = 6
D_EXPERT = 256
MOE_ROUTED_SCALE = 2.5
MOE_ROW_BLOCK = 256

NT_DIMS = (((1,), (1,)), ((), ()))


def _params(*sem):
    return pltpu.CompilerParams(dimension_semantics=sem, vmem_limit_bytes=VMEM_LIMIT)


def _full(shape):
    nd = len(shape)
    return pl.BlockSpec(shape, lambda *_: (0,) * nd)


def _rope_slab(y, cos, sin, half):
    lane = lax.broadcasted_iota(I32, y.shape, 1)
    first = (lane % (2 * half)) < half
    rot = jnp.where(first, pltpu.roll(y, LANES - half, 1), pltpu.roll(y, half, 1))
    return y * cos + rot * sin


def _layer_norm_rows(v, g, b):
    mu = jnp.mean(v, axis=-1, keepdims=True)
    vc = v - mu
    var = jnp.mean(vc * vc, axis=-1, keepdims=True)
    return vc * lax.rsqrt(var + LN_EPS) * g + b


def _silu(v):
    return v * (1.0 / (1.0 + jnp.exp(-v)))


def _sigmoid(v):
    return 1.0 / (1.0 + jnp.exp(-v))


def _stack_group_queries(q_ref, group, rep):
    lane = lax.broadcasted_iota(I32, (q_ref.shape[0], LANES), 1)
    keep_lo = jnp.where(lane < HEAD_DIM, 1.0, 0.0).astype(q_ref.dtype)
    keep_hi = jnp.where(lane < HEAD_DIM, 0.0, 1.0).astype(q_ref.dtype)
    parts = []
    for r in range(rep):
        h = group * rep + r
        slab = q_ref[:, (h // 2) * LANES:(h // 2 + 1) * LANES]
        parts.append(slab * (keep_lo if h % 2 == 0 else keep_hi))
    return jnp.concatenate(parts, axis=0)


def _rank_desc(score, n_valid, axis):
    idx = lax.broadcasted_iota(I32, score.shape, axis)
    rank = jnp.zeros(score.shape, I32)
    for j in range(n_valid):
        other = lax.slice_in_dim(score, j, j + 1, axis=axis)
        ahead = jnp.where(other > score, 1, jnp.where(other == score, jnp.where(idx > j, 1, 0), 0))
        rank = rank + ahead
    return rank


_MIX0_OUTS = (
    ("qa", 512, "rope_q", BF16), ("ka", 256, "rope", BF16), ("va", 256, "plain", BF16),
    ("qb", 512, "rope_q", BF16), ("kc", 128, "plain", BF16), ("vc", 128, "plain", BF16),
    ("ks", 256, "rope", BF16), ("vs", 256, "plain", BF16),
    ("kw", 256, "rope", BF16), ("vw", 256, "plain", BF16),
    ("gates", 128, "sigmoid", F32),
)


def _mix0_proj_kernel(x_ref, w_ref, cos_ref, sin_ref, *out_refs):
    xb = x_ref[...].astype(BF16)
    cos = cos_ref[...]
    sin = sin_ref[...]
    off = 0
    for (_, width, kind, _), o_ref in zip(_MIX0_OUTS, out_refs):
        y = jnp.dot(xb, w_ref[:, off:off + width], preferred_element_type=F32)
        for c in range(width // LANES):
            yc = y[:, c * LANES:(c + 1) * LANES]
            if kind in ("rope", "rope_q"):
                yc = _rope_slab(yc, cos, sin, HEAD_DIM // 2)
            if kind == "rope_q":
                yc = yc * (HEAD_DIM ** -0.5)
            if kind == "sigmoid":
                yc = _sigmoid(yc)
            o_ref[:, c * LANES:(c + 1) * LANES] = yc.astype(o_ref.dtype)
        off += width


def _mix0_proj(x2, w_all, cos, sin, seq, tm):
    n = x2.shape[0]
    d = x2.shape[1]
    wtot = w_all.shape[1]
    sblk = seq // tm
    return pl.pallas_call(
        _mix0_proj_kernel,
        grid=(n // tm,),
        in_specs=[pl.BlockSpec((tm, d), lambda i: (i, 0)),
                  _full((d, wtot)),
                  pl.BlockSpec((tm, LANES), lambda i: (i % sblk, 0)),
                  pl.BlockSpec((tm, LANES), lambda i: (i % sblk, 0))],
        out_specs=[pl.BlockSpec((tm, w), lambda i: (i, 0)) for _, w, _, _ in _MIX0_OUTS],
        out_shape=[jax.ShapeDtypeStruct((n, w), dt) for _, w, _, dt in _MIX0_OUTS],
        compiler_params=_params("parallel"),
        name="mix0_proj",
    )(x2, w_all, cos, sin)


def _banded_kernel(*refs, window, blk, seq, has_sink, gate_col):
    if has_sink:
        sink_ref, refs = refs[0], refs[1:]
    q_ref, k_ref, v_ref = refs[:3]
    g_ref = refs[3] if gate_col is not None else None
    o_ref = refs[-1]
    i = pl.program_id(1)
    start = i * blk
    span = min(window + blk, seq)
    k0 = pl.multiple_of(jnp.maximum(start - window, 0), LANES)
    tq = start + lax.broadcasted_iota(I32, (blk, span), 0)
    tk = k0 + lax.broadcasted_iota(I32, (blk, span), 1)
    diff = tq - tk
    bias = jnp.where(diff >= 0, jnp.where(diff < window, 0.0, -jnp.inf), -jnp.inf).astype(F32)
    bias = jnp.concatenate([bias] * GQA_REP, axis=0)
    lane = lax.broadcasted_iota(I32, (blk, LANES), 1)
    lo = lane < HEAD_DIM
    n_groups = q_ref.shape[1] // (GQA_REP * HEAD_DIM)
    for g in range(n_groups):
        k = k_ref[pl.ds(k0, span), g * LANES:(g + 1) * LANES]
        v = v_ref[pl.ds(k0, span), g * LANES:(g + 1) * LANES]
        q = _stack_group_queries(q_ref, g, GQA_REP)
        s = lax.dot_general(q, k, NT_DIMS, preferred_element_type=F32) + bias
        m = jnp.max(s, axis=-1, keepdims=True)
        if has_sink:
            sink = jnp.concatenate(
                [jnp.full((blk, 1), sink_ref[g * GQA_REP + r], F32) for r in range(GQA_REP)], axis=0)
            m = jnp.maximum(m, sink)
        p = jnp.exp(s - m)
        l = jnp.sum(p, axis=-1, keepdims=True)
        if has_sink:
            l = l + jnp.exp(sink - m)
        o = jnp.dot(p.astype(BF16), v, preferred_element_type=F32) / l
        for j in range(GQA_REP // 2):
            slab = jnp.where(lo, o[(2 * j) * blk:(2 * j + 1) * blk], o[(2 * j + 1) * blk:(2 * j + 2) * blk])
            if gate_col is not None:
                c = gate_col + g * GQA_REP + 2 * j
                gate = jnp.where(lo, g_ref[:, c:c + 1], g_ref[:, c + 1:c + 2])
                slab = slab * gate
            col = (g * (GQA_REP // 2) + j) * LANES
            o_ref[:, col:col + LANES] = slab.astype(o_ref.dtype)


def _banded_attention(q, k, v, batch, seq, window, blk, sinks=None, gates=None, gate_col=None,
                      out_dtype=None, name="banded"):
    n, qw = q.shape
    kw = k.shape[1]
    nblk = seq // blk
    kern = functools.partial(_banded_kernel, window=window, blk=blk, seq=seq,
                             has_sink=sinks is not None, gate_col=gate_col)
    in_specs = []
    args = []
    if sinks is not None:
        in_specs.append(pl.BlockSpec(memory_space=pltpu.SMEM))
        args.append(sinks)
    in_specs += [pl.BlockSpec((blk, qw), lambda b, i: (b * nblk + i, 0)),
                 pl.BlockSpec((seq, kw), lambda b, i: (b, 0)),
                 pl.BlockSpec((seq, kw), lambda b, i: (b, 0))]
    args += [q, k, v]
    if gates is not None:
        in_specs.append(pl.BlockSpec((blk, LANES), lambda b, i: (b * nblk + i, 0)))
        args.append(gates)
    return pl.pallas_call(
        kern,
        grid=(batch, nblk),
        in_specs=in_specs,
        out_specs=pl.BlockSpec((blk, qw), lambda b, i: (b * nblk + i, 0)),
        out_shape=jax.ShapeDtypeStruct((n, qw), BF16 if out_dtype is None else out_dtype),
        compiler_params=_params("parallel", "arbitrary"),
        name=name,
    )(*args)


def _gelu_tanh(v):
    return 0.5 * v * (1.0 + jnp.tanh(math.sqrt(2.0 / math.pi) * (v + 0.044715 * (v * v * v))))


def _compress_kernel(tk_ref, tv_ref, w1k_ref, w1v_ref, pek_ref, pev_ref, w1ko_ref, w1vo_ref,
                     w2k_ref, w2v_ref, cos_ref, sin_ref, kc_ref, vc_ref):
    n_chunk = tk_ref.shape[0]
    hid = NSA_CMP_HIDDEN
    for t_ref, w1_ref, pe_ref, w1o_ref, w2_ref, o_ref, rope in (
            (tk_ref, w1k_ref, pek_ref, w1ko_ref, w2k_ref, kc_ref, True),
            (tv_ref, w1v_ref, pev_ref, w1vo_ref, w2v_ref, vc_ref, False)):
        pe_term = jnp.dot(pe_ref[...], w1o_ref[...], preferred_element_type=F32)[0:1, :]
        t = t_ref[...]
        for g in range(NSA_KV_HEADS):
            uv = jnp.dot(t, w1_ref[g], preferred_element_type=F32)
            nxt = pltpu.roll(uv[:, hid:], n_chunk - 1, 0)
            h = _gelu_tanh(uv[:, :hid] + nxt + pe_term)
            c = jnp.dot(h.astype(BF16), w2_ref[...], preferred_element_type=F32)
            if rope:
                c = _rope_slab(c, cos_ref[...], sin_ref[...], HEAD_DIM // 2)
            o_ref[0, g] = c.astype(o_ref.dtype)


def _compress(tk, tv, w1k, w1v, pek, pev, w1ko, w1vo, w2k, w2v, cos, sin, batch, n_chunk):
    width = tk.shape[1]
    out = jax.ShapeDtypeStruct((batch, NSA_KV_HEADS, n_chunk, LANES), BF16)
    ospec = pl.BlockSpec((1, NSA_KV_HEADS, n_chunk, LANES), lambda b: (b, 0, 0, 0))
    return pl.pallas_call(
        _compress_kernel,
        grid=(batch,),
        in_specs=[pl.BlockSpec((n_chunk, width), lambda b: (b, 0)),
                  pl.BlockSpec((n_chunk, width), lambda b: (b, 0)),
                  _full(w1k.shape), _full(w1v.shape), _full(pek.shape), _full(pev.shape),
                  _full(w1ko.shape), _full(w1vo.shape), _full(w2k.shape), _full(w2v.shape),
                  _full(cos.shape), _full(sin.shape)],
        out_specs=[ospec, ospec],
        out_shape=[out, out],
        compiler_params=_params("parallel"),
        name="nsa_compress",
    )(tk, tv, w1k, w1v, pek, pev, w1ko, w1vo, w2k, w2v, cos, sin)


def _nsa_kernel(q_ref, kc_ref, vc_ref, ks_ref, vs_ref, g_ref, win_ref, ov_ref, e_ref, o_ref,
                *, qblk, seq, n_sb, n_sel, tk_sel):
    grp = pl.program_id(1)
    i = pl.program_id(2)
    start = i * qblk
    n_cmp = kc_ref.shape[2]
    rows = GQA_REP * qblk
    lane = lax.broadcasted_iota(I32, (qblk, LANES), 1)
    lo = lane < HEAD_DIM
    q = _stack_group_queries(q_ref, 0, GQA_REP)
    tq = start + lax.broadcasted_iota(I32, (qblk, 1), 0)

    kc = kc_ref[0, 0]
    vc = vc_ref[0, 0]
    cmp_end = lax.broadcasted_iota(I32, (qblk, n_cmp), 1) * NSA_CMP_STRIDE + (NSA_CMP_LEN - 1)
    vis = jnp.where(cmp_end <= tq, 1.0, 0.0).astype(F32)
    vis4 = jnp.concatenate([vis] * GQA_REP, axis=0)
    s_c = lax.dot_general(q, kc, NT_DIMS, preferred_element_type=F32)
    s_c = jnp.where(vis4 > 0.5, s_c, NEG_BIG)
    p_c = jnp.exp(s_c - jnp.max(s_c, axis=-1, keepdims=True)) * vis4
    p_c = p_c / jnp.maximum(jnp.sum(p_c, axis=-1, keepdims=True), 1e-30)
    o_c = jnp.dot(p_c.astype(BF16), vc, preferred_element_type=F32)

    p_sum = p_c[0:qblk]
    for r in range(1, GQA_REP):
        p_sum = p_sum + p_c[r * qblk:(r + 1) * qblk]
    imp = jnp.dot(p_sum, ov_ref[...], preferred_element_type=F32, precision=lax.Precision.HIGHEST)
    cur = tq // NSA_SEL_LEN
    forced = jnp.where(lane == 0, 1, jnp.where(lane == cur, 1, jnp.where(lane == cur - 1, 1, 0)))
    score = jnp.where(forced == 1, FORCE_SCORE, jnp.where(lane * NSA_SEL_LEN > tq, -1.0, imp))
    score = jnp.where(lane < n_sb, score, -2.0)
    sel = jnp.where(_rank_desc(score, n_sb, 1) < n_sel, 1.0, 0.0).astype(BF16)

    def body(j, carry):
        m, l, acc = carry
        k0 = pl.multiple_of(j * tk_sel, tk_sel)
        k = ks_ref[pl.ds(k0, tk_sel), :]
        v = vs_ref[pl.ds(k0, tk_sel), :]
        picked = jnp.dot(sel, e_ref[:, pl.ds(k0, tk_sel)], preferred_element_type=F32)
        tk = k0 + lax.broadcasted_iota(I32, (qblk, tk_sel), 1)
        bias = jnp.where(picked > 0.5, jnp.where(tk <= tq, 0.0, -jnp.inf), -jnp.inf).astype(F32)
        bias = jnp.concatenate([bias] * GQA_REP, axis=0)
        s = lax.dot_general(q, k, NT_DIMS, preferred_element_type=F32) + bias
        m_new = jnp.maximum(m, jnp.max(s, axis=-1, keepdims=True))
        a = jnp.exp(m - m_new)
        p = jnp.exp(s - m_new)
        l = a * l + jnp.sum(p, axis=-1, keepdims=True)
        acc = a * acc + jnp.dot(p.astype(BF16), v, preferred_element_type=F32)
        return m_new, l, acc

    n_tiles = (start + qblk + tk_sel - 1) // tk_sel
    init = (jnp.full((rows, 1), -jnp.inf, F32), jnp.zeros((rows, 1), F32), jnp.zeros((rows, LANES), F32))
    _, l_s, acc_s = lax.fori_loop(0, n_tiles, body, init)
    o_s = acc_s / l_s

    nh = NSA_Q_HEADS
    for j in range(GQA_REP // 2):
        h_even = grp * GQA_REP + 2 * j

        def gate(branch, h0=h_even):
            col = branch * nh + h0
            ids = lax.broadcasted_iota(I32, (qblk, LANES), 1)
            g_all = g_ref[...]
            ge = jnp.sum(jnp.where(ids == col, g_all, 0.0), axis=-1, keepdims=True)
            go = jnp.sum(jnp.where(ids == col + 1, g_all, 0.0), axis=-1, keepdims=True)
            return jnp.where(lo, ge, go)

        rows_e = slice((2 * j) * qblk, (2 * j + 1) * qblk)
        rows_o = slice((2 * j + 1) * qblk, (2 * j + 2) * qblk)
        oc = jnp.where(lo, o_c[rows_e], o_c[rows_o])
        os_ = jnp.where(lo, o_s[rows_e], o_s[rows_o])
        out = gate(0) * oc + gate(1) * os_ + win_ref[:, j * LANES:(j + 1) * LANES]
        o_ref[:, j * LANES:(j + 1) * LANES] = out.astype(o_ref.dtype)


def _nsa_cmp_sel(qb, kcmp, vcmp, ks, vs, gates, win, overlap, expand, batch, seq, qblk, n_sb, n_sel, tk_sel):
    n = qb.shape[0]
    nq = seq // qblk
    n_cmp = kcmp.shape[2]
    gw = GQA_REP * HEAD_DIM
    kern = functools.partial(_nsa_kernel, qblk=qblk, seq=seq, n_sb=n_sb, n_sel=n_sel, tk_sel=tk_sel)
    return pl.pallas_call(
        kern,
        grid=(batch, NSA_KV_HEADS, nq),
        in_specs=[pl.BlockSpec((qblk, gw), lambda b, g, i: (b * nq + i, g)),
                  pl.BlockSpec((1, 1, n_cmp, LANES), lambda b, g, i: (b, g, 0, 0)),
                  pl.BlockSpec((1, 1, n_cmp, LANES), lambda b, g, i: (b, g, 0, 0)),
                  pl.BlockSpec((seq, LANES), lambda b, g, i: (b, g)),
                  pl.BlockSpec((seq, LANES), lambda b, g, i: (b, g)),
                  pl.BlockSpec((qblk, LANES), lambda b, g, i: (b * nq + i, 0)),
                  pl.BlockSpec((qblk, gw), lambda b, g, i: (b * nq + i, g)),
                  _full(overlap.shape), _full(expand.shape)],
        out_specs=pl.BlockSpec((qblk, gw), lambda b, g, i: (b * nq + i, g)),
        out_shape=jax.ShapeDtypeStruct((n, NSA_Q_HEADS * HEAD_DIM), BF16),
        compiler_params=_params("parallel", "parallel", "arbitrary"),
        name="nsa_cmp_sel",
    )(qb, kcmp, vcmp, ks, vs, gates, win, overlap, expand)


def _oproj_ln_kernel(*refs, n_parts, alpha):
    o_refs = refs[:n_parts]
    w_refs = refs[n_parts:2 * n_parts]
    x_ref, g_ref, b_ref, y_ref = refs[2 * n_parts:]
    h = jnp.dot(o_refs[0][...], w_refs[0][...], preferred_element_type=F32)
    for o_ref, w_ref in zip(o_refs[1:], w_refs[1:]):
        h = h + jnp.dot(o_ref[...], w_ref[...], preferred_element_type=F32)
    y_ref[...] = _layer_norm_rows(alpha * x_ref[...] + h, g_ref[...], b_ref[...])


def _oproj_ln(o_parts, w_parts, x2, g, b, alpha, tm):
    n, d = x2.shape
    kern = functools.partial(_oproj_ln_kernel, n_parts=len(o_parts), alpha=alpha)
    in_specs = ([pl.BlockSpec((tm, o.shape[1]), lambda i: (i, 0)) for o in o_parts]
                + [_full(w.shape) for w in w_parts]
                + [pl.BlockSpec((tm, d), lambda i: (i, 0)), _full((1, d)), _full((1, d))])
    return pl.pallas_call(
        kern,
        grid=(n // tm,),
        in_specs=in_specs,
        out_specs=pl.BlockSpec((tm, d), lambda i: (i, 0)),
        out_shape=jax.ShapeDtypeStruct((n, d), F32),
        compiler_params=_params("parallel"),
        name="oproj_ln",
    )(*o_parts, *w_parts, x2, g, b)


def _rms_rows(v, g):
    return v * lax.rsqrt(jnp.mean(v * v, axis=-1, keepdims=True) + RMS_EPS) * g


def _mla_proj_kernel(x_ref, win_ref, qn_ref, kvn_ref, wuq_ref, wuk_ref, wuv_ref, cos_ref, sin_ref,
                     q_ref, k_ref, v_ref):
    xb = x_ref[...].astype(BF16)
    cos = cos_ref[...]
    sin = sin_ref[...]
    lat = jnp.dot(xb, win_ref[...], preferred_element_type=F32)
    cq = _rms_rows(lat[:, :MLA_Q_LORA], qn_ref[...]).astype(BF16)
    ckv = _rms_rows(lat[:, MLA_Q_LORA:MLA_Q_LORA + MLA_KV_LORA], kvn_ref[...]).astype(BF16)
    kr = _rope_slab(lat[:, MLA_Q_LORA + MLA_KV_LORA:], cos, sin, MLA_ROPE // 2)
    q = jnp.dot(cq, wuq_ref[...], preferred_element_type=F32)
    k = jnp.dot(ckv, wuk_ref[...], preferred_element_type=F32)
    for h in range(MLA_HEADS):
        sl = slice(h * LANES, (h + 1) * LANES)
        q_ref[:, sl] = _rope_slab(q[:, sl], cos, sin, MLA_ROPE // 2).astype(BF16)
        k_ref[:, sl] = (k[:, sl] + kr).astype(BF16)
    v_ref[...] = jnp.dot(ckv, wuv_ref[...], preferred_element_type=F32).astype(BF16)


def _mla_proj(x2, w_in, qn, kvn, wuq, wuk, wuv, cos, sin, seq, tm):
    n, d = x2.shape
    sblk = seq // tm
    hw = MLA_HEADS * LANES
    vw = MLA_HEADS * MLA_V
    return pl.pallas_call(
        _mla_proj_kernel,
        grid=(n // tm,),
        in_specs=[pl.BlockSpec((tm, d), lambda i: (i, 0)),
                  _full(w_in.shape), _full(qn.shape), _full(kvn.shape),
                  _full(wuq.shape), _full(wuk.shape), _full(wuv.shape),
                  pl.BlockSpec((tm, LANES), lambda i: (i % sblk, 0)),
                  pl.BlockSpec((tm, LANES), lambda i: (i % sblk, 0))],
        out_specs=[pl.BlockSpec((tm, hw), lambda i: (i, 0)),
                   pl.BlockSpec((tm, hw), lambda i: (i, 0)),
                   pl.BlockSpec((tm, vw), lambda i: (i, 0))],
        out_shape=[jax.ShapeDtypeStruct((n, hw), BF16), jax.ShapeDtypeStruct((n, hw), BF16),
                   jax.ShapeDtypeStruct((n, vw), BF16)],
        compiler_params=_params("parallel"),
        name="mla_proj",
    )(x2, w_in, qn, kvn, wuq, wuk, wuv, cos, sin)


def _mla_attn_kernel(q_ref, k_ref, v_ref, o_ref, *, tq, scale):
    i = pl.program_id(2)
    q_start = i * tq
    lane = lax.broadcasted_iota(I32, (tq, LANES), 1)
    lo = lane < MLA_V
    qs = [q_ref[:, e * LANES:(e + 1) * LANES] for e in range(2)]

    def body(j, carry):
        k0 = pl.multiple_of(j * tq, tq)
        v = v_ref[pl.ds(k0, tq), :]
        t_q = q_start + lax.broadcasted_iota(I32, (tq, tq), 0)
        t_k = k0 + lax.broadcasted_iota(I32, (tq, tq), 1)
        bias = jnp.where(t_k <= t_q, 0.0, -jnp.inf).astype(F32)
        out = []
        for e in range(2):
            m, l, acc = carry[e]
            k = k_ref[pl.ds(k0, tq), e * LANES:(e + 1) * LANES]
            s = lax.dot_general(qs[e], k, NT_DIMS, preferred_element_type=F32) * scale + bias
            m_new = jnp.maximum(m, jnp.max(s, axis=-1, keepdims=True))
            a = jnp.exp(m - m_new)
            p = jnp.exp(s - m_new)
            l = a * l + jnp.sum(p, axis=-1, keepdims=True)
            acc = a * acc + jnp.dot(p.astype(BF16), v, preferred_element_type=F32)
            out.append((m_new, l, acc))
        return tuple(out)

    one = (jnp.full((tq, 1), -jnp.inf, F32), jnp.zeros((tq, 1), F32), jnp.zeros((tq, LANES), F32))
    res = lax.fori_loop(0, i + 1, body, (one, one))
    o0 = res[0][2] / res[0][1]
    o1 = res[1][2] / res[1][1]
    o_ref[...] = jnp.where(lo, o0, o1).astype(o_ref.dtype)


def _mla_attention(q, k, v, batch, seq, tq):
    n = q.shape[0]
    nq = seq // tq
    kern = functools.partial(_mla_attn_kernel, tq=tq, scale=(MLA_NOPE + MLA_ROPE) ** -0.5)
    return pl.pallas_call(
        kern,
        grid=(batch, MLA_HEADS // 2, nq),
        in_specs=[pl.BlockSpec((tq, 2 * LANES), lambda b, p, i: (b * nq + i, p)),
                  pl.BlockSpec((seq, 2 * LANES), lambda b, p, i: (b, p)),
                  pl.BlockSpec((seq, LANES), lambda b, p, i: (b, p))],
        out_specs=pl.BlockSpec((tq, LANES), lambda b, p, i: (b * nq + i, p)),
        out_shape=jax.ShapeDtypeStruct((n, MLA_HEADS * MLA_V), BF16),
        compiler_params=_params("parallel", "parallel", "arbitrary"),
        name="mla_attn",
    )(q, k, v)


def _router_kernel(x_ref, rw_ref, rb_ref, eidx_ref, gate_ref, pos_ref, cnt_ref, carry_ref, *, steps_per_chunk):
    i = pl.program_id(0)
    tm = x_ref.shape[0]
    per_group = N_EXPERTS // MOE_GROUPS

    @pl.when(i % steps_per_chunk == 0)
    def _():
        carry_ref[...] = jnp.zeros_like(carry_ref)

    logits = jnp.dot(x_ref[...], rw_ref[...], preferred_element_type=F32, precision=lax.Precision.HIGHEST)
    lt = jnp.transpose(logits)[:N_EXPERTS]
    s = _sigmoid(lt)
    sb = s + rb_ref[...]

    g3 = sb.reshape(MOE_GROUPS, per_group, tm)
    idx3 = lax.broadcasted_iota(I32, g3.shape, 1).astype(F32)
    m1 = jnp.max(g3, axis=1, keepdims=True)
    first = jnp.min(jnp.where(g3 == m1, idx3, float(per_group)), axis=1, keepdims=True)
    m2 = jnp.max(jnp.where(idx3 == first, -jnp.inf, g3), axis=1, keepdims=True)
    gscore = (m1 + m2).reshape(MOE_GROUPS, tm)
    gsel = _rank_desc(gscore, MOE_GROUPS, 0) < MOE_TOPK_GROUPS
    gsel3 = jnp.where(gsel, 1.0, 0.0).astype(F32).reshape(MOE_GROUPS, 1, tm)
    masked = jnp.where(gsel3 > 0.5, g3, -jnp.inf).reshape(N_EXPERTS, tm)
    rank = _rank_desc(masked, N_EXPERTS, 0)
    sel = rank < MOE_TOP_K
    gate = jnp.where(sel, s, 0.0)
    gate = gate / jnp.sum(gate, axis=0, keepdims=True) * MOE_ROUTED_SCALE

    sel_b = jnp.where(sel, 1.0, 0.0).astype(BF16)
    r_i = lax.broadcasted_iota(I32, (tm, tm), 0)
    c_i = lax.broadcasted_iota(I32, (tm, tm), 1)
    tri = jnp.where(r_i < c_i, 1.0, 0.0).astype(BF16)
    carry = carry_ref[:, 0:1]
    before = jnp.dot(sel_b, tri, preferred_element_type=F32) + carry
    carry_new = carry + jnp.sum(sel_b.astype(F32), axis=1, keepdims=True)
    carry_ref[...] = jnp.broadcast_to(carry_new, carry_ref.shape)
    cnt_ref[0] = jnp.broadcast_to(carry_new, carry_ref.shape).astype(I32)

    e_iota = lax.broadcasted_iota(I32, (N_EXPERTS, tm), 0).astype(F32)
    rows_e, rows_g, rows_p = [], [], []
    for r in range(MOE_TOP_K):
        hit = rank == r
        rows_e.append(jnp.sum(jnp.where(hit, e_iota, 0.0), axis=0, keepdims=True))
        rows_g.append(jnp.sum(jnp.where(hit, gate, 0.0), axis=0, keepdims=True))
        rows_p.append(jnp.sum(jnp.where(hit, before, 0.0), axis=0, keepdims=True))
    pad = 8 - MOE_TOP_K
    eidx_ref[...] = jnp.concatenate(rows_e + [jnp.zeros((pad, tm), F32)], axis=0).astype(I32)
    gate_ref[...] = jnp.concatenate(rows_g + [jnp.zeros((pad, tm), F32)], axis=0)
    pos_ref[...] = jnp.concatenate(rows_p + [jnp.zeros((pad, tm), F32)], axis=0).astype(I32)


def _router(x2, rw, rb, n_chunk, tm):
    n, d = x2.shape
    steps = n // tm
    spc = steps // n_chunk
    kern = functools.partial(_router_kernel, steps_per_chunk=spc)
    row8 = pl.BlockSpec((8, tm), lambda i: (0, i))
    return pl.pallas_call(
        kern,
        grid=(steps,),
        in_specs=[pl.BlockSpec((tm, d), lambda i: (i, 0)), _full(rw.shape), _full(rb.shape)],
        out_specs=[row8, row8, row8, pl.BlockSpec((1, N_EXPERTS, LANES), lambda i: (i // spc, 0, 0))],
        out_shape=[jax.ShapeDtypeStruct((8, n), I32), jax.ShapeDtypeStruct((8, n), F32),
                   jax.ShapeDtypeStruct((8, n), I32),
                   jax.ShapeDtypeStruct((n_chunk, N_EXPERTS, LANES), I32)],
        scratch_shapes=[pltpu.VMEM((N_EXPERTS, LANES), F32)],
        compiler_params=_params("arbitrary"),
        name="moe_router",
    )(x2, rw, rb)


def _dest_kernel(tab_ref, eidx_ref, pos_ref, dest_ref, *, steps_per_chunk):
    chunk = pl.program_id(0) // steps_per_chunk
    eidx = eidx_ref[...]
    dest = pos_ref[...]
    for e in range(N_EXPERTS):
        dest = dest + jnp.where(eidx == e, tab_ref[chunk * N_EXPERTS + e], 0)
    dest_ref[...] = dest


def _dest_rows(pad_starts, eidx, pos, n_chunk, tm):
    n = eidx.shape[1]
    steps = n // tm
    row8 = lambda i, tab: (0, i)
    return pl.pallas_call(
        functools.partial(_dest_kernel, steps_per_chunk=steps // n_chunk),
        grid_spec=pltpu.PrefetchScalarGridSpec(
            num_scalar_prefetch=1, grid=(steps,),
            in_specs=[pl.BlockSpec((8, tm), row8), pl.BlockSpec((8, tm), row8)],
            out_specs=pl.BlockSpec((8, tm), row8)),
        out_shape=jax.ShapeDtypeStruct((8, n), I32),
        compiler_params=_params("parallel"),
        name="moe_dest",
    )(pad_starts, eidx, pos)


SC_ROWS = 64
SC_WORKERS = 32


def _sc_mesh():
    return plsc.VectorSubcoreMesh(core_axis_name="c", subcore_axis_name="s")


def _sc_worker_base(per_worker):
    return (lax.axis_index("s") * 2 + lax.axis_index("c")) * per_worker


def _sc_scatter_rows(x, dest, n_out):
    n, d = x.shape
    slots = dest.shape[0] // n
    per_w = n // SC_WORKERS

    def body(x_hbm, i_hbm, o_hbm, idx_v, rows_v, sem):
        base = _sc_worker_base(per_w)

        @pl.loop(0, per_w // SC_ROWS)
        def _(c):
            off = pl.multiple_of(base + c * SC_ROWS, 8)
            pltpu.sync_copy(x_hbm.at[pl.ds(off, SC_ROWS)], rows_v)
            for k in range(slots):
                pltpu.sync_copy(i_hbm.at[pl.ds(k * n + off, SC_ROWS)], idx_v)
                pltpu.async_copy(rows_v, o_hbm.at[idx_v], sem).wait()

    return pl.kernel(
        body, out_type=jax.ShapeDtypeStruct((n_out, d), x.dtype), mesh=_sc_mesh(),
        scratch_types=[pltpu.VMEM((SC_ROWS,), I32), pltpu.VMEM((SC_ROWS, d), x.dtype), pltpu.SemaphoreType.DMA],
        name="moe_dispatch_sc")(x, dest)


def _sc_gather_rows(table, idx):
    d = table.shape[1]
    r = idx.shape[0]
    per_w = r // SC_WORKERS

    def body(t_hbm, i_hbm, o_hbm, idx_v, rows_v, sem):
        base = _sc_worker_base(per_w)

        @pl.loop(0, per_w // SC_ROWS)
        def _(c):
            off = pl.multiple_of(base + c * SC_ROWS, 8)
            pltpu.sync_copy(i_hbm.at[pl.ds(off, SC_ROWS)], idx_v)
            pltpu.async_copy(t_hbm.at[idx_v], rows_v, sem).wait()
            pltpu.sync_copy(rows_v, o_hbm.at[pl.ds(off, SC_ROWS)])

    return pl.kernel(
        body, out_type=jax.ShapeDtypeStruct((r, d), table.dtype), mesh=_sc_mesh(),
        scratch_types=[pltpu.VMEM((SC_ROWS,), I32), pltpu.VMEM((SC_ROWS, d), table.dtype), pltpu.SemaphoreType.DMA],
        name="moe_combine_sc")(table, idx)


def _experts_kernel(be_ref, nv_ref, xs_ref, wgu_ref, wd_ref, y_ref):
    i = pl.program_id(0)

    @pl.when(i < nv_ref[0])
    def _():
        gu = jnp.dot(xs_ref[...].astype(BF16), wgu_ref[0], preferred_element_type=F32)
        h = _silu(gu[:, :D_EXPERT]) * gu[:, D_EXPERT:]
        y_ref[...] = jnp.dot(h.astype(BF16), wd_ref[0], preferred_element_type=F32).astype(y_ref.dtype)

    @pl.when(i >= nv_ref[0])
    def _():
        y_ref[...] = jnp.zeros_like(y_ref)


def _experts(blk_exp, n_valid, xs, wgu, wd, tb):
    rows, d = xs.shape
    grid_spec = pltpu.PrefetchScalarGridSpec(
        num_scalar_prefetch=2,
        grid=(rows // tb,),
        in_specs=[pl.BlockSpec((tb, d), lambda i, be, nv: (i, 0)),
                  pl.BlockSpec((1, d, 2 * D_EXPERT), lambda i, be, nv: (be[i], 0, 0)),
                  pl.BlockSpec((1, D_EXPERT, d), lambda i, be, nv: (be[i], 0, 0))],
        out_specs=pl.BlockSpec((tb, d), lambda i, be, nv: (i, 0)),
    )
    return pl.pallas_call(
        _experts_kernel,
        grid_spec=grid_spec,
        out_shape=jax.ShapeDtypeStruct((rows, d), F32),
        compiler_params=_params("arbitrary"),
        name="moe_experts",
    )(blk_exp, n_valid, xs, wgu, wd)


def _combine_ln_kernel(x_ref, yg_ref, gt_ref, wgu_ref, wd_ref, g_ref, b_ref, o_ref, *, alpha):
    x = x_ref[...]
    gu = jnp.dot(x.astype(BF16), wgu_ref[...], preferred_element_type=F32)
    dsh = gu.shape[1] // 2
    h = _silu(gu[:, :dsh]) * gu[:, dsh:]
    f = jnp.dot(h.astype(BF16), wd_ref[...], preferred_element_type=F32)
    for k in range(MOE_TOP_K):
        f = f + gt_ref[:, k:k + 1] * yg_ref[k].astype(F32)
    o_ref[...] = _layer_norm_rows(alpha * x + f, g_ref[...], b_ref[...])


def _combine_ln(x2, yg, gates_t, wgu, wd, g, b, alpha, tm):
    n, d = x2.shape
    kern = functools.partial(_combine_ln_kernel, alpha=alpha)
    return pl.pallas_call(
        kern,
        grid=(n // tm,),
        in_specs=[pl.BlockSpec((tm, d), lambda i: (i, 0)),
                  pl.BlockSpec((MOE_TOP_K, tm, d), lambda i: (0, i, 0)),
                  pl.BlockSpec((tm, 8), lambda i: (i, 0)),
                  _full(wgu.shape), _full(wd.shape), _full((1, d)), _full((1, d))],
        out_specs=pl.BlockSpec((tm, d), lambda i: (i, 0)),
        out_shape=jax.ShapeDtypeStruct((n, d), F32),
        compiler_params=_params("parallel"),
        name="moe_combine_ln",
    )(x2, yg, gates_t, wgu, wd, g, b)


def _rope_tables(positions, dim, lane_offset, period):
    half = dim // 2
    inv_freq = ROPE_THETA ** (-2.0 * jnp.arange(half, dtype=jnp.float32) / dim)
    ang = positions.astype(jnp.float32)[:, None] * inv_freq[None, :]
    cos_h, sin_h = jnp.cos(ang), jnp.sin(ang)
    lanes = np.arange(LANES)
    rel = (lanes - lane_offset) % period
    active = (lanes >= lane_offset) & (rel < dim)
    fidx = np.where(active, rel % half, 0)
    sign = np.where(rel < half, -1.0, 1.0)
    cos = jnp.where(active[None, :], cos_h[:, fidx], 1.0)
    sin = jnp.where(active[None, :], sin_h[:, fidx] * sign[None, :], 0.0)
    return cos.astype(F32), sin.astype(F32)


def _dup_heads(w):
    a, b = w[:, :HEAD_DIM], w[:, HEAD_DIM:]
    return jnp.concatenate([a, a, b, b], axis=1)


def _moe_layer(x1, rw, rb, wg, wu, wd, shg, shu, shd, ln_g, ln_b, alpha, tm):
    n, d = x1.shape
    n_chunk = 1
    tb = MOE_ROW_BLOCK
    rw_p = jnp.pad(rw, ((0, 0), (0, LANES - N_EXPERTS)))
    eidx, gate, pos, cnt = _router(x1, rw_p, rb.reshape(N_EXPERTS, 1), n_chunk, tm)
    counts = cnt[:, :, 0].reshape(-1)
    padded = (counts + tb - 1) // tb * tb
    pad_ends = jnp.cumsum(padded)
    pad_starts = (pad_ends - padded).astype(I32)
    dest = _dest_rows(pad_starts, eidx, pos, n_chunk, tm)[:MOE_TOP_K].reshape(-1)
    n_rows = n * MOE_TOP_K
    n_blk = -(-(n_rows + n_chunk * N_EXPERTS * (tb - 1)) // tb)
    blk_first_row = jnp.arange(n_blk, dtype=I32) * tb
    owner = jnp.sum((pad_ends[None, :] <= blk_first_row[:, None]).astype(I32), axis=1)
    blk_exp = jnp.minimum(owner, n_chunk * N_EXPERTS - 1).astype(I32) % N_EXPERTS
    n_valid = (pad_ends[-1] // tb).astype(I32).reshape(1)

    xs = _sc_scatter_rows(x1, dest, n_blk * tb)
    wgu = jnp.concatenate([wg, wu], axis=-1).astype(BF16)
    y = _experts(blk_exp, n_valid, xs, wgu, wd.astype(BF16), tb)
    yg = _sc_gather_rows(y, dest).reshape(MOE_TOP_K, n, d)
    gates_t = gate.T
    sh_gu = jnp.concatenate([shg, shu], axis=-1).astype(BF16)
    return _combine_ln(x1, yg, gates_t, sh_gu, shd.astype(BF16), ln_g.reshape(1, d), ln_b.reshape(1, d), alpha, tm)


def _swa_nsa_layer(x2, batch, seq, w_in, sinks, pe_k, w_ck1, w_ck2, pe_v, w_cv1, w_cv2, w_out,
                   ln_g, ln_b, alpha, tm):
    n, d = x2.shape
    splits = np.cumsum([512, 128, 128, 512, 128, 128, 128, 128, 128, 128, 24])[:-1]
    qa, ka, va, qb, kc, vc, ksl, vsl, kw, vw, gts = jnp.split(w_in, [int(c) for c in splits], axis=1)
    w_all = jnp.concatenate(
        [qa, _dup_heads(ka), _dup_heads(va), qb, kc, vc, _dup_heads(ksl), _dup_heads(vsl),
         _dup_heads(kw), _dup_heads(vw), jnp.pad(gts, ((0, 0), (0, LANES - gts.shape[1])))], axis=1).astype(BF16)
    cos, sin = _rope_tables(jnp.arange(seq), HEAD_DIM, 0, HEAD_DIM)
    (q_a, k_a, v_a, q_b, k_c, v_c, k_s, v_s, k_w, v_w, gates) = _mix0_proj(x2, w_all, cos, sin, seq, tm)

    o_swa = _banded_attention(q_a, k_a, v_a, batch, seq, SWA_WINDOW, 128, sinks=sinks, name="swa")
    o_win = _banded_attention(q_b, k_w, v_w, batch, seq, NSA_WINDOW, 128, gates=gates,
                              gate_col=2 * NSA_Q_HEADS, out_dtype=F32, name="nsa_win")

    n_chunk = seq // NSA_CMP_STRIDE
    cw = NSA_CMP_STRIDE * NSA_KV_HEADS * HEAD_DIM
    hid = NSA_CMP_HIDDEN

    def expand_w1(w1):
        halves = w1.reshape(2, NSA_CMP_STRIDE, HEAD_DIM, hid)
        out = jnp.zeros((NSA_KV_HEADS, NSA_CMP_STRIDE, NSA_KV_HEADS, HEAD_DIM, 2 * hid), w1.dtype)
        for g in range(NSA_KV_HEADS):
            out = out.at[g, :, g, :, :hid].set(halves[0]).at[g, :, g, :, hid:].set(halves[1])
        return out.reshape(NSA_KV_HEADS, cw, 2 * hid).astype(BF16)

    def pe_rows(pe):
        return jnp.pad(pe.reshape(1, NSA_CMP_LEN * HEAD_DIM), ((0, 15), (0, 0))).astype(BF16)

    cos_c, sin_c = _rope_tables(jnp.arange(n_chunk) * NSA_CMP_STRIDE + NSA_CMP_LEN - 1, HEAD_DIM, 0, HEAD_DIM)
    k_cmp, v_cmp = _compress(
        k_c.reshape(batch * n_chunk, cw), v_c.reshape(batch * n_chunk, cw),
        expand_w1(w_ck1), expand_w1(w_cv1), pe_rows(pe_k), pe_rows(pe_v),
        w_ck1.astype(BF16), w_cv1.astype(BF16),
        jnp.concatenate([w_ck2, w_ck2], axis=1).astype(BF16),
        jnp.concatenate([w_cv2, w_cv2], axis=1).astype(BF16),
        cos_c, sin_c, batch, n_chunk)

    n_sb = seq // NSA_SEL_LEN
    n_sel = min(NSA_N_SEL, n_sb)
    cs = np.arange(n_chunk) * NSA_CMP_STRIDE
    bs = np.arange(n_sb) * NSA_SEL_LEN
    ov = np.clip(np.minimum(cs[:, None] + NSA_CMP_LEN, bs[None, :] + NSA_SEL_LEN)
                 - np.maximum(cs[:, None], bs[None, :]), 0, None) / NSA_CMP_LEN
    overlap = jnp.asarray(np.pad(ov, ((0, 0), (0, LANES - n_sb))), F32)
    expand = jnp.asarray(np.arange(LANES)[:, None] == (np.arange(seq)[None, :] // NSA_SEL_LEN), BF16)
    tk_sel = min(512, seq)
    o_nsa = _nsa_cmp_sel(q_b, k_cmp, v_cmp, k_s, v_s, gates, o_win, overlap, expand,
                         batch, seq, 128, n_sb, n_sel, tk_sel)
    w_o = w_out.astype(BF16)
    half = SWA_Q_HEADS * HEAD_DIM
    return _oproj_ln([o_swa, o_nsa], [w_o[:half], w_o[half:]], x2, ln_g.reshape(1, d), ln_b.reshape(1, d),
                     alpha, tm)


def _mla_layer(x2, batch, seq, w_in, q_norm, kv_norm, w_uq, w_ukv, w_out, ln_g, ln_b, alpha, tm):
    n, d = x2.shape
    dq = MLA_NOPE + MLA_ROPE
    w_kr = jnp.zeros((d, LANES), w_in.dtype).at[:, MLA_NOPE:dq].set(w_in[:, MLA_Q_LORA + MLA_KV_LORA:])
    w_in_p = jnp.concatenate([w_in[:, :MLA_Q_LORA + MLA_KV_LORA], w_kr], axis=1).astype(BF16)
    wuq = jnp.pad(w_uq.reshape(MLA_Q_LORA, MLA_HEADS, dq), ((0, 0), (0, 0), (0, LANES - dq)))
    wuq = wuq.reshape(MLA_Q_LORA, MLA_HEADS * LANES).astype(BF16)
    wukv = w_ukv.reshape(MLA_KV_LORA, MLA_HEADS, MLA_NOPE + MLA_V)
    wuk = jnp.pad(wukv[:, :, :MLA_NOPE], ((0, 0), (0, 0), (0, LANES - MLA_NOPE)))
    wuk = wuk.reshape(MLA_KV_LORA, MLA_HEADS * LANES).astype(BF16)
    wuv = wukv[:, :, MLA_NOPE:].reshape(MLA_KV_LORA, MLA_HEADS * MLA_V).astype(BF16)
    cos, sin = _rope_tables(jnp.arange(seq), MLA_ROPE, MLA_NOPE, LANES)
    q, k, v = _mla_proj(x2, w_in_p, q_norm.reshape(1, -1), kv_norm.reshape(1, -1), wuq, wuk, wuv, cos, sin, seq, tm)
    o = _mla_attention(q, k, v, batch, seq, min(512, seq))
    return _oproj_ln([o], [w_out.astype(BF16)], x2, ln_g.reshape(1, d), ln_b.reshape(1, d), alpha, tm)


def kernel(x, swa_nsa_w_in, swa_sinks, nsa_cmp_pe_k, nsa_cmp_k_w1, nsa_cmp_k_w2, nsa_cmp_pe_v, nsa_cmp_v_w1, nsa_cmp_v_w2, swa_nsa_w_out, mla_w_in, mla_q_norm, mla_kv_norm, mla_w_uq, mla_w_ukv, mla_w_out, ln_mix_g, ln_mix_b, ln_ffn_g, ln_ffn_b, router_w, router_bias, expert_w_gate, expert_w_up, expert_w_down, shared_w_gate, shared_w_up, shared_w_down):
    batch, seq, d = x.shape
    depth = ln_mix_g.shape[0]
    alpha = (2 * depth) ** 0.25
    tm = min(512, seq)
    x2 = x.reshape(batch * seq, d)
    for layer in range(depth):
        j = layer // 2
        if layer % 2 == 0:
            x2 = _swa_nsa_layer(x2, batch, seq, swa_nsa_w_in[j], swa_sinks[j], nsa_cmp_pe_k[j],
                                nsa_cmp_k_w1[j], nsa_cmp_k_w2[j], nsa_cmp_pe_v[j], nsa_cmp_v_w1[j],
                                nsa_cmp_v_w2[j], swa_nsa_w_out[j], ln_mix_g[layer], ln_mix_b[layer], alpha, tm)
        else:
            x2 = _mla_layer(x2, batch, seq, mla_w_in[j], mla_q_norm[j], mla_kv_norm[j], mla_w_uq[j],
                            mla_w_ukv[j], mla_w_out[j], ln_mix_g[layer], ln_mix_b[layer], alpha, tm)
        x2 = _moe_layer(x2, router_w[layer], router_bias[layer], expert_w_gate[layer], expert_w_up[layer],
                        expert_w_down[layer], shared_w_gate[layer], shared_w_up[layer], shared_w_down[layer],
                        ln_ffn_g[layer], ln_ffn_b[layer], alpha, tm)
    return x2.reshape(batch, seq, d)
```

```python
import functools
import math

import numpy as np
import jax
import jax.numpy as jnp
from jax import lax
from jax.experimental import pallas as pl
from jax.experimental.pallas import tpu as pltpu
from jax.experimental.pallas import tpu_sc as plsc

F32 = jnp.float32
BF16 = jnp.bfloat16
I32 = jnp.int32

LANES = 128
VMEM_LIMIT = 48 * 1024 * 1024

ROPE_THETA = 10000.0
LN_EPS = 1e-5
RMS_EPS = 1e-6
NEG_BIG = -1e30
FORCE_SCORE = 1e4

HEAD_DIM = 64
SWA_Q_HEADS = 8
SWA_KV_HEADS = 2
SWA_WINDOW = 128
NSA_Q_HEADS = 8
NSA_KV_HEADS = 2
NSA_CMP_LEN = 32
NSA_CMP_STRIDE = 16
NSA_CMP_HIDDEN = 128
NSA_SEL_LEN = 64
NSA_N_SEL = 16
NSA_WINDOW = 512
GQA_REP = 4

MLA_HEADS = 16
MLA_NOPE = 64
MLA_ROPE = 32
MLA_V = 64
MLA_Q_LORA = 384
MLA_KV_LORA = 256

N_EXPERTS = 64
MOE_GROUPS = 8
MOE_TOPK_GROUPS = 4
MOE_TOP_K = 6
D_EXPERT = 256
MOE_ROUTED_SCALE = 2.5
MOE_ROW_BLOCK = 256

NT_DIMS = (((1,), (1,)), ((), ()))


def _params(*sem):
    return pltpu.CompilerParams(dimension_semantics=sem, vmem_limit_bytes=VMEM_LIMIT)


def _full(shape):
    nd = len(shape)
    return pl.BlockSpec(shape, lambda *_: (0,) * nd)


def _rope_slab(y, cos, sin, half):
    lane = lax.broadcasted_iota(I32, y.shape, 1)
    first = (lane % (2 * half)) < half
    rot = jnp.where(first, pltpu.roll(y, LANES - half, 1), pltpu.roll(y, half, 1))
    return y * cos + rot * sin


def _layer_norm_rows(v, g, b):
    mu = jnp.mean(v, axis=-1, keepdims=True)
    vc = v - mu
    var = jnp.mean(vc * vc, axis=-1, keepdims=True)
    return vc * lax.rsqrt(var + LN_EPS) * g + b


def _silu(v):
    return v * (1.0 / (1.0 + jnp.exp(-v)))


def _sigmoid(v):
    return 1.0 / (1.0 + jnp.exp(-v))


def _stack_group_queries(q_ref, group, rep):
    lane = lax.broadcasted_iota(I32, (q_ref.shape[0], LANES), 1)
    keep_lo = jnp.where(lane < HEAD_DIM, 1.0, 0.0).astype(q_ref.dtype)
    keep_hi = jnp.where(lane < HEAD_DIM, 0.0, 1.0).astype(q_ref.dtype)
    parts = []
    for r in range(rep):
        h = group * rep + r
        slab = q_ref[:, (h // 2) * LANES:(h // 2 + 1) * LANES]
        parts.append(slab * (keep_lo if h % 2 == 0 else keep_hi))
    return jnp.concatenate(parts, axis=0)


def _rank_desc(score, n_valid, axis):
    idx = lax.broadcasted_iota(I32, score.shape, axis)
    rank = jnp.zeros(score.shape, I32)
    for j in range(n_valid):
        other = lax.slice_in_dim(score, j, j + 1, axis=axis)
        ahead = jnp.where(other > score, 1, jnp.where(other == score, jnp.where(idx > j, 1, 0), 0))
        rank = rank + ahead
    return rank


_MIX0_OUTS = (
    ("qa", 512, "rope_q", BF16), ("ka", 256, "rope", BF16), ("va", 256, "plain", BF16),
    ("qb", 512, "rope_q", BF16), ("kc", 128, "plain", BF16), ("vc", 128, "plain", BF16),
    ("ks", 256, "rope", BF16), ("vs", 256, "plain", BF16),
    ("kw", 256, "rope", BF16), ("vw", 256, "plain", BF16),
    ("gates", 256, "sigmoid", F32),
)


def _mix0_proj_kernel(x_ref, w_ref, cos_ref, sin_ref, *out_refs):
    xb = x_ref[...].astype(BF16)
    cos = cos_ref[...]
    sin = sin_ref[...]
    off = 0
    for (_, width, kind, _), o_ref in zip(_MIX0_OUTS, out_refs):
        y = jnp.dot(xb, w_ref[:, off:off + width], preferred_element_type=F32)
        for c in range(width // LANES):
            yc = y[:, c * LANES:(c + 1) * LANES]
            if kind in ("rope", "rope_q"):
                yc = _rope_slab(yc, cos, sin, HEAD_DIM // 2)
            if kind == "rope_q":
                yc = yc * (HEAD_DIM ** -0.5)
            if kind == "sigmoid":
                yc = _sigmoid(yc)
            o_ref[:, c * LANES:(c + 1) * LANES] = yc.astype(o_ref.dtype)
        off += width


def _mix0_proj(x2, w_all, cos, sin, seq, tm):
    n = x2.shape[0]
    d = x2.shape[1]
    wtot = w_all.shape[1]
    sblk = seq // tm
    return pl.pallas_call(
        _mix0_proj_kernel,
        grid=(n // tm,),
        in_specs=[pl.BlockSpec((tm, d), lambda i: (i, 0)),
                  _full((d, wtot)),
                  pl.BlockSpec((tm, LANES), lambda i: (i % sblk, 0)),
                  pl.BlockSpec((tm, LANES), lambda i: (i % sblk, 0))],
        out_specs=[pl.BlockSpec((tm, w), lambda i: (i, 0)) for _, w, _, _ in _MIX0_OUTS],
        out_shape=[jax.ShapeDtypeStruct((n, w), dt) for _, w, _, dt in _MIX0_OUTS],
        compiler_params=_params("parallel"),
        name="mix0_proj",
    )(x2, w_all, cos, sin)


def _banded_kernel(*refs, window, blk, seq, has_sink, gate_col):
    if has_sink:
        sink_ref, refs = refs[0], refs[1:]
    q_ref, k_ref, v_ref = refs[:3]
    g_ref = refs[3] if gate_col is not None else None
    o_ref = refs[-1]
    i = pl.program_id(1)
    start = i * blk
    span = min(window + blk, seq)
    k0 = pl.multiple_of(jnp.maximum(start - window, 0), LANES)
    tq = start + lax.broadcasted_iota(I32, (blk, span), 0)
    tk = k0 + lax.broadcasted_iota(I32, (blk, span), 1)
    diff = tq - tk
    bias = jnp.where(diff >= 0, jnp.where(diff < window, 0.0, -jnp.inf), -jnp.inf).astype(F32)
    bias = jnp.concatenate([bias] * GQA_REP, axis=0)
    lane = lax.broadcasted_iota(I32, (blk, LANES), 1)
    lo = lane < HEAD_DIM
    n_groups = q_ref.shape[1] // (GQA_REP * HEAD_DIM)
    for g in range(n_groups):
        k = k_ref[pl.ds(k0, span), g * LANES:(g + 1) * LANES]
        v = v_ref[pl.ds(k0, span), g * LANES:(g + 1) * LANES]
        q = _stack_group_queries(q_ref, g, GQA_REP)
        s = lax.dot_general(q, k, NT_DIMS, preferred_element_type=F32) + bias
        m = jnp.max(s, axis=-1, keepdims=True)
        if has_sink:
            sink = jnp.concatenate(
                [jnp.full((blk, 1), sink_ref[g * GQA_REP + r], F32) for r in range(GQA_REP)], axis=0)
            m = jnp.maximum(m, sink)
        p = jnp.exp(s - m)
        l = jnp.sum(p, axis=-1, keepdims=True)
        if has_sink:
            l = l + jnp.exp(sink - m)
        o = jnp.dot(p.astype(BF16), v, preferred_element_type=F32) / l
        for j in range(GQA_REP // 2):
            slab = jnp.where(lo, o[(2 * j) * blk:(2 * j + 1) * blk], o[(2 * j + 1) * blk:(2 * j + 2) * blk])
            if gate_col is not None:
                c = g * LANES + gate_col + 2 * j
                gate = jnp.where(lo, g_ref[:, c:c + 1], g_ref[:, c + 1:c + 2])
                slab = slab * gate
            col = (g * (GQA_REP // 2) + j) * LANES
            o_ref[:, col:col + LANES] = slab.astype(o_ref.dtype)


def _banded_attention(q, k, v, batch, seq, window, blk, sinks=None, gates=None, gate_col=None,
                      out_dtype=None, name="banded"):
    n, qw = q.shape
    kw = k.shape[1]
    nblk = seq // blk
    kern = functools.partial(_banded_kernel, window=window, blk=blk, seq=seq,
                             has_sink=sinks is not None, gate_col=gate_col)
    in_specs = []
    args = []
    if sinks is not None:
        in_specs.append(pl.BlockSpec(memory_space=pltpu.SMEM))
        args.append(sinks)
    in_specs += [pl.BlockSpec((blk, qw), lambda b, i: (b * nblk + i, 0)),
                 pl.BlockSpec((seq, kw), lambda b, i: (b, 0)),
                 pl.BlockSpec((seq, kw), lambda b, i: (b, 0))]
    args += [q, k, v]
    if gates is not None:
        in_specs.append(pl.BlockSpec((blk, gates.shape[1]), lambda b, i: (b * nblk + i, 0)))
        args.append(gates)
    return pl.pallas_call(
        kern,
        grid=(batch, nblk),
        in_specs=in_specs,
        out_specs=pl.BlockSpec((blk, qw), lambda b, i: (b * nblk + i, 0)),
        out_shape=jax.ShapeDtypeStruct((n, qw), BF16 if out_dtype is None else out_dtype),
        compiler_params=_params("parallel", "arbitrary"),
        name=name,
    )(*args)


def _gelu_tanh(v):
    return 0.5 * v * (1.0 + jnp.tanh(math.sqrt(2.0 / math.pi) * (v + 0.044715 * (v * v * v))))


def _compress_kernel(tk_ref, tv_ref, w1k_ref, w1v_ref, pek_ref, pev_ref, w1ko_ref, w1vo_ref,
                     w2k_ref, w2v_ref, cos_ref, sin_ref, kc_ref, vc_ref):
    n_chunk = tk_ref.shape[0]
    hid = NSA_CMP_HIDDEN
    for t_ref, w1_ref, pe_ref, w1o_ref, w2_ref, o_ref, rope in (
            (tk_ref, w1k_ref, pek_ref, w1ko_ref, w2k_ref, kc_ref, True),
            (tv_ref, w1v_ref, pev_ref, w1vo_ref, w2v_ref, vc_ref, False)):
        pe_term = jnp.dot(pe_ref[...], w1o_ref[...], preferred_element_type=F32)[0:1, :]
        t = t_ref[...]
        for g in range(NSA_KV_HEADS):
            uv = jnp.dot(t, w1_ref[g], preferred_element_type=F32)
            nxt = pltpu.roll(uv[:, hid:], n_chunk - 1, 0)
            h = _gelu_tanh(uv[:, :hid] + nxt + pe_term)
            c = jnp.dot(h.astype(BF16), w2_ref[...], preferred_element_type=F32)
            if rope:
                c = _rope_slab(c, cos_ref[...], sin_ref[...], HEAD_DIM // 2)
            o_ref[0, g] = c.astype(o_ref.dtype)


def _compress(tk, tv, w1k, w1v, pek, pev, w1ko, w1vo, w2k, w2v, cos, sin, batch, n_chunk):
    width = tk.shape[1]
    out = jax.ShapeDtypeStruct((batch, NSA_KV_HEADS, n_chunk, LANES), BF16)
    ospec = pl.BlockSpec((1, NSA_KV_HEADS, n_chunk, LANES), lambda b: (b, 0, 0, 0))
    return pl.pallas_call(
        _compress_kernel,
        grid=(batch,),
        in_specs=[pl.BlockSpec((n_chunk, width), lambda b: (b, 0)),
                  pl.BlockSpec((n_chunk, width), lambda b: (b, 0)),
                  _full(w1k.shape), _full(w1v.shape), _full(pek.shape), _full(pev.shape),
                  _full(w1ko.shape), _full(w1vo.shape), _full(w2k.shape), _full(w2v.shape),
                  _full(cos.shape), _full(sin.shape)],
        out_specs=[ospec, ospec],
        out_shape=[out, out],
        compiler_params=_params("parallel"),
        name="nsa_compress",
    )(tk, tv, w1k, w1v, pek, pev, w1ko, w1vo, w2k, w2v, cos, sin)


def _nsa_kernel(q_ref, kc_ref, vc_ref, ks_ref, vs_ref, g_ref, win_ref, ov_ref, e_ref, o_ref,
                bias_scr, s_scr, p_scr, m_scr, l_scr, a_scr, acc_scr, *, qblk, seq, n_sb, n_sel, tk_sel):
    i = pl.program_id(2)
    start = i * qblk
    n_cmp = kc_ref.shape[2]
    lane = lax.broadcasted_iota(I32, (qblk, LANES), 1)
    lo = lane < HEAD_DIM
    q = _stack_group_queries(q_ref, 0, GQA_REP)
    tq = start + lax.broadcasted_iota(I32, (qblk, 1), 0)

    kc = kc_ref[0, 0]
    vc = vc_ref[0, 0]
    cmp_end = lax.broadcasted_iota(I32, (qblk, n_cmp), 1) * NSA_CMP_STRIDE + (NSA_CMP_LEN - 1)
    vis = jnp.where(cmp_end <= tq, 1.0, 0.0).astype(F32)
    vis4 = jnp.concatenate([vis] * GQA_REP, axis=0)
    s_c = lax.dot_general(q, kc, NT_DIMS, preferred_element_type=F32)
    s_c = jnp.where(vis4 > 0.5, s_c, NEG_BIG)
    p_c = jnp.exp(s_c - jnp.max(s_c, axis=-1, keepdims=True)) * vis4
    p_c = p_c / jnp.maximum(jnp.sum(p_c, axis=-1, keepdims=True), 1e-30)
    o_c = jnp.dot(p_c.astype(BF16), vc, preferred_element_type=F32)

    p_sum = p_c[0:qblk]
    for r in range(1, GQA_REP):
        p_sum = p_sum + p_c[r * qblk:(r + 1) * qblk]
    imp_t = lax.dot_general(ov_ref[...], p_sum, NT_DIMS, preferred_element_type=F32,
                            precision=lax.Precision.HIGHEST)[:n_sb]
    blk_t = lax.broadcasted_iota(I32, (n_sb, qblk), 0)
    tq_t = start + lax.broadcasted_iota(I32, (n_sb, qblk), 1)
    cur_t = tq_t // NSA_SEL_LEN
    forced = jnp.where(blk_t == 0, 1, jnp.where(blk_t == cur_t, 1, jnp.where(blk_t == cur_t - 1, 1, 0)))
    score_t = jnp.where(forced == 1, FORCE_SCORE, jnp.where(blk_t * NSA_SEL_LEN > tq_t, -1.0, imp_t))
    sel_t = jnp.where(_rank_desc(score_t, n_sb, 0) < n_sel, 1.0, 0.0).astype(F32)
    if n_sb < LANES:
        sel_t = jnp.concatenate([sel_t, jnp.zeros((LANES - n_sb, qblk), F32)], axis=0)
    sel = jnp.transpose(sel_t).astype(BF16)

    _flash_init(m_scr, l_scr, acc_scr)
    chunks_per_head = qblk // FLASH_ROWS

    def body(j, carry):
        k0 = pl.multiple_of(j * tk_sel, tk_sel)
        picked = jnp.dot(sel, e_ref[:, pl.ds(k0, tk_sel)], preferred_element_type=F32)
        tk = k0 + lax.broadcasted_iota(I32, (qblk, tk_sel), 1)
        bias_scr[...] = jnp.where(picked > 0.5, jnp.where(tk <= tq, 0.0, -jnp.inf), -jnp.inf).astype(F32)

        def bias_fn(c):
            r0 = (c % chunks_per_head) * FLASH_ROWS
            return bias_scr[r0:r0 + FLASH_ROWS, :]

        _flash_step(q, ks_ref[pl.ds(k0, tk_sel), :], vs_ref[pl.ds(k0, tk_sel), :], bias_fn, LOG2_E,
                    s_scr, p_scr, m_scr, l_scr, a_scr, acc_scr)
        return carry

    n_tiles = (start + qblk + tk_sel - 1) // tk_sel
    lax.fori_loop(0, n_tiles, body, 0)
    o_s = acc_scr[...] / jnp.sum(l_scr[...], axis=-1, keepdims=True)

    for j in range(GQA_REP // 2):

        def gate(branch, pair=j):
            c = branch * GQA_REP + 2 * pair
            return jnp.where(lo, g_ref[:, c:c + 1], g_ref[:, c + 1:c + 2])

        rows_e = slice((2 * j) * qblk, (2 * j + 1) * qblk)
        rows_o = slice((2 * j + 1) * qblk, (2 * j + 2) * qblk)
        oc = jnp.where(lo, o_c[rows_e], o_c[rows_o])
        os_ = jnp.where(lo, o_s[rows_e], o_s[rows_o])
        out = gate(0) * oc + gate(1) * os_ + win_ref[:, j * LANES:(j + 1) * LANES]
        o_ref[:, j * LANES:(j + 1) * LANES] = out.astype(o_ref.dtype)


def _nsa_cmp_sel(qb, kcmp, vcmp, ks, vs, gates, win, overlap, expand, batch, seq, qblk, n_sb, n_sel, tk_sel):
    n = qb.shape[0]
    nq = seq // qblk
    n_cmp = kcmp.shape[2]
    gw = GQA_REP * HEAD_DIM
    rows = GQA_REP * qblk
    kern = functools.partial(_nsa_kernel, qblk=qblk, seq=seq, n_sb=n_sb, n_sel=n_sel, tk_sel=tk_sel)
    return pl.pallas_call(
        kern,
        grid=(batch, NSA_KV_HEADS, nq),
        in_specs=[pl.BlockSpec((qblk, gw), lambda b, g, i: (b * nq + i, g)),
                  pl.BlockSpec((1, 1, n_cmp, LANES), lambda b, g, i: (b, g, 0, 0)),
                  pl.BlockSpec((1, 1, n_cmp, LANES), lambda b, g, i: (b, g, 0, 0)),
                  pl.BlockSpec((seq, LANES), lambda b, g, i: (b, g)),
                  pl.BlockSpec((seq, LANES), lambda b, g, i: (b, g)),
                  pl.BlockSpec((qblk, LANES), lambda b, g, i: (b * nq + i, g)),
                  pl.BlockSpec((qblk, gw), lambda b, g, i: (b * nq + i, g)),
                  _full(overlap.shape), _full(expand.shape)],
        out_specs=pl.BlockSpec((qblk, gw), lambda b, g, i: (b * nq + i, g)),
        out_shape=jax.ShapeDtypeStruct((n, NSA_Q_HEADS * HEAD_DIM), BF16),
        scratch_shapes=[pltpu.VMEM((qblk, tk_sel), F32),
                        pltpu.VMEM((rows, tk_sel), F32), pltpu.VMEM((rows, tk_sel), BF16),
                        pltpu.VMEM((rows, LANES), F32), pltpu.VMEM((rows, LANES), F32),
                        pltpu.VMEM((rows, LANES), F32), pltpu.VMEM((rows, LANES), F32)],
        compiler_params=_params("parallel", "parallel", "arbitrary"),
        name="nsa_cmp_sel",
    )(qb, kcmp, vcmp, ks, vs, gates, win, overlap, expand)


def _oproj_ln_kernel(*refs, n_parts, alpha):
    o_refs = refs[:n_parts]
    w_refs = refs[n_parts:2 * n_parts]
    x_ref, g_ref, b_ref, y_ref = refs[2 * n_parts:]
    h = jnp.dot(o_refs[0][...], w_refs[0][...], preferred_element_type=F32)
    for o_ref, w_ref in zip(o_refs[1:], w_refs[1:]):
        h = h + jnp.dot(o_ref[...], w_ref[...], preferred_element_type=F32)
    y_ref[...] = _layer_norm_rows(alpha * x_ref[...] + h, g_ref[...], b_ref[...])


def _oproj_ln(o_parts, w_parts, x2, g, b, alpha, tm):
    n, d = x2.shape
    kern = functools.partial(_oproj_ln_kernel, n_parts=len(o_parts), alpha=alpha)
    in_specs = ([pl.BlockSpec((tm, o.shape[1]), lambda i: (i, 0)) for o in o_parts]
                + [_full(w.shape) for w in w_parts]
                + [pl.BlockSpec((tm, d), lambda i: (i, 0)), _full((1, d)), _full((1, d))])
    return pl.pallas_call(
        kern,
        grid=(n // tm,),
        in_specs=in_specs,
        out_specs=pl.BlockSpec((tm, d), lambda i: (i, 0)),
        out_shape=jax.ShapeDtypeStruct((n, d), F32),
        compiler_params=_params("parallel"),
        name="oproj_ln",
    )(*o_parts, *w_parts, x2, g, b)


def _rms_rows(v, g):
    return v * lax.rsqrt(jnp.mean(v * v, axis=-1, keepdims=True) + RMS_EPS) * g


def _mla_proj_kernel(x_ref, win_ref, qn_ref, kvn_ref, wuq_ref, wuk_ref, wuv_ref, cos_ref, sin_ref,
                     q_ref, k_ref, v_ref):
    xb = x_ref[...].astype(BF16)
    cos = cos_ref[...]
    sin = sin_ref[...]
    lat = jnp.dot(xb, win_ref[...], preferred_element_type=F32)
    cq = _rms_rows(lat[:, :MLA_Q_LORA], qn_ref[...]).astype(BF16)
    ckv = _rms_rows(lat[:, MLA_Q_LORA:MLA_Q_LORA + MLA_KV_LORA], kvn_ref[...]).astype(BF16)
    kr = _rope_slab(lat[:, MLA_Q_LORA + MLA_KV_LORA:], cos, sin, MLA_ROPE // 2)
    q = jnp.dot(cq, wuq_ref[...], preferred_element_type=F32)
    k = jnp.dot(ckv, wuk_ref[...], preferred_element_type=F32)
    for h in range(MLA_HEADS):
        sl = slice(h * LANES, (h + 1) * LANES)
        q_ref[:, sl] = _rope_slab(q[:, sl], cos, sin, MLA_ROPE // 2).astype(BF16)
        k_ref[:, sl] = (k[:, sl] + kr).astype(BF16)
    v_ref[...] = jnp.dot(ckv, wuv_ref[...], preferred_element_type=F32).astype(BF16)


def _mla_proj(x2, w_in, qn, kvn, wuq, wuk, wuv, cos, sin, seq, tm):
    n, d = x2.shape
    sblk = seq // tm
    hw = MLA_HEADS * LANES
    vw = MLA_HEADS * MLA_V
    return pl.pallas_call(
        _mla_proj_kernel,
        grid=(n // tm,),
        in_specs=[pl.BlockSpec((tm, d), lambda i: (i, 0)),
                  _full(w_in.shape), _full(qn.shape), _full(kvn.shape),
                  _full(wuq.shape), _full(wuk.shape), _full(wuv.shape),
                  pl.BlockSpec((tm, LANES), lambda i: (i % sblk, 0)),
                  pl.BlockSpec((tm, LANES), lambda i: (i % sblk, 0))],
        out_specs=[pl.BlockSpec((tm, hw), lambda i: (i, 0)),
                   pl.BlockSpec((tm, hw), lambda i: (i, 0)),
                   pl.BlockSpec((tm, vw), lambda i: (i, 0))],
        out_shape=[jax.ShapeDtypeStruct((n, hw), BF16), jax.ShapeDtypeStruct((n, hw), BF16),
                   jax.ShapeDtypeStruct((n, vw), BF16)],
        compiler_params=_params("parallel"),
        name="mla_proj",
    )(x2, w_in, qn, kvn, wuq, wuk, wuv, cos, sin)


FLASH_ROWS = 32
LOG2_E = 1.4426950408889634


def _flash_init(m_scr, l_scr, acc_scr):
    m_scr[...] = jnp.full(m_scr.shape, -jnp.inf, F32)
    l_scr[...] = jnp.zeros(l_scr.shape, F32)
    acc_scr[...] = jnp.zeros(acc_scr.shape, F32)


def _flash_step(q, k, v, bias_fn, exp_scale, s_scr, p_scr, m_scr, l_scr, a_scr, acc_scr):
    rows, keys = q.shape[0], k.shape[0]
    n_chunks = rows // FLASH_ROWS
    slabs = keys // LANES
    s_scr[...] = lax.dot_general(q, k, NT_DIMS, preferred_element_type=F32)
    for c in range(n_chunks):
        r = slice(c * FLASH_ROWS, (c + 1) * FLASH_ROWS)
        s = s_scr[r, :]
        if bias_fn is not None:
            s = s + bias_fn(c)
            s_scr[r, :] = s
        mx = s[:, 0:LANES]
        for j in range(1, slabs):
            mx = jnp.maximum(mx, s[:, j * LANES:(j + 1) * LANES])
        a_scr[r, :] = mx
    m_old = m_scr[...]
    m_new = jnp.maximum(m_old, jnp.broadcast_to(jnp.max(a_scr[...], axis=-1, keepdims=True), m_old.shape))
    m_scr[...] = m_new
    a_scr[...] = jnp.exp2((m_old - m_new) * exp_scale)
    for c in range(n_chunks):
        r = slice(c * FLASH_ROWS, (c + 1) * FLASH_ROWS)
        m_rows = m_scr[r, :]
        ps = None
        for j in range(slabs):
            cols = slice(j * LANES, (j + 1) * LANES)
            p = jnp.exp2((s_scr[r, cols] - m_rows) * exp_scale)
            p_scr[r, cols] = p.astype(p_scr.dtype)
            ps = p if ps is None else ps + p
        l_scr[r, :] = a_scr[r, :] * l_scr[r, :] + ps
    acc_scr[...] = a_scr[...] * acc_scr[...] + jnp.dot(p_scr[...], v, preferred_element_type=F32)


def _mla_attn_kernel(q_ref, k_ref, v_ref, o_ref, s_scr, p_scr, m_scr, l_scr, a_scr, acc_scr, *, tq, scale):
    i = pl.program_id(2)
    lane = lax.broadcasted_iota(I32, (tq, LANES), 1)
    lo = lane < MLA_V
    for e in range(2):
        _flash_init(m_scr.at[e], l_scr.at[e], acc_scr.at[e])

    def diag_bias(c):
        row = c * FLASH_ROWS + lax.broadcasted_iota(I32, (FLASH_ROWS, tq), 0)
        col = lax.broadcasted_iota(I32, (FLASH_ROWS, tq), 1)
        return jnp.where(col <= row, 0.0, -jnp.inf).astype(F32)

    def tile(j, bias_fn):
        k0 = pl.multiple_of(j * tq, tq)
        v = v_ref[pl.ds(k0, tq), :]
        for e in range(2):
            _flash_step(q_ref[:, e * LANES:(e + 1) * LANES], k_ref[pl.ds(k0, tq), e * LANES:(e + 1) * LANES], v,
                        bias_fn, scale * LOG2_E, s_scr.at[e], p_scr.at[e], m_scr.at[e], l_scr.at[e],
                        a_scr.at[e], acc_scr.at[e])

    def body(j, carry):
        tile(j, None)
        return carry

    lax.fori_loop(0, i, body, 0)
    tile(i, diag_bias)
    o0 = acc_scr[0] / jnp.sum(l_scr[0], axis=-1, keepdims=True)
    o1 = acc_scr[1] / jnp.sum(l_scr[1], axis=-1, keepdims=True)
    o_ref[...] = jnp.where(lo, o0, o1).astype(o_ref.dtype)


def _mla_attention(q, k, v, batch, seq, tq):
    n = q.shape[0]
    nq = seq // tq
    kern = functools.partial(_mla_attn_kernel, tq=tq, scale=(MLA_NOPE + MLA_ROPE) ** -0.5)
    return pl.pallas_call(
        kern,
        grid=(batch, MLA_HEADS // 2, nq),
        in_specs=[pl.BlockSpec((tq, 2 * LANES), lambda b, p, i: (b * nq + i, p)),
                  pl.BlockSpec((seq, 2 * LANES), lambda b, p, i: (b, p)),
                  pl.BlockSpec((seq, LANES), lambda b, p, i: (b, p))],
        out_specs=pl.BlockSpec((tq, LANES), lambda b, p, i: (b * nq + i, p)),
        out_shape=jax.ShapeDtypeStruct((n, MLA_HEADS * MLA_V), BF16),
        scratch_shapes=[pltpu.VMEM((2, tq, tq), F32), pltpu.VMEM((2, tq, tq), BF16),
                        pltpu.VMEM((2, tq, LANES), F32), pltpu.VMEM((2, tq, LANES), F32), pltpu.VMEM((2, tq, LANES), F32),
                        pltpu.VMEM((2, tq, LANES), F32)],
        compiler_params=_params("parallel", "parallel", "arbitrary"),
        name="mla_attn",
    )(q, k, v)


def _router_kernel(x_ref, rw_ref, rb_ref, eidx_ref, gate_ref, pos_ref, cnt_ref, carry_ref, *, steps_per_chunk):
    i = pl.program_id(0)
    tm = x_ref.shape[0]
    per_group = N_EXPERTS // MOE_GROUPS

    @pl.when(i % steps_per_chunk == 0)
    def _():
        carry_ref[...] = jnp.zeros_like(carry_ref)

    logits = jnp.dot(x_ref[...], rw_ref[...], preferred_element_type=F32, precision=lax.Precision.HIGHEST)
    lt = jnp.transpose(logits)[:N_EXPERTS]
    s = _sigmoid(lt)
    sb = s + rb_ref[...]

    g3 = sb.reshape(MOE_GROUPS, per_group, tm)
    idx3 = lax.broadcasted_iota(I32, g3.shape, 1).astype(F32)
    m1 = jnp.max(g3, axis=1, keepdims=True)
    first = jnp.min(jnp.where(g3 == m1, idx3, float(per_group)), axis=1, keepdims=True)
    m2 = jnp.max(jnp.where(idx3 == first, -jnp.inf, g3), axis=1, keepdims=True)
    gscore = (m1 + m2).reshape(MOE_GROUPS, tm)
    gsel = _rank_desc(gscore, MOE_GROUPS, 0) < MOE_TOPK_GROUPS
    gsel3 = jnp.where(gsel, 1.0, 0.0).astype(F32).reshape(MOE_GROUPS, 1, tm)
    masked = jnp.where(gsel3 > 0.5, g3, -jnp.inf).reshape(N_EXPERTS, tm)
    rank = _rank_desc(masked, N_EXPERTS, 0)
    sel = rank < MOE_TOP_K
    gate = jnp.where(sel, s, 0.0)
    gate = gate / jnp.sum(gate, axis=0, keepdims=True) * MOE_ROUTED_SCALE

    sel_b = jnp.where(sel, 1.0, 0.0).astype(BF16)
    r_i = lax.broadcasted_iota(I32, (tm, tm), 0)
    c_i = lax.broadcasted_iota(I32, (tm, tm), 1)
    tri = jnp.where(r_i < c_i, 1.0, 0.0).astype(BF16)
    carry = carry_ref[:, 0:1]
    before = jnp.dot(sel_b, tri, preferred_element_type=F32) + carry
    carry_new = carry + jnp.sum(sel_b.astype(F32), axis=1, keepdims=True)
    carry_ref[...] = jnp.broadcast_to(carry_new, carry_ref.shape)
    cnt_ref[0] = jnp.broadcast_to(carry_new, carry_ref.shape).astype(I32)

    e_iota = lax.broadcasted_iota(I32, (N_EXPERTS, tm), 0).astype(F32)
    rows_e, rows_g, rows_p = [], [], []
    for r in range(MOE_TOP_K):
        hit = rank == r
        rows_e.append(jnp.sum(jnp.where(hit, e_iota, 0.0), axis=0, keepdims=True))
        rows_g.append(jnp.sum(jnp.where(hit, gate, 0.0), axis=0, keepdims=True))
        rows_p.append(jnp.sum(jnp.where(hit, before, 0.0), axis=0, keepdims=True))
    pad = 8 - MOE_TOP_K
    eidx_ref[...] = jnp.concatenate(rows_e + [jnp.zeros((pad, tm), F32)], axis=0).astype(I32)
    gate_ref[...] = jnp.concatenate(rows_g + [jnp.zeros((pad, tm), F32)], axis=0)
    pos_ref[...] = jnp.concatenate(rows_p + [jnp.zeros((pad, tm), F32)], axis=0).astype(I32)


def _router(x2, rw, rb, n_chunk, tm):
    n, d = x2.shape
    steps = n // tm
    spc = steps // n_chunk
    kern = functools.partial(_router_kernel, steps_per_chunk=spc)
    row8 = pl.BlockSpec((8, tm), lambda i: (0, i))
    return pl.pallas_call(
        kern,
        grid=(steps,),
        in_specs=[pl.BlockSpec((tm, d), lambda i: (i, 0)), _full(rw.shape), _full(rb.shape)],
        out_specs=[row8, row8, row8, pl.BlockSpec((1, N_EXPERTS, LANES), lambda i: (i // spc, 0, 0))],
        out_shape=[jax.ShapeDtypeStruct((8, n), I32), jax.ShapeDtypeStruct((8, n), F32),
                   jax.ShapeDtypeStruct((8, n), I32),
                   jax.ShapeDtypeStruct((n_chunk, N_EXPERTS, LANES), I32)],
        scratch_shapes=[pltpu.VMEM((N_EXPERTS, LANES), F32)],
        compiler_params=_params("arbitrary"),
        name="moe_router",
    )(x2, rw, rb)


def _dest_kernel(tab_ref, eidx_ref, pos_ref, dest_ref, *, steps_per_chunk):
    chunk = pl.program_id(0) // steps_per_chunk
    eidx = eidx_ref[...]
    dest = pos_ref[...]
    for e in range(N_EXPERTS):
        dest = dest + jnp.where(eidx == e, tab_ref[chunk * N_EXPERTS + e], 0)
    dest_ref[...] = dest


def _dest_rows(pad_starts, eidx, pos, n_chunk, tm):
    n = eidx.shape[1]
    steps = n // tm
    row8 = lambda i, tab: (0, i)
    return pl.pallas_call(
        functools.partial(_dest_kernel, steps_per_chunk=steps // n_chunk),
        grid_spec=pltpu.PrefetchScalarGridSpec(
            num_scalar_prefetch=1, grid=(steps,),
            in_specs=[pl.BlockSpec((8, tm), row8), pl.BlockSpec((8, tm), row8)],
            out_specs=pl.BlockSpec((8, tm), row8)),
        out_shape=jax.ShapeDtypeStruct((8, n), I32),
        compiler_params=_params("parallel"),
        name="moe_dest",
    )(pad_starts, eidx, pos)


SC_ROWS = 64
SC_WORKERS = 32


def _sc_mesh():
    return plsc.VectorSubcoreMesh(core_axis_name="c", subcore_axis_name="s")


def _sc_worker_base(per_worker):
    return (lax.axis_index("s") * 2 + lax.axis_index("c")) * per_worker


def _sc_scatter_rows(x, dest, n_out):
    n, d = x.shape
    slots = dest.shape[0] // n
    per_w = n // SC_WORKERS

    def body(x_hbm, i_hbm, o_hbm, idx_v, rows_v, sem):
        base = _sc_worker_base(per_w)

        @pl.loop(0, per_w // SC_ROWS)
        def _(c):
            off = pl.multiple_of(base + c * SC_ROWS, 8)
            pltpu.sync_copy(x_hbm.at[pl.ds(off, SC_ROWS)], rows_v)
            for k in range(slots):
                pltpu.sync_copy(i_hbm.at[pl.ds(k * n + off, SC_ROWS)], idx_v)
                pltpu.async_copy(rows_v, o_hbm.at[idx_v], sem).wait()

    return pl.kernel(
        body, out_type=jax.ShapeDtypeStruct((n_out, d), x.dtype), mesh=_sc_mesh(),
        scratch_types=[pltpu.VMEM((SC_ROWS,), I32), pltpu.VMEM((SC_ROWS, d), x.dtype), pltpu.SemaphoreType.DMA],
        name="moe_dispatch_sc")(x, dest)


def _sc_gather_rows(table, idx):
    d = table.shape[1]
    r = idx.shape[0]
    per_w = r // SC_WORKERS

    def body(t_hbm, i_hbm, o_hbm, idx_v, rows_v, sem):
        base = _sc_worker_base(per_w)

        @pl.loop(0, per_w // SC_ROWS)
        def _(c):
            off = pl.multiple_of(base + c * SC_ROWS, 8)
            pltpu.sync_copy(i_hbm.at[pl.ds(off, SC_ROWS)], idx_v)
            pltpu.async_copy(t_hbm.at[idx_v], rows_v, sem).wait()
            pltpu.sync_copy(rows_v, o_hbm.at[pl.ds(off, SC_ROWS)])

    return pl.kernel(
        body, out_type=jax.ShapeDtypeStruct((r, d), table.dtype), mesh=_sc_mesh(),
        scratch_types=[pltpu.VMEM((SC_ROWS,), I32), pltpu.VMEM((SC_ROWS, d), table.dtype), pltpu.SemaphoreType.DMA],
        name="moe_combine_sc")(table, idx)


def _experts_kernel(be_ref, nv_ref, xs_ref, wgu_ref, wd_ref, y_ref):
    i = pl.program_id(0)

    @pl.when(i < nv_ref[0])
    def _():
        gu = jnp.dot(xs_ref[...].astype(BF16), wgu_ref[0], preferred_element_type=F32)
        h = _silu(gu[:, :D_EXPERT]) * gu[:, D_EXPERT:]
        y_ref[...] = jnp.dot(h.astype(BF16), wd_ref[0], preferred_element_type=F32).astype(y_ref.dtype)

    @pl.when(i >= nv_ref[0])
    def _():
        y_ref[...] = jnp.zeros_like(y_ref)


def _experts(blk_exp, n_valid, xs, wgu, wd, tb):
    rows, d = xs.shape
    grid_spec = pltpu.PrefetchScalarGridSpec(
        num_scalar_prefetch=2,
        grid=(rows // tb,),
        in_specs=[pl.BlockSpec((tb, d), lambda i, be, nv: (i, 0)),
                  pl.BlockSpec((1, d, 2 * D_EXPERT), lambda i, be, nv: (be[i], 0, 0)),
                  pl.BlockSpec((1, D_EXPERT, d), lambda i, be, nv: (be[i], 0, 0))],
        out_specs=pl.BlockSpec((tb, d), lambda i, be, nv: (i, 0)),
    )
    return pl.pallas_call(
        _experts_kernel,
        grid_spec=grid_spec,
        out_shape=jax.ShapeDtypeStruct((rows, d), F32),
        compiler_params=_params("arbitrary"),
        name="moe_experts",
    )(blk_exp, n_valid, xs, wgu, wd)


def _combine_ln_kernel(x_ref, yg_ref, gt_ref, wgu_ref, wd_ref, g_ref, b_ref, o_ref, *, alpha):
    x = x_ref[...]
    gu = jnp.dot(x.astype(BF16), wgu_ref[...], preferred_element_type=F32)
    dsh = gu.shape[1] // 2
    h = _silu(gu[:, :dsh]) * gu[:, dsh:]
    f = jnp.dot(h.astype(BF16), wd_ref[...], preferred_element_type=F32)
    for k in range(MOE_TOP_K):
        f = f + gt_ref[:, k:k + 1] * yg_ref[k].astype(F32)
    o_ref[...] = _layer_norm_rows(alpha * x + f, g_ref[...], b_ref[...])


def _combine_ln(x2, yg, gates_t, wgu, wd, g, b, alpha, tm):
    n, d = x2.shape
    kern = functools.partial(_combine_ln_kernel, alpha=alpha)
    return pl.pallas_call(
        kern,
        grid=(n // tm,),
        in_specs=[pl.BlockSpec((tm, d), lambda i: (i, 0)),
                  pl.BlockSpec((MOE_TOP_K, tm, d), lambda i: (0, i, 0)),
                  pl.BlockSpec((tm, 8), lambda i: (i, 0)),
                  _full(wgu.shape), _full(wd.shape), _full((1, d)), _full((1, d))],
        out_specs=pl.BlockSpec((tm, d), lambda i: (i, 0)),
        out_shape=jax.ShapeDtypeStruct((n, d), F32),
        compiler_params=_params("parallel"),
        name="moe_combine_ln",
    )(x2, yg, gates_t, wgu, wd, g, b)


def _rope_tables(positions, dim, lane_offset, period):
    half = dim // 2
    inv_freq = ROPE_THETA ** (-2.0 * jnp.arange(half, dtype=jnp.float32) / dim)
    ang = positions.astype(jnp.float32)[:, None] * inv_freq[None, :]
    cos_h, sin_h = jnp.cos(ang), jnp.sin(ang)
    lanes = np.arange(LANES)
    rel = (lanes - lane_offset) % period
    active = (lanes >= lane_offset) & (rel < dim)
    fidx = np.where(active, rel % half, 0)
    sign = np.where(rel < half, -1.0, 1.0)
    cos = jnp.where(active[None, :], cos_h[:, fidx], 1.0)
    sin = jnp.where(active[None, :], sin_h[:, fidx] * sign[None, :], 0.0)
    return cos.astype(F32), sin.astype(F32)


def _gate_slabs(w):
    k = w.shape[0]
    g = w.reshape(k, 3, NSA_KV_HEADS, GQA_REP).transpose(0, 2, 1, 3).reshape(k, NSA_KV_HEADS, 3 * GQA_REP)
    return jnp.pad(g, ((0, 0), (0, 0), (0, LANES - 3 * GQA_REP))).reshape(k, NSA_KV_HEADS * LANES)


def _dup_heads(w):
    a, b = w[:, :HEAD_DIM], w[:, HEAD_DIM:]
    return jnp.concatenate([a, a, b, b], axis=1)


def _moe_layer(x1, rw, rb, wg, wu, wd, shg, shu, shd, ln_g, ln_b, alpha, tm):
    n, d = x1.shape
    n_chunk = 1
    tb = MOE_ROW_BLOCK
    rw_p = jnp.pad(rw, ((0, 0), (0, LANES - N_EXPERTS)))
    eidx, gate, pos, cnt = _router(x1, rw_p, rb.reshape(N_EXPERTS, 1), n_chunk, tm)
    counts = cnt[:, :, 0].reshape(-1)
    padded = (counts + tb - 1) // tb * tb
    pad_ends = jnp.cumsum(padded)
    pad_starts = (pad_ends - padded).astype(I32)
    dest = _dest_rows(pad_starts, eidx, pos, n_chunk, tm)[:MOE_TOP_K].reshape(-1)
    n_rows = n * MOE_TOP_K
    n_blk = -(-(n_rows + n_chunk * N_EXPERTS * (tb - 1)) // tb)
    blk_first_row = jnp.arange(n_blk, dtype=I32) * tb
    owner = jnp.sum((pad_ends[None, :] <= blk_first_row[:, None]).astype(I32), axis=1)
    blk_exp = jnp.minimum(owner, n_chunk * N_EXPERTS - 1).astype(I32) % N_EXPERTS
    n_valid = (pad_ends[-1] // tb).astype(I32).reshape(1)

    xs = _sc_scatter_rows(x1, dest, n_blk * tb)
    wgu = jnp.concatenate([wg, wu], axis=-1).astype(BF16)
    y = _experts(blk_exp, n_valid, xs, wgu, wd.astype(BF16), tb)
    yg = _sc_gather_rows(y, dest).reshape(MOE_TOP_K, n, d)
    gates_t = gate.T
    sh_gu = jnp.concatenate([shg, shu], axis=-1).astype(BF16)
    return _combine_ln(x1, yg, gates_t, sh_gu, shd.astype(BF16), ln_g.reshape(1, d), ln_b.reshape(1, d), alpha, tm)


def _swa_nsa_layer(x2, batch, seq, w_in, sinks, pe_k, w_ck1, w_ck2, pe_v, w_cv1, w_cv2, w_out,
                   ln_g, ln_b, alpha, tm):
    n, d = x2.shape
    splits = np.cumsum([512, 128, 128, 512, 128, 128, 128, 128, 128, 128, 24])[:-1]
    qa, ka, va, qb, kc, vc, ksl, vsl, kw, vw, gts = jnp.split(w_in, [int(c) for c in splits], axis=1)
    w_all = jnp.concatenate(
        [qa, _dup_heads(ka), _dup_heads(va), qb, kc, vc, _dup_heads(ksl), _dup_heads(vsl),
         _dup_heads(kw), _dup_heads(vw), _gate_slabs(gts)], axis=1).astype(BF16)
    cos, sin = _rope_tables(jnp.arange(seq), HEAD_DIM, 0, HEAD_DIM)
    (q_a, k_a, v_a, q_b, k_c, v_c, k_s, v_s, k_w, v_w, gates) = _mix0_proj(x2, w_all, cos, sin, seq, tm)

    o_swa = _banded_attention(q_a, k_a, v_a, batch, seq, SWA_WINDOW, 128, sinks=sinks, name="swa")
    o_win = _banded_attention(q_b, k_w, v_w, batch, seq, NSA_WINDOW, 128, gates=gates,
                              gate_col=2 * GQA_REP, out_dtype=F32, name="nsa_win")

    n_chunk = seq // NSA_CMP_STRIDE
    cw = NSA_CMP_STRIDE * NSA_KV_HEADS * HEAD_DIM
    hid = NSA_CMP_HIDDEN

    def expand_w1(w1):
        halves = w1.reshape(2, NSA_CMP_STRIDE, HEAD_DIM, hid)
        out = jnp.zeros((NSA_KV_HEADS, NSA_CMP_STRIDE, NSA_KV_HEADS, HEAD_DIM, 2 * hid), w1.dtype)
        for g in range(NSA_KV_HEADS):
            out = out.at[g, :, g, :, :hid].set(halves[0]).at[g, :, g, :, hid:].set(halves[1])
        return out.reshape(NSA_KV_HEADS, cw, 2 * hid).astype(BF16)

    def pe_rows(pe):
        return jnp.pad(pe.reshape(1, NSA_CMP_LEN * HEAD_DIM), ((0, 15), (0, 0))).astype(BF16)

    cos_c, sin_c = _rope_tables(jnp.arange(n_chunk) * NSA_CMP_STRIDE + NSA_CMP_LEN - 1, HEAD_DIM, 0, HEAD_DIM)
    k_cmp, v_cmp = _compress(
        k_c.reshape(batch * n_chunk, cw), v_c.reshape(batch * n_chunk, cw),
        expand_w1(w_ck1), expand_w1(w_cv1), pe_rows(pe_k), pe_rows(pe_v),
        w_ck1.astype(BF16), w_cv1.astype(BF16),
        jnp.concatenate([w_ck2, w_ck2], axis=1).astype(BF16),
        jnp.concatenate([w_cv2, w_cv2], axis=1).astype(BF16),
        cos_c, sin_c, batch, n_chunk)

    n_sb = seq // NSA_SEL_LEN
    n_sel = min(NSA_N_SEL, n_sb)
    cs = np.arange(n_chunk) * NSA_CMP_STRIDE
    bs = np.arange(n_sb) * NSA_SEL_LEN
    ov = np.clip(np.minimum(cs[:, None] + NSA_CMP_LEN, bs[None, :] + NSA_SEL_LEN)
                 - np.maximum(cs[:, None], bs[None, :]), 0, None) / NSA_CMP_LEN
    overlap = jnp.asarray(np.pad(ov, ((0, 0), (0, LANES - n_sb))).T, F32)
    expand = jnp.asarray(np.arange(LANES)[:, None] == (np.arange(seq)[None, :] // NSA_SEL_LEN), BF16)
    tk_sel = min(512, seq)
    o_nsa = _nsa_cmp_sel(q_b, k_cmp, v_cmp, k_s, v_s, gates, o_win, overlap, expand,
                         batch, seq, 128, n_sb, n_sel, tk_sel)
    w_o = w_out.astype(BF16)
    half = SWA_Q_HEADS * HEAD_DIM
    return _oproj_ln([o_swa, o_nsa], [w_o[:half], w_o[half:]], x2, ln_g.reshape(1, d), ln_b.reshape(1, d),
                     alpha, tm)


def _mla_layer(x2, batch, seq, w_in, q_norm, kv_norm, w_uq, w_ukv, w_out, ln_g, ln_b, alpha, tm):
    n, d = x2.shape
    dq = MLA_NOPE + MLA_ROPE
    w_kr = jnp.zeros((d, LANES), w_in.dtype).at[:, MLA_NOPE:dq].set(w_in[:, MLA_Q_LORA + MLA_KV_LORA:])
    w_in_p = jnp.concatenate([w_in[:, :MLA_Q_LORA + MLA_KV_LORA], w_kr], axis=1).astype(BF16)
    wuq = jnp.pad(w_uq.reshape(MLA_Q_LORA, MLA_HEADS, dq), ((0, 0), (0, 0), (0, LANES - dq)))
    wuq = wuq.reshape(MLA_Q_LORA, MLA_HEADS * LANES).astype(BF16)
    wukv = w_ukv.reshape(MLA_KV_LORA, MLA_HEADS, MLA_NOPE + MLA_V)
    wuk = jnp.pad(wukv[:, :, :MLA_NOPE], ((0, 0), (0, 0), (0, LANES - MLA_NOPE)))
    wuk = wuk.reshape(MLA_KV_LORA, MLA_HEADS * LANES).astype(BF16)
    wuv = wukv[:, :, MLA_NOPE:].reshape(MLA_KV_LORA, MLA_HEADS * MLA_V).astype(BF16)
    cos, sin = _rope_tables(jnp.arange(seq), MLA_ROPE, MLA_NOPE, LANES)
    q, k, v = _mla_proj(x2, w_in_p, q_norm.reshape(1, -1), kv_norm.reshape(1, -1), wuq, wuk, wuv, cos, sin, seq, tm)
    o = _mla_attention(q, k, v, batch, seq, min(512, seq))
    return _oproj_ln([o], [w_out.astype(BF16)], x2, ln_g.reshape(1, d), ln_b.reshape(1, d), alpha, tm)


def kernel(x, swa_nsa_w_in, swa_sinks, nsa_cmp_pe_k, nsa_cmp_k_w1, nsa_cmp_k_w2, nsa_cmp_pe_v, nsa_cmp_v_w1, nsa_cmp_v_w2, swa_nsa_w_out, mla_w_in, mla_q_norm, mla_kv_norm, mla_w_uq, mla_w_ukv, mla_w_out, ln_mix_g, ln_mix_b, ln_ffn_g, ln_ffn_b, router_w, router_bias, expert_w_gate, expert_w_up, expert_w_down, shared_w_gate, shared_w_up, shared_w_down):
    batch, seq, d = x.shape
    depth = ln_mix_g.shape[0]
    alpha = (2 * depth) ** 0.25
    tm = min(512, seq)
    x2 = x.reshape(batch * seq, d)
    for layer in range(depth):
        j = layer // 2
        if layer % 2 == 0:
            x2 = _swa_nsa_layer(x2, batch, seq, swa_nsa_w_in[j], swa_sinks[j], nsa_cmp_pe_k[j],
                                nsa_cmp_k_w1[j], nsa_cmp_k_w2[j], nsa_cmp_pe_v[j], nsa_cmp_v_w1[j],
                                nsa_cmp_v_w2[j], swa_nsa_w_out[j], ln_mix_g[layer], ln_mix_b[layer], alpha, tm)
        else:
            x2 = _mla_layer(x2, batch, seq, mla_w_in[j], mla_q_norm[j], mla_kv_norm[j], mla_w_uq[j],
                            mla_w_ukv[j], mla_w_out[j], ln_mix_g[layer], ln_mix_b[layer], alpha, tm)
        x2 = _moe_layer(x2, router_w[layer], router_bias[layer], expert_w_gate[layer], expert_w_up[layer],
                        expert_w_down[layer], shared_w_gate[layer], shared_w_up[layer], shared_w_down[layer],
                        ln_ffn_g[layer], ln_ffn_b[layer], alpha, tm)
    return x2.reshape(batch, seq, d)
```

```python
import functools
import math

import numpy as np
import jax
import jax.numpy as jnp
from jax import lax
from jax.experimental import pallas as pl
from jax.experimental.pallas import tpu as pltpu
from jax.experimental.pallas import tpu_sc as plsc

F32 = jnp.float32
BF16 = jnp.bfloat16
I32 = jnp.int32

LANES = 128
VMEM_LIMIT = 48 * 1024 * 1024

ROPE_THETA = 10000.0
LN_EPS = 1e-5
RMS_EPS = 1e-6
NEG_BIG = -1e30
FORCE_SCORE = 1e4

HEAD_DIM = 64
SWA_Q_HEADS = 8
SWA_KV_HEADS = 2
SWA_WINDOW = 128
NSA_Q_HEADS = 8
NSA_KV_HEADS = 2
NSA_CMP_LEN = 32
NSA_CMP_STRIDE = 16
NSA_CMP_HIDDEN = 128
NSA_SEL_LEN = 64
NSA_N_SEL = 16
NSA_WINDOW = 512
GQA_REP = 4

MLA_HEADS = 16
MLA_NOPE = 64
MLA_ROPE = 32
MLA_V = 64
MLA_Q_LORA = 384
MLA_KV_LORA = 256

N_EXPERTS = 64
MOE_GROUPS = 8
MOE_TOPK_GROUPS = 4
MOE_TOP_K = 6
D_EXPERT = 256
MOE_ROUTED_SCALE = 2.5
MOE_ROW_BLOCK = 256

NT_DIMS = (((1,), (1,)), ((), ()))


def _params(*sem):
    return pltpu.CompilerParams(dimension_semantics=sem, vmem_limit_bytes=VMEM_LIMIT)


def _full(shape):
    nd = len(shape)
    return pl.BlockSpec(shape, lambda *_: (0,) * nd)


def _rope_slab(y, cos, sin, half):
    lane = lax.broadcasted_iota(I32, y.shape, 1)
    first = (lane % (2 * half)) < half
    rot = jnp.where(first, pltpu.roll(y, LANES - half, 1), pltpu.roll(y, half, 1))
    return y * cos + rot * sin


def _layer_norm_rows(v, g, b):
    mu = jnp.mean(v, axis=-1, keepdims=True)
    vc = v - mu
    var = jnp.mean(vc * vc, axis=-1, keepdims=True)
    return vc * lax.rsqrt(var + LN_EPS) * g + b


def _silu(v):
    return v * (1.0 / (1.0 + jnp.exp(-v)))


def _sigmoid(v):
    return 1.0 / (1.0 + jnp.exp(-v))


U32 = jnp.uint32


def _pack_halves(v):
    w = v.shape[1] // 2
    lo = pltpu.bitcast(v[:, :w].astype(BF16).astype(F32), U32)
    hi = pltpu.bitcast(v[:, w:].astype(BF16).astype(F32), U32)
    return (lo >> 16) | (hi & jnp.uint32(0xFFFF0000))


def _unpack_halves(u):
    lo = pltpu.bitcast(u << 16, F32)
    hi = pltpu.bitcast(u & jnp.uint32(0xFFFF0000), F32)
    return lo, hi


def _stack_group_queries(q_ref, group, rep):
    lane = lax.broadcasted_iota(I32, (q_ref.shape[0], LANES), 1)
    keep_lo = jnp.where(lane < HEAD_DIM, 1.0, 0.0).astype(q_ref.dtype)
    keep_hi = jnp.where(lane < HEAD_DIM, 0.0, 1.0).astype(q_ref.dtype)
    parts = []
    for r in range(rep):
        h = group * rep + r
        slab = q_ref[:, (h // 2) * LANES:(h // 2 + 1) * LANES]
        parts.append(slab * (keep_lo if h % 2 == 0 else keep_hi))
    return jnp.concatenate(parts, axis=0)


def _rank_desc(score, n_valid, axis):
    idx = lax.broadcasted_iota(I32, score.shape, axis)
    rank = jnp.zeros(score.shape, I32)
    for j in range(n_valid):
        other = lax.slice_in_dim(score, j, j + 1, axis=axis)
        ahead = jnp.where(other > score, 1, jnp.where(other == score, jnp.where(idx > j, 1, 0), 0))
        rank = rank + ahead
    return rank


_MIX0_OUTS = (
    ("qa", 512, "rope_q", BF16), ("ka", 256, "rope", BF16), ("va", 256, "plain", BF16),
    ("qb", 512, "rope_q", BF16), ("kc", 128, "plain", BF16), ("vc", 128, "plain", BF16),
    ("ks", 256, "rope", BF16), ("vs", 256, "plain", BF16),
    ("kw", 256, "rope", BF16), ("vw", 256, "plain", BF16),
    ("gates", 256, "sigmoid", F32),
)


def _mix0_proj_kernel(x_ref, w_ref, cos_ref, sin_ref, *out_refs):
    xb = x_ref[...].astype(BF16)
    cos = cos_ref[...]
    sin = sin_ref[...]
    off = 0
    for (_, width, kind, _), o_ref in zip(_MIX0_OUTS, out_refs):
        y = jnp.dot(xb, w_ref[:, off:off + width], preferred_element_type=F32)
        for c in range(width // LANES):
            yc = y[:, c * LANES:(c + 1) * LANES]
            if kind in ("rope", "rope_q"):
                yc = _rope_slab(yc, cos, sin, HEAD_DIM // 2)
            if kind == "rope_q":
                yc = yc * (HEAD_DIM ** -0.5)
            if kind == "sigmoid":
                yc = _sigmoid(yc)
            o_ref[:, c * LANES:(c + 1) * LANES] = yc.astype(o_ref.dtype)
        off += width


def _mix0_proj(x2, w_all, cos, sin, seq, tm):
    n = x2.shape[0]
    d = x2.shape[1]
    wtot = w_all.shape[1]
    sblk = seq // tm
    return pl.pallas_call(
        _mix0_proj_kernel,
        grid=(n // tm,),
        in_specs=[pl.BlockSpec((tm, d), lambda i: (i, 0)),
                  _full((d, wtot)),
                  pl.BlockSpec((tm, LANES), lambda i: (i % sblk, 0)),
                  pl.BlockSpec((tm, LANES), lambda i: (i % sblk, 0))],
        out_specs=[pl.BlockSpec((tm, w), lambda i: (i, 0)) for _, w, _, _ in _MIX0_OUTS],
        out_shape=[jax.ShapeDtypeStruct((n, w), dt) for _, w, _, dt in _MIX0_OUTS],
        compiler_params=_params("parallel"),
        name="mix0_proj",
    )(x2, w_all, cos, sin)


def _banded_kernel(*refs, window, blk, seq, has_sink, gate_col):
    if has_sink:
        sink_ref, refs = refs[0], refs[1:]
    q_ref, k_ref, v_ref = refs[:3]
    g_ref = refs[3] if gate_col is not None else None
    o_ref = refs[-1]
    i = pl.program_id(1)
    start = i * blk
    span = min(window + blk, seq)
    k0 = pl.multiple_of(jnp.maximum(start - window, 0), LANES)
    tq = start + lax.broadcasted_iota(I32, (blk, span), 0)
    tk = k0 + lax.broadcasted_iota(I32, (blk, span), 1)
    diff = tq - tk
    bias = jnp.where(diff >= 0, jnp.where(diff < window, 0.0, -jnp.inf), -jnp.inf).astype(F32)
    bias = jnp.concatenate([bias] * GQA_REP, axis=0)
    lane = lax.broadcasted_iota(I32, (blk, LANES), 1)
    lo = lane < HEAD_DIM
    n_groups = q_ref.shape[1] // (GQA_REP * HEAD_DIM)
    for g in range(n_groups):
        k = k_ref[pl.ds(k0, span), g * LANES:(g + 1) * LANES]
        v = v_ref[pl.ds(k0, span), g * LANES:(g + 1) * LANES]
        q = _stack_group_queries(q_ref, g, GQA_REP)
        s = lax.dot_general(q, k, NT_DIMS, preferred_element_type=F32) + bias
        m = jnp.max(s, axis=-1, keepdims=True)
        if has_sink:
            sink = jnp.concatenate(
                [jnp.full((blk, 1), sink_ref[g * GQA_REP + r], F32) for r in range(GQA_REP)], axis=0)
            m = jnp.maximum(m, sink)
        p = jnp.exp(s - m)
        l = jnp.sum(p, axis=-1, keepdims=True)
        if has_sink:
            l = l + jnp.exp(sink - m)
        o = jnp.dot(p.astype(BF16), v, preferred_element_type=F32) / l
        for j in range(GQA_REP // 2):
            slab = jnp.where(lo, o[(2 * j) * blk:(2 * j + 1) * blk], o[(2 * j + 1) * blk:(2 * j + 2) * blk])
            if gate_col is not None:
                c = g * LANES + gate_col + 2 * j
                gate = jnp.where(lo, g_ref[:, c:c + 1], g_ref[:, c + 1:c + 2])
                slab = slab * gate
            col = (g * (GQA_REP // 2) + j) * LANES
            o_ref[:, col:col + LANES] = slab.astype(o_ref.dtype)


def _banded_attention(q, k, v, batch, seq, window, blk, sinks=None, gates=None, gate_col=None,
                      out_dtype=None, name="banded"):
    n, qw = q.shape
    kw = k.shape[1]
    nblk = seq // blk
    kern = functools.partial(_banded_kernel, window=window, blk=blk, seq=seq,
                             has_sink=sinks is not None, gate_col=gate_col)
    in_specs = []
    args = []
    if sinks is not None:
        in_specs.append(pl.BlockSpec(memory_space=pltpu.SMEM))
        args.append(sinks)
    in_specs += [pl.BlockSpec((blk, qw), lambda b, i: (b * nblk + i, 0)),
                 pl.BlockSpec((seq, kw), lambda b, i: (b, 0)),
                 pl.BlockSpec((seq, kw), lambda b, i: (b, 0))]
    args += [q, k, v]
    if gates is not None:
        in_specs.append(pl.BlockSpec((blk, gates.shape[1]), lambda b, i: (b * nblk + i, 0)))
        args.append(gates)
    return pl.pallas_call(
        kern,
        grid=(batch, nblk),
        in_specs=in_specs,
        out_specs=pl.BlockSpec((blk, qw), lambda b, i: (b * nblk + i, 0)),
        out_shape=jax.ShapeDtypeStruct((n, qw), BF16 if out_dtype is None else out_dtype),
        compiler_params=_params("parallel", "arbitrary"),
        name=name,
    )(*args)


def _gelu_tanh(v):
    return 0.5 * v * (1.0 + jnp.tanh(math.sqrt(2.0 / math.pi) * (v + 0.044715 * (v * v * v))))


def _compress_kernel(tk_ref, tv_ref, w1k_ref, w1v_ref, pek_ref, pev_ref, w1ko_ref, w1vo_ref,
                     w2k_ref, w2v_ref, cos_ref, sin_ref, kc_ref, vc_ref):
    n_chunk = tk_ref.shape[0]
    hid = NSA_CMP_HIDDEN
    for t_ref, w1_ref, pe_ref, w1o_ref, w2_ref, o_ref, rope in (
            (tk_ref, w1k_ref, pek_ref, w1ko_ref, w2k_ref, kc_ref, True),
            (tv_ref, w1v_ref, pev_ref, w1vo_ref, w2v_ref, vc_ref, False)):
        pe_term = jnp.dot(pe_ref[...], w1o_ref[...], preferred_element_type=F32)[0:1, :]
        t = t_ref[...]
        for g in range(NSA_KV_HEADS):
            uv = jnp.dot(t, w1_ref[g], preferred_element_type=F32)
            nxt = pltpu.roll(uv[:, hid:], n_chunk - 1, 0)
            h = _gelu_tanh(uv[:, :hid] + nxt + pe_term)
            c = jnp.dot(h.astype(BF16), w2_ref[...], preferred_element_type=F32)
            if rope:
                c = _rope_slab(c, cos_ref[...], sin_ref[...], HEAD_DIM // 2)
            o_ref[0, g] = c.astype(o_ref.dtype)


def _compress(tk, tv, w1k, w1v, pek, pev, w1ko, w1vo, w2k, w2v, cos, sin, batch, n_chunk):
    width = tk.shape[1]
    out = jax.ShapeDtypeStruct((batch, NSA_KV_HEADS, n_chunk, LANES), BF16)
    ospec = pl.BlockSpec((1, NSA_KV_HEADS, n_chunk, LANES), lambda b: (b, 0, 0, 0))
    return pl.pallas_call(
        _compress_kernel,
        grid=(batch,),
        in_specs=[pl.BlockSpec((n_chunk, width), lambda b: (b, 0)),
                  pl.BlockSpec((n_chunk, width), lambda b: (b, 0)),
                  _full(w1k.shape), _full(w1v.shape), _full(pek.shape), _full(pev.shape),
                  _full(w1ko.shape), _full(w1vo.shape), _full(w2k.shape), _full(w2v.shape),
                  _full(cos.shape), _full(sin.shape)],
        out_specs=[ospec, ospec],
        out_shape=[out, out],
        compiler_params=_params("parallel"),
        name="nsa_compress",
    )(tk, tv, w1k, w1v, pek, pev, w1ko, w1vo, w2k, w2v, cos, sin)


def _nsa_kernel(q_ref, kc_ref, vc_ref, ks_ref, vs_ref, g_ref, win_ref, ov_ref, e_ref, o_ref,
                bias_scr, s_scr, p_scr, m_scr, l_scr, a_scr, acc_scr, *, qblk, seq, n_sb, n_sel, tk_sel):
    i = pl.program_id(2)
    start = i * qblk
    n_cmp = kc_ref.shape[2]
    lane = lax.broadcasted_iota(I32, (qblk, LANES), 1)
    lo = lane < HEAD_DIM
    q = _stack_group_queries(q_ref, 0, GQA_REP)
    tq = start + lax.broadcasted_iota(I32, (qblk, 1), 0)

    kc = kc_ref[0, 0]
    vc = vc_ref[0, 0]
    cmp_end = lax.broadcasted_iota(I32, (qblk, n_cmp), 1) * NSA_CMP_STRIDE + (NSA_CMP_LEN - 1)
    vis = jnp.where(cmp_end <= tq, 1.0, 0.0).astype(F32)
    vis4 = jnp.concatenate([vis] * GQA_REP, axis=0)
    s_c = lax.dot_general(q, kc, NT_DIMS, preferred_element_type=F32)
    s_c = jnp.where(vis4 > 0.5, s_c, NEG_BIG)
    p_c = jnp.exp(s_c - jnp.max(s_c, axis=-1, keepdims=True)) * vis4
    p_c = p_c / jnp.maximum(jnp.sum(p_c, axis=-1, keepdims=True), 1e-30)
    o_c = jnp.dot(p_c.astype(BF16), vc, preferred_element_type=F32)

    p_sum = p_c[0:qblk]
    for r in range(1, GQA_REP):
        p_sum = p_sum + p_c[r * qblk:(r + 1) * qblk]
    imp_t = lax.dot_general(ov_ref[...], p_sum, NT_DIMS, preferred_element_type=F32,
                            precision=lax.Precision.HIGHEST)[:n_sb]
    blk_t = lax.broadcasted_iota(I32, (n_sb, qblk), 0)
    tq_t = start + lax.broadcasted_iota(I32, (n_sb, qblk), 1)
    cur_t = tq_t // NSA_SEL_LEN
    forced = jnp.where(blk_t == 0, 1, jnp.where(blk_t == cur_t, 1, jnp.where(blk_t == cur_t - 1, 1, 0)))
    score_t = jnp.where(forced == 1, FORCE_SCORE, jnp.where(blk_t * NSA_SEL_LEN > tq_t, -1.0, imp_t))
    sel_t = jnp.where(_rank_desc(score_t, n_sb, 0) < n_sel, 1.0, 0.0).astype(F32)
    if n_sb < LANES:
        sel_t = jnp.concatenate([sel_t, jnp.zeros((LANES - n_sb, qblk), F32)], axis=0)
    sel = jnp.transpose(sel_t).astype(BF16)

    _flash_init(m_scr, l_scr, acc_scr)
    chunks_per_head = qblk // FLASH_ROWS

    def body(j, carry):
        k0 = pl.multiple_of(j * tk_sel, tk_sel)
        picked = jnp.dot(sel, e_ref[:, pl.ds(k0, tk_sel)], preferred_element_type=F32)
        tk = k0 + lax.broadcasted_iota(I32, (qblk, tk_sel), 1)
        bias_scr[...] = jnp.where(picked > 0.5, jnp.where(tk <= tq, 0.0, -jnp.inf), -jnp.inf).astype(F32)

        def bias_fn(c):
            r0 = (c % chunks_per_head) * FLASH_ROWS
            return bias_scr[r0:r0 + FLASH_ROWS, :]

        _flash_step(q, ks_ref[pl.ds(k0, tk_sel), :], vs_ref[pl.ds(k0, tk_sel), :], bias_fn, LOG2_E,
                    s_scr, p_scr, m_scr, l_scr, a_scr, acc_scr)
        return carry

    n_tiles = (start + qblk + tk_sel - 1) // tk_sel
    lax.fori_loop(0, n_tiles, body, 0)
    o_s = acc_scr[...] / jnp.sum(l_scr[...], axis=-1, keepdims=True)

    for j in range(GQA_REP // 2):

        def gate(branch, pair=j):
            c = branch * GQA_REP + 2 * pair
            return jnp.where(lo, g_ref[:, c:c + 1], g_ref[:, c + 1:c + 2])

        rows_e = slice((2 * j) * qblk, (2 * j + 1) * qblk)
        rows_o = slice((2 * j + 1) * qblk, (2 * j + 2) * qblk)
        oc = jnp.where(lo, o_c[rows_e], o_c[rows_o])
        os_ = jnp.where(lo, o_s[rows_e], o_s[rows_o])
        out = gate(0) * oc + gate(1) * os_ + win_ref[:, j * LANES:(j + 1) * LANES]
        o_ref[:, j * LANES:(j + 1) * LANES] = out.astype(o_ref.dtype)


def _nsa_cmp_sel(qb, kcmp, vcmp, ks, vs, gates, win, overlap, expand, batch, seq, qblk, n_sb, n_sel, tk_sel):
    n = qb.shape[0]
    nq = seq // qblk
    n_cmp = kcmp.shape[2]
    gw = GQA_REP * HEAD_DIM
    rows = GQA_REP * qblk
    kern = functools.partial(_nsa_kernel, qblk=qblk, seq=seq, n_sb=n_sb, n_sel=n_sel, tk_sel=tk_sel)
    return pl.pallas_call(
        kern,
        grid=(batch, NSA_KV_HEADS, nq),
        in_specs=[pl.BlockSpec((qblk, gw), lambda b, g, i: (b * nq + i, g)),
                  pl.BlockSpec((1, 1, n_cmp, LANES), lambda b, g, i: (b, g, 0, 0)),
                  pl.BlockSpec((1, 1, n_cmp, LANES), lambda b, g, i: (b, g, 0, 0)),
                  pl.BlockSpec((seq, LANES), lambda b, g, i: (b, g)),
                  pl.BlockSpec((seq, LANES), lambda b, g, i: (b, g)),
                  pl.BlockSpec((qblk, LANES), lambda b, g, i: (b * nq + i, g)),
                  pl.BlockSpec((qblk, gw), lambda b, g, i: (b * nq + i, g)),
                  _full(overlap.shape), _full(expand.shape)],
        out_specs=pl.BlockSpec((qblk, gw), lambda b, g, i: (b * nq + i, g)),
        out_shape=jax.ShapeDtypeStruct((n, NSA_Q_HEADS * HEAD_DIM), BF16),
        scratch_shapes=[pltpu.VMEM((qblk, tk_sel), F32),
                        pltpu.VMEM((rows, tk_sel), F32), pltpu.VMEM((rows, tk_sel), BF16),
                        pltpu.VMEM((rows, LANES), F32), pltpu.VMEM((rows, LANES), F32),
                        pltpu.VMEM((rows, LANES), F32), pltpu.VMEM((rows, LANES), F32)],
        compiler_params=_params("parallel", "parallel", "arbitrary"),
        name="nsa_cmp_sel",
    )(qb, kcmp, vcmp, ks, vs, gates, win, overlap, expand)


def _oproj_ln_kernel(*refs, n_parts, alpha):
    o_refs = refs[:n_parts]
    w_refs = refs[n_parts:2 * n_parts]
    x_ref, g_ref, b_ref, y_ref, ypk_ref = refs[2 * n_parts:]
    h = jnp.dot(o_refs[0][...], w_refs[0][...], preferred_element_type=F32)
    for o_ref, w_ref in zip(o_refs[1:], w_refs[1:]):
        h = h + jnp.dot(o_ref[...], w_ref[...], preferred_element_type=F32)
    y = _layer_norm_rows(alpha * x_ref[...] + h, g_ref[...], b_ref[...])
    y_ref[...] = y
    ypk_ref[...] = _pack_halves(y)


def _oproj_ln(o_parts, w_parts, x2, g, b, alpha, tm):
    n, d = x2.shape
    kern = functools.partial(_oproj_ln_kernel, n_parts=len(o_parts), alpha=alpha)
    in_specs = ([pl.BlockSpec((tm, o.shape[1]), lambda i: (i, 0)) for o in o_parts]
                + [_full(w.shape) for w in w_parts]
                + [pl.BlockSpec((tm, d), lambda i: (i, 0)), _full((1, d)), _full((1, d))])
    return pl.pallas_call(
        kern,
        grid=(n // tm,),
        in_specs=in_specs,
        out_specs=[pl.BlockSpec((tm, d), lambda i: (i, 0)), pl.BlockSpec((tm, d // 2), lambda i: (i, 0))],
        out_shape=[jax.ShapeDtypeStruct((n, d), F32), jax.ShapeDtypeStruct((n, d // 2), U32)],
        compiler_params=_params("parallel"),
        name="oproj_ln",
    )(*o_parts, *w_parts, x2, g, b)


def _rms_rows(v, g):
    return v * lax.rsqrt(jnp.mean(v * v, axis=-1, keepdims=True) + RMS_EPS) * g


def _mla_proj_kernel(x_ref, win_ref, qn_ref, kvn_ref, wuq_ref, wuk_ref, wuv_ref, cos_ref, sin_ref,
                     q_ref, k_ref, v_ref):
    xb = x_ref[...].astype(BF16)
    cos = cos_ref[...]
    sin = sin_ref[...]
    lat = jnp.dot(xb, win_ref[...], preferred_element_type=F32)
    cq = _rms_rows(lat[:, :MLA_Q_LORA], qn_ref[...]).astype(BF16)
    ckv = _rms_rows(lat[:, MLA_Q_LORA:MLA_Q_LORA + MLA_KV_LORA], kvn_ref[...]).astype(BF16)
    kr = _rope_slab(lat[:, MLA_Q_LORA + MLA_KV_LORA:], cos, sin, MLA_ROPE // 2)
    q = jnp.dot(cq, wuq_ref[...], preferred_element_type=F32)
    k = jnp.dot(ckv, wuk_ref[...], preferred_element_type=F32)
    for h in range(MLA_HEADS):
        sl = slice(h * LANES, (h + 1) * LANES)
        q_ref[:, sl] = _rope_slab(q[:, sl], cos, sin, MLA_ROPE // 2).astype(BF16)
        k_ref[:, sl] = (k[:, sl] + kr).astype(BF16)
    v_ref[...] = jnp.dot(ckv, wuv_ref[...], preferred_element_type=F32).astype(BF16)


def _mla_proj(x2, w_in, qn, kvn, wuq, wuk, wuv, cos, sin, seq, tm):
    n, d = x2.shape
    sblk = seq // tm
    hw = MLA_HEADS * LANES
    vw = MLA_HEADS * MLA_V
    return pl.pallas_call(
        _mla_proj_kernel,
        grid=(n // tm,),
        in_specs=[pl.BlockSpec((tm, d), lambda i: (i, 0)),
                  _full(w_in.shape), _full(qn.shape), _full(kvn.shape),
                  _full(wuq.shape), _full(wuk.shape), _full(wuv.shape),
                  pl.BlockSpec((tm, LANES), lambda i: (i % sblk, 0)),
                  pl.BlockSpec((tm, LANES), lambda i: (i % sblk, 0))],
        out_specs=[pl.BlockSpec((tm, hw), lambda i: (i, 0)),
                   pl.BlockSpec((tm, hw), lambda i: (i, 0)),
                   pl.BlockSpec((tm, vw), lambda i: (i, 0))],
        out_shape=[jax.ShapeDtypeStruct((n, hw), BF16), jax.ShapeDtypeStruct((n, hw), BF16),
                   jax.ShapeDtypeStruct((n, vw), BF16)],
        compiler_params=_params("parallel"),
        name="mla_proj",
    )(x2, w_in, qn, kvn, wuq, wuk, wuv, cos, sin)


FLASH_ROWS = 32
LOG2_E = 1.4426950408889634


def _flash_init(m_scr, l_scr, acc_scr):
    m_scr[...] = jnp.full(m_scr.shape, -jnp.inf, F32)
    l_scr[...] = jnp.zeros(l_scr.shape, F32)
    acc_scr[...] = jnp.zeros(acc_scr.shape, F32)


def _flash_step(q, k, v, bias_fn, exp_scale, s_scr, p_scr, m_scr, l_scr, a_scr, acc_scr):
    rows, keys = q.shape[0], k.shape[0]
    n_chunks = rows // FLASH_ROWS
    slabs = keys // LANES
    s_scr[...] = lax.dot_general(q, k, NT_DIMS, preferred_element_type=F32)
    for c in range(n_chunks):
        r = slice(c * FLASH_ROWS, (c + 1) * FLASH_ROWS)
        s = s_scr[r, :]
        if bias_fn is not None:
            s = s + bias_fn(c)
            s_scr[r, :] = s
        mx = s[:, 0:LANES]
        for j in range(1, slabs):
            mx = jnp.maximum(mx, s[:, j * LANES:(j + 1) * LANES])
        a_scr[r, :] = mx
    m_old = m_scr[...]
    m_new = jnp.maximum(m_old, jnp.broadcast_to(jnp.max(a_scr[...], axis=-1, keepdims=True), m_old.shape))
    m_scr[...] = m_new
    a_scr[...] = jnp.exp2((m_old - m_new) * exp_scale)
    for c in range(n_chunks):
        r = slice(c * FLASH_ROWS, (c + 1) * FLASH_ROWS)
        m_rows = m_scr[r, :]
        ps = None
        for j in range(slabs):
            cols = slice(j * LANES, (j + 1) * LANES)
            p = jnp.exp2((s_scr[r, cols] - m_rows) * exp_scale)
            p_scr[r, cols] = p.astype(p_scr.dtype)
            ps = p if ps is None else ps + p
        l_scr[r, :] = a_scr[r, :] * l_scr[r, :] + ps
    acc_scr[...] = a_scr[...] * acc_scr[...] + jnp.dot(p_scr[...], v, preferred_element_type=F32)


def _mla_attn_kernel(q_ref, k_ref, v_ref, o_ref, s_scr, p_scr, m_scr, l_scr, a_scr, acc_scr, *, tq, scale):
    i = pl.program_id(2)
    lane = lax.broadcasted_iota(I32, (tq, LANES), 1)
    lo = lane < MLA_V
    for e in range(2):
        _flash_init(m_scr.at[e], l_scr.at[e], acc_scr.at[e])

    def diag_bias(c):
        row = c * FLASH_ROWS + lax.broadcasted_iota(I32, (FLASH_ROWS, tq), 0)
        col = lax.broadcasted_iota(I32, (FLASH_ROWS, tq), 1)
        return jnp.where(col <= row, 0.0, -jnp.inf).astype(F32)

    def tile(j, bias_fn):
        k0 = pl.multiple_of(j * tq, tq)
        v = v_ref[pl.ds(k0, tq), :]
        for e in range(2):
            _flash_step(q_ref[:, e * LANES:(e + 1) * LANES], k_ref[pl.ds(k0, tq), e * LANES:(e + 1) * LANES], v,
                        bias_fn, scale * LOG2_E, s_scr.at[e], p_scr.at[e], m_scr.at[e], l_scr.at[e],
                        a_scr.at[e], acc_scr.at[e])

    def body(j, carry):
        tile(j, None)
        return carry

    lax.fori_loop(0, i, body, 0)
    tile(i, diag_bias)
    o0 = acc_scr[0] / jnp.sum(l_scr[0], axis=-1, keepdims=True)
    o1 = acc_scr[1] / jnp.sum(l_scr[1], axis=-1, keepdims=True)
    o_ref[...] = jnp.where(lo, o0, o1).astype(o_ref.dtype)


def _mla_attention(q, k, v, batch, seq, tq):
    n = q.shape[0]
    nq = seq // tq
    kern = functools.partial(_mla_attn_kernel, tq=tq, scale=(MLA_NOPE + MLA_ROPE) ** -0.5)
    return pl.pallas_call(
        kern,
        grid=(batch, MLA_HEADS // 2, nq),
        in_specs=[pl.BlockSpec((tq, 2 * LANES), lambda b, p, i: (b * nq + i, p)),
                  pl.BlockSpec((seq, 2 * LANES), lambda b, p, i: (b, p)),
                  pl.BlockSpec((seq, LANES), lambda b, p, i: (b, p))],
        out_specs=pl.BlockSpec((tq, LANES), lambda b, p, i: (b * nq + i, p)),
        out_shape=jax.ShapeDtypeStruct((n, MLA_HEADS * MLA_V), BF16),
        scratch_shapes=[pltpu.VMEM((2, tq, tq), F32), pltpu.VMEM((2, tq, tq), BF16),
                        pltpu.VMEM((2, tq, LANES), F32), pltpu.VMEM((2, tq, LANES), F32), pltpu.VMEM((2, tq, LANES), F32),
                        pltpu.VMEM((2, tq, LANES), F32)],
        compiler_params=_params("parallel", "parallel", "arbitrary"),
        name="mla_attn",
    )(q, k, v)


def _router_kernel(x_ref, rw_ref, rb_ref, eidx_ref, gate_ref, pos_ref, cnt_ref, carry_ref, *, steps_per_chunk):
    i = pl.program_id(0)
    tm = x_ref.shape[0]
    per_group = N_EXPERTS // MOE_GROUPS

    @pl.when(i % steps_per_chunk == 0)
    def _():
        carry_ref[...] = jnp.zeros_like(carry_ref)

    logits = jnp.dot(x_ref[...], rw_ref[...], preferred_element_type=F32, precision=lax.Precision.HIGHEST)
    lt = jnp.transpose(logits)[:N_EXPERTS]
    s = _sigmoid(lt)
    sb = s + rb_ref[...]

    g3 = sb.reshape(MOE_GROUPS, per_group, tm)
    idx3 = lax.broadcasted_iota(I32, g3.shape, 1).astype(F32)
    m1 = jnp.max(g3, axis=1, keepdims=True)
    first = jnp.min(jnp.where(g3 == m1, idx3, float(per_group)), axis=1, keepdims=True)
    m2 = jnp.max(jnp.where(idx3 == first, -jnp.inf, g3), axis=1, keepdims=True)
    gscore = (m1 + m2).reshape(MOE_GROUPS, tm)
    gsel = _rank_desc(gscore, MOE_GROUPS, 0) < MOE_TOPK_GROUPS
    gsel3 = jnp.where(gsel, 1.0, 0.0).astype(F32).reshape(MOE_GROUPS, 1, tm)
    masked = jnp.where(gsel3 > 0.5, g3, -jnp.inf).reshape(N_EXPERTS, tm)
    rank = _rank_desc(masked, N_EXPERTS, 0)
    sel = rank < MOE_TOP_K
    gate = jnp.where(sel, s, 0.0)
    gate = gate / jnp.sum(gate, axis=0, keepdims=True) * MOE_ROUTED_SCALE

    sel_b = jnp.where(sel, 1.0, 0.0).astype(BF16)
    r_i = lax.broadcasted_iota(I32, (tm, tm), 0)
    c_i = lax.broadcasted_iota(I32, (tm, tm), 1)
    tri = jnp.where(r_i < c_i, 1.0, 0.0).astype(BF16)
    carry = carry_ref[:, 0:1]
    before = jnp.dot(sel_b, tri, preferred_element_type=F32) + carry
    carry_new = carry + jnp.sum(sel_b.astype(F32), axis=1, keepdims=True)
    carry_ref[...] = jnp.broadcast_to(carry_new, carry_ref.shape)
    cnt_ref[0] = jnp.broadcast_to(carry_new, carry_ref.shape).astype(I32)

    e_iota = lax.broadcasted_iota(I32, (N_EXPERTS, tm), 0).astype(F32)
    rows_e, rows_g, rows_p = [], [], []
    for r in range(MOE_TOP_K):
        hit = rank == r
        rows_e.append(jnp.sum(jnp.where(hit, e_iota, 0.0), axis=0, keepdims=True))
        rows_g.append(jnp.sum(jnp.where(hit, gate, 0.0), axis=0, keepdims=True))
        rows_p.append(jnp.sum(jnp.where(hit, before, 0.0), axis=0, keepdims=True))
    pad = 8 - MOE_TOP_K
    eidx_ref[...] = jnp.concatenate(rows_e + [jnp.zeros((pad, tm), F32)], axis=0).astype(I32)
    gate_ref[...] = jnp.concatenate(rows_g + [jnp.zeros((pad, tm), F32)], axis=0)
    pos_ref[...] = jnp.concatenate(rows_p + [jnp.zeros((pad, tm), F32)], axis=0).astype(I32)


def _router(x2, rw, rb, n_chunk, tm):
    n, d = x2.shape
    steps = n // tm
    spc = steps // n_chunk
    kern = functools.partial(_router_kernel, steps_per_chunk=spc)
    row8 = pl.BlockSpec((8, tm), lambda i: (0, i))
    return pl.pallas_call(
        kern,
        grid=(steps,),
        in_specs=[pl.BlockSpec((tm, d), lambda i: (i, 0)), _full(rw.shape), _full(rb.shape)],
        out_specs=[row8, row8, row8, pl.BlockSpec((1, N_EXPERTS, LANES), lambda i: (i // spc, 0, 0))],
        out_shape=[jax.ShapeDtypeStruct((8, n), I32), jax.ShapeDtypeStruct((8, n), F32),
                   jax.ShapeDtypeStruct((8, n), I32),
                   jax.ShapeDtypeStruct((n_chunk, N_EXPERTS, LANES), I32)],
        scratch_shapes=[pltpu.VMEM((N_EXPERTS, LANES), F32)],
        compiler_params=_params("arbitrary"),
        name="moe_router",
    )(x2, rw, rb)


def _dest_kernel(tab_ref, eidx_ref, pos_ref, dest_ref, *, steps_per_chunk):
    chunk = pl.program_id(0) // steps_per_chunk
    eidx = eidx_ref[...]
    dest = pos_ref[...]
    for e in range(N_EXPERTS):
        dest = dest + jnp.where(eidx == e, tab_ref[chunk * N_EXPERTS + e], 0)
    dest_ref[...] = dest


def _dest_rows(pad_starts, eidx, pos, n_chunk, tm):
    n = eidx.shape[1]
    steps = n // tm
    row8 = lambda i, tab: (0, i)
    return pl.pallas_call(
        functools.partial(_dest_kernel, steps_per_chunk=steps // n_chunk),
        grid_spec=pltpu.PrefetchScalarGridSpec(
            num_scalar_prefetch=1, grid=(steps,),
            in_specs=[pl.BlockSpec((8, tm), row8), pl.BlockSpec((8, tm), row8)],
            out_specs=pl.BlockSpec((8, tm), row8)),
        out_shape=jax.ShapeDtypeStruct((8, n), I32),
        compiler_params=_params("parallel"),
        name="moe_dest",
    )(pad_starts, eidx, pos)


SC_ROWS = 64
SC_WORKERS = 32


def _sc_mesh():
    return plsc.VectorSubcoreMesh(core_axis_name="c", subcore_axis_name="s")


def _sc_worker_base(per_worker):
    return (lax.axis_index("s") * 2 + lax.axis_index("c")) * per_worker


def _sc_scatter_rows(x, dest, n_out):
    n, d = x.shape
    slots = dest.shape[0] // n
    per_w = n // SC_WORKERS

    def body(x_hbm, i_hbm, o_hbm, idx_v, rows_v, sem):
        base = _sc_worker_base(per_w)

        @pl.loop(0, per_w // SC_ROWS)
        def _(c):
            off = pl.multiple_of(base + c * SC_ROWS, 8)
            pltpu.sync_copy(x_hbm.at[pl.ds(off, SC_ROWS)], rows_v)
            for k in range(slots):
                pltpu.sync_copy(i_hbm.at[pl.ds(k * n + off, SC_ROWS)], idx_v)
                pltpu.async_copy(rows_v, o_hbm.at[idx_v], sem).wait()

    return pl.kernel(
        body, out_type=jax.ShapeDtypeStruct((n_out, d), x.dtype), mesh=_sc_mesh(),
        scratch_types=[pltpu.VMEM((SC_ROWS,), I32), pltpu.VMEM((SC_ROWS, d), x.dtype), pltpu.SemaphoreType.DMA],
        name="moe_dispatch_sc")(x, dest)


def _sc_gather_rows(table, idx):
    d = table.shape[1]
    r = idx.shape[0]
    per_w = r // SC_WORKERS

    def body(t_hbm, i_hbm, o_hbm, idx_v, rows_v, sem):
        base = _sc_worker_base(per_w)

        @pl.loop(0, per_w // SC_ROWS)
        def _(c):
            off = pl.multiple_of(base + c * SC_ROWS, 8)
            pltpu.sync_copy(i_hbm.at[pl.ds(off, SC_ROWS)], idx_v)
            pltpu.async_copy(t_hbm.at[idx_v], rows_v, sem).wait()
            pltpu.sync_copy(rows_v, o_hbm.at[pl.ds(off, SC_ROWS)])

    return pl.kernel(
        body, out_type=jax.ShapeDtypeStruct((r, d), table.dtype), mesh=_sc_mesh(),
        scratch_types=[pltpu.VMEM((SC_ROWS,), I32), pltpu.VMEM((SC_ROWS, d), table.dtype), pltpu.SemaphoreType.DMA],
        name="moe_combine_sc")(table, idx)


def _experts_kernel(be_ref, nv_ref, xs_ref, wgu_ref, wd_ref, y_ref):
    i = pl.program_id(0)

    @pl.when(i < nv_ref[0])
    def _():
        x_lo, x_hi = _unpack_halves(xs_ref[...])
        half = x_lo.shape[1]
        gu = (jnp.dot(x_lo.astype(BF16), wgu_ref[0, :half, :], preferred_element_type=F32)
              + jnp.dot(x_hi.astype(BF16), wgu_ref[0, half:, :], preferred_element_type=F32))
        h = _silu(gu[:, :D_EXPERT]) * gu[:, D_EXPERT:]
        y_ref[...] = _pack_halves(jnp.dot(h.astype(BF16), wd_ref[0], preferred_element_type=F32))

    @pl.when(i >= nv_ref[0])
    def _():
        y_ref[...] = jnp.zeros_like(y_ref)


def _experts(blk_exp, n_valid, xs, wgu, wd, tb):
    rows, dp = xs.shape
    d = 2 * dp
    grid_spec = pltpu.PrefetchScalarGridSpec(
        num_scalar_prefetch=2,
        grid=(rows // tb,),
        in_specs=[pl.BlockSpec((tb, dp), lambda i, be, nv: (i, 0)),
                  pl.BlockSpec((1, d, 2 * D_EXPERT), lambda i, be, nv: (be[i], 0, 0)),
                  pl.BlockSpec((1, D_EXPERT, d), lambda i, be, nv: (be[i], 0, 0))],
        out_specs=pl.BlockSpec((tb, dp), lambda i, be, nv: (i, 0)),
    )
    return pl.pallas_call(
        _experts_kernel,
        grid_spec=grid_spec,
        out_shape=jax.ShapeDtypeStruct((rows, dp), U32),
        compiler_params=_params("arbitrary"),
        name="moe_experts",
    )(blk_exp, n_valid, xs, wgu, wd)


def _combine_ln_kernel(x_ref, yg_ref, gt_ref, wgu_ref, wd_ref, g_ref, b_ref, o_ref, *, alpha):
    x = x_ref[...]
    gu = jnp.dot(x.astype(BF16), wgu_ref[...], preferred_element_type=F32)
    dsh = gu.shape[1] // 2
    h = _silu(gu[:, :dsh]) * gu[:, dsh:]
    f = jnp.dot(h.astype(BF16), wd_ref[...], preferred_element_type=F32)
    r_lo = r_hi = None
    for k in range(MOE_TOP_K):
        y_lo, y_hi = _unpack_halves(yg_ref[k])
        gate = gt_ref[:, k:k + 1]
        r_lo = gate * y_lo if r_lo is None else r_lo + gate * y_lo
        r_hi = gate * y_hi if r_hi is None else r_hi + gate * y_hi
    f = f + jnp.concatenate([r_lo, r_hi], axis=1)
    o_ref[...] = _layer_norm_rows(alpha * x + f, g_ref[...], b_ref[...])


def _combine_ln(x2, yg, gates_t, wgu, wd, g, b, alpha, tm):
    n, d = x2.shape
    kern = functools.partial(_combine_ln_kernel, alpha=alpha)
    return pl.pallas_call(
        kern,
        grid=(n // tm,),
        in_specs=[pl.BlockSpec((tm, d), lambda i: (i, 0)),
                  pl.BlockSpec((MOE_TOP_K, tm, d // 2), lambda i: (0, i, 0)),
                  pl.BlockSpec((tm, 8), lambda i: (i, 0)),
                  _full(wgu.shape), _full(wd.shape), _full((1, d)), _full((1, d))],
        out_specs=pl.BlockSpec((tm, d), lambda i: (i, 0)),
        out_shape=jax.ShapeDtypeStruct((n, d), F32),
        compiler_params=_params("parallel"),
        name="moe_combine_ln",
    )(x2, yg, gates_t, wgu, wd, g, b)


def _rope_tables(positions, dim, lane_offset, period):
    half = dim // 2
    inv_freq = ROPE_THETA ** (-2.0 * jnp.arange(half, dtype=jnp.float32) / dim)
    ang = positions.astype(jnp.float32)[:, None] * inv_freq[None, :]
    cos_h, sin_h = jnp.cos(ang), jnp.sin(ang)
    lanes = np.arange(LANES)
    rel = (lanes - lane_offset) % period
    active = (lanes >= lane_offset) & (rel < dim)
    fidx = np.where(active, rel % half, 0)
    sign = np.where(rel < half, -1.0, 1.0)
    cos = jnp.where(active[None, :], cos_h[:, fidx], 1.0)
    sin = jnp.where(active[None, :], sin_h[:, fidx] * sign[None, :], 0.0)
    return cos.astype(F32), sin.astype(F32)


def _gate_slabs(w):
    k = w.shape[0]
    g = w.reshape(k, 3, NSA_KV_HEADS, GQA_REP).transpose(0, 2, 1, 3).reshape(k, NSA_KV_HEADS, 3 * GQA_REP)
    return jnp.pad(g, ((0, 0), (0, 0), (0, LANES - 3 * GQA_REP))).reshape(k, NSA_KV_HEADS * LANES)


def _dup_heads(w):
    a, b = w[:, :HEAD_DIM], w[:, HEAD_DIM:]
    return jnp.concatenate([a, a, b, b], axis=1)


def _moe_layer(x1_and_packed, rw, rb, wg, wu, wd, shg, shu, shd, ln_g, ln_b, alpha, tm):
    x1, x1_pk = x1_and_packed
    n, d = x1.shape
    n_chunk = 1
    tb = MOE_ROW_BLOCK
    rw_p = jnp.pad(rw, ((0, 0), (0, LANES - N_EXPERTS)))
    eidx, gate, pos, cnt = _router(x1, rw_p, rb.reshape(N_EXPERTS, 1), n_chunk, tm)
    counts = cnt[:, :, 0].reshape(-1)
    padded = (counts + tb - 1) // tb * tb
    pad_ends = jnp.cumsum(padded)
    pad_starts = (pad_ends - padded).astype(I32)
    dest = _dest_rows(pad_starts, eidx, pos, n_chunk, tm)[:MOE_TOP_K].reshape(-1)
    n_rows = n * MOE_TOP_K
    n_blk = -(-(n_rows + n_chunk * N_EXPERTS * (tb - 1)) // tb)
    blk_first_row = jnp.arange(n_blk, dtype=I32) * tb
    owner = jnp.sum((pad_ends[None, :] <= blk_first_row[:, None]).astype(I32), axis=1)
    blk_exp = jnp.minimum(owner, n_chunk * N_EXPERTS - 1).astype(I32) % N_EXPERTS
    n_valid = (pad_ends[-1] // tb).astype(I32).reshape(1)

    xs = _sc_scatter_rows(x1_pk, dest, n_blk * tb)
    wgu = jnp.concatenate([wg, wu], axis=-1).astype(BF16)
    y = _experts(blk_exp, n_valid, xs, wgu, wd.astype(BF16), tb)
    yg = _sc_gather_rows(y, dest).reshape(MOE_TOP_K, n, d // 2)
    gates_t = gate.T
    sh_gu = jnp.concatenate([shg, shu], axis=-1).astype(BF16)
    return _combine_ln(x1, yg, gates_t, sh_gu, shd.astype(BF16), ln_g.reshape(1, d), ln_b.reshape(1, d), alpha, tm)


def _swa_nsa_layer(x2, batch, seq, w_in, sinks, pe_k, w_ck1, w_ck2, pe_v, w_cv1, w_cv2, w_out,
                   ln_g, ln_b, alpha, tm):
    n, d = x2.shape
    splits = np.cumsum([512, 128, 128, 512, 128, 128, 128, 128, 128, 128, 24])[:-1]
    qa, ka, va, qb, kc, vc, ksl, vsl, kw, vw, gts = jnp.split(w_in, [int(c) for c in splits], axis=1)
    w_all = jnp.concatenate(
        [qa, _dup_heads(ka), _dup_heads(va), qb, kc, vc, _dup_heads(ksl), _dup_heads(vsl),
         _dup_heads(kw), _dup_heads(vw), _gate_slabs(gts)], axis=1).astype(BF16)
    cos, sin = _rope_tables(jnp.arange(seq), HEAD_DIM, 0, HEAD_DIM)
    (q_a, k_a, v_a, q_b, k_c, v_c, k_s, v_s, k_w, v_w, gates) = _mix0_proj(x2, w_all, cos, sin, seq, tm)

    o_swa = _banded_attention(q_a, k_a, v_a, batch, seq, SWA_WINDOW, 128, sinks=sinks, name="swa")
    o_win = _banded_attention(q_b, k_w, v_w, batch, seq, NSA_WINDOW, 128, gates=gates,
                              gate_col=2 * GQA_REP, out_dtype=F32, name="nsa_win")

    n_chunk = seq // NSA_CMP_STRIDE
    cw = NSA_CMP_STRIDE * NSA_KV_HEADS * HEAD_DIM
    hid = NSA_CMP_HIDDEN

    def expand_w1(w1):
        halves = w1.reshape(2, NSA_CMP_STRIDE, HEAD_DIM, hid)
        out = jnp.zeros((NSA_KV_HEADS, NSA_CMP_STRIDE, NSA_KV_HEADS, HEAD_DIM, 2 * hid), w1.dtype)
        for g in range(NSA_KV_HEADS):
            out = out.at[g, :, g, :, :hid].set(halves[0]).at[g, :, g, :, hid:].set(halves[1])
        return out.reshape(NSA_KV_HEADS, cw, 2 * hid).astype(BF16)

    def pe_rows(pe):
        return jnp.pad(pe.reshape(1, NSA_CMP_LEN * HEAD_DIM), ((0, 15), (0, 0))).astype(BF16)

    cos_c, sin_c = _rope_tables(jnp.arange(n_chunk) * NSA_CMP_STRIDE + NSA_CMP_LEN - 1, HEAD_DIM, 0, HEAD_DIM)
    k_cmp, v_cmp = _compress(
        k_c.reshape(batch * n_chunk, cw), v_c.reshape(batch * n_chunk, cw),
        expand_w1(w_ck1), expand_w1(w_cv1), pe_rows(pe_k), pe_rows(pe_v),
        w_ck1.astype(BF16), w_cv1.astype(BF16),
        jnp.concatenate([w_ck2, w_ck2], axis=1).astype(BF16),
        jnp.concatenate([w_cv2, w_cv2], axis=1).astype(BF16),
        cos_c, sin_c, batch, n_chunk)

    n_sb = seq // NSA_SEL_LEN
    n_sel = min(NSA_N_SEL, n_sb)
    cs = np.arange(n_chunk) * NSA_CMP_STRIDE
    bs = np.arange(n_sb) * NSA_SEL_LEN
    ov = np.clip(np.minimum(cs[:, None] + NSA_CMP_LEN, bs[None, :] + NSA_SEL_LEN)
                 - np.maximum(cs[:, None], bs[None, :]), 0, None) / NSA_CMP_LEN
    overlap = jnp.asarray(np.pad(ov, ((0, 0), (0, LANES - n_sb))).T, F32)
    expand = jnp.asarray(np.arange(LANES)[:, None] == (np.arange(seq)[None, :] // NSA_SEL_LEN), BF16)
    tk_sel = min(512, seq)
    o_nsa = _nsa_cmp_sel(q_b, k_cmp, v_cmp, k_s, v_s, gates, o_win, overlap, expand,
                         batch, seq, 128, n_sb, n_sel, tk_sel)
    w_o = w_out.astype(BF16)
    half = SWA_Q_HEADS * HEAD_DIM
    return _oproj_ln([o_swa, o_nsa], [w_o[:half], w_o[half:]], x2, ln_g.reshape(1, d), ln_b.reshape(1, d),
                     alpha, tm)


def _mla_layer(x2, batch, seq, w_in, q_norm, kv_norm, w_uq, w_ukv, w_out, ln_g, ln_b, alpha, tm):
    n, d = x2.shape
    dq = MLA_NOPE + MLA_ROPE
    w_kr = jnp.zeros((d, LANES), w_in.dtype).at[:, MLA_NOPE:dq].set(w_in[:, MLA_Q_LORA + MLA_KV_LORA:])
    w_in_p = jnp.concatenate([w_in[:, :MLA_Q_LORA + MLA_KV_LORA], w_kr], axis=1).astype(BF16)
    wuq = jnp.pad(w_uq.reshape(MLA_Q_LORA, MLA_HEADS, dq), ((0, 0), (0, 0), (0, LANES - dq)))
    wuq = wuq.reshape(MLA_Q_LORA, MLA_HEADS * LANES).astype(BF16)
    wukv = w_ukv.reshape(MLA_KV_LORA, MLA_HEADS, MLA_NOPE + MLA_V)
    wuk = jnp.pad(wukv[:, :, :MLA_NOPE], ((0, 0), (0, 0), (0, LANES - MLA_NOPE)))
    wuk = wuk.reshape(MLA_KV_LORA, MLA_HEADS * LANES).astype(BF16)
    wuv = wukv[:, :, MLA_NOPE:].reshape(MLA_KV_LORA, MLA_HEADS * MLA_V).astype(BF16)
    cos, sin = _rope_tables(jnp.arange(seq), MLA_ROPE, MLA_NOPE, LANES)
    q, k, v = _mla_proj(x2, w_in_p, q_norm.reshape(1, -1), kv_norm.reshape(1, -1), wuq, wuk, wuv, cos, sin, seq, tm)
    o = _mla_attention(q, k, v, batch, seq, min(512, seq))
    return _oproj_ln([o], [w_out.astype(BF16)], x2, ln_g.reshape(1, d), ln_b.reshape(1, d), alpha, tm)


def kernel(x, swa_nsa_w_in, swa_sinks, nsa_cmp_pe_k, nsa_cmp_k_w1, nsa_cmp_k_w2, nsa_cmp_pe_v, nsa_cmp_v_w1, nsa_cmp_v_w2, swa_nsa_w_out, mla_w_in, mla_q_norm, mla_kv_norm, mla_w_uq, mla_w_ukv, mla_w_out, ln_mix_g, ln_mix_b, ln_ffn_g, ln_ffn_b, router_w, router_bias, expert_w_gate, expert_w_up, expert_w_down, shared_w_gate, shared_w_up, shared_w_down):
    batch, seq, d = x.shape
    depth = ln_mix_g.shape[0]
    alpha = (2 * depth) ** 0.25
    tm = min(512, seq)
    x2 = x.reshape(batch * seq, d)
    for layer in range(depth):
        j = layer // 2
        if layer % 2 == 0:
            x2 = _swa_nsa_layer(x2, batch, seq, swa_nsa_w_in[j], swa_sinks[j], nsa_cmp_pe_k[j],
                                nsa_cmp_k_w1[j], nsa_cmp_k_w2[j], nsa_cmp_pe_v[j], nsa_cmp_v_w1[j],
                                nsa_cmp_v_w2[j], swa_nsa_w_out[j], ln_mix_g[layer], ln_mix_b[layer], alpha, tm)
        else:
            x2 = _mla_layer(x2, batch, seq, mla_w_in[j], mla_q_norm[j], mla_kv_norm[j], mla_w_uq[j],
                            mla_w_ukv[j], mla_w_out[j], ln_mix_g[layer], ln_mix_b[layer], alpha, tm)
        x2 = _moe_layer(x2, router_w[layer], router_bias[layer], expert_w_gate[layer], expert_w_up[layer],
                        expert_w_down[layer], shared_w_gate[layer], shared_w_up[layer], shared_w_down[layer],
                        ln_ffn_g[layer], ln_ffn_b[layer], alpha, tm)
    return x2.reshape(batch, seq, d)
```

```python
import functools
import math

import numpy as np
import jax
import jax.numpy as jnp
from jax import lax
from jax.experimental import pallas as pl
from jax.experimental.pallas import tpu as pltpu
from jax.experimental.pallas import tpu_sc as plsc

F32 = jnp.float32
BF16 = jnp.bfloat16
I32 = jnp.int32

LANES = 128
VMEM_LIMIT = 48 * 1024 * 1024

ROPE_THETA = 10000.0
LN_EPS = 1e-5
RMS_EPS = 1e-6
NEG_BIG = -1e30
FORCE_SCORE = 1e4

HEAD_DIM = 64
SWA_Q_HEADS = 8
SWA_KV_HEADS = 2
SWA_WINDOW = 128
NSA_Q_HEADS = 8
NSA_KV_HEADS = 2
NSA_CMP_LEN = 32
NSA_CMP_STRIDE = 16
NSA_CMP_HIDDEN = 128
NSA_SEL_LEN = 64
NSA_N_SEL = 16
NSA_WINDOW = 512
GQA_REP = 4

MLA_HEADS = 16
MLA_NOPE = 64
MLA_ROPE = 32
MLA_V = 64
MLA_Q_LORA = 384
MLA_KV_LORA = 256

N_EXPERTS = 64
MOE_GROUPS = 8
MOE_TOPK_GROUPS = 4
MOE_TOP_K = 6
D_EXPERT = 256
MOE_ROUTED_SCALE = 2.5
MOE_ROW_BLOCK = 512

NT_DIMS = (((1,), (1,)), ((), ()))


def _params(*sem):
    return pltpu.CompilerParams(dimension_semantics=sem, vmem_limit_bytes=VMEM_LIMIT)


def _full(shape):
    nd = len(shape)
    return pl.BlockSpec(shape, lambda *_: (0,) * nd)


def _rope_slab(y, cos, sin, half):
    lane = lax.broadcasted_iota(I32, y.shape, 1)
    first = (lane % (2 * half)) < half
    rot = jnp.where(first, pltpu.roll(y, LANES - half, 1), pltpu.roll(y, half, 1))
    return y * cos + rot * sin


def _layer_norm_rows(v, g, b):
    mu = jnp.mean(v, axis=-1, keepdims=True)
    vc = v - mu
    var = jnp.mean(vc * vc, axis=-1, keepdims=True)
    return vc * lax.rsqrt(var + LN_EPS) * g + b


def _silu(v):
    return v * (1.0 / (1.0 + jnp.exp(-v)))


def _sigmoid(v):
    return 1.0 / (1.0 + jnp.exp(-v))


U32 = jnp.uint32


def _pack_halves(v):
    w = v.shape[1] // 2
    lo = pltpu.bitcast(v[:, :w].astype(BF16).astype(F32), U32)
    hi = pltpu.bitcast(v[:, w:].astype(BF16).astype(F32), U32)
    return (lo >> 16) | (hi & jnp.uint32(0xFFFF0000))


def _unpack_halves(u):
    lo = pltpu.bitcast(u << 16, F32)
    hi = pltpu.bitcast(u & jnp.uint32(0xFFFF0000), F32)
    return lo, hi


def _stack_group_queries(q_ref, group, rep):
    lane = lax.broadcasted_iota(I32, (q_ref.shape[0], LANES), 1)
    keep_lo = jnp.where(lane < HEAD_DIM, 1.0, 0.0).astype(q_ref.dtype)
    keep_hi = jnp.where(lane < HEAD_DIM, 0.0, 1.0).astype(q_ref.dtype)
    parts = []
    for r in range(rep):
        h = group * rep + r
        slab = q_ref[:, (h // 2) * LANES:(h // 2 + 1) * LANES]
        parts.append(slab * (keep_lo if h % 2 == 0 else keep_hi))
    return jnp.concatenate(parts, axis=0)


def _rank_desc(score, n_valid, axis):
    idx = lax.broadcasted_iota(I32, score.shape, axis)
    rank = jnp.zeros(score.shape, I32)
    for j in range(n_valid):
        other = lax.slice_in_dim(score, j, j + 1, axis=axis)
        ahead = jnp.where(other > score, 1, jnp.where(other == score, jnp.where(idx > j, 1, 0), 0))
        rank = rank + ahead
    return rank


_MIX0_OUTS = (
    ("qa", 512, "rope_q", BF16), ("ka", 256, "rope", BF16), ("va", 256, "plain", BF16),
    ("qb", 512, "rope_q", BF16), ("kc", 128, "plain", BF16), ("vc", 128, "plain", BF16),
    ("ks", 256, "rope", BF16), ("vs", 256, "plain", BF16),
    ("kw", 256, "rope", BF16), ("vw", 256, "plain", BF16),
    ("gates", 256, "sigmoid", F32),
)


def _mix0_proj_kernel(x_ref, w_ref, cos_ref, sin_ref, *out_refs):
    xb = x_ref[...].astype(BF16)
    cos = cos_ref[...]
    sin = sin_ref[...]
    off = 0
    for (_, width, kind, _), o_ref in zip(_MIX0_OUTS, out_refs):
        y = jnp.dot(xb, w_ref[:, off:off + width], preferred_element_type=F32)
        for c in range(width // LANES):
            yc = y[:, c * LANES:(c + 1) * LANES]
            if kind in ("rope", "rope_q"):
                yc = _rope_slab(yc, cos, sin, HEAD_DIM // 2)
            if kind == "rope_q":
                yc = yc * (HEAD_DIM ** -0.5)
            if kind == "sigmoid":
                yc = _sigmoid(yc)
            o_ref[:, c * LANES:(c + 1) * LANES] = yc.astype(o_ref.dtype)
        off += width


def _mix0_proj(x2, w_all, cos, sin, seq, tm):
    n = x2.shape[0]
    d = x2.shape[1]
    wtot = w_all.shape[1]
    sblk = seq // tm
    return pl.pallas_call(
        _mix0_proj_kernel,
        grid=(n // tm,),
        in_specs=[pl.BlockSpec((tm, d), lambda i: (i, 0)),
                  _full((d, wtot)),
                  pl.BlockSpec((tm, LANES), lambda i: (i % sblk, 0)),
                  pl.BlockSpec((tm, LANES), lambda i: (i % sblk, 0))],
        out_specs=[pl.BlockSpec((tm, w), lambda i: (i, 0)) for _, w, _, _ in _MIX0_OUTS],
        out_shape=[jax.ShapeDtypeStruct((n, w), dt) for _, w, _, dt in _MIX0_OUTS],
        compiler_params=_params("parallel"),
        name="mix0_proj",
    )(x2, w_all, cos, sin)


def _banded_kernel(*refs, window, blk, seq, has_sink, gate_col):
    if has_sink:
        sink_ref, refs = refs[0], refs[1:]
    q_ref, k_ref, v_ref = refs[:3]
    g_ref = refs[3] if gate_col is not None else None
    o_ref, bias_scr, s_scr, p_scr, m_scr, l_scr, a_scr, acc_scr = refs[-8:]
    i = pl.program_id(1)
    start = i * blk
    span = min(window + blk, seq)
    k0 = pl.multiple_of(jnp.maximum(start - window, 0), LANES)
    tq = start + lax.broadcasted_iota(I32, (blk, span), 0)
    tk = k0 + lax.broadcasted_iota(I32, (blk, span), 1)
    diff = tq - tk
    bias_scr[...] = jnp.where(diff >= 0, jnp.where(diff < window, 0.0, -jnp.inf), -jnp.inf).astype(F32)
    chunks_per_head = blk // FLASH_ROWS

    def bias_fn(c):
        r0 = (c % chunks_per_head) * FLASH_ROWS
        return bias_scr[r0:r0 + FLASH_ROWS, :]

    lane = lax.broadcasted_iota(I32, (blk, LANES), 1)
    lo = lane < HEAD_DIM
    n_groups = q_ref.shape[1] // (GQA_REP * HEAD_DIM)
    for g in range(n_groups):
        _flash_init(m_scr.at[g], l_scr.at[g], acc_scr.at[g])
        _flash_scores(_stack_group_queries(q_ref, g, GQA_REP), k_ref[pl.ds(k0, span), g * LANES:(g + 1) * LANES],
                      s_scr.at[g])
    for g in range(n_groups):
        _flash_max(bias_fn, LOG2_E, s_scr.at[g], m_scr.at[g], a_scr.at[g])
    for g in range(n_groups):
        _flash_weights(v_ref[pl.ds(k0, span), g * LANES:(g + 1) * LANES], LOG2_E, s_scr.at[g], p_scr.at[g],
                       m_scr.at[g], l_scr.at[g], a_scr.at[g], acc_scr.at[g])
    for g in range(n_groups):
        m = m_scr[g]
        l = jnp.sum(l_scr[g], axis=-1, keepdims=True)
        o = acc_scr[g]
        if has_sink:
            sink = jnp.concatenate(
                [jnp.full((blk, LANES), sink_ref[g * GQA_REP + r], F32) for r in range(GQA_REP)], axis=0)
            m_all = jnp.maximum(m, sink)
            scale_old = jnp.exp2((m - m_all) * LOG2_E)
            o = o * scale_old
            l = l * scale_old[:, 0:1] + jnp.exp2((sink - m_all) * LOG2_E)[:, 0:1]
        o = o / l
        for j in range(GQA_REP // 2):
            slab = jnp.where(lo, o[(2 * j) * blk:(2 * j + 1) * blk], o[(2 * j + 1) * blk:(2 * j + 2) * blk])
            if gate_col is not None:
                c = g * LANES + gate_col + 2 * j
                gate = jnp.where(lo, g_ref[:, c:c + 1], g_ref[:, c + 1:c + 2])
                slab = slab * gate
            col = (g * (GQA_REP // 2) + j) * LANES
            o_ref[:, col:col + LANES] = slab.astype(o_ref.dtype)


def _banded_attention(q, k, v, batch, seq, window, blk, sinks=None, gates=None, gate_col=None,
                      out_dtype=None, name="banded"):
    n, qw = q.shape
    kw = k.shape[1]
    nblk = seq // blk
    span = min(window + blk, seq)
    groups = qw // (GQA_REP * HEAD_DIM)
    rows = GQA_REP * blk
    kern = functools.partial(_banded_kernel, window=window, blk=blk, seq=seq,
                             has_sink=sinks is not None, gate_col=gate_col)
    in_specs = []
    args = []
    if sinks is not None:
        in_specs.append(pl.BlockSpec(memory_space=pltpu.SMEM))
        args.append(sinks)
    in_specs += [pl.BlockSpec((blk, qw), lambda b, i: (b * nblk + i, 0)),
                 pl.BlockSpec((seq, kw), lambda b, i: (b, 0)),
                 pl.BlockSpec((seq, kw), lambda b, i: (b, 0))]
    args += [q, k, v]
    if gates is not None:
        in_specs.append(pl.BlockSpec((blk, gates.shape[1]), lambda b, i: (b * nblk + i, 0)))
        args.append(gates)
    return pl.pallas_call(
        kern,
        grid=(batch, nblk),
        in_specs=in_specs,
        out_specs=pl.BlockSpec((blk, qw), lambda b, i: (b * nblk + i, 0)),
        out_shape=jax.ShapeDtypeStruct((n, qw), BF16 if out_dtype is None else out_dtype),
        scratch_shapes=[pltpu.VMEM((blk, span), F32),
                        pltpu.VMEM((groups, rows, span), F32), pltpu.VMEM((groups, rows, span), BF16)]
                       + [pltpu.VMEM((groups, rows, LANES), F32)] * 4,
        compiler_params=_params("parallel", "arbitrary"),
        name=name,
    )(*args)


def _gelu_tanh(v):
    return 0.5 * v * (1.0 + jnp.tanh(math.sqrt(2.0 / math.pi) * (v + 0.044715 * (v * v * v))))


def _compress_kernel(tk_ref, tv_ref, w1k_ref, w1v_ref, pek_ref, pev_ref, w1ko_ref, w1vo_ref,
                     w2k_ref, w2v_ref, cos_ref, sin_ref, kc_ref, vc_ref):
    n_chunk = tk_ref.shape[0]
    hid = NSA_CMP_HIDDEN
    for t_ref, w1_ref, pe_ref, w1o_ref, w2_ref, o_ref, rope in (
            (tk_ref, w1k_ref, pek_ref, w1ko_ref, w2k_ref, kc_ref, True),
            (tv_ref, w1v_ref, pev_ref, w1vo_ref, w2v_ref, vc_ref, False)):
        pe_term = jnp.dot(pe_ref[...], w1o_ref[...], preferred_element_type=F32)[0:1, :]
        t = t_ref[...]
        for g in range(NSA_KV_HEADS):
            uv = jnp.dot(t, w1_ref[g], preferred_element_type=F32)
            nxt = pltpu.roll(uv[:, hid:], n_chunk - 1, 0)
            h = _gelu_tanh(uv[:, :hid] + nxt + pe_term)
            c = jnp.dot(h.astype(BF16), w2_ref[...], preferred_element_type=F32)
            if rope:
                c = _rope_slab(c, cos_ref[...], sin_ref[...], HEAD_DIM // 2)
            o_ref[0, g] = c.astype(o_ref.dtype)


def _compress(tk, tv, w1k, w1v, pek, pev, w1ko, w1vo, w2k, w2v, cos, sin, batch, n_chunk):
    width = tk.shape[1]
    out = jax.ShapeDtypeStruct((batch, NSA_KV_HEADS, n_chunk, LANES), BF16)
    ospec = pl.BlockSpec((1, NSA_KV_HEADS, n_chunk, LANES), lambda b: (b, 0, 0, 0))
    return pl.pallas_call(
        _compress_kernel,
        grid=(batch,),
        in_specs=[pl.BlockSpec((n_chunk, width), lambda b: (b, 0)),
                  pl.BlockSpec((n_chunk, width), lambda b: (b, 0)),
                  _full(w1k.shape), _full(w1v.shape), _full(pek.shape), _full(pev.shape),
                  _full(w1ko.shape), _full(w1vo.shape), _full(w2k.shape), _full(w2v.shape),
                  _full(cos.shape), _full(sin.shape)],
        out_specs=[ospec, ospec],
        out_shape=[out, out],
        compiler_params=_params("parallel"),
        name="nsa_compress",
    )(tk, tv, w1k, w1v, pek, pev, w1ko, w1vo, w2k, w2v, cos, sin)


def _nsa_kernel(q_ref, kc_ref, vc_ref, ks_ref, vs_ref, g_ref, win_ref, ov_ref, e_ref, o_ref,
                bias_scr, s_scr, p_scr, m_scr, l_scr, a_scr, acc_scr, *, qblk, seq, n_sb, n_sel, tk_sel):
    i = pl.program_id(2)
    start = i * qblk
    n_cmp = kc_ref.shape[2]
    lane = lax.broadcasted_iota(I32, (qblk, LANES), 1)
    lo = lane < HEAD_DIM
    q = _stack_group_queries(q_ref, 0, GQA_REP)
    tq = start + lax.broadcasted_iota(I32, (qblk, 1), 0)

    kc = kc_ref[0, 0]
    vc = vc_ref[0, 0]
    cmp_end = lax.broadcasted_iota(I32, (qblk, n_cmp), 1) * NSA_CMP_STRIDE + (NSA_CMP_LEN - 1)
    vis = jnp.where(cmp_end <= tq, 1.0, 0.0).astype(F32)
    vis4 = jnp.concatenate([vis] * GQA_REP, axis=0)
    s_c = lax.dot_general(q, kc, NT_DIMS, preferred_element_type=F32)
    s_c = jnp.where(vis4 > 0.5, s_c, NEG_BIG)
    p_c = jnp.exp(s_c - jnp.max(s_c, axis=-1, keepdims=True)) * vis4
    p_c = p_c / jnp.maximum(jnp.sum(p_c, axis=-1, keepdims=True), 1e-30)
    o_c = jnp.dot(p_c.astype(BF16), vc, preferred_element_type=F32)

    p_sum = p_c[0:qblk]
    for r in range(1, GQA_REP):
        p_sum = p_sum + p_c[r * qblk:(r + 1) * qblk]
    imp_t = lax.dot_general(ov_ref[...], p_sum, NT_DIMS, preferred_element_type=F32,
                            precision=lax.Precision.HIGHEST)[:n_sb]
    blk_t = lax.broadcasted_iota(I32, (n_sb, qblk), 0)
    tq_t = start + lax.broadcasted_iota(I32, (n_sb, qblk), 1)
    cur_t = tq_t // NSA_SEL_LEN
    forced = jnp.where(blk_t == 0, 1, jnp.where(blk_t == cur_t, 1, jnp.where(blk_t == cur_t - 1, 1, 0)))
    score_t = jnp.where(forced == 1, FORCE_SCORE, jnp.where(blk_t * NSA_SEL_LEN > tq_t, -1.0, imp_t))
    sel_t = jnp.where(_rank_desc(score_t, n_sb, 0) < n_sel, 1.0, 0.0).astype(F32)
    if n_sb < LANES:
        sel_t = jnp.concatenate([sel_t, jnp.zeros((LANES - n_sb, qblk), F32)], axis=0)
    sel = jnp.transpose(sel_t).astype(BF16)

    _flash_init(m_scr, l_scr, acc_scr)
    chunks_per_head = qblk // FLASH_ROWS

    def body(j, carry):
        k0 = pl.multiple_of(j * tk_sel, tk_sel)
        picked = jnp.dot(sel, e_ref[:, pl.ds(k0, tk_sel)], preferred_element_type=F32)
        tk = k0 + lax.broadcasted_iota(I32, (qblk, tk_sel), 1)
        bias_scr[...] = jnp.where(picked > 0.5, jnp.where(tk <= tq, 0.0, -jnp.inf), -jnp.inf).astype(F32)

        def bias_fn(c):
            r0 = (c % chunks_per_head) * FLASH_ROWS
            return bias_scr[r0:r0 + FLASH_ROWS, :]

        _flash_step(q, ks_ref[pl.ds(k0, tk_sel), :], vs_ref[pl.ds(k0, tk_sel), :], bias_fn, LOG2_E,
                    s_scr, p_scr, m_scr, l_scr, a_scr, acc_scr)
        return carry

    n_tiles = (start + qblk + tk_sel - 1) // tk_sel
    lax.fori_loop(0, n_tiles, body, 0)
    o_s = acc_scr[...] / jnp.sum(l_scr[...], axis=-1, keepdims=True)

    for j in range(GQA_REP // 2):

        def gate(branch, pair=j):
            c = branch * GQA_REP + 2 * pair
            return jnp.where(lo, g_ref[:, c:c + 1], g_ref[:, c + 1:c + 2])

        rows_e = slice((2 * j) * qblk, (2 * j + 1) * qblk)
        rows_o = slice((2 * j + 1) * qblk, (2 * j + 2) * qblk)
        oc = jnp.where(lo, o_c[rows_e], o_c[rows_o])
        os_ = jnp.where(lo, o_s[rows_e], o_s[rows_o])
        out = gate(0) * oc + gate(1) * os_ + win_ref[:, j * LANES:(j + 1) * LANES]
        o_ref[:, j * LANES:(j + 1) * LANES] = out.astype(o_ref.dtype)


def _nsa_cmp_sel(qb, kcmp, vcmp, ks, vs, gates, win, overlap, expand, batch, seq, qblk, n_sb, n_sel, tk_sel):
    n = qb.shape[0]
    nq = seq // qblk
    n_cmp = kcmp.shape[2]
    gw = GQA_REP * HEAD_DIM
    rows = GQA_REP * qblk
    kern = functools.partial(_nsa_kernel, qblk=qblk, seq=seq, n_sb=n_sb, n_sel=n_sel, tk_sel=tk_sel)
    return pl.pallas_call(
        kern,
        grid=(batch, NSA_KV_HEADS, nq),
        in_specs=[pl.BlockSpec((qblk, gw), lambda b, g, i: (b * nq + i, g)),
                  pl.BlockSpec((1, 1, n_cmp, LANES), lambda b, g, i: (b, g, 0, 0)),
                  pl.BlockSpec((1, 1, n_cmp, LANES), lambda b, g, i: (b, g, 0, 0)),
                  pl.BlockSpec((seq, LANES), lambda b, g, i: (b, g)),
                  pl.BlockSpec((seq, LANES), lambda b, g, i: (b, g)),
                  pl.BlockSpec((qblk, LANES), lambda b, g, i: (b * nq + i, g)),
                  pl.BlockSpec((qblk, gw), lambda b, g, i: (b * nq + i, g)),
                  _full(overlap.shape), _full(expand.shape)],
        out_specs=pl.BlockSpec((qblk, gw), lambda b, g, i: (b * nq + i, g)),
        out_shape=jax.ShapeDtypeStruct((n, NSA_Q_HEADS * HEAD_DIM), BF16),
        scratch_shapes=[pltpu.VMEM((qblk, tk_sel), F32),
                        pltpu.VMEM((rows, tk_sel), F32), pltpu.VMEM((rows, tk_sel), BF16),
                        pltpu.VMEM((rows, LANES), F32), pltpu.VMEM((rows, LANES), F32),
                        pltpu.VMEM((rows, LANES), F32), pltpu.VMEM((rows, LANES), F32)],
        compiler_params=_params("parallel", "parallel", "arbitrary"),
        name="nsa_cmp_sel",
    )(qb, kcmp, vcmp, ks, vs, gates, win, overlap, expand)


def _oproj_ln_kernel(*refs, n_parts, alpha):
    o_refs = refs[:n_parts]
    w_refs = refs[n_parts:2 * n_parts]
    x_ref, g_ref, b_ref, y_ref, ypk_ref = refs[2 * n_parts:]
    h = jnp.dot(o_refs[0][...], w_refs[0][...], preferred_element_type=F32)
    for o_ref, w_ref in zip(o_refs[1:], w_refs[1:]):
        h = h + jnp.dot(o_ref[...], w_ref[...], preferred_element_type=F32)
    y = _layer_norm_rows(alpha * x_ref[...] + h, g_ref[...], b_ref[...])
    y_ref[...] = y
    ypk_ref[...] = _pack_halves(y)


def _oproj_ln(o_parts, w_parts, x2, g, b, alpha, tm):
    n, d = x2.shape
    kern = functools.partial(_oproj_ln_kernel, n_parts=len(o_parts), alpha=alpha)
    in_specs = ([pl.BlockSpec((tm, o.shape[1]), lambda i: (i, 0)) for o in o_parts]
                + [_full(w.shape) for w in w_parts]
                + [pl.BlockSpec((tm, d), lambda i: (i, 0)), _full((1, d)), _full((1, d))])
    return pl.pallas_call(
        kern,
        grid=(n // tm,),
        in_specs=in_specs,
        out_specs=[pl.BlockSpec((tm, d), lambda i: (i, 0)), pl.BlockSpec((tm, d // 2), lambda i: (i, 0))],
        out_shape=[jax.ShapeDtypeStruct((n, d), F32), jax.ShapeDtypeStruct((n, d // 2), U32)],
        compiler_params=_params("parallel"),
        name="oproj_ln",
    )(*o_parts, *w_parts, x2, g, b)


def _rms_rows(v, g):
    return v * lax.rsqrt(jnp.mean(v * v, axis=-1, keepdims=True) + RMS_EPS) * g


def _mla_proj_kernel(x_ref, win_ref, qn_ref, kvn_ref, wuq_ref, wuk_ref, wuv_ref, cos_ref, sin_ref,
                     q_ref, k_ref, v_ref):
    xb = x_ref[...].astype(BF16)
    cos = cos_ref[...]
    sin = sin_ref[...]
    lat = jnp.dot(xb, win_ref[...], preferred_element_type=F32)
    cq = _rms_rows(lat[:, :MLA_Q_LORA], qn_ref[...]).astype(BF16)
    ckv = _rms_rows(lat[:, MLA_Q_LORA:MLA_Q_LORA + MLA_KV_LORA], kvn_ref[...]).astype(BF16)
    kr = _rope_slab(lat[:, MLA_Q_LORA + MLA_KV_LORA:], cos, sin, MLA_ROPE // 2)
    q = jnp.dot(cq, wuq_ref[...], preferred_element_type=F32)
    k = jnp.dot(ckv, wuk_ref[...], preferred_element_type=F32)
    for h in range(MLA_HEADS):
        sl = slice(h * LANES, (h + 1) * LANES)
        q_ref[:, sl] = _rope_slab(q[:, sl], cos, sin, MLA_ROPE // 2).astype(BF16)
        k_ref[:, sl] = (k[:, sl] + kr).astype(BF16)
    v_ref[...] = jnp.dot(ckv, wuv_ref[...], preferred_element_type=F32).astype(BF16)


def _mla_proj(x2, w_in, qn, kvn, wuq, wuk, wuv, cos, sin, seq, tm):
    n, d = x2.shape
    sblk = seq // tm
    hw = MLA_HEADS * LANES
    vw = MLA_HEADS * MLA_V
    return pl.pallas_call(
        _mla_proj_kernel,
        grid=(n // tm,),
        in_specs=[pl.BlockSpec((tm, d), lambda i: (i, 0)),
                  _full(w_in.shape), _full(qn.shape), _full(kvn.shape),
                  _full(wuq.shape), _full(wuk.shape), _full(wuv.shape),
                  pl.BlockSpec((tm, LANES), lambda i: (i % sblk, 0)),
                  pl.BlockSpec((tm, LANES), lambda i: (i % sblk, 0))],
        out_specs=[pl.BlockSpec((tm, hw), lambda i: (i, 0)),
                   pl.BlockSpec((tm, hw), lambda i: (i, 0)),
                   pl.BlockSpec((tm, vw), lambda i: (i, 0))],
        out_shape=[jax.ShapeDtypeStruct((n, hw), BF16), jax.ShapeDtypeStruct((n, hw), BF16),
                   jax.ShapeDtypeStruct((n, vw), BF16)],
        compiler_params=_params("parallel"),
        name="mla_proj",
    )(x2, w_in, qn, kvn, wuq, wuk, wuv, cos, sin)


FLASH_ROWS = 32
LOG2_E = 1.4426950408889634


def _flash_init(m_scr, l_scr, acc_scr):
    m_scr[...] = jnp.full(m_scr.shape, -jnp.inf, F32)
    l_scr[...] = jnp.zeros(l_scr.shape, F32)
    acc_scr[...] = jnp.zeros(acc_scr.shape, F32)


def _flash_step(q, k, v, bias_fn, exp_scale, s_scr, p_scr, m_scr, l_scr, a_scr, acc_scr):
    _flash_scores(q, k, s_scr)
    _flash_update(v, bias_fn, exp_scale, s_scr, p_scr, m_scr, l_scr, a_scr, acc_scr)


def _flash_scores(q, k, s_scr):
    s_scr[...] = lax.dot_general(q, k, NT_DIMS, preferred_element_type=F32)


def _flash_update(v, bias_fn, exp_scale, s_scr, p_scr, m_scr, l_scr, a_scr, acc_scr):
    _flash_max(bias_fn, exp_scale, s_scr, m_scr, a_scr)
    _flash_weights(v, exp_scale, s_scr, p_scr, m_scr, l_scr, a_scr, acc_scr)


def _flash_max(bias_fn, exp_scale, s_scr, m_scr, a_scr):
    rows, keys = s_scr.shape
    n_chunks = rows // FLASH_ROWS
    slabs = keys // LANES
    for c in range(n_chunks):
        r = slice(c * FLASH_ROWS, (c + 1) * FLASH_ROWS)
        s = s_scr[r, :]
        if bias_fn is not None:
            s = s + bias_fn(c)
            s_scr[r, :] = s
        mx = s[:, 0:LANES]
        for j in range(1, slabs):
            mx = jnp.maximum(mx, s[:, j * LANES:(j + 1) * LANES])
        a_scr[r, :] = mx
    m_old = m_scr[...]
    m_new = jnp.maximum(m_old, jnp.broadcast_to(jnp.max(a_scr[...], axis=-1, keepdims=True), m_old.shape))
    m_scr[...] = m_new
    a_scr[...] = jnp.exp2((m_old - m_new) * exp_scale)


def _flash_weights(v, exp_scale, s_scr, p_scr, m_scr, l_scr, a_scr, acc_scr):
    rows, keys = s_scr.shape
    n_chunks = rows // FLASH_ROWS
    slabs = keys // LANES
    for c in range(n_chunks):
        r = slice(c * FLASH_ROWS, (c + 1) * FLASH_ROWS)
        m_rows = m_scr[r, :]
        ps = None
        for j in range(slabs):
            cols = slice(j * LANES, (j + 1) * LANES)
            p = jnp.exp2((s_scr[r, cols] - m_rows) * exp_scale)
            p_scr[r, cols] = p.astype(p_scr.dtype)
            ps = p if ps is None else ps + p
        l_scr[r, :] = a_scr[r, :] * l_scr[r, :] + ps
    acc_scr[...] = a_scr[...] * acc_scr[...] + jnp.dot(p_scr[...], v, preferred_element_type=F32)


def _mla_attn_kernel(q_ref, k_ref, v_ref, o_ref, s_scr, p_scr, m_scr, l_scr, a_scr, acc_scr, *, tq, scale):
    i = pl.program_id(2)
    lane = lax.broadcasted_iota(I32, (tq, LANES), 1)
    lo = lane < MLA_V
    for e in range(2):
        _flash_init(m_scr.at[e], l_scr.at[e], acc_scr.at[e])

    def diag_bias(c):
        row = c * FLASH_ROWS + lax.broadcasted_iota(I32, (FLASH_ROWS, tq), 0)
        col = lax.broadcasted_iota(I32, (FLASH_ROWS, tq), 1)
        return jnp.where(col <= row, 0.0, -jnp.inf).astype(F32)

    def tile(j, bias_fn):
        k0 = pl.multiple_of(j * tq, tq)
        v = v_ref[pl.ds(k0, tq), :]
        for e in range(2):
            _flash_scores(q_ref[:, e * LANES:(e + 1) * LANES], k_ref[pl.ds(k0, tq), e * LANES:(e + 1) * LANES],
                          s_scr.at[e])
        for e in range(2):
            _flash_max(bias_fn, scale * LOG2_E, s_scr.at[e], m_scr.at[e], a_scr.at[e])
        for e in range(2):
            _flash_weights(v, scale * LOG2_E, s_scr.at[e], p_scr.at[e], m_scr.at[e], l_scr.at[e],
                           a_scr.at[e], acc_scr.at[e])

    def body(j, carry):
        tile(j, None)
        return carry

    lax.fori_loop(0, i, body, 0)
    tile(i, diag_bias)
    o0 = acc_scr[0] / jnp.sum(l_scr[0], axis=-1, keepdims=True)
    o1 = acc_scr[1] / jnp.sum(l_scr[1], axis=-1, keepdims=True)
    o_ref[...] = jnp.where(lo, o0, o1).astype(o_ref.dtype)


def _mla_attention(q, k, v, batch, seq, tq):
    n = q.shape[0]
    nq = seq // tq
    kern = functools.partial(_mla_attn_kernel, tq=tq, scale=(MLA_NOPE + MLA_ROPE) ** -0.5)
    return pl.pallas_call(
        kern,
        grid=(batch, MLA_HEADS // 2, nq),
        in_specs=[pl.BlockSpec((tq, 2 * LANES), lambda b, p, i: (b * nq + i, p)),
                  pl.BlockSpec((seq, 2 * LANES), lambda b, p, i: (b, p)),
                  pl.BlockSpec((seq, LANES), lambda b, p, i: (b, p))],
        out_specs=pl.BlockSpec((tq, LANES), lambda b, p, i: (b * nq + i, p)),
        out_shape=jax.ShapeDtypeStruct((n, MLA_HEADS * MLA_V), BF16),
        scratch_shapes=[pltpu.VMEM((2, tq, tq), F32), pltpu.VMEM((2, tq, tq), BF16),
                        pltpu.VMEM((2, tq, LANES), F32), pltpu.VMEM((2, tq, LANES), F32), pltpu.VMEM((2, tq, LANES), F32),
                        pltpu.VMEM((2, tq, LANES), F32)],
        compiler_params=_params("parallel", "parallel", "arbitrary"),
        name="mla_attn",
    )(q, k, v)


def _router_kernel(x_ref, rw_ref, rb_ref, eidx_ref, gate_ref, pos_ref, cnt_ref, carry_ref, *, steps_per_chunk):
    i = pl.program_id(0)
    tm = x_ref.shape[0]
    per_group = N_EXPERTS // MOE_GROUPS

    @pl.when(i % steps_per_chunk == 0)
    def _():
        carry_ref[...] = jnp.zeros_like(carry_ref)

    logits = jnp.dot(x_ref[...], rw_ref[...], preferred_element_type=F32, precision=lax.Precision.HIGHEST)
    lt = jnp.transpose(logits)[:N_EXPERTS]
    s = _sigmoid(lt)
    sb = s + rb_ref[...]

    g3 = sb.reshape(MOE_GROUPS, per_group, tm)
    idx3 = lax.broadcasted_iota(I32, g3.shape, 1).astype(F32)
    m1 = jnp.max(g3, axis=1, keepdims=True)
    first = jnp.min(jnp.where(g3 == m1, idx3, float(per_group)), axis=1, keepdims=True)
    m2 = jnp.max(jnp.where(idx3 == first, -jnp.inf, g3), axis=1, keepdims=True)
    gscore = (m1 + m2).reshape(MOE_GROUPS, tm)
    gsel = _rank_desc(gscore, MOE_GROUPS, 0) < MOE_TOPK_GROUPS
    gsel3 = jnp.where(gsel, 1.0, 0.0).astype(F32).reshape(MOE_GROUPS, 1, tm)
    masked = jnp.where(gsel3 > 0.5, g3, -jnp.inf).reshape(N_EXPERTS, tm)
    rank = _rank_desc(masked, N_EXPERTS, 0)
    sel = rank < MOE_TOP_K
    gate = jnp.where(sel, s, 0.0)
    gate = gate / jnp.sum(gate, axis=0, keepdims=True) * MOE_ROUTED_SCALE

    sel_b = jnp.where(sel, 1.0, 0.0).astype(BF16)
    r_i = lax.broadcasted_iota(I32, (tm, tm), 0)
    c_i = lax.broadcasted_iota(I32, (tm, tm), 1)
    tri = jnp.where(r_i < c_i, 1.0, 0.0).astype(BF16)
    carry = carry_ref[:, 0:1]
    before = jnp.dot(sel_b, tri, preferred_element_type=F32) + carry
    carry_new = carry + jnp.sum(sel_b.astype(F32), axis=1, keepdims=True)
    carry_ref[...] = jnp.broadcast_to(carry_new, carry_ref.shape)
    cnt_ref[0] = jnp.broadcast_to(carry_new, carry_ref.shape).astype(I32)

    e_iota = lax.broadcasted_iota(I32, (N_EXPERTS, tm), 0).astype(F32)
    rows_e, rows_g, rows_p = [], [], []
    for r in range(MOE_TOP_K):
        hit = rank == r
        rows_e.append(jnp.sum(jnp.where(hit, e_iota, 0.0), axis=0, keepdims=True))
        rows_g.append(jnp.sum(jnp.where(hit, gate, 0.0), axis=0, keepdims=True))
        rows_p.append(jnp.sum(jnp.where(hit, before, 0.0), axis=0, keepdims=True))
    pad = 8 - MOE_TOP_K
    eidx_ref[...] = jnp.concatenate(rows_e + [jnp.zeros((pad, tm), F32)], axis=0).astype(I32)
    gate_ref[...] = jnp.concatenate(rows_g + [jnp.zeros((pad, tm), F32)], axis=0)
    pos_ref[...] = jnp.concatenate(rows_p + [jnp.zeros((pad, tm), F32)], axis=0).astype(I32)


def _router(x2, rw, rb, n_chunk, tm):
    n, d = x2.shape
    steps = n // tm
    spc = steps // n_chunk
    kern = functools.partial(_router_kernel, steps_per_chunk=spc)
    row8 = pl.BlockSpec((8, tm), lambda i: (0, i))
    return pl.pallas_call(
        kern,
        grid=(steps,),
        in_specs=[pl.BlockSpec((tm, d), lambda i: (i, 0)), _full(rw.shape), _full(rb.shape)],
        out_specs=[row8, row8, row8, pl.BlockSpec((1, N_EXPERTS, LANES), lambda i: (i // spc, 0, 0))],
        out_shape=[jax.ShapeDtypeStruct((8, n), I32), jax.ShapeDtypeStruct((8, n), F32),
                   jax.ShapeDtypeStruct((8, n), I32),
                   jax.ShapeDtypeStruct((n_chunk, N_EXPERTS, LANES), I32)],
        scratch_shapes=[pltpu.VMEM((N_EXPERTS, LANES), F32)],
        compiler_params=_params("arbitrary"),
        name="moe_router",
    )(x2, rw, rb)


def _dest_kernel(tab_ref, eidx_ref, pos_ref, dest_ref, *, steps_per_chunk):
    chunk = pl.program_id(0) // steps_per_chunk
    eidx = eidx_ref[...]
    dest = pos_ref[...]
    for e in range(N_EXPERTS):
        dest = dest + jnp.where(eidx == e, tab_ref[chunk * N_EXPERTS + e], 0)
    dest_ref[...] = dest


def _dest_rows(pad_starts, eidx, pos, n_chunk, tm):
    n = eidx.shape[1]
    steps = n // tm
    row8 = lambda i, tab: (0, i)
    return pl.pallas_call(
        functools.partial(_dest_kernel, steps_per_chunk=steps // n_chunk),
        grid_spec=pltpu.PrefetchScalarGridSpec(
            num_scalar_prefetch=1, grid=(steps,),
            in_specs=[pl.BlockSpec((8, tm), row8), pl.BlockSpec((8, tm), row8)],
            out_specs=pl.BlockSpec((8, tm), row8)),
        out_shape=jax.ShapeDtypeStruct((8, n), I32),
        compiler_params=_params("parallel"),
        name="moe_dest",
    )(pad_starts, eidx, pos)


SC_ROWS = 64
SC_WORKERS = 32


def _sc_mesh():
    return plsc.VectorSubcoreMesh(core_axis_name="c", subcore_axis_name="s")


def _sc_worker_base(per_worker):
    return (lax.axis_index("s") * 2 + lax.axis_index("c")) * per_worker


def _sc_scatter_rows(x, dest, n_out):
    n, d = x.shape
    slots = dest.shape[0] // n
    per_w = n // SC_WORKERS

    def body(x_hbm, i_hbm, o_hbm, idx_v, rows_v, sem):
        base = _sc_worker_base(per_w)

        @pl.loop(0, per_w // SC_ROWS)
        def _(c):
            off = pl.multiple_of(base + c * SC_ROWS, 8)
            pltpu.sync_copy(x_hbm.at[pl.ds(off, SC_ROWS)], rows_v)
            for k in range(slots):
                pltpu.sync_copy(i_hbm.at[pl.ds(k * n + off, SC_ROWS)], idx_v)
                pltpu.async_copy(rows_v, o_hbm.at[idx_v], sem).wait()

    return pl.kernel(
        body, out_type=jax.ShapeDtypeStruct((n_out, d), x.dtype), mesh=_sc_mesh(),
        scratch_types=[pltpu.VMEM((SC_ROWS,), I32), pltpu.VMEM((SC_ROWS, d), x.dtype), pltpu.SemaphoreType.DMA],
        name="moe_dispatch_sc")(x, dest)


def _sc_gather_rows(table, idx):
    d = table.shape[1]
    r = idx.shape[0]
    per_w = r // SC_WORKERS

    def body(t_hbm, i_hbm, o_hbm, idx_v, rows_v, sem):
        base = _sc_worker_base(per_w)

        @pl.loop(0, per_w // SC_ROWS)
        def _(c):
            off = pl.multiple_of(base + c * SC_ROWS, 8)
            pltpu.sync_copy(i_hbm.at[pl.ds(off, SC_ROWS)], idx_v)
            pltpu.async_copy(t_hbm.at[idx_v], rows_v, sem).wait()
            pltpu.sync_copy(rows_v, o_hbm.at[pl.ds(off, SC_ROWS)])

    return pl.kernel(
        body, out_type=jax.ShapeDtypeStruct((r, d), table.dtype), mesh=_sc_mesh(),
        scratch_types=[pltpu.VMEM((SC_ROWS,), I32), pltpu.VMEM((SC_ROWS, d), table.dtype), pltpu.SemaphoreType.DMA],
        name="moe_combine_sc")(table, idx)


def _experts_kernel(be_ref, nv_ref, xs_ref, wgu_ref, wd_ref, y_ref):
    i = pl.program_id(0)

    @pl.when(i < nv_ref[0])
    def _():
        x_lo, x_hi = _unpack_halves(xs_ref[...])
        half = x_lo.shape[1]
        gu = (jnp.dot(x_lo.astype(BF16), wgu_ref[0, :half, :], preferred_element_type=F32)
              + jnp.dot(x_hi.astype(BF16), wgu_ref[0, half:, :], preferred_element_type=F32))
        h = _silu(gu[:, :D_EXPERT]) * gu[:, D_EXPERT:]
        y_ref[...] = _pack_halves(jnp.dot(h.astype(BF16), wd_ref[0], preferred_element_type=F32))

    @pl.when(i >= nv_ref[0])
    def _():
        y_ref[...] = jnp.zeros_like(y_ref)


def _experts(blk_exp, n_valid, xs, wgu, wd, tb):
    rows, dp = xs.shape
    d = 2 * dp
    grid_spec = pltpu.PrefetchScalarGridSpec(
        num_scalar_prefetch=2,
        grid=(rows // tb,),
        in_specs=[pl.BlockSpec((tb, dp), lambda i, be, nv: (i, 0)),
                  pl.BlockSpec((1, d, 2 * D_EXPERT), lambda i, be, nv: (be[i], 0, 0)),
                  pl.BlockSpec((1, D_EXPERT, d), lambda i, be, nv: (be[i], 0, 0))],
        out_specs=pl.BlockSpec((tb, dp), lambda i, be, nv: (i, 0)),
    )
    return pl.pallas_call(
        _experts_kernel,
        grid_spec=grid_spec,
        out_shape=jax.ShapeDtypeStruct((rows, dp), U32),
        compiler_params=_params("arbitrary"),
        name="moe_experts",
    )(blk_exp, n_valid, xs, wgu, wd)


def _combine_ln_kernel(x_ref, yg_ref, gt_ref, wgu_ref, wd_ref, g_ref, b_ref, o_ref, *, alpha):
    x = x_ref[...]
    gu = jnp.dot(x.astype(BF16), wgu_ref[...], preferred_element_type=F32)
    dsh = gu.shape[1] // 2
    h = _silu(gu[:, :dsh]) * gu[:, dsh:]
    f = jnp.dot(h.astype(BF16), wd_ref[...], preferred_element_type=F32)
    r_lo = r_hi = None
    for k in range(MOE_TOP_K):
        y_lo, y_hi = _unpack_halves(yg_ref[k])
        gate = gt_ref[:, k:k + 1]
        r_lo = gate * y_lo if r_lo is None else r_lo + gate * y_lo
        r_hi = gate * y_hi if r_hi is None else r_hi + gate * y_hi
    f = f + jnp.concatenate([r_lo, r_hi], axis=1)
    o_ref[...] = _layer_norm_rows(alpha * x + f, g_ref[...], b_ref[...])


def _combine_ln(x2, yg, gates_t, wgu, wd, g, b, alpha, tm):
    n, d = x2.shape
    kern = functools.partial(_combine_ln_kernel, alpha=alpha)
    return pl.pallas_call(
        kern,
        grid=(n // tm,),
        in_specs=[pl.BlockSpec((tm, d), lambda i: (i, 0)),
                  pl.BlockSpec((MOE_TOP_K, tm, d // 2), lambda i: (0, i, 0)),
                  pl.BlockSpec((tm, 8), lambda i: (i, 0)),
                  _full(wgu.shape), _full(wd.shape), _full((1, d)), _full((1, d))],
        out_specs=pl.BlockSpec((tm, d), lambda i: (i, 0)),
        out_shape=jax.ShapeDtypeStruct((n, d), F32),
        compiler_params=_params("parallel"),
        name="moe_combine_ln",
    )(x2, yg, gates_t, wgu, wd, g, b)


def _rope_tables(positions, dim, lane_offset, period):
    half = dim // 2
    inv_freq = ROPE_THETA ** (-2.0 * jnp.arange(half, dtype=jnp.float32) / dim)
    ang = positions.astype(jnp.float32)[:, None] * inv_freq[None, :]
    cos_h, sin_h = jnp.cos(ang), jnp.sin(ang)
    lanes = np.arange(LANES)
    rel = (lanes - lane_offset) % period
    active = (lanes >= lane_offset) & (rel < dim)
    fidx = np.where(active, rel % half, 0)
    sign = np.where(rel < half, -1.0, 1.0)
    cos = jnp.where(active[None, :], cos_h[:, fidx], 1.0)
    sin = jnp.where(active[None, :], sin_h[:, fidx] * sign[None, :], 0.0)
    return cos.astype(F32), sin.astype(F32)


def _gate_slabs(w):
    k = w.shape[0]
    g = w.reshape(k, 3, NSA_KV_HEADS, GQA_REP).transpose(0, 2, 1, 3).reshape(k, NSA_KV_HEADS, 3 * GQA_REP)
    return jnp.pad(g, ((0, 0), (0, 0), (0, LANES - 3 * GQA_REP))).reshape(k, NSA_KV_HEADS * LANES)


def _dup_heads(w):
    a, b = w[:, :HEAD_DIM], w[:, HEAD_DIM:]
    return jnp.concatenate([a, a, b, b], axis=1)


def _moe_layer(x1_and_packed, rw, rb, wg, wu, wd, shg, shu, shd, ln_g, ln_b, alpha, tm):
    x1, x1_pk = x1_and_packed
    n, d = x1.shape
    n_chunk = 1
    tb = MOE_ROW_BLOCK
    rw_p = jnp.pad(rw, ((0, 0), (0, LANES - N_EXPERTS)))
    eidx, gate, pos, cnt = _router(x1, rw_p, rb.reshape(N_EXPERTS, 1), n_chunk, tm)
    counts = cnt[:, :, 0].reshape(-1)
    padded = (counts + tb - 1) // tb * tb
    pad_ends = jnp.cumsum(padded)
    pad_starts = (pad_ends - padded).astype(I32)
    dest = _dest_rows(pad_starts, eidx, pos, n_chunk, tm)[:MOE_TOP_K].reshape(-1)
    n_rows = n * MOE_TOP_K
    n_blk = -(-(n_rows + n_chunk * N_EXPERTS * (tb - 1)) // tb)
    blk_first_row = jnp.arange(n_blk, dtype=I32) * tb
    owner = jnp.sum((pad_ends[None, :] <= blk_first_row[:, None]).astype(I32), axis=1)
    blk_exp = jnp.minimum(owner, n_chunk * N_EXPERTS - 1).astype(I32) % N_EXPERTS
    n_valid = (pad_ends[-1] // tb).astype(I32).reshape(1)

    xs = _sc_scatter_rows(x1_pk, dest, n_blk * tb)
    wgu = jnp.concatenate([wg, wu], axis=-1).astype(BF16)
    y = _experts(blk_exp, n_valid, xs, wgu, wd.astype(BF16), tb)
    yg = _sc_gather_rows(y, dest).reshape(MOE_TOP_K, n, d // 2)
    gates_t = gate.T
    sh_gu = jnp.concatenate([shg, shu], axis=-1).astype(BF16)
    return _combine_ln(x1, yg, gates_t, sh_gu, shd.astype(BF16), ln_g.reshape(1, d), ln_b.reshape(1, d), alpha, tm)


def _swa_nsa_layer(x2, batch, seq, w_in, sinks, pe_k, w_ck1, w_ck2, pe_v, w_cv1, w_cv2, w_out,
                   ln_g, ln_b, alpha, tm):
    n, d = x2.shape
    splits = np.cumsum([512, 128, 128, 512, 128, 128, 128, 128, 128, 128, 24])[:-1]
    qa, ka, va, qb, kc, vc, ksl, vsl, kw, vw, gts = jnp.split(w_in, [int(c) for c in splits], axis=1)
    w_all = jnp.concatenate(
        [qa, _dup_heads(ka), _dup_heads(va), qb, kc, vc, _dup_heads(ksl), _dup_heads(vsl),
         _dup_heads(kw), _dup_heads(vw), _gate_slabs(gts)], axis=1).astype(BF16)
    cos, sin = _rope_tables(jnp.arange(seq), HEAD_DIM, 0, HEAD_DIM)
    (q_a, k_a, v_a, q_b, k_c, v_c, k_s, v_s, k_w, v_w, gates) = _mix0_proj(x2, w_all, cos, sin, seq, tm)

    o_swa = _banded_attention(q_a, k_a, v_a, batch, seq, SWA_WINDOW, 128, sinks=sinks, name="swa")
    o_win = _banded_attention(q_b, k_w, v_w, batch, seq, NSA_WINDOW, 128, gates=gates,
                              gate_col=2 * GQA_REP, out_dtype=F32, name="nsa_win")

    n_chunk = seq // NSA_CMP_STRIDE
    cw = NSA_CMP_STRIDE * NSA_KV_HEADS * HEAD_DIM
    hid = NSA_CMP_HIDDEN

    def expand_w1(w1):
        halves = w1.reshape(2, NSA_CMP_STRIDE, HEAD_DIM, hid)
        out = jnp.zeros((NSA_KV_HEADS, NSA_CMP_STRIDE, NSA_KV_HEADS, HEAD_DIM, 2 * hid), w1.dtype)
        for g in range(NSA_KV_HEADS):
            out = out.at[g, :, g, :, :hid].set(halves[0]).at[g, :, g, :, hid:].set(halves[1])
        return out.reshape(NSA_KV_HEADS, cw, 2 * hid).astype(BF16)

    def pe_rows(pe):
        return jnp.pad(pe.reshape(1, NSA_CMP_LEN * HEAD_DIM), ((0, 15), (0, 0))).astype(BF16)

    cos_c, sin_c = _rope_tables(jnp.arange(n_chunk) * NSA_CMP_STRIDE + NSA_CMP_LEN - 1, HEAD_DIM, 0, HEAD_DIM)
    k_cmp, v_cmp = _compress(
        k_c.reshape(batch * n_chunk, cw), v_c.reshape(batch * n_chunk, cw),
        expand_w1(w_ck1), expand_w1(w_cv1), pe_rows(pe_k), pe_rows(pe_v),
        w_ck1.astype(BF16), w_cv1.astype(BF16),
        jnp.concatenate([w_ck2, w_ck2], axis=1).astype(BF16),
        jnp.concatenate([w_cv2, w_cv2], axis=1).astype(BF16),
        cos_c, sin_c, batch, n_chunk)

    n_sb = seq // NSA_SEL_LEN
    n_sel = min(NSA_N_SEL, n_sb)
    cs = np.arange(n_chunk) * NSA_CMP_STRIDE
    bs = np.arange(n_sb) * NSA_SEL_LEN
    ov = np.clip(np.minimum(cs[:, None] + NSA_CMP_LEN, bs[None, :] + NSA_SEL_LEN)
                 - np.maximum(cs[:, None], bs[None, :]), 0, None) / NSA_CMP_LEN
    overlap = jnp.asarray(np.pad(ov, ((0, 0), (0, LANES - n_sb))).T, F32)
    expand = jnp.asarray(np.arange(LANES)[:, None] == (np.arange(seq)[None, :] // NSA_SEL_LEN), BF16)
    tk_sel = min(512, seq)
    o_nsa = _nsa_cmp_sel(q_b, k_cmp, v_cmp, k_s, v_s, gates, o_win, overlap, expand,
                         batch, seq, 128, n_sb, n_sel, tk_sel)
    w_o = w_out.astype(BF16)
    half = SWA_Q_HEADS * HEAD_DIM
    return _oproj_ln([o_swa, o_nsa], [w_o[:half], w_o[half:]], x2, ln_g.reshape(1, d), ln_b.reshape(1, d),
                     alpha, tm)


def _mla_layer(x2, batch, seq, w_in, q_norm, kv_norm, w_uq, w_ukv, w_out, ln_g, ln_b, alpha, tm):
    n, d = x2.shape
    dq = MLA_NOPE + MLA_ROPE
    w_kr = jnp.zeros((d, LANES), w_in.dtype).at[:, MLA_NOPE:dq].set(w_in[:, MLA_Q_LORA + MLA_KV_LORA:])
    w_in_p = jnp.concatenate([w_in[:, :MLA_Q_LORA + MLA_KV_LORA], w_kr], axis=1).astype(BF16)
    wuq = jnp.pad(w_uq.reshape(MLA_Q_LORA, MLA_HEADS, dq), ((0, 0), (0, 0), (0, LANES - dq)))
    wuq = wuq.reshape(MLA_Q_LORA, MLA_HEADS * LANES).astype(BF16)
    wukv = w_ukv.reshape(MLA_KV_LORA, MLA_HEADS, MLA_NOPE + MLA_V)
    wuk = jnp.pad(wukv[:, :, :MLA_NOPE], ((0, 0), (0, 0), (0, LANES - MLA_NOPE)))
    wuk = wuk.reshape(MLA_KV_LORA, MLA_HEADS * LANES).astype(BF16)
    wuv = wukv[:, :, MLA_NOPE:].reshape(MLA_KV_LORA, MLA_HEADS * MLA_V).astype(BF16)
    cos, sin = _rope_tables(jnp.arange(seq), MLA_ROPE, MLA_NOPE, LANES)
    q, k, v = _mla_proj(x2, w_in_p, q_norm.reshape(1, -1), kv_norm.reshape(1, -1), wuq, wuk, wuv, cos, sin, seq, tm)
    o = _mla_attention(q, k, v, batch, seq, min(512, seq))
    return _oproj_ln([o], [w_out.astype(BF16)], x2, ln_g.reshape(1, d), ln_b.reshape(1, d), alpha, tm)


def kernel(x, swa_nsa_w_in, swa_sinks, nsa_cmp_pe_k, nsa_cmp_k_w1, nsa_cmp_k_w2, nsa_cmp_pe_v, nsa_cmp_v_w1, nsa_cmp_v_w2, swa_nsa_w_out, mla_w_in, mla_q_norm, mla_kv_norm, mla_w_uq, mla_w_ukv, mla_w_out, ln_mix_g, ln_mix_b, ln_ffn_g, ln_ffn_b, router_w, router_bias, expert_w_gate, expert_w_up, expert_w_down, shared_w_gate, shared_w_up, shared_w_down):
    batch, seq, d = x.shape
    depth = ln_mix_g.shape[0]
    alpha = (2 * depth) ** 0.25
    tm = min(512, seq)
    x2 = x.reshape(batch * seq, d)
    for layer in range(depth):
        j = layer // 2
        if layer % 2 == 0:
            x2 = _swa_nsa_layer(x2, batch, seq, swa_nsa_w_in[j], swa_sinks[j], nsa_cmp_pe_k[j],
                                nsa_cmp_k_w1[j], nsa_cmp_k_w2[j], nsa_cmp_pe_v[j], nsa_cmp_v_w1[j],
                                nsa_cmp_v_w2[j], swa_nsa_w_out[j], ln_mix_g[layer], ln_mix_b[layer], alpha, tm)
        else:
            x2 = _mla_layer(x2, batch, seq, mla_w_in[j], mla_q_norm[j], mla_kv_norm[j], mla_w_uq[j],
                            mla_w_ukv[j], mla_w_out[j], ln_mix_g[layer], ln_mix_b[layer], alpha, tm)
        x2 = _moe_layer(x2, router_w[layer], router_bias[layer], expert_w_gate[layer], expert_w_up[layer],
                        expert_w_down[layer], shared_w_gate[layer], shared_w_up[layer], shared_w_down[layer],
                        ln_ffn_g[layer], ln_ffn_b[layer], alpha, tm)
    return x2.reshape(batch, seq, d)
```

```python
import functools
import math

import numpy as np
import jax
import jax.numpy as jnp
from jax import lax
from jax.experimental import pallas as pl
from jax.experimental.pallas import tpu as pltpu
from jax.experimental.pallas import tpu_sc as plsc

F32 = jnp.float32
BF16 = jnp.bfloat16
I32 = jnp.int32

LANES = 128
VMEM_LIMIT = 48 * 1024 * 1024

ROPE_THETA = 10000.0
LN_EPS = 1e-5
RMS_EPS = 1e-6
NEG_BIG = -1e30
FORCE_SCORE = 1e4

HEAD_DIM = 64
SWA_Q_HEADS = 8
SWA_KV_HEADS = 2
SWA_WINDOW = 128
NSA_Q_HEADS = 8
NSA_KV_HEADS = 2
NSA_CMP_LEN = 32
NSA_CMP_STRIDE = 16
NSA_CMP_HIDDEN = 128
NSA_SEL_LEN = 64
NSA_N_SEL = 16
NSA_WINDOW = 512
GQA_REP = 4

MLA_HEADS = 16
MLA_NOPE = 64
MLA_ROPE = 32
MLA_V = 64
MLA_Q_LORA = 384
MLA_KV_LORA = 256

N_EXPERTS = 64
MOE_GROUPS = 8
MOE_TOPK_GROUPS = 4
MOE_TOP_K = 6
D_EXPERT = 256
MOE_ROUTED_SCALE = 2.5
MOE_ROW_BLOCK = 512

NT_DIMS = (((1,), (1,)), ((), ()))


def _params(*sem):
    return pltpu.CompilerParams(dimension_semantics=sem, vmem_limit_bytes=VMEM_LIMIT)


def _full(shape):
    nd = len(shape)
    return pl.BlockSpec(shape, lambda *_: (0,) * nd)


def _rope_slab(y, cos, sin, half):
    lane = lax.broadcasted_iota(I32, y.shape, 1)
    first = (lane % (2 * half)) < half
    rot = jnp.where(first, pltpu.roll(y, LANES - half, 1), pltpu.roll(y, half, 1))
    return y * cos + rot * sin


def _layer_norm_rows(v, g, b):
    mu = jnp.mean(v, axis=-1, keepdims=True)
    vc = v - mu
    var = jnp.mean(vc * vc, axis=-1, keepdims=True)
    return vc * lax.rsqrt(var + LN_EPS) * g + b


def _silu(v):
    return v * (1.0 / (1.0 + jnp.exp(-v)))


def _sigmoid(v):
    return 1.0 / (1.0 + jnp.exp(-v))


U32 = jnp.uint32


def _pack_halves(v):
    w = v.shape[1] // 2
    lo = pltpu.bitcast(v[:, :w].astype(BF16).astype(F32), U32)
    hi = pltpu.bitcast(v[:, w:].astype(BF16).astype(F32), U32)
    return (lo >> 16) | (hi & jnp.uint32(0xFFFF0000))


def _unpack_halves(u):
    lo = pltpu.bitcast(u << 16, F32)
    hi = pltpu.bitcast(u & jnp.uint32(0xFFFF0000), F32)
    return lo, hi


def _stack_group_queries(q_ref, group, rep):
    lane = lax.broadcasted_iota(I32, (q_ref.shape[0], LANES), 1)
    keep_lo = jnp.where(lane < HEAD_DIM, 1.0, 0.0).astype(q_ref.dtype)
    keep_hi = jnp.where(lane < HEAD_DIM, 0.0, 1.0).astype(q_ref.dtype)
    parts = []
    for r in range(rep):
        h = group * rep + r
        slab = q_ref[:, (h // 2) * LANES:(h // 2 + 1) * LANES]
        parts.append(slab * (keep_lo if h % 2 == 0 else keep_hi))
    return jnp.concatenate(parts, axis=0)


def _rank_desc(score, n_valid, axis):
    idx = lax.broadcasted_iota(I32, score.shape, axis)
    rank = jnp.zeros(score.shape, I32)
    for j in range(n_valid):
        other = lax.slice_in_dim(score, j, j + 1, axis=axis)
        ahead = jnp.where(other > score, 1, jnp.where(other == score, jnp.where(idx > j, 1, 0), 0))
        rank = rank + ahead
    return rank


_MIX0_OUTS = (
    ("qa", 512, "rope_q", BF16), ("ka", 256, "rope", BF16), ("va", 256, "plain", BF16),
    ("qb", 512, "rope_q", BF16), ("kc", 128, "plain", BF16), ("vc", 128, "plain", BF16),
    ("ks", 256, "rope", BF16), ("vs", 256, "plain", BF16),
    ("kw", 256, "rope", BF16), ("vw", 256, "plain", BF16),
    ("gates", 256, "sigmoid", F32),
)


def _mix0_proj_kernel(x_ref, w_ref, cos_ref, sin_ref, *out_refs):
    xb = x_ref[...].astype(BF16)
    cos = cos_ref[...]
    sin = sin_ref[...]
    off = 0
    for (_, width, kind, _), o_ref in zip(_MIX0_OUTS, out_refs):
        y = jnp.dot(xb, w_ref[:, off:off + width], preferred_element_type=F32)
        for c in range(width // LANES):
            yc = y[:, c * LANES:(c + 1) * LANES]
            if kind in ("rope", "rope_q"):
                yc = _rope_slab(yc, cos, sin, HEAD_DIM // 2)
            if kind == "rope_q":
                yc = yc * (HEAD_DIM ** -0.5)
            if kind == "sigmoid":
                yc = _sigmoid(yc)
            o_ref[:, c * LANES:(c + 1) * LANES] = yc.astype(o_ref.dtype)
        off += width


def _mix0_proj(x2, w_all, cos, sin, seq, tm):
    n = x2.shape[0]
    d = x2.shape[1]
    wtot = w_all.shape[1]
    sblk = seq // tm
    return pl.pallas_call(
        _mix0_proj_kernel,
        grid=(n // tm,),
        in_specs=[pl.BlockSpec((tm, d), lambda i: (i, 0)),
                  _full((d, wtot)),
                  pl.BlockSpec((tm, LANES), lambda i: (i % sblk, 0)),
                  pl.BlockSpec((tm, LANES), lambda i: (i % sblk, 0))],
        out_specs=[pl.BlockSpec((tm, w), lambda i: (i, 0)) for _, w, _, _ in _MIX0_OUTS],
        out_shape=[jax.ShapeDtypeStruct((n, w), dt) for _, w, _, dt in _MIX0_OUTS],
        compiler_params=_params("parallel"),
        name="mix0_proj",
    )(x2, w_all, cos, sin)


def _banded_kernel(*refs, window, blk, seq, has_sink, gate_col):
    if has_sink:
        sink_ref, refs = refs[0], refs[1:]
    q_ref, k_ref, v_ref = refs[:3]
    g_ref = refs[3] if gate_col is not None else None
    o_ref, bias_scr, s_scr, p_scr, m_scr, l_scr, a_scr, acc_scr = refs[-8:]
    i = pl.program_id(1)
    start = i * blk
    span = min(window + blk, seq)
    k0 = pl.multiple_of(jnp.maximum(start - window, 0), LANES)
    tq = start + lax.broadcasted_iota(I32, (blk, span), 0)
    tk = k0 + lax.broadcasted_iota(I32, (blk, span), 1)
    diff = tq - tk
    bias_scr[...] = jnp.where(diff >= 0, jnp.where(diff < window, 0.0, -jnp.inf), -jnp.inf).astype(F32)
    chunks_per_head = blk // FLASH_ROWS

    def bias_fn(c):
        r0 = (c % chunks_per_head) * FLASH_ROWS
        return bias_scr[r0:r0 + FLASH_ROWS, :]

    lane = lax.broadcasted_iota(I32, (blk, LANES), 1)
    lo = lane < HEAD_DIM
    n_groups = q_ref.shape[1] // (GQA_REP * HEAD_DIM)
    for g in range(n_groups):
        _flash_init(m_scr.at[g], l_scr.at[g], acc_scr.at[g])
        _flash_scores(_stack_group_queries(q_ref, g, GQA_REP), k_ref[pl.ds(k0, span), g * LANES:(g + 1) * LANES],
                      s_scr.at[g])
    for g in range(n_groups):
        _flash_max(bias_fn, LOG2_E, s_scr.at[g], m_scr.at[g], a_scr.at[g])
    for g in range(n_groups):
        _flash_weights(v_ref[pl.ds(k0, span), g * LANES:(g + 1) * LANES], LOG2_E, s_scr.at[g], p_scr.at[g],
                       m_scr.at[g], l_scr.at[g], a_scr.at[g], acc_scr.at[g])
    for g in range(n_groups):
        m = m_scr[g]
        l = jnp.sum(l_scr[g], axis=-1, keepdims=True)
        o = acc_scr[g]
        if has_sink:
            sink = jnp.concatenate(
                [jnp.full((blk, LANES), sink_ref[g * GQA_REP + r], F32) for r in range(GQA_REP)], axis=0)
            m_all = jnp.maximum(m, sink)
            scale_old = jnp.exp2((m - m_all) * LOG2_E)
            o = o * scale_old
            l = l * scale_old[:, 0:1] + jnp.exp2((sink - m_all) * LOG2_E)[:, 0:1]
        o = o / l
        for j in range(GQA_REP // 2):
            slab = jnp.where(lo, o[(2 * j) * blk:(2 * j + 1) * blk], o[(2 * j + 1) * blk:(2 * j + 2) * blk])
            if gate_col is not None:
                c = g * LANES + gate_col + 2 * j
                gate = jnp.where(lo, g_ref[:, c:c + 1], g_ref[:, c + 1:c + 2])
                slab = slab * gate
            col = (g * (GQA_REP // 2) + j) * LANES
            o_ref[:, col:col + LANES] = slab.astype(o_ref.dtype)


def _banded_attention(q, k, v, batch, seq, window, blk, sinks=None, gates=None, gate_col=None,
                      out_dtype=None, name="banded"):
    n, qw = q.shape
    kw = k.shape[1]
    nblk = seq // blk
    span = min(window + blk, seq)
    groups = qw // (GQA_REP * HEAD_DIM)
    rows = GQA_REP * blk
    kern = functools.partial(_banded_kernel, window=window, blk=blk, seq=seq,
                             has_sink=sinks is not None, gate_col=gate_col)
    in_specs = []
    args = []
    if sinks is not None:
        in_specs.append(pl.BlockSpec(memory_space=pltpu.SMEM))
        args.append(sinks)
    in_specs += [pl.BlockSpec((blk, qw), lambda b, i: (b * nblk + i, 0)),
                 pl.BlockSpec((seq, kw), lambda b, i: (b, 0)),
                 pl.BlockSpec((seq, kw), lambda b, i: (b, 0))]
    args += [q, k, v]
    if gates is not None:
        in_specs.append(pl.BlockSpec((blk, gates.shape[1]), lambda b, i: (b * nblk + i, 0)))
        args.append(gates)
    return pl.pallas_call(
        kern,
        grid=(batch, nblk),
        in_specs=in_specs,
        out_specs=pl.BlockSpec((blk, qw), lambda b, i: (b * nblk + i, 0)),
        out_shape=jax.ShapeDtypeStruct((n, qw), BF16 if out_dtype is None else out_dtype),
        scratch_shapes=[pltpu.VMEM((blk, span), F32),
                        pltpu.VMEM((groups, rows, span), F32), pltpu.VMEM((groups, rows, span), BF16)]
                       + [pltpu.VMEM((groups, rows, LANES), F32)] * 4,
        compiler_params=_params("parallel", "arbitrary"),
        name=name,
    )(*args)


def _gelu_tanh(v):
    return 0.5 * v * (1.0 + jnp.tanh(math.sqrt(2.0 / math.pi) * (v + 0.044715 * (v * v * v))))


def _compress_kernel(tk_ref, tv_ref, w1k_ref, w1v_ref, pek_ref, pev_ref, w1ko_ref, w1vo_ref,
                     w2k_ref, w2v_ref, cos_ref, sin_ref, kc_ref, vc_ref):
    n_chunk = tk_ref.shape[0]
    hid = NSA_CMP_HIDDEN
    for t_ref, w1_ref, pe_ref, w1o_ref, w2_ref, o_ref, rope in (
            (tk_ref, w1k_ref, pek_ref, w1ko_ref, w2k_ref, kc_ref, True),
            (tv_ref, w1v_ref, pev_ref, w1vo_ref, w2v_ref, vc_ref, False)):
        pe_term = jnp.dot(pe_ref[...], w1o_ref[...], preferred_element_type=F32)[0:1, :]
        t = t_ref[...]
        for g in range(NSA_KV_HEADS):
            uv = jnp.dot(t, w1_ref[g], preferred_element_type=F32)
            nxt = pltpu.roll(uv[:, hid:], n_chunk - 1, 0)
            h = _gelu_tanh(uv[:, :hid] + nxt + pe_term)
            c = jnp.dot(h.astype(BF16), w2_ref[...], preferred_element_type=F32)
            if rope:
                c = _rope_slab(c, cos_ref[...], sin_ref[...], HEAD_DIM // 2)
            o_ref[0, g] = c.astype(o_ref.dtype)


def _compress(tk, tv, w1k, w1v, pek, pev, w1ko, w1vo, w2k, w2v, cos, sin, batch, n_chunk):
    width = tk.shape[1]
    out = jax.ShapeDtypeStruct((batch, NSA_KV_HEADS, n_chunk, LANES), BF16)
    ospec = pl.BlockSpec((1, NSA_KV_HEADS, n_chunk, LANES), lambda b: (b, 0, 0, 0))
    return pl.pallas_call(
        _compress_kernel,
        grid=(batch,),
        in_specs=[pl.BlockSpec((n_chunk, width), lambda b: (b, 0)),
                  pl.BlockSpec((n_chunk, width), lambda b: (b, 0)),
                  _full(w1k.shape), _full(w1v.shape), _full(pek.shape), _full(pev.shape),
                  _full(w1ko.shape), _full(w1vo.shape), _full(w2k.shape), _full(w2v.shape),
                  _full(cos.shape), _full(sin.shape)],
        out_specs=[ospec, ospec],
        out_shape=[out, out],
        compiler_params=_params("parallel"),
        name="nsa_compress",
    )(tk, tv, w1k, w1v, pek, pev, w1ko, w1vo, w2k, w2v, cos, sin)


def _nsa_kernel(q_ref, kc_ref, vc_ref, ks_ref, vs_ref, g_ref, win_ref, ov_ref, e_ref, o_ref,
                bias_scr, s_scr, p_scr, m_scr, l_scr, a_scr, acc_scr, *, qblk, seq, n_sb, n_sel, tk_sel):
    i = pl.program_id(2)
    start = i * qblk
    n_cmp = kc_ref.shape[2]
    lane = lax.broadcasted_iota(I32, (qblk, LANES), 1)
    lo = lane < HEAD_DIM
    q = _stack_group_queries(q_ref, 0, GQA_REP)
    tq = start + lax.broadcasted_iota(I32, (qblk, 1), 0)

    kc = kc_ref[0, 0]
    vc = vc_ref[0, 0]
    cmp_end = lax.broadcasted_iota(I32, (qblk, n_cmp), 1) * NSA_CMP_STRIDE + (NSA_CMP_LEN - 1)
    vis = jnp.where(cmp_end <= tq, 1.0, 0.0).astype(F32)
    vis4 = jnp.concatenate([vis] * GQA_REP, axis=0)
    s_c = lax.dot_general(q, kc, NT_DIMS, preferred_element_type=F32)
    s_c = jnp.where(vis4 > 0.5, s_c, NEG_BIG)
    p_c = jnp.exp(s_c - jnp.max(s_c, axis=-1, keepdims=True)) * vis4
    p_c = p_c / jnp.maximum(jnp.sum(p_c, axis=-1, keepdims=True), 1e-30)
    o_c = jnp.dot(p_c.astype(BF16), vc, preferred_element_type=F32)

    p_sum = p_c[0:qblk]
    for r in range(1, GQA_REP):
        p_sum = p_sum + p_c[r * qblk:(r + 1) * qblk]
    imp_t = lax.dot_general(ov_ref[...], p_sum, NT_DIMS, preferred_element_type=F32,
                            precision=lax.Precision.HIGHEST)[:n_sb]
    blk_t = lax.broadcasted_iota(I32, (n_sb, qblk), 0)
    tq_t = start + lax.broadcasted_iota(I32, (n_sb, qblk), 1)
    cur_t = tq_t // NSA_SEL_LEN
    forced = jnp.where(blk_t == 0, 1, jnp.where(blk_t == cur_t, 1, jnp.where(blk_t == cur_t - 1, 1, 0)))
    score_t = jnp.where(forced == 1, FORCE_SCORE, jnp.where(blk_t * NSA_SEL_LEN > tq_t, -1.0, imp_t))
    sel_t = jnp.where(_rank_desc(score_t, n_sb, 0) < n_sel, 1.0, 0.0).astype(F32)
    if n_sb < LANES:
        sel_t = jnp.concatenate([sel_t, jnp.zeros((LANES - n_sb, qblk), F32)], axis=0)
    sel = jnp.transpose(sel_t).astype(BF16)

    _flash_init(m_scr, l_scr, acc_scr)
    chunks_per_head = qblk // FLASH_ROWS

    def body(j, carry):
        k0 = pl.multiple_of(j * tk_sel, tk_sel)
        picked = jnp.dot(sel, e_ref[:, pl.ds(k0, tk_sel)], preferred_element_type=F32)
        tk = k0 + lax.broadcasted_iota(I32, (qblk, tk_sel), 1)
        bias_scr[...] = jnp.where(picked > 0.5, jnp.where(tk <= tq, 0.0, -jnp.inf), -jnp.inf).astype(F32)

        def bias_fn(c):
            r0 = (c % chunks_per_head) * FLASH_ROWS
            return bias_scr[r0:r0 + FLASH_ROWS, :]

        _flash_step(q, ks_ref[pl.ds(k0, tk_sel), :], vs_ref[pl.ds(k0, tk_sel), :], bias_fn, LOG2_E,
                    s_scr, p_scr, m_scr, l_scr, a_scr, acc_scr)
        return carry

    n_tiles = (start + qblk + tk_sel - 1) // tk_sel
    lax.fori_loop(0, n_tiles, body, 0)
    o_s = acc_scr[...] / jnp.sum(l_scr[...], axis=-1, keepdims=True)

    for j in range(GQA_REP // 2):

        def gate(branch, pair=j):
            c = branch * GQA_REP + 2 * pair
            return jnp.where(lo, g_ref[:, c:c + 1], g_ref[:, c + 1:c + 2])

        rows_e = slice((2 * j) * qblk, (2 * j + 1) * qblk)
        rows_o = slice((2 * j + 1) * qblk, (2 * j + 2) * qblk)
        oc = jnp.where(lo, o_c[rows_e], o_c[rows_o])
        os_ = jnp.where(lo, o_s[rows_e], o_s[rows_o])
        out = gate(0) * oc + gate(1) * os_ + win_ref[:, j * LANES:(j + 1) * LANES]
        o_ref[:, j * LANES:(j + 1) * LANES] = out.astype(o_ref.dtype)


def _nsa_cmp_sel(qb, kcmp, vcmp, ks, vs, gates, win, overlap, expand, batch, seq, qblk, n_sb, n_sel, tk_sel):
    n = qb.shape[0]
    nq = seq // qblk
    n_cmp = kcmp.shape[2]
    gw = GQA_REP * HEAD_DIM
    rows = GQA_REP * qblk
    kern = functools.partial(_nsa_kernel, qblk=qblk, seq=seq, n_sb=n_sb, n_sel=n_sel, tk_sel=tk_sel)
    return pl.pallas_call(
        kern,
        grid=(batch, NSA_KV_HEADS, nq),
        in_specs=[pl.BlockSpec((qblk, gw), lambda b, g, i: (b * nq + i, g)),
                  pl.BlockSpec((1, 1, n_cmp, LANES), lambda b, g, i: (b, g, 0, 0)),
                  pl.BlockSpec((1, 1, n_cmp, LANES), lambda b, g, i: (b, g, 0, 0)),
                  pl.BlockSpec((seq, LANES), lambda b, g, i: (b, g)),
                  pl.BlockSpec((seq, LANES), lambda b, g, i: (b, g)),
                  pl.BlockSpec((qblk, LANES), lambda b, g, i: (b * nq + i, g)),
                  pl.BlockSpec((qblk, gw), lambda b, g, i: (b * nq + i, g)),
                  _full(overlap.shape), _full(expand.shape)],
        out_specs=pl.BlockSpec((qblk, gw), lambda b, g, i: (b * nq + i, g)),
        out_shape=jax.ShapeDtypeStruct((n, NSA_Q_HEADS * HEAD_DIM), BF16),
        scratch_shapes=[pltpu.VMEM((qblk, tk_sel), F32),
                        pltpu.VMEM((rows, tk_sel), F32), pltpu.VMEM((rows, tk_sel), BF16),
                        pltpu.VMEM((rows, LANES), F32), pltpu.VMEM((rows, LANES), F32),
                        pltpu.VMEM((rows, LANES), F32), pltpu.VMEM((rows, LANES), F32)],
        compiler_params=_params("parallel", "parallel", "arbitrary"),
        name="nsa_cmp_sel",
    )(qb, kcmp, vcmp, ks, vs, gates, win, overlap, expand)


def _oproj_ln_kernel(*refs, n_parts, alpha):
    o_refs = refs[:n_parts]
    w_refs = refs[n_parts:2 * n_parts]
    x_ref, g_ref, b_ref, y_ref, ypk_ref = refs[2 * n_parts:]
    h = jnp.dot(o_refs[0][...], w_refs[0][...], preferred_element_type=F32)
    for o_ref, w_ref in zip(o_refs[1:], w_refs[1:]):
        h = h + jnp.dot(o_ref[...], w_ref[...], preferred_element_type=F32)
    y = _layer_norm_rows(alpha * x_ref[...] + h, g_ref[...], b_ref[...])
    y_ref[...] = y
    ypk_ref[...] = _pack_halves(y)


def _oproj_ln(o_parts, w_parts, x2, g, b, alpha, tm):
    n, d = x2.shape
    kern = functools.partial(_oproj_ln_kernel, n_parts=len(o_parts), alpha=alpha)
    in_specs = ([pl.BlockSpec((tm, o.shape[1]), lambda i: (i, 0)) for o in o_parts]
                + [_full(w.shape) for w in w_parts]
                + [pl.BlockSpec((tm, d), lambda i: (i, 0)), _full((1, d)), _full((1, d))])
    return pl.pallas_call(
        kern,
        grid=(n // tm,),
        in_specs=in_specs,
        out_specs=[pl.BlockSpec((tm, d), lambda i: (i, 0)), pl.BlockSpec((tm, d // 2), lambda i: (i, 0))],
        out_shape=[jax.ShapeDtypeStruct((n, d), F32), jax.ShapeDtypeStruct((n, d // 2), U32)],
        compiler_params=_params("parallel"),
        name="oproj_ln",
    )(*o_parts, *w_parts, x2, g, b)


def _rms_rows(v, g):
    return v * lax.rsqrt(jnp.mean(v * v, axis=-1, keepdims=True) + RMS_EPS) * g


def _mla_proj_kernel(x_ref, win_ref, qn_ref, kvn_ref, wuq_ref, wuk_ref, wuv_ref, cos_ref, sin_ref,
                     q_ref, k_ref, v_ref):
    xb = x_ref[...].astype(BF16)
    cos = cos_ref[...]
    sin = sin_ref[...]
    lat = jnp.dot(xb, win_ref[...], preferred_element_type=F32)
    cq = _rms_rows(lat[:, :MLA_Q_LORA], qn_ref[...]).astype(BF16)
    ckv = _rms_rows(lat[:, MLA_Q_LORA:MLA_Q_LORA + MLA_KV_LORA], kvn_ref[...]).astype(BF16)
    kr = _rope_slab(lat[:, MLA_Q_LORA + MLA_KV_LORA:], cos, sin, MLA_ROPE // 2)
    q = jnp.dot(cq, wuq_ref[...], preferred_element_type=F32)
    k = jnp.dot(ckv, wuk_ref[...], preferred_element_type=F32)
    for h in range(MLA_HEADS):
        sl = slice(h * LANES, (h + 1) * LANES)
        q_ref[:, sl] = _rope_slab(q[:, sl], cos, sin, MLA_ROPE // 2).astype(BF16)
        k_ref[:, sl] = (k[:, sl] + kr).astype(BF16)
    v_ref[...] = jnp.dot(ckv, wuv_ref[...], preferred_element_type=F32).astype(BF16)


def _mla_proj(x2, w_in, qn, kvn, wuq, wuk, wuv, cos, sin, seq, tm):
    n, d = x2.shape
    sblk = seq // tm
    hw = MLA_HEADS * LANES
    vw = MLA_HEADS * MLA_V
    return pl.pallas_call(
        _mla_proj_kernel,
        grid=(n // tm,),
        in_specs=[pl.BlockSpec((tm, d), lambda i: (i, 0)),
                  _full(w_in.shape), _full(qn.shape), _full(kvn.shape),
                  _full(wuq.shape), _full(wuk.shape), _full(wuv.shape),
                  pl.BlockSpec((tm, LANES), lambda i: (i % sblk, 0)),
                  pl.BlockSpec((tm, LANES), lambda i: (i % sblk, 0))],
        out_specs=[pl.BlockSpec((tm, hw), lambda i: (i, 0)),
                   pl.BlockSpec((tm, hw), lambda i: (i, 0)),
                   pl.BlockSpec((tm, vw), lambda i: (i, 0))],
        out_shape=[jax.ShapeDtypeStruct((n, hw), BF16), jax.ShapeDtypeStruct((n, hw), BF16),
                   jax.ShapeDtypeStruct((n, vw), BF16)],
        compiler_params=_params("parallel"),
        name="mla_proj",
    )(x2, w_in, qn, kvn, wuq, wuk, wuv, cos, sin)


FLASH_ROWS = 32
LOG2_E = 1.4426950408889634


def _flash_init(m_scr, l_scr, acc_scr):
    m_scr[...] = jnp.full(m_scr.shape, -jnp.inf, F32)
    l_scr[...] = jnp.zeros(l_scr.shape, F32)
    acc_scr[...] = jnp.zeros(acc_scr.shape, F32)


def _flash_step(q, k, v, bias_fn, exp_scale, s_scr, p_scr, m_scr, l_scr, a_scr, acc_scr):
    _flash_scores(q, k, s_scr)
    _flash_update(v, bias_fn, exp_scale, s_scr, p_scr, m_scr, l_scr, a_scr, acc_scr)


def _flash_scores(q, k, s_scr):
    s_scr[...] = lax.dot_general(q, k, NT_DIMS, preferred_element_type=F32)


def _flash_update(v, bias_fn, exp_scale, s_scr, p_scr, m_scr, l_scr, a_scr, acc_scr):
    _flash_max(bias_fn, exp_scale, s_scr, m_scr, a_scr)
    _flash_weights(v, exp_scale, s_scr, p_scr, m_scr, l_scr, a_scr, acc_scr)


def _flash_max(bias_fn, exp_scale, s_scr, m_scr, a_scr):
    rows, keys = s_scr.shape
    n_chunks = rows // FLASH_ROWS
    slabs = keys // LANES
    for c in range(n_chunks):
        r = slice(c * FLASH_ROWS, (c + 1) * FLASH_ROWS)
        s = s_scr[r, :]
        if bias_fn is not None:
            s = s + bias_fn(c)
            s_scr[r, :] = s
        mx = s[:, 0:LANES]
        for j in range(1, slabs):
            mx = jnp.maximum(mx, s[:, j * LANES:(j + 1) * LANES])
        a_scr[r, :] = mx
    m_old = m_scr[...]
    m_new = jnp.maximum(m_old, jnp.broadcast_to(jnp.max(a_scr[...], axis=-1, keepdims=True), m_old.shape))
    m_scr[...] = m_new
    a_scr[...] = jnp.exp2((m_old - m_new) * exp_scale)


def _flash_weights(v, exp_scale, s_scr, p_scr, m_scr, l_scr, a_scr, acc_scr):
    rows, keys = s_scr.shape
    n_chunks = rows // FLASH_ROWS
    slabs = keys // LANES
    for c in range(n_chunks):
        r = slice(c * FLASH_ROWS, (c + 1) * FLASH_ROWS)
        m_rows = m_scr[r, :]
        ps = None
        for j in range(slabs):
            cols = slice(j * LANES, (j + 1) * LANES)
            p = jnp.exp2((s_scr[r, cols] - m_rows) * exp_scale)
            p_scr[r, cols] = p.astype(p_scr.dtype)
            ps = p if ps is None else ps + p
        l_scr[r, :] = a_scr[r, :] * l_scr[r, :] + ps
    acc_scr[...] = a_scr[...] * acc_scr[...] + jnp.dot(p_scr[...], v, preferred_element_type=F32)


def _mla_attn_kernel(q_ref, k_ref, v_ref, o_ref, s_scr, p_scr, m_scr, l_scr, a_scr, acc_scr, *, tq, scale):
    i = pl.program_id(2)
    lane = lax.broadcasted_iota(I32, (tq, LANES), 1)
    lo = lane < MLA_V
    for e in range(2):
        _flash_init(m_scr.at[e], l_scr.at[e], acc_scr.at[e])

    def diag_bias(c):
        row = c * FLASH_ROWS + lax.broadcasted_iota(I32, (FLASH_ROWS, tq), 0)
        col = lax.broadcasted_iota(I32, (FLASH_ROWS, tq), 1)
        return jnp.where(col <= row, 0.0, -jnp.inf).astype(F32)

    def tile(j, bias_fn):
        k0 = pl.multiple_of(j * tq, tq)
        v = v_ref[pl.ds(k0, tq), :]
        for e in range(2):
            _flash_scores(q_ref[:, e * LANES:(e + 1) * LANES], k_ref[pl.ds(k0, tq), e * LANES:(e + 1) * LANES],
                          s_scr.at[e])
        for e in range(2):
            _flash_max(bias_fn, scale * LOG2_E, s_scr.at[e], m_scr.at[e], a_scr.at[e])
        for e in range(2):
            _flash_weights(v, scale * LOG2_E, s_scr.at[e], p_scr.at[e], m_scr.at[e], l_scr.at[e],
                           a_scr.at[e], acc_scr.at[e])

    def body(j, carry):
        tile(j, None)
        return carry

    lax.fori_loop(0, i, body, 0)
    tile(i, diag_bias)
    o0 = acc_scr[0] / jnp.sum(l_scr[0], axis=-1, keepdims=True)
    o1 = acc_scr[1] / jnp.sum(l_scr[1], axis=-1, keepdims=True)
    o_ref[...] = jnp.where(lo, o0, o1).astype(o_ref.dtype)


def _mla_attention(q, k, v, batch, seq, tq):
    n = q.shape[0]
    nq = seq // tq
    kern = functools.partial(_mla_attn_kernel, tq=tq, scale=(MLA_NOPE + MLA_ROPE) ** -0.5)
    return pl.pallas_call(
        kern,
        grid=(batch, MLA_HEADS // 2, nq),
        in_specs=[pl.BlockSpec((tq, 2 * LANES), lambda b, p, i: (b * nq + i, p)),
                  pl.BlockSpec((seq, 2 * LANES), lambda b, p, i: (b, p)),
                  pl.BlockSpec((seq, LANES), lambda b, p, i: (b, p))],
        out_specs=pl.BlockSpec((tq, LANES), lambda b, p, i: (b * nq + i, p)),
        out_shape=jax.ShapeDtypeStruct((n, MLA_HEADS * MLA_V), BF16),
        scratch_shapes=[pltpu.VMEM((2, tq, tq), F32), pltpu.VMEM((2, tq, tq), BF16),
                        pltpu.VMEM((2, tq, LANES), F32), pltpu.VMEM((2, tq, LANES), F32), pltpu.VMEM((2, tq, LANES), F32),
                        pltpu.VMEM((2, tq, LANES), F32)],
        compiler_params=_params("parallel", "parallel", "arbitrary"),
        name="mla_attn",
    )(q, k, v)


def _router_kernel(x_ref, rw_ref, rb_ref, eidx_ref, gate_ref, pos_ref, cnt_ref, carry_ref, *, steps_per_chunk):
    i = pl.program_id(0)
    tm = x_ref.shape[0]
    per_group = N_EXPERTS // MOE_GROUPS

    @pl.when(i % steps_per_chunk == 0)
    def _():
        carry_ref[...] = jnp.zeros_like(carry_ref)

    x = x_ref[...]
    x_top = pltpu.bitcast(pltpu.bitcast(x, U32) & jnp.uint32(0xFFFF0000), F32)
    x_hi = x_top.astype(BF16)
    x_lo = (x - x_top).astype(BF16)
    logits = (jnp.dot(x_hi, rw_ref[0], preferred_element_type=F32)
              + jnp.dot(x_hi, rw_ref[1], preferred_element_type=F32)
              + jnp.dot(x_lo, rw_ref[0], preferred_element_type=F32))
    lt = jnp.transpose(logits)[:N_EXPERTS]
    s = _sigmoid(lt)
    sb = s + rb_ref[...]

    g3 = sb.reshape(MOE_GROUPS, per_group, tm)
    idx3 = lax.broadcasted_iota(I32, g3.shape, 1).astype(F32)
    m1 = jnp.max(g3, axis=1, keepdims=True)
    first = jnp.min(jnp.where(g3 == m1, idx3, float(per_group)), axis=1, keepdims=True)
    m2 = jnp.max(jnp.where(idx3 == first, -jnp.inf, g3), axis=1, keepdims=True)
    gscore = (m1 + m2).reshape(MOE_GROUPS, tm)
    gsel = _rank_desc(gscore, MOE_GROUPS, 0) < MOE_TOPK_GROUPS
    gsel3 = jnp.where(gsel, 1.0, 0.0).astype(F32).reshape(MOE_GROUPS, 1, tm)
    masked = jnp.where(gsel3 > 0.5, g3, -jnp.inf).reshape(N_EXPERTS, tm)
    e_iota = lax.broadcasted_iota(I32, (N_EXPERTS, tm), 0).astype(F32)
    work = masked
    picks, rows_g = [], []
    sel_f = jnp.zeros((N_EXPERTS, tm), F32)
    for r in range(MOE_TOP_K):
        top = jnp.max(work, axis=0, keepdims=True)
        first = jnp.min(jnp.where(work == top, e_iota, float(N_EXPERTS)), axis=0, keepdims=True)
        hit = e_iota == first
        picks.append(first)
        rows_g.append(jnp.sum(jnp.where(hit, s, 0.0), axis=0, keepdims=True))
        work = jnp.where(hit, -jnp.inf, work)
        sel_f = jnp.where(hit, 1.0, sel_f)
    gate_sum = rows_g[0]
    for g_row in rows_g[1:]:
        gate_sum = gate_sum + g_row
    rows_g = [g_row / gate_sum * MOE_ROUTED_SCALE for g_row in rows_g]

    sel_b = sel_f.astype(BF16)
    r_i = lax.broadcasted_iota(I32, (tm, tm), 0)
    c_i = lax.broadcasted_iota(I32, (tm, tm), 1)
    tri = jnp.where(r_i < c_i, 1.0, 0.0).astype(BF16)
    carry = carry_ref[:, 0:1]
    before = jnp.dot(sel_b, tri, preferred_element_type=F32) + carry
    carry_new = carry + jnp.sum(sel_b.astype(F32), axis=1, keepdims=True)
    carry_ref[...] = jnp.broadcast_to(carry_new, carry_ref.shape)
    cnt_ref[0] = jnp.broadcast_to(carry_new, carry_ref.shape).astype(I32)

    rows_p = [jnp.sum(jnp.where(e_iota == first, before, 0.0), axis=0, keepdims=True) for first in picks]
    pad = 8 - MOE_TOP_K
    eidx_ref[...] = jnp.concatenate(picks + [jnp.zeros((pad, tm), F32)], axis=0).astype(I32)
    gate_ref[...] = jnp.concatenate(rows_g + [jnp.zeros((pad, tm), F32)], axis=0)
    pos_ref[...] = jnp.concatenate(rows_p + [jnp.zeros((pad, tm), F32)], axis=0).astype(I32)


def _router(x2, rw, rb, n_chunk, tm):
    n, d = x2.shape
    steps = n // tm
    spc = steps // n_chunk
    kern = functools.partial(_router_kernel, steps_per_chunk=spc)
    row8 = pl.BlockSpec((8, tm), lambda i: (0, i))
    return pl.pallas_call(
        kern,
        grid=(steps,),
        in_specs=[pl.BlockSpec((tm, d), lambda i: (i, 0)), _full(rw.shape), _full(rb.shape)],
        out_specs=[row8, row8, row8, pl.BlockSpec((1, N_EXPERTS, LANES), lambda i: (i // spc, 0, 0))],
        out_shape=[jax.ShapeDtypeStruct((8, n), I32), jax.ShapeDtypeStruct((8, n), F32),
                   jax.ShapeDtypeStruct((8, n), I32),
                   jax.ShapeDtypeStruct((n_chunk, N_EXPERTS, LANES), I32)],
        scratch_shapes=[pltpu.VMEM((N_EXPERTS, LANES), F32)],
        compiler_params=_params("arbitrary"),
        name="moe_router",
    )(x2, rw, rb)


def _dest_kernel(tab_ref, eidx_ref, pos_ref, dest_ref, *, steps_per_chunk):
    chunk = pl.program_id(0) // steps_per_chunk
    eidx = eidx_ref[...]
    dest = pos_ref[...]
    for e in range(N_EXPERTS):
        dest = dest + jnp.where(eidx == e, tab_ref[chunk * N_EXPERTS + e], 0)
    dest_ref[...] = dest


def _dest_rows(pad_starts, eidx, pos, n_chunk, tm):
    n = eidx.shape[1]
    steps = n // tm
    row8 = lambda i, tab: (0, i)
    return pl.pallas_call(
        functools.partial(_dest_kernel, steps_per_chunk=steps // n_chunk),
        grid_spec=pltpu.PrefetchScalarGridSpec(
            num_scalar_prefetch=1, grid=(steps,),
            in_specs=[pl.BlockSpec((8, tm), row8), pl.BlockSpec((8, tm), row8)],
            out_specs=pl.BlockSpec((8, tm), row8)),
        out_shape=jax.ShapeDtypeStruct((8, n), I32),
        compiler_params=_params("parallel"),
        name="moe_dest",
    )(pad_starts, eidx, pos)


SC_ROWS = 64
SC_WORKERS = 32


def _sc_mesh():
    return plsc.VectorSubcoreMesh(core_axis_name="c", subcore_axis_name="s")


def _sc_worker_base(per_worker):
    return (lax.axis_index("s") * 2 + lax.axis_index("c")) * per_worker


def _sc_scatter_rows(x, dest, n_out):
    n, d = x.shape
    slots = dest.shape[0] // n
    per_w = n // SC_WORKERS

    def body(x_hbm, i_hbm, o_hbm, idx_v, rows_v, sem):
        base = _sc_worker_base(per_w)

        @pl.loop(0, per_w // SC_ROWS)
        def _(c):
            off = pl.multiple_of(base + c * SC_ROWS, 8)
            pltpu.sync_copy(x_hbm.at[pl.ds(off, SC_ROWS)], rows_v)
            for k in range(slots):
                pltpu.sync_copy(i_hbm.at[pl.ds(k * n + off, SC_ROWS)], idx_v)
                pltpu.async_copy(rows_v, o_hbm.at[idx_v], sem).wait()

    return pl.kernel(
        body, out_type=jax.ShapeDtypeStruct((n_out, d), x.dtype), mesh=_sc_mesh(),
        scratch_types=[pltpu.VMEM((SC_ROWS,), I32), pltpu.VMEM((SC_ROWS, d), x.dtype), pltpu.SemaphoreType.DMA],
        name="moe_dispatch_sc")(x, dest)


def _sc_gather_rows(table, idx):
    d = table.shape[1]
    r = idx.shape[0]
    per_w = r // SC_WORKERS

    def body(t_hbm, i_hbm, o_hbm, idx_v, rows_v, sem):
        base = _sc_worker_base(per_w)

        @pl.loop(0, per_w // SC_ROWS)
        def _(c):
            off = pl.multiple_of(base + c * SC_ROWS, 8)
            pltpu.sync_copy(i_hbm.at[pl.ds(off, SC_ROWS)], idx_v)
            pltpu.async_copy(t_hbm.at[idx_v], rows_v, sem).wait()
            pltpu.sync_copy(rows_v, o_hbm.at[pl.ds(off, SC_ROWS)])

    return pl.kernel(
        body, out_type=jax.ShapeDtypeStruct((r, d), table.dtype), mesh=_sc_mesh(),
        scratch_types=[pltpu.VMEM((SC_ROWS,), I32), pltpu.VMEM((SC_ROWS, d), table.dtype), pltpu.SemaphoreType.DMA],
        name="moe_combine_sc")(table, idx)


def _experts_kernel(be_ref, nv_ref, xs_ref, wg_ref, wu_ref, wd_ref, y_ref, wg_b, wu_b, wd_b):
    i = pl.program_id(0)
    prev = be_ref[jnp.maximum(i - 1, 0)]

    @pl.when(jnp.logical_or(i == 0, be_ref[i] != prev))
    def _():
        wg_b[...] = wg_ref[0, 0].astype(BF16)
        wu_b[...] = wu_ref[0, 0].astype(BF16)
        wd_b[...] = wd_ref[0, 0].astype(BF16)

    @pl.when(i < nv_ref[0])
    def _():
        x_lo, x_hi = _unpack_halves(xs_ref[...])
        x_lo, x_hi = x_lo.astype(BF16), x_hi.astype(BF16)
        half = x_lo.shape[1]

        def proj(w):
            return (jnp.dot(x_lo, w[:half, :], preferred_element_type=F32)
                    + jnp.dot(x_hi, w[half:, :], preferred_element_type=F32))

        h = _silu(proj(wg_b)) * proj(wu_b)
        y_ref[...] = _pack_halves(jnp.dot(h.astype(BF16), wd_b[...], preferred_element_type=F32))

    @pl.when(i >= nv_ref[0])
    def _():
        y_ref[...] = jnp.zeros_like(y_ref)


def _experts(blk_exp, n_valid, xs, wg, wu, wd, layer, tb):
    rows, dp = xs.shape
    d = 2 * dp
    grid_spec = pltpu.PrefetchScalarGridSpec(
        num_scalar_prefetch=2,
        grid=(rows // tb,),
        in_specs=[pl.BlockSpec((tb, dp), lambda i, be, nv: (i, 0)),
                  pl.BlockSpec((1, 1, d, D_EXPERT), lambda i, be, nv: (layer, be[i], 0, 0)),
                  pl.BlockSpec((1, 1, d, D_EXPERT), lambda i, be, nv: (layer, be[i], 0, 0)),
                  pl.BlockSpec((1, 1, D_EXPERT, d), lambda i, be, nv: (layer, be[i], 0, 0))],
        out_specs=pl.BlockSpec((tb, dp), lambda i, be, nv: (i, 0)),
        scratch_shapes=[pltpu.VMEM((d, D_EXPERT), BF16), pltpu.VMEM((d, D_EXPERT), BF16),
                        pltpu.VMEM((D_EXPERT, d), BF16)],
    )
    return pl.pallas_call(
        _experts_kernel,
        grid_spec=grid_spec,
        out_shape=jax.ShapeDtypeStruct((rows, dp), U32),
        compiler_params=_params("arbitrary"),
        name="moe_experts",
    )(blk_exp, n_valid, xs, wg, wu, wd)


def _combine_ln_kernel(x_ref, yg_ref, gt_ref, wgu_ref, wd_ref, g_ref, b_ref, o_ref, *, alpha):
    x = x_ref[...]
    gu = jnp.dot(x.astype(BF16), wgu_ref[...], preferred_element_type=F32)
    dsh = gu.shape[1] // 2
    h = _silu(gu[:, :dsh]) * gu[:, dsh:]
    f = jnp.dot(h.astype(BF16), wd_ref[...], preferred_element_type=F32)
    r_lo = r_hi = None
    for k in range(MOE_TOP_K):
        y_lo, y_hi = _unpack_halves(yg_ref[k])
        gate = gt_ref[:, k:k + 1]
        r_lo = gate * y_lo if r_lo is None else r_lo + gate * y_lo
        r_hi = gate * y_hi if r_hi is None else r_hi + gate * y_hi
    f = f + jnp.concatenate([r_lo, r_hi], axis=1)
    o_ref[...] = _layer_norm_rows(alpha * x + f, g_ref[...], b_ref[...])


def _combine_ln(x2, yg, gates_t, wgu, wd, g, b, alpha, tm):
    n, d = x2.shape
    kern = functools.partial(_combine_ln_kernel, alpha=alpha)
    return pl.pallas_call(
        kern,
        grid=(n // tm,),
        in_specs=[pl.BlockSpec((tm, d), lambda i: (i, 0)),
                  pl.BlockSpec((MOE_TOP_K, tm, d // 2), lambda i: (0, i, 0)),
                  pl.BlockSpec((tm, 8), lambda i: (i, 0)),
                  _full(wgu.shape), _full(wd.shape), _full((1, d)), _full((1, d))],
        out_specs=pl.BlockSpec((tm, d), lambda i: (i, 0)),
        out_shape=jax.ShapeDtypeStruct((n, d), F32),
        compiler_params=_params("parallel"),
        name="moe_combine_ln",
    )(x2, yg, gates_t, wgu, wd, g, b)


def _rope_tables(positions, dim, lane_offset, period):
    half = dim // 2
    inv_freq = ROPE_THETA ** (-2.0 * jnp.arange(half, dtype=jnp.float32) / dim)
    ang = positions.astype(jnp.float32)[:, None] * inv_freq[None, :]
    cos_h, sin_h = jnp.cos(ang), jnp.sin(ang)
    lanes = np.arange(LANES)
    rel = (lanes - lane_offset) % period
    active = (lanes >= lane_offset) & (rel < dim)
    fidx = np.where(active, rel % half, 0)
    sign = np.where(rel < half, -1.0, 1.0)
    cos = jnp.where(active[None, :], cos_h[:, fidx], 1.0)
    sin = jnp.where(active[None, :], sin_h[:, fidx] * sign[None, :], 0.0)
    return cos.astype(F32), sin.astype(F32)


def _gate_slabs(w):
    k = w.shape[0]
    g = w.reshape(k, 3, NSA_KV_HEADS, GQA_REP).transpose(0, 2, 1, 3).reshape(k, NSA_KV_HEADS, 3 * GQA_REP)
    return jnp.pad(g, ((0, 0), (0, 0), (0, LANES - 3 * GQA_REP))).reshape(k, NSA_KV_HEADS * LANES)


def _dup_heads(w):
    a, b = w[:, :HEAD_DIM], w[:, HEAD_DIM:]
    return jnp.concatenate([a, a, b, b], axis=1)


def _moe_layer(x1_and_packed, rw, rb, wg, wu, wd, layer, shg, shu, shd, ln_g, ln_b, alpha, tm):
    x1, x1_pk = x1_and_packed
    n, d = x1.shape
    n_chunk = 1
    tb = MOE_ROW_BLOCK
    rw_p = jnp.pad(rw, ((0, 0), (0, LANES - N_EXPERTS)))
    rw_top = lax.bitcast_convert_type(lax.bitcast_convert_type(rw_p, U32) & jnp.uint32(0xFFFF0000), F32)
    rw_split = jnp.stack([rw_top.astype(BF16), (rw_p - rw_top).astype(BF16)])
    eidx, gate, pos, cnt = _router(x1, rw_split, rb.reshape(N_EXPERTS, 1), n_chunk, tm)
    counts = cnt[:, :, 0].reshape(-1)
    padded = (counts + tb - 1) // tb * tb
    pad_ends = jnp.cumsum(padded)
    pad_starts = (pad_ends - padded).astype(I32)
    dest = _dest_rows(pad_starts, eidx, pos, n_chunk, tm)[:MOE_TOP_K].reshape(-1)
    n_rows = n * MOE_TOP_K
    n_blk = -(-(n_rows + n_chunk * N_EXPERTS * (tb - 1)) // tb)
    blk_first_row = jnp.arange(n_blk, dtype=I32) * tb
    owner = jnp.sum((pad_ends[None, :] <= blk_first_row[:, None]).astype(I32), axis=1)
    blk_exp = jnp.minimum(owner, n_chunk * N_EXPERTS - 1).astype(I32) % N_EXPERTS
    n_valid = (pad_ends[-1] // tb).astype(I32).reshape(1)

    xs = _sc_scatter_rows(x1_pk, dest, n_blk * tb)
    y = _experts(blk_exp, n_valid, xs, wg, wu, wd, layer, tb)
    yg = _sc_gather_rows(y, dest).reshape(MOE_TOP_K, n, d // 2)
    gates_t = gate.T
    sh_gu = jnp.concatenate([shg, shu], axis=-1).astype(BF16)
    return _combine_ln(x1, yg, gates_t, sh_gu, shd.astype(BF16), ln_g.reshape(1, d), ln_b.reshape(1, d), alpha, tm)


def _swa_nsa_layer(x2, batch, seq, w_in, sinks, pe_k, w_ck1, w_ck2, pe_v, w_cv1, w_cv2, w_out,
                   ln_g, ln_b, alpha, tm):
    n, d = x2.shape
    splits = np.cumsum([512, 128, 128, 512, 128, 128, 128, 128, 128, 128, 24])[:-1]
    qa, ka, va, qb, kc, vc, ksl, vsl, kw, vw, gts = jnp.split(w_in, [int(c) for c in splits], axis=1)
    w_all = jnp.concatenate(
        [qa, _dup_heads(ka), _dup_heads(va), qb, kc, vc, _dup_heads(ksl), _dup_heads(vsl),
         _dup_heads(kw), _dup_heads(vw), _gate_slabs(gts)], axis=1).astype(BF16)
    cos, sin = _rope_tables(jnp.arange(seq), HEAD_DIM, 0, HEAD_DIM)
    (q_a, k_a, v_a, q_b, k_c, v_c, k_s, v_s, k_w, v_w, gates) = _mix0_proj(x2, w_all, cos, sin, seq, tm)

    o_swa = _banded_attention(q_a, k_a, v_a, batch, seq, SWA_WINDOW, 128, sinks=sinks, name="swa")
    o_win = _banded_attention(q_b, k_w, v_w, batch, seq, NSA_WINDOW, 128, gates=gates,
                              gate_col=2 * GQA_REP, out_dtype=F32, name="nsa_win")

    n_chunk = seq // NSA_CMP_STRIDE
    cw = NSA_CMP_STRIDE * NSA_KV_HEADS * HEAD_DIM
    hid = NSA_CMP_HIDDEN

    def expand_w1(w1):
        halves = w1.reshape(2, NSA_CMP_STRIDE, HEAD_DIM, hid)
        out = jnp.zeros((NSA_KV_HEADS, NSA_CMP_STRIDE, NSA_KV_HEADS, HEAD_DIM, 2 * hid), w1.dtype)
        for g in range(NSA_KV_HEADS):
            out = out.at[g, :, g, :, :hid].set(halves[0]).at[g, :, g, :, hid:].set(halves[1])
        return out.reshape(NSA_KV_HEADS, cw, 2 * hid).astype(BF16)

    def pe_rows(pe):
        return jnp.pad(pe.reshape(1, NSA_CMP_LEN * HEAD_DIM), ((0, 15), (0, 0))).astype(BF16)

    cos_c, sin_c = _rope_tables(jnp.arange(n_chunk) * NSA_CMP_STRIDE + NSA_CMP_LEN - 1, HEAD_DIM, 0, HEAD_DIM)
    k_cmp, v_cmp = _compress(
        k_c.reshape(batch * n_chunk, cw), v_c.reshape(batch * n_chunk, cw),
        expand_w1(w_ck1), expand_w1(w_cv1), pe_rows(pe_k), pe_rows(pe_v),
        w_ck1.astype(BF16), w_cv1.astype(BF16),
        jnp.concatenate([w_ck2, w_ck2], axis=1).astype(BF16),
        jnp.concatenate([w_cv2, w_cv2], axis=1).astype(BF16),
        cos_c, sin_c, batch, n_chunk)

    n_sb = seq // NSA_SEL_LEN
    n_sel = min(NSA_N_SEL, n_sb)
    cs = np.arange(n_chunk) * NSA_CMP_STRIDE
    bs = np.arange(n_sb) * NSA_SEL_LEN
    ov = np.clip(np.minimum(cs[:, None] + NSA_CMP_LEN, bs[None, :] + NSA_SEL_LEN)
                 - np.maximum(cs[:, None], bs[None, :]), 0, None) / NSA_CMP_LEN
    overlap = jnp.asarray(np.pad(ov, ((0, 0), (0, LANES - n_sb))).T, F32)
    expand = jnp.asarray(np.arange(LANES)[:, None] == (np.arange(seq)[None, :] // NSA_SEL_LEN), BF16)
    tk_sel = min(512, seq)
    o_nsa = _nsa_cmp_sel(q_b, k_cmp, v_cmp, k_s, v_s, gates, o_win, overlap, expand,
                         batch, seq, 128, n_sb, n_sel, tk_sel)
    w_o = w_out.astype(BF16)
    half = SWA_Q_HEADS * HEAD_DIM
    return _oproj_ln([o_swa, o_nsa], [w_o[:half], w_o[half:]], x2, ln_g.reshape(1, d), ln_b.reshape(1, d),
                     alpha, tm)


def _mla_layer(x2, batch, seq, w_in, q_norm, kv_norm, w_uq, w_ukv, w_out, ln_g, ln_b, alpha, tm):
    n, d = x2.shape
    dq = MLA_NOPE + MLA_ROPE
    w_kr = jnp.zeros((d, LANES), w_in.dtype).at[:, MLA_NOPE:dq].set(w_in[:, MLA_Q_LORA + MLA_KV_LORA:])
    w_in_p = jnp.concatenate([w_in[:, :MLA_Q_LORA + MLA_KV_LORA], w_kr], axis=1).astype(BF16)
    wuq = jnp.pad(w_uq.reshape(MLA_Q_LORA, MLA_HEADS, dq), ((0, 0), (0, 0), (0, LANES - dq)))
    wuq = wuq.reshape(MLA_Q_LORA, MLA_HEADS * LANES).astype(BF16)
    wukv = w_ukv.reshape(MLA_KV_LORA, MLA_HEADS, MLA_NOPE + MLA_V)
    wuk = jnp.pad(wukv[:, :, :MLA_NOPE], ((0, 0), (0, 0), (0, LANES - MLA_NOPE)))
    wuk = wuk.reshape(MLA_KV_LORA, MLA_HEADS * LANES).astype(BF16)
    wuv = wukv[:, :, MLA_NOPE:].reshape(MLA_KV_LORA, MLA_HEADS * MLA_V).astype(BF16)
    cos, sin = _rope_tables(jnp.arange(seq), MLA_ROPE, MLA_NOPE, LANES)
    q, k, v = _mla_proj(x2, w_in_p, q_norm.reshape(1, -1), kv_norm.reshape(1, -1), wuq, wuk, wuv, cos, sin, seq, tm)
    o = _mla_attention(q, k, v, batch, seq, min(512, seq))
    return _oproj_ln([o], [w_out.astype(BF16)], x2, ln_g.reshape(1, d), ln_b.reshape(1, d), alpha, tm)


def kernel(x, swa_nsa_w_in, swa_sinks, nsa_cmp_pe_k, nsa_cmp_k_w1, nsa_cmp_k_w2, nsa_cmp_pe_v, nsa_cmp_v_w1, nsa_cmp_v_w2, swa_nsa_w_out, mla_w_in, mla_q_norm, mla_kv_norm, mla_w_uq, mla_w_ukv, mla_w_out, ln_mix_g, ln_mix_b, ln_ffn_g, ln_ffn_b, router_w, router_bias, expert_w_gate, expert_w_up, expert_w_down, shared_w_gate, shared_w_up, shared_w_down):
    batch, seq, d = x.shape
    depth = ln_mix_g.shape[0]
    alpha = (2 * depth) ** 0.25
    tm = min(512, seq)
    x2 = x.reshape(batch * seq, d)
    for layer in range(depth):
        j = layer // 2
        if layer % 2 == 0:
            x2 = _swa_nsa_layer(x2, batch, seq, swa_nsa_w_in[j], swa_sinks[j], nsa_cmp_pe_k[j],
                                nsa_cmp_k_w1[j], nsa_cmp_k_w2[j], nsa_cmp_pe_v[j], nsa_cmp_v_w1[j],
                                nsa_cmp_v_w2[j], swa_nsa_w_out[j], ln_mix_g[layer], ln_mix_b[layer], alpha, tm)
        else:
            x2 = _mla_layer(x2, batch, seq, mla_w_in[j], mla_q_norm[j], mla_kv_norm[j], mla_w_uq[j],
                            mla_w_ukv[j], mla_w_out[j], ln_mix_g[layer], ln_mix_b[layer], alpha, tm)
        x2 = _moe_layer(x2, router_w[layer], router_bias[layer], expert_w_gate, expert_w_up, expert_w_down, layer,
                        shared_w_gate[layer], shared_w_up[layer], shared_w_down[layer],
                        ln_ffn_g[layer], ln_ffn_b[layer], alpha, tm)
    return x2.reshape(batch, seq, d)
```

```python
import functools
import math

import numpy as np
import jax
import jax.numpy as jnp
from jax import lax
from jax.experimental import pallas as pl
from jax.experimental.pallas import tpu as pltpu
from jax.experimental.pallas import tpu_sc as plsc

F32 = jnp.float32
BF16 = jnp.bfloat16
I32 = jnp.int32

LANES = 128
VMEM_LIMIT = 48 * 1024 * 1024

ROPE_THETA = 10000.0
LN_EPS = 1e-5
RMS_EPS = 1e-6
NEG_BIG = -1e30
FORCE_SCORE = 1e4

HEAD_DIM = 64
SWA_Q_HEADS = 8
SWA_KV_HEADS = 2
SWA_WINDOW = 128
NSA_Q_HEADS = 8
NSA_KV_HEADS = 2
NSA_CMP_LEN = 32
NSA_CMP_STRIDE = 16
NSA_CMP_HIDDEN = 128
NSA_SEL_LEN = 64
NSA_N_SEL = 16
NSA_WINDOW = 512
GQA_REP = 4

MLA_HEADS = 16
MLA_NOPE = 64
MLA_ROPE = 32
MLA_V = 64
MLA_Q_LORA = 384
MLA_KV_LORA = 256

N_EXPERTS = 64
MOE_GROUPS = 8
MOE_TOPK_GROUPS = 4
MOE_TOP_K = 6
D_EXPERT = 256
MOE_ROUTED_SCALE = 2.5
MOE_ROW_BLOCK = 512

NT_DIMS = (((1,), (1,)), ((), ()))


def _params(*sem):
    return pltpu.CompilerParams(dimension_semantics=sem, vmem_limit_bytes=VMEM_LIMIT)


def _full(shape):
    nd = len(shape)
    return pl.BlockSpec(shape, lambda *_: (0,) * nd)


def _rope_slab(y, cos, sin, half):
    lane = lax.broadcasted_iota(I32, y.shape, 1)
    first = (lane % (2 * half)) < half
    rot = jnp.where(first, pltpu.roll(y, LANES - half, 1), pltpu.roll(y, half, 1))
    return y * cos + rot * sin


def _layer_norm_rows(v, g, b):
    mu = jnp.mean(v, axis=-1, keepdims=True)
    vc = v - mu
    var = jnp.mean(vc * vc, axis=-1, keepdims=True)
    return vc * lax.rsqrt(var + LN_EPS) * g + b


def _silu(v):
    return v * (1.0 / (1.0 + jnp.exp(-v)))


def _sigmoid(v):
    return 1.0 / (1.0 + jnp.exp(-v))


U32 = jnp.uint32


def _pack_halves(v):
    w = v.shape[1] // 2
    lo = pltpu.bitcast(v[:, :w].astype(BF16).astype(F32), U32)
    hi = pltpu.bitcast(v[:, w:].astype(BF16).astype(F32), U32)
    return (lo >> 16) | (hi & jnp.uint32(0xFFFF0000))


def _unpack_halves(u):
    lo = pltpu.bitcast(u << 16, F32)
    hi = pltpu.bitcast(u & jnp.uint32(0xFFFF0000), F32)
    return lo, hi


def _ones_half(shape):
    lane = lax.broadcasted_iota(I32, shape, 1)
    return jnp.where(lane < HEAD_DIM, 0.0, 1.0).astype(F32)


def _stack_group_queries(q_ref, group, rep):
    lane = lax.broadcasted_iota(I32, (q_ref.shape[0], LANES), 1)
    keep_lo = jnp.where(lane < HEAD_DIM, 1.0, 0.0).astype(q_ref.dtype)
    keep_hi = jnp.where(lane < HEAD_DIM, 0.0, 1.0).astype(q_ref.dtype)
    parts = []
    for r in range(rep):
        h = group * rep + r
        slab = q_ref[:, (h // 2) * LANES:(h // 2 + 1) * LANES]
        parts.append(slab * (keep_lo if h % 2 == 0 else keep_hi))
    return jnp.concatenate(parts, axis=0)


def _rank_desc(score, n_valid, axis):
    idx = lax.broadcasted_iota(I32, score.shape, axis)
    rank = jnp.zeros(score.shape, I32)
    for j in range(n_valid):
        other = lax.slice_in_dim(score, j, j + 1, axis=axis)
        ahead = jnp.where(other > score, 1, jnp.where(other == score, jnp.where(idx > j, 1, 0), 0))
        rank = rank + ahead
    return rank


_MIX0_OUTS = (
    ("qa", 512, "rope_q", BF16), ("ka", 256, "rope", BF16), ("va", 256, "v_ones", BF16),
    ("qb", 512, "rope_q", BF16), ("kc", 128, "plain", BF16), ("vc", 128, "plain", BF16),
    ("ks", 256, "rope", BF16), ("vs", 256, "v_ones", BF16),
    ("kw", 256, "rope", BF16), ("vw", 256, "v_ones", BF16),
    ("gates", 256, "sigmoid", F32),
)


def _mix0_proj_kernel(x_ref, w_ref, cos_ref, sin_ref, *out_refs):
    xb = x_ref[...].astype(BF16)
    cos = cos_ref[...]
    sin = sin_ref[...]
    off = 0
    for (_, width, kind, _), o_ref in zip(_MIX0_OUTS, out_refs):
        y = jnp.dot(xb, w_ref[:, off:off + width], preferred_element_type=F32)
        for c in range(width // LANES):
            yc = y[:, c * LANES:(c + 1) * LANES]
            if kind in ("rope", "rope_q"):
                yc = _rope_slab(yc, cos, sin, HEAD_DIM // 2)
            if kind == "rope_q":
                yc = yc * (HEAD_DIM ** -0.5 * LOG2_E)
            if kind == "v_ones":
                yc = yc + _ones_half(yc.shape)
            if kind == "sigmoid":
                yc = _sigmoid(yc)
            o_ref[:, c * LANES:(c + 1) * LANES] = yc.astype(o_ref.dtype)
        off += width


def _mix0_proj(x2, w_all, cos, sin, seq, tm):
    n = x2.shape[0]
    d = x2.shape[1]
    wtot = w_all.shape[1]
    sblk = seq // tm
    return pl.pallas_call(
        _mix0_proj_kernel,
        grid=(n // tm,),
        in_specs=[pl.BlockSpec((tm, d), lambda i: (i, 0)),
                  _full((d, wtot)),
                  pl.BlockSpec((tm, LANES), lambda i: (i % sblk, 0)),
                  pl.BlockSpec((tm, LANES), lambda i: (i % sblk, 0))],
        out_specs=[pl.BlockSpec((tm, w), lambda i: (i, 0)) for _, w, _, _ in _MIX0_OUTS],
        out_shape=[jax.ShapeDtypeStruct((n, w), dt) for _, w, _, dt in _MIX0_OUTS],
        compiler_params=_params("parallel"),
        name="mix0_proj",
    )(x2, w_all, cos, sin)


def _banded_kernel(*refs, window, blk, seq, has_sink, gate_col):
    if has_sink:
        sink_ref, refs = refs[0], refs[1:]
    q_ref, k_ref, v_ref = refs[:3]
    g_ref = refs[3] if gate_col is not None else None
    o_ref, bias_scr, s_scr, p_scr, m_scr, a_scr, acc_scr = refs[-7:]
    i = pl.program_id(1)
    start = i * blk
    span = min(window + blk, seq)
    k0 = pl.multiple_of(jnp.maximum(start - window, 0), LANES)
    tq = start + lax.broadcasted_iota(I32, (blk, span), 0)
    tk = k0 + lax.broadcasted_iota(I32, (blk, span), 1)
    diff = tq - tk
    bias_scr[...] = jnp.where(diff >= 0, jnp.where(diff < window, 0.0, -jnp.inf), -jnp.inf).astype(F32)
    chunks_per_head = blk // FLASH_ROWS

    def bias_fn(c):
        r0 = (c % chunks_per_head) * FLASH_ROWS
        return bias_scr[r0:r0 + FLASH_ROWS, :]

    lane = lax.broadcasted_iota(I32, (blk, LANES), 1)
    lo = lane < HEAD_DIM
    n_groups = q_ref.shape[1] // (GQA_REP * HEAD_DIM)
    for g in range(n_groups):
        _flash_init(m_scr.at[g], acc_scr.at[g])
        _flash_scores(_stack_group_queries(q_ref, g, GQA_REP), k_ref[pl.ds(k0, span), g * LANES:(g + 1) * LANES],
                      s_scr.at[g])
    for g in range(n_groups):
        _flash_max(bias_fn, s_scr.at[g], m_scr.at[g], a_scr.at[g])
    for g in range(n_groups):
        _flash_weights(v_ref[pl.ds(k0, span), g * LANES:(g + 1) * LANES], s_scr.at[g], p_scr.at[g],
                       m_scr.at[g], a_scr.at[g], acc_scr.at[g])
    for g in range(n_groups):
        acc = acc_scr[g]
        if has_sink:
            m = m_scr[g]
            sink = jnp.concatenate(
                [jnp.full((blk, LANES), sink_ref[g * GQA_REP + r] * LOG2_E, F32) for r in range(GQA_REP)], axis=0)
            m_all = jnp.maximum(m, sink)
            acc = acc * jnp.exp2(m - m_all) + _ones_half(acc.shape) * jnp.exp2(sink - m_all)
        o = _flash_finish(acc)
        for j in range(GQA_REP // 2):
            slab = _pair_slab(o[(2 * j) * blk:(2 * j + 1) * blk], o[(2 * j + 1) * blk:(2 * j + 2) * blk])
            if gate_col is not None:
                c = g * LANES + gate_col + 2 * j
                gate = jnp.where(lo, g_ref[:, c:c + 1], g_ref[:, c + 1:c + 2])
                slab = slab * gate
            col = (g * (GQA_REP // 2) + j) * LANES
            o_ref[:, col:col + LANES] = slab.astype(o_ref.dtype)


def _banded_attention(q, k, v, batch, seq, window, blk, sinks=None, gates=None, gate_col=None,
                      out_dtype=None, name="banded"):
    n, qw = q.shape
    kw = k.shape[1]
    nblk = seq // blk
    span = min(window + blk, seq)
    groups = qw // (GQA_REP * HEAD_DIM)
    rows = GQA_REP * blk
    kern = functools.partial(_banded_kernel, window=window, blk=blk, seq=seq,
                             has_sink=sinks is not None, gate_col=gate_col)
    in_specs = []
    args = []
    if sinks is not None:
        in_specs.append(pl.BlockSpec(memory_space=pltpu.SMEM))
        args.append(sinks)
    in_specs += [pl.BlockSpec((blk, qw), lambda b, i: (b * nblk + i, 0)),
                 pl.BlockSpec((seq, kw), lambda b, i: (b, 0)),
                 pl.BlockSpec((seq, kw), lambda b, i: (b, 0))]
    args += [q, k, v]
    if gates is not None:
        in_specs.append(pl.BlockSpec((blk, gates.shape[1]), lambda b, i: (b * nblk + i, 0)))
        args.append(gates)
    return pl.pallas_call(
        kern,
        grid=(batch, nblk),
        in_specs=in_specs,
        out_specs=pl.BlockSpec((blk, qw), lambda b, i: (b * nblk + i, 0)),
        out_shape=jax.ShapeDtypeStruct((n, qw), BF16 if out_dtype is None else out_dtype),
        scratch_shapes=[pltpu.VMEM((blk, span), F32),
                        pltpu.VMEM((groups, rows, span), F32), pltpu.VMEM((groups, rows, span), BF16)]
                       + [pltpu.VMEM((groups, rows, LANES), F32)] * 3,
        compiler_params=_params("parallel", "arbitrary"),
        name=name,
    )(*args)


def _gelu_tanh(v):
    return 0.5 * v * (1.0 + jnp.tanh(math.sqrt(2.0 / math.pi) * (v + 0.044715 * (v * v * v))))


def _compress_kernel(tk_ref, tv_ref, w1k_ref, w1v_ref, pek_ref, pev_ref, w1ko_ref, w1vo_ref,
                     w2k_ref, w2v_ref, cos_ref, sin_ref, kc_ref, vc_ref):
    n_chunk = tk_ref.shape[0]
    hid = NSA_CMP_HIDDEN
    for t_ref, w1_ref, pe_ref, w1o_ref, w2_ref, o_ref, rope in (
            (tk_ref, w1k_ref, pek_ref, w1ko_ref, w2k_ref, kc_ref, True),
            (tv_ref, w1v_ref, pev_ref, w1vo_ref, w2v_ref, vc_ref, False)):
        pe_term = jnp.dot(pe_ref[...], w1o_ref[...], preferred_element_type=F32)[0:1, :]
        t = t_ref[...]
        for g in range(NSA_KV_HEADS):
            uv = jnp.dot(t, w1_ref[g], preferred_element_type=F32)
            nxt = pltpu.roll(uv[:, hid:], n_chunk - 1, 0)
            h = _gelu_tanh(uv[:, :hid] + nxt + pe_term)
            c = jnp.dot(h.astype(BF16), w2_ref[...], preferred_element_type=F32)
            if rope:
                c = _rope_slab(c, cos_ref[...], sin_ref[...], HEAD_DIM // 2)
            o_ref[0, g] = c.astype(o_ref.dtype)


def _compress(tk, tv, w1k, w1v, pek, pev, w1ko, w1vo, w2k, w2v, cos, sin, batch, n_chunk):
    width = tk.shape[1]
    out = jax.ShapeDtypeStruct((batch, NSA_KV_HEADS, n_chunk, LANES), BF16)
    ospec = pl.BlockSpec((1, NSA_KV_HEADS, n_chunk, LANES), lambda b: (b, 0, 0, 0))
    return pl.pallas_call(
        _compress_kernel,
        grid=(batch,),
        in_specs=[pl.BlockSpec((n_chunk, width), lambda b: (b, 0)),
                  pl.BlockSpec((n_chunk, width), lambda b: (b, 0)),
                  _full(w1k.shape), _full(w1v.shape), _full(pek.shape), _full(pev.shape),
                  _full(w1ko.shape), _full(w1vo.shape), _full(w2k.shape), _full(w2v.shape),
                  _full(cos.shape), _full(sin.shape)],
        out_specs=[ospec, ospec],
        out_shape=[out, out],
        compiler_params=_params("parallel"),
        name="nsa_compress",
    )(tk, tv, w1k, w1v, pek, pev, w1ko, w1vo, w2k, w2v, cos, sin)


def _nsa_kernel(q_ref, kc_ref, vc_ref, ks_ref, vs_ref, g_ref, win_ref, ov_ref, e_ref, o_ref,
                bias_scr, s_scr, p_scr, m_scr, a_scr, acc_scr, *, qblk, seq, n_sb, n_sel, tk_sel):
    i = pl.program_id(2)
    start = i * qblk
    n_cmp = kc_ref.shape[2]
    lane = lax.broadcasted_iota(I32, (qblk, LANES), 1)
    lo = lane < HEAD_DIM
    q = _stack_group_queries(q_ref, 0, GQA_REP)
    tq = start + lax.broadcasted_iota(I32, (qblk, 1), 0)

    kc = kc_ref[0, 0]
    vc = vc_ref[0, 0]
    cmp_end = lax.broadcasted_iota(I32, (qblk, n_cmp), 1) * NSA_CMP_STRIDE + (NSA_CMP_LEN - 1)
    vis = jnp.where(cmp_end <= tq, 1.0, 0.0).astype(F32)
    vis4 = jnp.concatenate([vis] * GQA_REP, axis=0)
    s_c = lax.dot_general(q, kc, NT_DIMS, preferred_element_type=F32)
    s_c = jnp.where(vis4 > 0.5, s_c, NEG_BIG)
    p_c = jnp.exp2(s_c - jnp.max(s_c, axis=-1, keepdims=True)) * vis4
    p_c = p_c / jnp.maximum(jnp.sum(p_c, axis=-1, keepdims=True), 1e-30)
    o_c = jnp.dot(p_c.astype(BF16), vc, preferred_element_type=F32)

    p_sum = p_c[0:qblk]
    for r in range(1, GQA_REP):
        p_sum = p_sum + p_c[r * qblk:(r + 1) * qblk]
    imp_t = lax.dot_general(ov_ref[...], p_sum, NT_DIMS, preferred_element_type=F32,
                            precision=lax.Precision.HIGHEST)[:n_sb]
    blk_t = lax.broadcasted_iota(I32, (n_sb, qblk), 0)
    tq_t = start + lax.broadcasted_iota(I32, (n_sb, qblk), 1)
    cur_t = tq_t // NSA_SEL_LEN
    forced = jnp.where(blk_t == 0, 1, jnp.where(blk_t == cur_t, 1, jnp.where(blk_t == cur_t - 1, 1, 0)))
    score_t = jnp.where(forced == 1, FORCE_SCORE, jnp.where(blk_t * NSA_SEL_LEN > tq_t, -1.0, imp_t))
    sel_t = jnp.where(_rank_desc(score_t, n_sb, 0) < n_sel, 1.0, 0.0).astype(F32)
    if n_sb < LANES:
        sel_t = jnp.concatenate([sel_t, jnp.zeros((LANES - n_sb, qblk), F32)], axis=0)
    sel = jnp.transpose(sel_t).astype(BF16)

    _flash_init(m_scr, acc_scr)
    chunks_per_head = qblk // FLASH_ROWS

    def body(j, carry):
        k0 = pl.multiple_of(j * tk_sel, tk_sel)
        picked = jnp.dot(sel, e_ref[:, pl.ds(k0, tk_sel)], preferred_element_type=F32)
        tk = k0 + lax.broadcasted_iota(I32, (qblk, tk_sel), 1)
        bias_scr[...] = jnp.where(picked > 0.5, jnp.where(tk <= tq, 0.0, -jnp.inf), -jnp.inf).astype(F32)

        def bias_fn(c):
            r0 = (c % chunks_per_head) * FLASH_ROWS
            return bias_scr[r0:r0 + FLASH_ROWS, :]

        _flash_scores(q, ks_ref[pl.ds(k0, tk_sel), :], s_scr)
        _flash_max(bias_fn, s_scr, m_scr, a_scr)
        _flash_weights(vs_ref[pl.ds(k0, tk_sel), :], s_scr, p_scr, m_scr, a_scr, acc_scr)
        return carry

    n_tiles = (start + qblk + tk_sel - 1) // tk_sel
    lax.fori_loop(0, n_tiles, body, 0)
    o_s = _flash_finish(acc_scr[...])

    for j in range(GQA_REP // 2):

        def gate(branch, pair=j):
            c = branch * GQA_REP + 2 * pair
            return jnp.where(lo, g_ref[:, c:c + 1], g_ref[:, c + 1:c + 2])

        rows_e = slice((2 * j) * qblk, (2 * j + 1) * qblk)
        rows_o = slice((2 * j + 1) * qblk, (2 * j + 2) * qblk)
        oc = jnp.where(lo, o_c[rows_e], o_c[rows_o])
        os_ = _pair_slab(o_s[rows_e], o_s[rows_o])
        out = gate(0) * oc + gate(1) * os_ + win_ref[:, j * LANES:(j + 1) * LANES]
        o_ref[:, j * LANES:(j + 1) * LANES] = out.astype(o_ref.dtype)


def _nsa_cmp_sel(qb, kcmp, vcmp, ks, vs, gates, win, overlap, expand, batch, seq, qblk, n_sb, n_sel, tk_sel):
    n = qb.shape[0]
    nq = seq // qblk
    n_cmp = kcmp.shape[2]
    gw = GQA_REP * HEAD_DIM
    rows = GQA_REP * qblk
    kern = functools.partial(_nsa_kernel, qblk=qblk, seq=seq, n_sb=n_sb, n_sel=n_sel, tk_sel=tk_sel)
    return pl.pallas_call(
        kern,
        grid=(batch, NSA_KV_HEADS, nq),
        in_specs=[pl.BlockSpec((qblk, gw), lambda b, g, i: (b * nq + i, g)),
                  pl.BlockSpec((1, 1, n_cmp, LANES), lambda b, g, i: (b, g, 0, 0)),
                  pl.BlockSpec((1, 1, n_cmp, LANES), lambda b, g, i: (b, g, 0, 0)),
                  pl.BlockSpec((seq, LANES), lambda b, g, i: (b, g)),
                  pl.BlockSpec((seq, LANES), lambda b, g, i: (b, g)),
                  pl.BlockSpec((qblk, LANES), lambda b, g, i: (b * nq + i, g)),
                  pl.BlockSpec((qblk, gw), lambda b, g, i: (b * nq + i, g)),
                  _full(overlap.shape), _full(expand.shape)],
        out_specs=pl.BlockSpec((qblk, gw), lambda b, g, i: (b * nq + i, g)),
        out_shape=jax.ShapeDtypeStruct((n, NSA_Q_HEADS * HEAD_DIM), BF16),
        scratch_shapes=[pltpu.VMEM((qblk, tk_sel), F32),
                        pltpu.VMEM((rows, tk_sel), F32), pltpu.VMEM((rows, tk_sel), BF16),
                        pltpu.VMEM((rows, LANES), F32), pltpu.VMEM((rows, LANES), F32),
                        pltpu.VMEM((rows, LANES), F32)],
        compiler_params=_params("parallel", "parallel", "arbitrary"),
        name="nsa_cmp_sel",
    )(qb, kcmp, vcmp, ks, vs, gates, win, overlap, expand)


def _oproj_ln_kernel(*refs, n_parts, alpha):
    o_refs = refs[:n_parts]
    w_refs = refs[n_parts:2 * n_parts]
    x_ref, g_ref, b_ref, y_ref, ypk_ref = refs[2 * n_parts:]
    h = jnp.dot(o_refs[0][...], w_refs[0][...], preferred_element_type=F32)
    for o_ref, w_ref in zip(o_refs[1:], w_refs[1:]):
        h = h + jnp.dot(o_ref[...], w_ref[...], preferred_element_type=F32)
    y = _layer_norm_rows(alpha * x_ref[...] + h, g_ref[...], b_ref[...])
    y_ref[...] = y
    ypk_ref[...] = _pack_halves(y)


def _oproj_ln(o_parts, w_parts, x2, g, b, alpha, tm):
    n, d = x2.shape
    kern = functools.partial(_oproj_ln_kernel, n_parts=len(o_parts), alpha=alpha)
    in_specs = ([pl.BlockSpec((tm, o.shape[1]), lambda i: (i, 0)) for o in o_parts]
                + [_full(w.shape) for w in w_parts]
                + [pl.BlockSpec((tm, d), lambda i: (i, 0)), _full((1, d)), _full((1, d))])
    return pl.pallas_call(
        kern,
        grid=(n // tm,),
        in_specs=in_specs,
        out_specs=[pl.BlockSpec((tm, d), lambda i: (i, 0)), pl.BlockSpec((tm, d // 2), lambda i: (i, 0))],
        out_shape=[jax.ShapeDtypeStruct((n, d), F32), jax.ShapeDtypeStruct((n, d // 2), U32)],
        compiler_params=_params("parallel"),
        name="oproj_ln",
    )(*o_parts, *w_parts, x2, g, b)


def _rms_rows(v, g):
    return v * lax.rsqrt(jnp.mean(v * v, axis=-1, keepdims=True) + RMS_EPS) * g


def _mla_proj_kernel(x_ref, win_ref, qn_ref, kvn_ref, wuq_ref, wuk_ref, wuv_ref, cos_ref, sin_ref,
                     q_ref, k_ref, v_ref):
    xb = x_ref[...].astype(BF16)
    cos = cos_ref[...]
    sin = sin_ref[...]
    lat = jnp.dot(xb, win_ref[...], preferred_element_type=F32)
    cq = _rms_rows(lat[:, :MLA_Q_LORA], qn_ref[...]).astype(BF16)
    ckv = _rms_rows(lat[:, MLA_Q_LORA:MLA_Q_LORA + MLA_KV_LORA], kvn_ref[...]).astype(BF16)
    kr = _rope_slab(lat[:, MLA_Q_LORA + MLA_KV_LORA:], cos, sin, MLA_ROPE // 2)
    q = jnp.dot(cq, wuq_ref[...], preferred_element_type=F32)
    k = jnp.dot(ckv, wuk_ref[...], preferred_element_type=F32)
    v = jnp.dot(ckv, wuv_ref[...], preferred_element_type=F32)
    ones_half = _ones_half((x_ref.shape[0], LANES))
    q_scale = (MLA_NOPE + MLA_ROPE) ** -0.5 * LOG2_E
    for h in range(MLA_HEADS):
        sl = slice(h * LANES, (h + 1) * LANES)
        q_ref[:, sl] = (_rope_slab(q[:, sl], cos, sin, MLA_ROPE // 2) * q_scale).astype(BF16)
        k_ref[:, sl] = (k[:, sl] + kr).astype(BF16)
        v_ref[:, sl] = (v[:, sl] + ones_half).astype(BF16)


def _mla_proj(x2, w_in, qn, kvn, wuq, wuk, wuv, cos, sin, seq, tm):
    n, d = x2.shape
    sblk = seq // tm
    hw = MLA_HEADS * LANES
    vw = hw
    return pl.pallas_call(
        _mla_proj_kernel,
        grid=(n // tm,),
        in_specs=[pl.BlockSpec((tm, d), lambda i: (i, 0)),
                  _full(w_in.shape), _full(qn.shape), _full(kvn.shape),
                  _full(wuq.shape), _full(wuk.shape), _full(wuv.shape),
                  pl.BlockSpec((tm, LANES), lambda i: (i % sblk, 0)),
                  pl.BlockSpec((tm, LANES), lambda i: (i % sblk, 0))],
        out_specs=[pl.BlockSpec((tm, hw), lambda i: (i, 0)),
                   pl.BlockSpec((tm, hw), lambda i: (i, 0)),
                   pl.BlockSpec((tm, vw), lambda i: (i, 0))],
        out_shape=[jax.ShapeDtypeStruct((n, hw), BF16), jax.ShapeDtypeStruct((n, hw), BF16),
                   jax.ShapeDtypeStruct((n, vw), BF16)],
        compiler_params=_params("parallel"),
        name="mla_proj",
    )(x2, w_in, qn, kvn, wuq, wuk, wuv, cos, sin)


FLASH_ROWS = 32
LOG2_E = 1.4426950408889634


def _flash_init(m_scr, acc_scr):
    m_scr[...] = jnp.full(m_scr.shape, -jnp.inf, F32)
    acc_scr[...] = jnp.zeros(acc_scr.shape, F32)


def _flash_scores(q, k, s_scr):
    s_scr[...] = lax.dot_general(q, k, NT_DIMS, preferred_element_type=F32)


def _flash_max(bias_fn, s_scr, m_scr, a_scr):
    rows, keys = s_scr.shape
    n_chunks = rows // FLASH_ROWS
    slabs = keys // LANES
    for c in range(n_chunks):
        r = slice(c * FLASH_ROWS, (c + 1) * FLASH_ROWS)
        s = s_scr[r, :]
        if bias_fn is not None:
            s = s + bias_fn(c)
            s_scr[r, :] = s
        mx = s[:, 0:LANES]
        for j in range(1, slabs):
            mx = jnp.maximum(mx, s[:, j * LANES:(j + 1) * LANES])
        a_scr[r, :] = mx
    m_old = m_scr[...]
    m_new = jnp.maximum(m_old, jnp.broadcast_to(jnp.max(a_scr[...], axis=-1, keepdims=True), m_old.shape))
    m_scr[...] = m_new
    a_scr[...] = jnp.exp2(m_old - m_new)


def _flash_weights(v, s_scr, p_scr, m_scr, a_scr, acc_scr):
    rows, keys = s_scr.shape
    n_chunks = rows // FLASH_ROWS
    slabs = keys // LANES
    for c in range(n_chunks):
        r = slice(c * FLASH_ROWS, (c + 1) * FLASH_ROWS)
        m_rows = m_scr[r, :]
        for j in range(slabs):
            cols = slice(j * LANES, (j + 1) * LANES)
            p_scr[r, cols] = jnp.exp2(s_scr[r, cols] - m_rows).astype(p_scr.dtype)
    acc_scr[...] = a_scr[...] * acc_scr[...] + jnp.dot(p_scr[...], v, preferred_element_type=F32)


def _flash_finish(acc):
    return acc / acc[:, HEAD_DIM:HEAD_DIM + 1]


def _pair_slab(even, odd):
    lane = lax.broadcasted_iota(I32, even.shape, 1)
    return jnp.where(lane < HEAD_DIM, even, pltpu.roll(odd, HEAD_DIM, 1))


def _mla_attn_kernel(q_ref, k_ref, v_ref, o_ref, s_scr, p_scr, m_scr, a_scr, acc_scr, *, tq):
    i = pl.program_id(2)
    for e in range(2):
        _flash_init(m_scr.at[e], acc_scr.at[e])

    def diag_bias(c):
        row = c * FLASH_ROWS + lax.broadcasted_iota(I32, (FLASH_ROWS, tq), 0)
        col = lax.broadcasted_iota(I32, (FLASH_ROWS, tq), 1)
        return jnp.where(col <= row, 0.0, -jnp.inf).astype(F32)

    def tile(j, bias_fn):
        k0 = pl.multiple_of(j * tq, tq)
        for e in range(2):
            _flash_scores(q_ref[:, e * LANES:(e + 1) * LANES], k_ref[pl.ds(k0, tq), e * LANES:(e + 1) * LANES],
                          s_scr.at[e])
        for e in range(2):
            _flash_max(bias_fn, s_scr.at[e], m_scr.at[e], a_scr.at[e])
        for e in range(2):
            _flash_weights(v_ref[pl.ds(k0, tq), e * LANES:(e + 1) * LANES], s_scr.at[e], p_scr.at[e],
                           m_scr.at[e], a_scr.at[e], acc_scr.at[e])

    def body(j, carry):
        tile(j, None)
        return carry

    lax.fori_loop(0, i, body, 0)
    tile(i, diag_bias)
    o_ref[...] = _pair_slab(_flash_finish(acc_scr[0]), _flash_finish(acc_scr[1])).astype(o_ref.dtype)


def _mla_attention(q, k, v, batch, seq, tq):
    n = q.shape[0]
    nq = seq // tq
    kern = functools.partial(_mla_attn_kernel, tq=tq)
    return pl.pallas_call(
        kern,
        grid=(batch, MLA_HEADS // 2, nq),
        in_specs=[pl.BlockSpec((tq, 2 * LANES), lambda b, p, i: (b * nq + i, p)),
                  pl.BlockSpec((seq, 2 * LANES), lambda b, p, i: (b, p)),
                  pl.BlockSpec((seq, 2 * LANES), lambda b, p, i: (b, p))],
        out_specs=pl.BlockSpec((tq, LANES), lambda b, p, i: (b * nq + i, p)),
        out_shape=jax.ShapeDtypeStruct((n, MLA_HEADS * MLA_V), BF16),
        scratch_shapes=[pltpu.VMEM((2, tq, tq), F32), pltpu.VMEM((2, tq, tq), BF16)]
                       + [pltpu.VMEM((2, tq, LANES), F32)] * 3,
        compiler_params=_params("parallel", "parallel", "arbitrary"),
        name="mla_attn",
    )(q, k, v)


def _router_kernel(x_ref, rw_ref, rb_ref, eidx_ref, gate_ref, pos_ref, cnt_ref, carry_ref, *, steps_per_chunk):
    i = pl.program_id(0)
    tm = x_ref.shape[0]
    per_group = N_EXPERTS // MOE_GROUPS

    @pl.when(i % steps_per_chunk == 0)
    def _():
        carry_ref[...] = jnp.zeros_like(carry_ref)

    x = x_ref[...]
    x_top = pltpu.bitcast(pltpu.bitcast(x, U32) & jnp.uint32(0xFFFF0000), F32)
    x_hi = x_top.astype(BF16)
    x_lo = (x - x_top).astype(BF16)
    logits = (jnp.dot(x_hi, rw_ref[0], preferred_element_type=F32)
              + jnp.dot(x_hi, rw_ref[1], preferred_element_type=F32)
              + jnp.dot(x_lo, rw_ref[0], preferred_element_type=F32))
    lt = jnp.transpose(logits)[:N_EXPERTS]
    s = _sigmoid(lt)
    sb = s + rb_ref[...]

    g3 = sb.reshape(MOE_GROUPS, per_group, tm)
    idx3 = lax.broadcasted_iota(I32, g3.shape, 1).astype(F32)
    m1 = jnp.max(g3, axis=1, keepdims=True)
    first = jnp.min(jnp.where(g3 == m1, idx3, float(per_group)), axis=1, keepdims=True)
    m2 = jnp.max(jnp.where(idx3 == first, -jnp.inf, g3), axis=1, keepdims=True)
    gscore = (m1 + m2).reshape(MOE_GROUPS, tm)
    gsel = _rank_desc(gscore, MOE_GROUPS, 0) < MOE_TOPK_GROUPS
    gsel3 = jnp.where(gsel, 1.0, 0.0).astype(F32).reshape(MOE_GROUPS, 1, tm)
    masked = jnp.where(gsel3 > 0.5, g3, -jnp.inf).reshape(N_EXPERTS, tm)
    e_iota = lax.broadcasted_iota(I32, (N_EXPERTS, tm), 0).astype(F32)
    work = masked
    picks, rows_g = [], []
    sel_f = jnp.zeros((N_EXPERTS, tm), F32)
    for r in range(MOE_TOP_K):
        top = jnp.max(work, axis=0, keepdims=True)
        first = jnp.min(jnp.where(work == top, e_iota, float(N_EXPERTS)), axis=0, keepdims=True)
        hit = e_iota == first
        picks.append(first)
        rows_g.append(jnp.sum(jnp.where(hit, s, 0.0), axis=0, keepdims=True))
        work = jnp.where(hit, -jnp.inf, work)
        sel_f = jnp.where(hit, 1.0, sel_f)
    gate_sum = rows_g[0]
    for g_row in rows_g[1:]:
        gate_sum = gate_sum + g_row
    rows_g = [g_row / gate_sum * MOE_ROUTED_SCALE for g_row in rows_g]

    sel_b = sel_f.astype(BF16)
    r_i = lax.broadcasted_iota(I32, (tm, tm), 0)
    c_i = lax.broadcasted_iota(I32, (tm, tm), 1)
    tri = jnp.where(r_i < c_i, 1.0, 0.0).astype(BF16)
    carry = carry_ref[:, 0:1]
    before = jnp.dot(sel_b, tri, preferred_element_type=F32) + carry
    carry_new = carry + jnp.sum(sel_b.astype(F32), axis=1, keepdims=True)
    carry_ref[...] = jnp.broadcast_to(carry_new, carry_ref.shape)
    cnt_ref[0] = jnp.broadcast_to(carry_new, carry_ref.shape).astype(I32)

    rows_p = [jnp.sum(jnp.where(e_iota == first, before, 0.0), axis=0, keepdims=True) for first in picks]
    pad = 8 - MOE_TOP_K
    eidx_ref[...] = jnp.concatenate(picks + [jnp.zeros((pad, tm), F32)], axis=0).astype(I32)
    gate_ref[...] = jnp.concatenate(rows_g + [jnp.zeros((pad, tm), F32)], axis=0)
    pos_ref[...] = jnp.concatenate(rows_p + [jnp.zeros((pad, tm), F32)], axis=0).astype(I32)


def _router(x2, rw, rb, n_chunk, tm):
    n, d = x2.shape
    steps = n // tm
    spc = steps // n_chunk
    kern = functools.partial(_router_kernel, steps_per_chunk=spc)
    row8 = pl.BlockSpec((8, tm), lambda i: (0, i))
    return pl.pallas_call(
        kern,
        grid=(steps,),
        in_specs=[pl.BlockSpec((tm, d), lambda i: (i, 0)), _full(rw.shape), _full(rb.shape)],
        out_specs=[row8, row8, row8, pl.BlockSpec((1, N_EXPERTS, LANES), lambda i: (i // spc, 0, 0))],
        out_shape=[jax.ShapeDtypeStruct((8, n), I32), jax.ShapeDtypeStruct((8, n), F32),
                   jax.ShapeDtypeStruct((8, n), I32),
                   jax.ShapeDtypeStruct((n_chunk, N_EXPERTS, LANES), I32)],
        scratch_shapes=[pltpu.VMEM((N_EXPERTS, LANES), F32)],
        compiler_params=_params("arbitrary"),
        name="moe_router",
    )(x2, rw, rb)


def _dest_kernel(tab_ref, eidx_ref, pos_ref, dest_ref, *, steps_per_chunk):
    chunk = pl.program_id(0) // steps_per_chunk
    eidx = eidx_ref[...]
    dest = pos_ref[...]
    for e in range(N_EXPERTS):
        dest = dest + jnp.where(eidx == e, tab_ref[chunk * N_EXPERTS + e], 0)
    dest_ref[...] = dest


def _dest_rows(pad_starts, eidx, pos, n_chunk, tm):
    n = eidx.shape[1]
    steps = n // tm
    row8 = lambda i, tab: (0, i)
    return pl.pallas_call(
        functools.partial(_dest_kernel, steps_per_chunk=steps // n_chunk),
        grid_spec=pltpu.PrefetchScalarGridSpec(
            num_scalar_prefetch=1, grid=(steps,),
            in_specs=[pl.BlockSpec((8, tm), row8), pl.BlockSpec((8, tm), row8)],
            out_specs=pl.BlockSpec((8, tm), row8)),
        out_shape=jax.ShapeDtypeStruct((8, n), I32),
        compiler_params=_params("parallel"),
        name="moe_dest",
    )(pad_starts, eidx, pos)


SC_ROWS = 64
SC_WORKERS = 32


def _sc_mesh():
    return plsc.VectorSubcoreMesh(core_axis_name="c", subcore_axis_name="s")


def _sc_worker_base(per_worker):
    return (lax.axis_index("s") * 2 + lax.axis_index("c")) * per_worker


def _sc_scatter_rows(x, dest, n_out):
    n, d = x.shape
    slots = dest.shape[0] // n
    per_w = n // SC_WORKERS

    def body(x_hbm, i_hbm, o_hbm, idx_v, rows_v, sem):
        base = _sc_worker_base(per_w)

        @pl.loop(0, per_w // SC_ROWS)
        def _(c):
            off = pl.multiple_of(base + c * SC_ROWS, 8)
            pltpu.sync_copy(x_hbm.at[pl.ds(off, SC_ROWS)], rows_v)
            for k in range(slots):
                pltpu.sync_copy(i_hbm.at[pl.ds(k * n + off, SC_ROWS)], idx_v)
                pltpu.async_copy(rows_v, o_hbm.at[idx_v], sem).wait()

    return pl.kernel(
        body, out_type=jax.ShapeDtypeStruct((n_out, d), x.dtype), mesh=_sc_mesh(),
        scratch_types=[pltpu.VMEM((SC_ROWS,), I32), pltpu.VMEM((SC_ROWS, d), x.dtype), pltpu.SemaphoreType.DMA],
        name="moe_dispatch_sc")(x, dest)


def _sc_gather_rows(table, idx):
    d = table.shape[1]
    r = idx.shape[0]
    per_w = r // SC_WORKERS

    def body(t_hbm, i_hbm, o_hbm, idx_v, rows_v, sem):
        base = _sc_worker_base(per_w)

        @pl.loop(0, per_w // SC_ROWS)
        def _(c):
            off = pl.multiple_of(base + c * SC_ROWS, 8)
            pltpu.sync_copy(i_hbm.at[pl.ds(off, SC_ROWS)], idx_v)
            pltpu.async_copy(t_hbm.at[idx_v], rows_v, sem).wait()
            pltpu.sync_copy(rows_v, o_hbm.at[pl.ds(off, SC_ROWS)])

    return pl.kernel(
        body, out_type=jax.ShapeDtypeStruct((r, d), table.dtype), mesh=_sc_mesh(),
        scratch_types=[pltpu.VMEM((SC_ROWS,), I32), pltpu.VMEM((SC_ROWS, d), table.dtype), pltpu.SemaphoreType.DMA],
        name="moe_combine_sc")(table, idx)


def _experts_kernel(be_ref, nv_ref, xs_ref, wg_ref, wu_ref, wd_ref, y_ref, wg_b, wu_b, wd_b):
    i = pl.program_id(0)
    prev = be_ref[jnp.maximum(i - 1, 0)]

    @pl.when(jnp.logical_or(i == 0, be_ref[i] != prev))
    def _():
        wg_b[...] = wg_ref[0, 0].astype(BF16)
        wu_b[...] = wu_ref[0, 0].astype(BF16)
        wd_b[...] = wd_ref[0, 0].astype(BF16)

    @pl.when(i < nv_ref[0])
    def _():
        x_lo, x_hi = _unpack_halves(xs_ref[...])
        x_lo, x_hi = x_lo.astype(BF16), x_hi.astype(BF16)
        half = x_lo.shape[1]

        def proj(w):
            return (jnp.dot(x_lo, w[:half, :], preferred_element_type=F32)
                    + jnp.dot(x_hi, w[half:, :], preferred_element_type=F32))

        h = _silu(proj(wg_b)) * proj(wu_b)
        y_ref[...] = _pack_halves(jnp.dot(h.astype(BF16), wd_b[...], preferred_element_type=F32))

    @pl.when(i >= nv_ref[0])
    def _():
        y_ref[...] = jnp.zeros_like(y_ref)


def _experts(blk_exp, n_valid, xs, wg, wu, wd, layer, tb):
    rows, dp = xs.shape
    d = 2 * dp
    grid_spec = pltpu.PrefetchScalarGridSpec(
        num_scalar_prefetch=2,
        grid=(rows // tb,),
        in_specs=[pl.BlockSpec((tb, dp), lambda i, be, nv: (i, 0)),
                  pl.BlockSpec((1, 1, d, D_EXPERT), lambda i, be, nv: (layer, be[i], 0, 0)),
                  pl.BlockSpec((1, 1, d, D_EXPERT), lambda i, be, nv: (layer, be[i], 0, 0)),
                  pl.BlockSpec((1, 1, D_EXPERT, d), lambda i, be, nv: (layer, be[i], 0, 0))],
        out_specs=pl.BlockSpec((tb, dp), lambda i, be, nv: (i, 0)),
        scratch_shapes=[pltpu.VMEM((d, D_EXPERT), BF16), pltpu.VMEM((d, D_EXPERT), BF16),
                        pltpu.VMEM((D_EXPERT, d), BF16)],
    )
    return pl.pallas_call(
        _experts_kernel,
        grid_spec=grid_spec,
        out_shape=jax.ShapeDtypeStruct((rows, dp), U32),
        compiler_params=_params("arbitrary"),
        name="moe_experts",
    )(blk_exp, n_valid, xs, wg, wu, wd)


def _combine_ln_kernel(x_ref, yg_ref, gt_ref, wgu_ref, wd_ref, g_ref, b_ref, o_ref, *, alpha):
    x = x_ref[...]
    gu = jnp.dot(x.astype(BF16), wgu_ref[...], preferred_element_type=F32)
    dsh = gu.shape[1] // 2
    h = _silu(gu[:, :dsh]) * gu[:, dsh:]
    f = jnp.dot(h.astype(BF16), wd_ref[...], preferred_element_type=F32)
    r_lo = r_hi = None
    for k in range(MOE_TOP_K):
        y_lo, y_hi = _unpack_halves(yg_ref[k])
        gate = gt_ref[:, k:k + 1]
        r_lo = gate * y_lo if r_lo is None else r_lo + gate * y_lo
        r_hi = gate * y_hi if r_hi is None else r_hi + gate * y_hi
    f = f + jnp.concatenate([r_lo, r_hi], axis=1)
    o_ref[...] = _layer_norm_rows(alpha * x + f, g_ref[...], b_ref[...])


def _combine_ln(x2, yg, gates_t, wgu, wd, g, b, alpha, tm):
    n, d = x2.shape
    kern = functools.partial(_combine_ln_kernel, alpha=alpha)
    return pl.pallas_call(
        kern,
        grid=(n // tm,),
        in_specs=[pl.BlockSpec((tm, d), lambda i: (i, 0)),
                  pl.BlockSpec((MOE_TOP_K, tm, d // 2), lambda i: (0, i, 0)),
                  pl.BlockSpec((tm, 8), lambda i: (i, 0)),
                  _full(wgu.shape), _full(wd.shape), _full((1, d)), _full((1, d))],
        out_specs=pl.BlockSpec((tm, d), lambda i: (i, 0)),
        out_shape=jax.ShapeDtypeStruct((n, d), F32),
        compiler_params=_params("parallel"),
        name="moe_combine_ln",
    )(x2, yg, gates_t, wgu, wd, g, b)


def _rope_tables(positions, dim, lane_offset, period):
    half = dim // 2
    inv_freq = ROPE_THETA ** (-2.0 * jnp.arange(half, dtype=jnp.float32) / dim)
    ang = positions.astype(jnp.float32)[:, None] * inv_freq[None, :]
    cos_h, sin_h = jnp.cos(ang), jnp.sin(ang)
    lanes = np.arange(LANES)
    rel = (lanes - lane_offset) % period
    active = (lanes >= lane_offset) & (rel < dim)
    fidx = np.where(active, rel % half, 0)
    sign = np.where(rel < half, -1.0, 1.0)
    cos = jnp.where(active[None, :], cos_h[:, fidx], 1.0)
    sin = jnp.where(active[None, :], sin_h[:, fidx] * sign[None, :], 0.0)
    return cos.astype(F32), sin.astype(F32)


def _gate_slabs(w):
    k = w.shape[0]
    g = w.reshape(k, 3, NSA_KV_HEADS, GQA_REP).transpose(0, 2, 1, 3).reshape(k, NSA_KV_HEADS, 3 * GQA_REP)
    return jnp.pad(g, ((0, 0), (0, 0), (0, LANES - 3 * GQA_REP))).reshape(k, NSA_KV_HEADS * LANES)


def _dup_heads(w):
    a, b = w[:, :HEAD_DIM], w[:, HEAD_DIM:]
    return jnp.concatenate([a, a, b, b], axis=1)


def _pad_heads(w):
    a, b = w[:, :HEAD_DIM], w[:, HEAD_DIM:]
    z = jnp.zeros_like(a)
    return jnp.concatenate([a, z, b, z], axis=1)


def _moe_layer(x1_and_packed, rw, rb, wg, wu, wd, layer, shg, shu, shd, ln_g, ln_b, alpha, tm):
    x1, x1_pk = x1_and_packed
    n, d = x1.shape
    n_chunk = 1
    tb = MOE_ROW_BLOCK
    rw_p = jnp.pad(rw, ((0, 0), (0, LANES - N_EXPERTS)))
    rw_top = lax.bitcast_convert_type(lax.bitcast_convert_type(rw_p, U32) & jnp.uint32(0xFFFF0000), F32)
    rw_split = jnp.stack([rw_top.astype(BF16), (rw_p - rw_top).astype(BF16)])
    eidx, gate, pos, cnt = _router(x1, rw_split, rb.reshape(N_EXPERTS, 1), n_chunk, tm)
    counts = cnt[:, :, 0].reshape(-1)
    padded = (counts + tb - 1) // tb * tb
    pad_ends = jnp.cumsum(padded)
    pad_starts = (pad_ends - padded).astype(I32)
    dest = _dest_rows(pad_starts, eidx, pos, n_chunk, tm)[:MOE_TOP_K].reshape(-1)
    n_rows = n * MOE_TOP_K
    n_blk = -(-(n_rows + n_chunk * N_EXPERTS * (tb - 1)) // tb)
    blk_first_row = jnp.arange(n_blk, dtype=I32) * tb
    owner = jnp.sum((pad_ends[None, :] <= blk_first_row[:, None]).astype(I32), axis=1)
    blk_exp = jnp.minimum(owner, n_chunk * N_EXPERTS - 1).astype(I32) % N_EXPERTS
    n_valid = (pad_ends[-1] // tb).astype(I32).reshape(1)

    xs = _sc_scatter_rows(x1_pk, dest, n_blk * tb)
    y = _experts(blk_exp, n_valid, xs, wg, wu, wd, layer, tb)
    yg = _sc_gather_rows(y, dest).reshape(MOE_TOP_K, n, d // 2)
    gates_t = gate.T
    sh_gu = jnp.concatenate([shg, shu], axis=-1).astype(BF16)
    return _combine_ln(x1, yg, gates_t, sh_gu, shd.astype(BF16), ln_g.reshape(1, d), ln_b.reshape(1, d), alpha, tm)


def _swa_nsa_layer(x2, batch, seq, w_in, sinks, pe_k, w_ck1, w_ck2, pe_v, w_cv1, w_cv2, w_out,
                   ln_g, ln_b, alpha, tm):
    n, d = x2.shape
    splits = np.cumsum([512, 128, 128, 512, 128, 128, 128, 128, 128, 128, 24])[:-1]
    qa, ka, va, qb, kc, vc, ksl, vsl, kw, vw, gts = jnp.split(w_in, [int(c) for c in splits], axis=1)
    w_all = jnp.concatenate(
        [qa, _dup_heads(ka), _pad_heads(va), qb, kc, vc, _dup_heads(ksl), _pad_heads(vsl),
         _dup_heads(kw), _pad_heads(vw), _gate_slabs(gts)], axis=1).astype(BF16)
    cos, sin = _rope_tables(jnp.arange(seq), HEAD_DIM, 0, HEAD_DIM)
    (q_a, k_a, v_a, q_b, k_c, v_c, k_s, v_s, k_w, v_w, gates) = _mix0_proj(x2, w_all, cos, sin, seq, tm)

    o_swa = _banded_attention(q_a, k_a, v_a, batch, seq, SWA_WINDOW, 128, sinks=sinks, name="swa")
    o_win = _banded_attention(q_b, k_w, v_w, batch, seq, NSA_WINDOW, 128, gates=gates,
                              gate_col=2 * GQA_REP, out_dtype=F32, name="nsa_win")

    n_chunk = seq // NSA_CMP_STRIDE
    cw = NSA_CMP_STRIDE * NSA_KV_HEADS * HEAD_DIM
    hid = NSA_CMP_HIDDEN

    def expand_w1(w1):
        halves = w1.reshape(2, NSA_CMP_STRIDE, HEAD_DIM, hid)
        out = jnp.zeros((NSA_KV_HEADS, NSA_CMP_STRIDE, NSA_KV_HEADS, HEAD_DIM, 2 * hid), w1.dtype)
        for g in range(NSA_KV_HEADS):
            out = out.at[g, :, g, :, :hid].set(halves[0]).at[g, :, g, :, hid:].set(halves[1])
        return out.reshape(NSA_KV_HEADS, cw, 2 * hid).astype(BF16)

    def pe_rows(pe):
        return jnp.pad(pe.reshape(1, NSA_CMP_LEN * HEAD_DIM), ((0, 15), (0, 0))).astype(BF16)

    cos_c, sin_c = _rope_tables(jnp.arange(n_chunk) * NSA_CMP_STRIDE + NSA_CMP_LEN - 1, HEAD_DIM, 0, HEAD_DIM)
    k_cmp, v_cmp = _compress(
        k_c.reshape(batch * n_chunk, cw), v_c.reshape(batch * n_chunk, cw),
        expand_w1(w_ck1), expand_w1(w_cv1), pe_rows(pe_k), pe_rows(pe_v),
        w_ck1.astype(BF16), w_cv1.astype(BF16),
        jnp.concatenate([w_ck2, w_ck2], axis=1).astype(BF16),
        jnp.concatenate([w_cv2, w_cv2], axis=1).astype(BF16),
        cos_c, sin_c, batch, n_chunk)

    n_sb = seq // NSA_SEL_LEN
    n_sel = min(NSA_N_SEL, n_sb)
    cs = np.arange(n_chunk) * NSA_CMP_STRIDE
    bs = np.arange(n_sb) * NSA_SEL_LEN
    ov = np.clip(np.minimum(cs[:, None] + NSA_CMP_LEN, bs[None, :] + NSA_SEL_LEN)
                 - np.maximum(cs[:, None], bs[None, :]), 0, None) / NSA_CMP_LEN
    overlap = jnp.asarray(np.pad(ov, ((0, 0), (0, LANES - n_sb))).T, F32)
    expand = jnp.asarray(np.arange(LANES)[:, None] == (np.arange(seq)[None, :] // NSA_SEL_LEN), BF16)
    tk_sel = min(512, seq)
    o_nsa = _nsa_cmp_sel(q_b, k_cmp, v_cmp, k_s, v_s, gates, o_win, overlap, expand,
                         batch, seq, 128, n_sb, n_sel, tk_sel)
    w_o = w_out.astype(BF16)
    half = SWA_Q_HEADS * HEAD_DIM
    return _oproj_ln([o_swa, o_nsa], [w_o[:half], w_o[half:]], x2, ln_g.reshape(1, d), ln_b.reshape(1, d),
                     alpha, tm)


def _mla_layer(x2, batch, seq, w_in, q_norm, kv_norm, w_uq, w_ukv, w_out, ln_g, ln_b, alpha, tm):
    n, d = x2.shape
    dq = MLA_NOPE + MLA_ROPE
    w_kr = jnp.zeros((d, LANES), w_in.dtype).at[:, MLA_NOPE:dq].set(w_in[:, MLA_Q_LORA + MLA_KV_LORA:])
    w_in_p = jnp.concatenate([w_in[:, :MLA_Q_LORA + MLA_KV_LORA], w_kr], axis=1).astype(BF16)
    wuq = jnp.pad(w_uq.reshape(MLA_Q_LORA, MLA_HEADS, dq), ((0, 0), (0, 0), (0, LANES - dq)))
    wuq = wuq.reshape(MLA_Q_LORA, MLA_HEADS * LANES).astype(BF16)
    wukv = w_ukv.reshape(MLA_KV_LORA, MLA_HEADS, MLA_NOPE + MLA_V)
    wuk = jnp.pad(wukv[:, :, :MLA_NOPE], ((0, 0), (0, 0), (0, LANES - MLA_NOPE)))
    wuk = wuk.reshape(MLA_KV_LORA, MLA_HEADS * LANES).astype(BF16)
    wuv = jnp.pad(wukv[:, :, MLA_NOPE:], ((0, 0), (0, 0), (0, LANES - MLA_V)))
    wuv = wuv.reshape(MLA_KV_LORA, MLA_HEADS * LANES).astype(BF16)
    cos, sin = _rope_tables(jnp.arange(seq), MLA_ROPE, MLA_NOPE, LANES)
    q, k, v = _mla_proj(x2, w_in_p, q_norm.reshape(1, -1), kv_norm.reshape(1, -1), wuq, wuk, wuv, cos, sin, seq, tm)
    o = _mla_attention(q, k, v, batch, seq, min(512, seq))
    return _oproj_ln([o], [w_out.astype(BF16)], x2, ln_g.reshape(1, d), ln_b.reshape(1, d), alpha, tm)


def kernel(x, swa_nsa_w_in, swa_sinks, nsa_cmp_pe_k, nsa_cmp_k_w1, nsa_cmp_k_w2, nsa_cmp_pe_v, nsa_cmp_v_w1, nsa_cmp_v_w2, swa_nsa_w_out, mla_w_in, mla_q_norm, mla_kv_norm, mla_w_uq, mla_w_ukv, mla_w_out, ln_mix_g, ln_mix_b, ln_ffn_g, ln_ffn_b, router_w, router_bias, expert_w_gate, expert_w_up, expert_w_down, shared_w_gate, shared_w_up, shared_w_down):
    batch, seq, d = x.shape
    depth = ln_mix_g.shape[0]
    alpha = (2 * depth) ** 0.25
    tm = min(512, seq)
    x2 = x.reshape(batch * seq, d)
    for layer in range(depth):
        j = layer // 2
        if layer % 2 == 0:
            x2 = _swa_nsa_layer(x2, batch, seq, swa_nsa_w_in[j], swa_sinks[j], nsa_cmp_pe_k[j],
                                nsa_cmp_k_w1[j], nsa_cmp_k_w2[j], nsa_cmp_pe_v[j], nsa_cmp_v_w1[j],
                                nsa_cmp_v_w2[j], swa_nsa_w_out[j], ln_mix_g[layer], ln_mix_b[layer], alpha, tm)
        else:
            x2 = _mla_layer(x2, batch, seq, mla_w_in[j], mla_q_norm[j], mla_kv_norm[j], mla_w_uq[j],
                            mla_w_ukv[j], mla_w_out[j], ln_mix_g[layer], ln_mix_b[layer], alpha, tm)
        x2 = _moe_layer(x2, router_w[layer], router_bias[layer], expert_w_gate, expert_w_up, expert_w_down, layer,
                        shared_w_gate[layer], shared_w_up[layer], shared_w_down[layer],
                        ln_ffn_g[layer], ln_ffn_b[layer], alpha, tm)
    return x2.reshape(batch, seq, d)
```

```python
import functools
import math

import numpy as np
import jax
import jax.numpy as jnp
from jax import lax
from jax.experimental import pallas as pl
from jax.experimental.pallas import tpu as pltpu
from jax.experimental.pallas import tpu_sc as plsc

F32 = jnp.float32
BF16 = jnp.bfloat16
I32 = jnp.int32

LANES = 128
VMEM_LIMIT = 48 * 1024 * 1024

ROPE_THETA = 10000.0
LN_EPS = 1e-5
RMS_EPS = 1e-6
NEG_BIG = -1e30
FORCE_SCORE = 1e4

HEAD_DIM = 64
SWA_Q_HEADS = 8
SWA_KV_HEADS = 2
SWA_WINDOW = 128
NSA_Q_HEADS = 8
NSA_KV_HEADS = 2
NSA_CMP_LEN = 32
NSA_CMP_STRIDE = 16
NSA_CMP_HIDDEN = 128
NSA_SEL_LEN = 64
NSA_N_SEL = 16
NSA_WINDOW = 512
GQA_REP = 4

MLA_HEADS = 16
MLA_NOPE = 64
MLA_ROPE = 32
MLA_V = 64
MLA_Q_LORA = 384
MLA_KV_LORA = 256

N_EXPERTS = 64
MOE_GROUPS = 8
MOE_TOPK_GROUPS = 4
MOE_TOP_K = 6
D_EXPERT = 256
MOE_ROUTED_SCALE = 2.5
MOE_ROW_BLOCK = 512
BATCH_GROUPS = 2

NT_DIMS = (((1,), (1,)), ((), ()))


def _params(*sem):
    return pltpu.CompilerParams(dimension_semantics=sem, vmem_limit_bytes=VMEM_LIMIT)


def _full(shape):
    nd = len(shape)
    return pl.BlockSpec(shape, lambda *_: (0,) * nd)


def _rope_slab(y, cos, sin, half):
    lane = lax.broadcasted_iota(I32, y.shape, 1)
    first = (lane % (2 * half)) < half
    rot = jnp.where(first, pltpu.roll(y, LANES - half, 1), pltpu.roll(y, half, 1))
    return y * cos + rot * sin


def _layer_norm_rows(v, g, b):
    mu = jnp.mean(v, axis=-1, keepdims=True)
    vc = v - mu
    var = jnp.mean(vc * vc, axis=-1, keepdims=True)
    return vc * lax.rsqrt(var + LN_EPS) * g + b


def _silu(v):
    return v * (1.0 / (1.0 + jnp.exp(-v)))


def _sigmoid(v):
    return 1.0 / (1.0 + jnp.exp(-v))


U32 = jnp.uint32


def _pack_halves(v):
    w = v.shape[1] // 2
    lo = pltpu.bitcast(v[:, :w].astype(BF16).astype(F32), U32)
    hi = pltpu.bitcast(v[:, w:].astype(BF16).astype(F32), U32)
    return (lo >> 16) | (hi & jnp.uint32(0xFFFF0000))


def _unpack_halves(u):
    lo = pltpu.bitcast(u << 16, F32)
    hi = pltpu.bitcast(u & jnp.uint32(0xFFFF0000), F32)
    return lo, hi


def _ones_half(shape):
    lane = lax.broadcasted_iota(I32, shape, 1)
    return jnp.where(lane < HEAD_DIM, 0.0, 1.0).astype(F32)


def _stack_group_queries(q_ref, group, rep):
    lane = lax.broadcasted_iota(I32, (q_ref.shape[0], LANES), 1)
    keep_lo = jnp.where(lane < HEAD_DIM, 1.0, 0.0).astype(q_ref.dtype)
    keep_hi = jnp.where(lane < HEAD_DIM, 0.0, 1.0).astype(q_ref.dtype)
    parts = []
    for r in range(rep):
        h = group * rep + r
        slab = q_ref[:, (h // 2) * LANES:(h // 2 + 1) * LANES]
        parts.append(slab * (keep_lo if h % 2 == 0 else keep_hi))
    return jnp.concatenate(parts, axis=0)


def _rank_desc(score, n_valid, axis):
    idx = lax.broadcasted_iota(I32, score.shape, axis)
    rank = jnp.zeros(score.shape, I32)
    for j in range(n_valid):
        other = lax.slice_in_dim(score, j, j + 1, axis=axis)
        ahead = jnp.where(other > score, 1, jnp.where(other == score, jnp.where(idx > j, 1, 0), 0))
        rank = rank + ahead
    return rank


_MIX0_OUTS = (
    ("qa", 512, "rope_q", BF16), ("ka", 256, "rope", BF16), ("va", 256, "v_ones", BF16),
    ("qb", 512, "rope_q", BF16), ("kc", 128, "plain", BF16), ("vc", 128, "plain", BF16),
    ("ks", 256, "rope", BF16), ("vs", 256, "v_ones", BF16),
    ("kw", 256, "rope", BF16), ("vw", 256, "v_ones", BF16),
    ("gates", 256, "sigmoid", F32),
)


def _mix0_proj_kernel(x_ref, w_ref, cos_ref, sin_ref, *out_refs):
    xb = x_ref[...].astype(BF16)
    cos = cos_ref[...]
    sin = sin_ref[...]
    off = 0
    for (_, width, kind, _), o_ref in zip(_MIX0_OUTS, out_refs):
        y = jnp.dot(xb, w_ref[:, off:off + width], preferred_element_type=F32)
        for c in range(width // LANES):
            yc = y[:, c * LANES:(c + 1) * LANES]
            if kind in ("rope", "rope_q"):
                yc = _rope_slab(yc, cos, sin, HEAD_DIM // 2)
            if kind == "rope_q":
                yc = yc * (HEAD_DIM ** -0.5 * LOG2_E)
            if kind == "v_ones":
                yc = yc + _ones_half(yc.shape)
            if kind == "sigmoid":
                yc = _sigmoid(yc)
            o_ref[:, c * LANES:(c + 1) * LANES] = yc.astype(o_ref.dtype)
        off += width


def _mix0_proj(x2, w_all, cos, sin, seq, tm):
    n = x2.shape[0]
    d = x2.shape[1]
    wtot = w_all.shape[1]
    sblk = seq // tm
    return pl.pallas_call(
        _mix0_proj_kernel,
        grid=(n // tm,),
        in_specs=[pl.BlockSpec((tm, d), lambda i: (i, 0)),
                  _full((d, wtot)),
                  pl.BlockSpec((tm, LANES), lambda i: (i % sblk, 0)),
                  pl.BlockSpec((tm, LANES), lambda i: (i % sblk, 0))],
        out_specs=[pl.BlockSpec((tm, w), lambda i: (i, 0)) for _, w, _, _ in _MIX0_OUTS],
        out_shape=[jax.ShapeDtypeStruct((n, w), dt) for _, w, _, dt in _MIX0_OUTS],
        compiler_params=_params("parallel"),
        name="mix0_proj",
    )(x2, w_all, cos, sin)


def _banded_kernel(*refs, window, blk, seq, has_sink, gate_col):
    if has_sink:
        sink_ref, refs = refs[0], refs[1:]
    q_ref, k_ref, v_ref = refs[:3]
    g_ref = refs[3] if gate_col is not None else None
    o_ref, bias_scr, s_scr, p_scr, m_scr, a_scr, acc_scr = refs[-7:]
    i = pl.program_id(1)
    start = i * blk
    span = min(window + blk, seq)
    k0 = pl.multiple_of(jnp.maximum(start - window, 0), LANES)
    tq = start + lax.broadcasted_iota(I32, (blk, span), 0)
    tk = k0 + lax.broadcasted_iota(I32, (blk, span), 1)
    diff = tq - tk
    bias_scr[...] = jnp.where(diff >= 0, jnp.where(diff < window, 0.0, -jnp.inf), -jnp.inf).astype(F32)
    chunks_per_head = blk // FLASH_ROWS

    def bias_fn(c):
        r0 = (c % chunks_per_head) * FLASH_ROWS
        return bias_scr[r0:r0 + FLASH_ROWS, :]

    lane = lax.broadcasted_iota(I32, (blk, LANES), 1)
    lo = lane < HEAD_DIM
    n_groups = q_ref.shape[1] // (GQA_REP * HEAD_DIM)
    for g in range(n_groups):
        _flash_init(m_scr.at[g], acc_scr.at[g])
        _flash_scores(_stack_group_queries(q_ref, g, GQA_REP), k_ref[pl.ds(k0, span), g * LANES:(g + 1) * LANES],
                      s_scr.at[g])
    for g in range(n_groups):
        _flash_max(bias_fn, s_scr.at[g], m_scr.at[g], a_scr.at[g])
    for g in range(n_groups):
        _flash_weights(v_ref[pl.ds(k0, span), g * LANES:(g + 1) * LANES], s_scr.at[g], p_scr.at[g],
                       m_scr.at[g], a_scr.at[g], acc_scr.at[g])
    for g in range(n_groups):
        acc = acc_scr[g]
        if has_sink:
            m = m_scr[g]
            sink = jnp.concatenate(
                [jnp.full((blk, LANES), sink_ref[g * GQA_REP + r] * LOG2_E, F32) for r in range(GQA_REP)], axis=0)
            m_all = jnp.maximum(m, sink)
            acc = acc * jnp.exp2(m - m_all) + _ones_half(acc.shape) * jnp.exp2(sink - m_all)
        o = _flash_finish(acc)
        for j in range(GQA_REP // 2):
            slab = _pair_slab(o[(2 * j) * blk:(2 * j + 1) * blk], o[(2 * j + 1) * blk:(2 * j + 2) * blk])
            if gate_col is not None:
                c = g * LANES + gate_col + 2 * j
                gate = jnp.where(lo, g_ref[:, c:c + 1], g_ref[:, c + 1:c + 2])
                slab = slab * gate
            col = (g * (GQA_REP // 2) + j) * LANES
            o_ref[:, col:col + LANES] = slab.astype(o_ref.dtype)


def _banded_attention(q, k, v, batch, seq, window, blk, sinks=None, gates=None, gate_col=None,
                      out_dtype=None, name="banded"):
    n, qw = q.shape
    kw = k.shape[1]
    nblk = seq // blk
    span = min(window + blk, seq)
    groups = qw // (GQA_REP * HEAD_DIM)
    rows = GQA_REP * blk
    kern = functools.partial(_banded_kernel, window=window, blk=blk, seq=seq,
                             has_sink=sinks is not None, gate_col=gate_col)
    in_specs = []
    args = []
    if sinks is not None:
        in_specs.append(pl.BlockSpec(memory_space=pltpu.SMEM))
        args.append(sinks)
    in_specs += [pl.BlockSpec((blk, qw), lambda b, i: (b * nblk + i, 0)),
                 pl.BlockSpec((seq, kw), lambda b, i: (b, 0)),
                 pl.BlockSpec((seq, kw), lambda b, i: (b, 0))]
    args += [q, k, v]
    if gates is not None:
        in_specs.append(pl.BlockSpec((blk, gates.shape[1]), lambda b, i: (b * nblk + i, 0)))
        args.append(gates)
    return pl.pallas_call(
        kern,
        grid=(batch, nblk),
        in_specs=in_specs,
        out_specs=pl.BlockSpec((blk, qw), lambda b, i: (b * nblk + i, 0)),
        out_shape=jax.ShapeDtypeStruct((n, qw), BF16 if out_dtype is None else out_dtype),
        scratch_shapes=[pltpu.VMEM((blk, span), F32),
                        pltpu.VMEM((groups, rows, span), F32), pltpu.VMEM((groups, rows, span), BF16)]
                       + [pltpu.VMEM((groups, rows, LANES), F32)] * 3,
        compiler_params=_params("parallel", "arbitrary"),
        name=name,
    )(*args)


def _gelu_tanh(v):
    return 0.5 * v * (1.0 + jnp.tanh(math.sqrt(2.0 / math.pi) * (v + 0.044715 * (v * v * v))))


def _compress_kernel(tk_ref, tv_ref, w1k_ref, w1v_ref, pek_ref, pev_ref, w1ko_ref, w1vo_ref,
                     w2k_ref, w2v_ref, cos_ref, sin_ref, kc_ref, vc_ref):
    n_chunk = tk_ref.shape[0]
    hid = NSA_CMP_HIDDEN
    for t_ref, w1_ref, pe_ref, w1o_ref, w2_ref, o_ref, rope in (
            (tk_ref, w1k_ref, pek_ref, w1ko_ref, w2k_ref, kc_ref, True),
            (tv_ref, w1v_ref, pev_ref, w1vo_ref, w2v_ref, vc_ref, False)):
        pe_term = jnp.dot(pe_ref[...], w1o_ref[...], preferred_element_type=F32)[0:1, :]
        t = t_ref[...]
        for g in range(NSA_KV_HEADS):
            uv = jnp.dot(t, w1_ref[g], preferred_element_type=F32)
            nxt = pltpu.roll(uv[:, hid:], n_chunk - 1, 0)
            h = _gelu_tanh(uv[:, :hid] + nxt + pe_term)
            c = jnp.dot(h.astype(BF16), w2_ref[...], preferred_element_type=F32)
            if rope:
                c = _rope_slab(c, cos_ref[...], sin_ref[...], HEAD_DIM // 2)
            o_ref[0, g] = c.astype(o_ref.dtype)


def _compress(tk, tv, w1k, w1v, pek, pev, w1ko, w1vo, w2k, w2v, cos, sin, batch, n_chunk):
    width = tk.shape[1]
    out = jax.ShapeDtypeStruct((batch, NSA_KV_HEADS, n_chunk, LANES), BF16)
    ospec = pl.BlockSpec((1, NSA_KV_HEADS, n_chunk, LANES), lambda b: (b, 0, 0, 0))
    return pl.pallas_call(
        _compress_kernel,
        grid=(batch,),
        in_specs=[pl.BlockSpec((n_chunk, width), lambda b: (b, 0)),
                  pl.BlockSpec((n_chunk, width), lambda b: (b, 0)),
                  _full(w1k.shape), _full(w1v.shape), _full(pek.shape), _full(pev.shape),
                  _full(w1ko.shape), _full(w1vo.shape), _full(w2k.shape), _full(w2v.shape),
                  _full(cos.shape), _full(sin.shape)],
        out_specs=[ospec, ospec],
        out_shape=[out, out],
        compiler_params=_params("parallel"),
        name="nsa_compress",
    )(tk, tv, w1k, w1v, pek, pev, w1ko, w1vo, w2k, w2v, cos, sin)


def _nsa_kernel(q_ref, kc_ref, vc_ref, ks_ref, vs_ref, g_ref, win_ref, ov_ref, e_ref, o_ref,
                bias_scr, s_scr, p_scr, m_scr, a_scr, acc_scr, *, qblk, seq, n_sb, n_sel, tk_sel):
    i = pl.program_id(2)
    start = i * qblk
    n_cmp = kc_ref.shape[2]
    lane = lax.broadcasted_iota(I32, (qblk, LANES), 1)
    lo = lane < HEAD_DIM
    q = _stack_group_queries(q_ref, 0, GQA_REP)
    tq = start + lax.broadcasted_iota(I32, (qblk, 1), 0)

    kc = kc_ref[0, 0]
    vc = vc_ref[0, 0]
    cmp_end = lax.broadcasted_iota(I32, (qblk, n_cmp), 1) * NSA_CMP_STRIDE + (NSA_CMP_LEN - 1)
    vis = jnp.where(cmp_end <= tq, 1.0, 0.0).astype(F32)
    vis4 = jnp.concatenate([vis] * GQA_REP, axis=0)
    s_c = lax.dot_general(q, kc, NT_DIMS, preferred_element_type=F32)
    s_c = jnp.where(vis4 > 0.5, s_c, NEG_BIG)
    p_c = jnp.exp2(s_c - jnp.max(s_c, axis=-1, keepdims=True)) * vis4
    p_c = p_c / jnp.maximum(jnp.sum(p_c, axis=-1, keepdims=True), 1e-30)
    o_c = jnp.dot(p_c.astype(BF16), vc, preferred_element_type=F32)

    p_sum = p_c[0:qblk]
    for r in range(1, GQA_REP):
        p_sum = p_sum + p_c[r * qblk:(r + 1) * qblk]
    imp_t = lax.dot_general(ov_ref[...], p_sum, NT_DIMS, preferred_element_type=F32,
                            precision=lax.Precision.HIGHEST)[:n_sb]
    blk_t = lax.broadcasted_iota(I32, (n_sb, qblk), 0)
    tq_t = start + lax.broadcasted_iota(I32, (n_sb, qblk), 1)
    cur_t = tq_t // NSA_SEL_LEN
    forced = jnp.where(blk_t == 0, 1, jnp.where(blk_t == cur_t, 1, jnp.where(blk_t == cur_t - 1, 1, 0)))
    score_t = jnp.where(forced == 1, FORCE_SCORE, jnp.where(blk_t * NSA_SEL_LEN > tq_t, -1.0, imp_t))
    sel_t = jnp.where(_rank_desc(score_t, n_sb, 0) < n_sel, 1.0, 0.0).astype(F32)
    if n_sb < LANES:
        sel_t = jnp.concatenate([sel_t, jnp.zeros((LANES - n_sb, qblk), F32)], axis=0)
    sel = jnp.transpose(sel_t).astype(BF16)

    _flash_init(m_scr, acc_scr)
    chunks_per_head = qblk // FLASH_ROWS

    def body(j, carry):
        k0 = pl.multiple_of(j * tk_sel, tk_sel)
        picked = jnp.dot(sel, e_ref[:, pl.ds(k0, tk_sel)], preferred_element_type=F32)
        tk = k0 + lax.broadcasted_iota(I32, (qblk, tk_sel), 1)
        bias_scr[...] = jnp.where(picked > 0.5, jnp.where(tk <= tq, 0.0, -jnp.inf), -jnp.inf).astype(F32)

        def bias_fn(c):
            r0 = (c % chunks_per_head) * FLASH_ROWS
            return bias_scr[r0:r0 + FLASH_ROWS, :]

        _flash_scores(q, ks_ref[pl.ds(k0, tk_sel), :], s_scr)
        _flash_max(bias_fn, s_scr, m_scr, a_scr)
        _flash_weights(vs_ref[pl.ds(k0, tk_sel), :], s_scr, p_scr, m_scr, a_scr, acc_scr)
        return carry

    n_tiles = (start + qblk + tk_sel - 1) // tk_sel
    lax.fori_loop(0, n_tiles, body, 0)
    o_s = _flash_finish(acc_scr[...])

    for j in range(GQA_REP // 2):

        def gate(branch, pair=j):
            c = branch * GQA_REP + 2 * pair
            return jnp.where(lo, g_ref[:, c:c + 1], g_ref[:, c + 1:c + 2])

        rows_e = slice((2 * j) * qblk, (2 * j + 1) * qblk)
        rows_o = slice((2 * j + 1) * qblk, (2 * j + 2) * qblk)
        oc = jnp.where(lo, o_c[rows_e], o_c[rows_o])
        os_ = _pair_slab(o_s[rows_e], o_s[rows_o])
        out = gate(0) * oc + gate(1) * os_ + win_ref[:, j * LANES:(j + 1) * LANES]
        o_ref[:, j * LANES:(j + 1) * LANES] = out.astype(o_ref.dtype)


def _nsa_cmp_sel(qb, kcmp, vcmp, ks, vs, gates, win, overlap, expand, batch, seq, qblk, n_sb, n_sel, tk_sel):
    n = qb.shape[0]
    nq = seq // qblk
    n_cmp = kcmp.shape[2]
    gw = GQA_REP * HEAD_DIM
    rows = GQA_REP * qblk
    kern = functools.partial(_nsa_kernel, qblk=qblk, seq=seq, n_sb=n_sb, n_sel=n_sel, tk_sel=tk_sel)
    return pl.pallas_call(
        kern,
        grid=(batch, NSA_KV_HEADS, nq),
        in_specs=[pl.BlockSpec((qblk, gw), lambda b, g, i: (b * nq + i, g)),
                  pl.BlockSpec((1, 1, n_cmp, LANES), lambda b, g, i: (b, g, 0, 0)),
                  pl.BlockSpec((1, 1, n_cmp, LANES), lambda b, g, i: (b, g, 0, 0)),
                  pl.BlockSpec((seq, LANES), lambda b, g, i: (b, g)),
                  pl.BlockSpec((seq, LANES), lambda b, g, i: (b, g)),
                  pl.BlockSpec((qblk, LANES), lambda b, g, i: (b * nq + i, g)),
                  pl.BlockSpec((qblk, gw), lambda b, g, i: (b * nq + i, g)),
                  _full(overlap.shape), _full(expand.shape)],
        out_specs=pl.BlockSpec((qblk, gw), lambda b, g, i: (b * nq + i, g)),
        out_shape=jax.ShapeDtypeStruct((n, NSA_Q_HEADS * HEAD_DIM), BF16),
        scratch_shapes=[pltpu.VMEM((qblk, tk_sel), F32),
                        pltpu.VMEM((rows, tk_sel), F32), pltpu.VMEM((rows, tk_sel), BF16),
                        pltpu.VMEM((rows, LANES), F32), pltpu.VMEM((rows, LANES), F32),
                        pltpu.VMEM((rows, LANES), F32)],
        compiler_params=_params("parallel", "parallel", "arbitrary"),
        name="nsa_cmp_sel",
    )(qb, kcmp, vcmp, ks, vs, gates, win, overlap, expand)


def _oproj_ln_kernel(*refs, n_parts, alpha):
    o_refs = refs[:n_parts]
    w_refs = refs[n_parts:2 * n_parts]
    x_ref, g_ref, b_ref, y_ref, ypk_ref = refs[2 * n_parts:]
    h = jnp.dot(o_refs[0][...], w_refs[0][...], preferred_element_type=F32)
    for o_ref, w_ref in zip(o_refs[1:], w_refs[1:]):
        h = h + jnp.dot(o_ref[...], w_ref[...], preferred_element_type=F32)
    y = _layer_norm_rows(alpha * x_ref[...] + h, g_ref[...], b_ref[...])
    y_ref[...] = y
    ypk_ref[...] = _pack_halves(y)


def _oproj_ln(o_parts, w_parts, x2, g, b, alpha, tm):
    n, d = x2.shape
    kern = functools.partial(_oproj_ln_kernel, n_parts=len(o_parts), alpha=alpha)
    in_specs = ([pl.BlockSpec((tm, o.shape[1]), lambda i: (i, 0)) for o in o_parts]
                + [_full(w.shape) for w in w_parts]
                + [pl.BlockSpec((tm, d), lambda i: (i, 0)), _full((1, d)), _full((1, d))])
    return pl.pallas_call(
        kern,
        grid=(n // tm,),
        in_specs=in_specs,
        out_specs=[pl.BlockSpec((tm, d), lambda i: (i, 0)), pl.BlockSpec((tm, d // 2), lambda i: (i, 0))],
        out_shape=[jax.ShapeDtypeStruct((n, d), F32), jax.ShapeDtypeStruct((n, d // 2), U32)],
        compiler_params=_params("parallel"),
        name="oproj_ln",
    )(*o_parts, *w_parts, x2, g, b)


def _rms_rows(v, g):
    return v * lax.rsqrt(jnp.mean(v * v, axis=-1, keepdims=True) + RMS_EPS) * g


def _mla_proj_kernel(x_ref, win_ref, qn_ref, kvn_ref, wuq_ref, wuk_ref, wuv_ref, cos_ref, sin_ref,
                     q_ref, k_ref, v_ref):
    xb = x_ref[...].astype(BF16)
    cos = cos_ref[...]
    sin = sin_ref[...]
    lat = jnp.dot(xb, win_ref[...], preferred_element_type=F32)
    cq = _rms_rows(lat[:, :MLA_Q_LORA], qn_ref[...]).astype(BF16)
    ckv = _rms_rows(lat[:, MLA_Q_LORA:MLA_Q_LORA + MLA_KV_LORA], kvn_ref[...]).astype(BF16)
    kr = _rope_slab(lat[:, MLA_Q_LORA + MLA_KV_LORA:], cos, sin, MLA_ROPE // 2)
    q = jnp.dot(cq, wuq_ref[...], preferred_element_type=F32)
    k = jnp.dot(ckv, wuk_ref[...], preferred_element_type=F32)
    v = jnp.dot(ckv, wuv_ref[...], preferred_element_type=F32)
    ones_half = _ones_half((x_ref.shape[0], LANES))
    q_scale = (MLA_NOPE + MLA_ROPE) ** -0.5 * LOG2_E
    for h in range(MLA_HEADS):
        sl = slice(h * LANES, (h + 1) * LANES)
        q_ref[:, sl] = (_rope_slab(q[:, sl], cos, sin, MLA_ROPE // 2) * q_scale).astype(BF16)
        k_ref[:, sl] = (k[:, sl] + kr).astype(BF16)
        v_ref[:, sl] = (v[:, sl] + ones_half).astype(BF16)


def _mla_proj(x2, w_in, qn, kvn, wuq, wuk, wuv, cos, sin, seq, tm):
    n, d = x2.shape
    sblk = seq // tm
    hw = MLA_HEADS * LANES
    vw = hw
    return pl.pallas_call(
        _mla_proj_kernel,
        grid=(n // tm,),
        in_specs=[pl.BlockSpec((tm, d), lambda i: (i, 0)),
                  _full(w_in.shape), _full(qn.shape), _full(kvn.shape),
                  _full(wuq.shape), _full(wuk.shape), _full(wuv.shape),
                  pl.BlockSpec((tm, LANES), lambda i: (i % sblk, 0)),
                  pl.BlockSpec((tm, LANES), lambda i: (i % sblk, 0))],
        out_specs=[pl.BlockSpec((tm, hw), lambda i: (i, 0)),
                   pl.BlockSpec((tm, hw), lambda i: (i, 0)),
                   pl.BlockSpec((tm, vw), lambda i: (i, 0))],
        out_shape=[jax.ShapeDtypeStruct((n, hw), BF16), jax.ShapeDtypeStruct((n, hw), BF16),
                   jax.ShapeDtypeStruct((n, vw), BF16)],
        compiler_params=_params("parallel"),
        name="mla_proj",
    )(x2, w_in, qn, kvn, wuq, wuk, wuv, cos, sin)


FLASH_ROWS = 32
LOG2_E = 1.4426950408889634


def _flash_init(m_scr, acc_scr):
    m_scr[...] = jnp.full(m_scr.shape, -jnp.inf, F32)
    acc_scr[...] = jnp.zeros(acc_scr.shape, F32)


def _flash_scores(q, k, s_scr):
    s_scr[...] = lax.dot_general(q, k, NT_DIMS, preferred_element_type=F32)


def _flash_max(bias_fn, s_scr, m_scr, a_scr):
    rows, keys = s_scr.shape
    n_chunks = rows // FLASH_ROWS
    slabs = keys // LANES
    for c in range(n_chunks):
        r = slice(c * FLASH_ROWS, (c + 1) * FLASH_ROWS)
        s = s_scr[r, :]
        if bias_fn is not None:
            s = s + bias_fn(c)
            s_scr[r, :] = s
        mx = s[:, 0:LANES]
        for j in range(1, slabs):
            mx = jnp.maximum(mx, s[:, j * LANES:(j + 1) * LANES])
        a_scr[r, :] = mx
    m_old = m_scr[...]
    m_new = jnp.maximum(m_old, jnp.broadcast_to(jnp.max(a_scr[...], axis=-1, keepdims=True), m_old.shape))
    m_scr[...] = m_new
    a_scr[...] = jnp.exp2(m_old - m_new)


def _flash_weights(v, s_scr, p_scr, m_scr, a_scr, acc_scr):
    rows, keys = s_scr.shape
    n_chunks = rows // FLASH_ROWS
    slabs = keys // LANES
    for c in range(n_chunks):
        r = slice(c * FLASH_ROWS, (c + 1) * FLASH_ROWS)
        m_rows = m_scr[r, :]
        for j in range(slabs):
            cols = slice(j * LANES, (j + 1) * LANES)
            p_scr[r, cols] = jnp.exp2(s_scr[r, cols] - m_rows).astype(p_scr.dtype)
    acc_scr[...] = a_scr[...] * acc_scr[...] + jnp.dot(p_scr[...], v, preferred_element_type=F32)


def _flash_finish(acc):
    return acc / acc[:, HEAD_DIM:HEAD_DIM + 1]


def _pair_slab(even, odd):
    lane = lax.broadcasted_iota(I32, even.shape, 1)
    return jnp.where(lane < HEAD_DIM, even, pltpu.roll(odd, HEAD_DIM, 1))


def _mla_attn_kernel(q_ref, k_ref, v_ref, o_ref, s_scr, p_scr, m_scr, a_scr, acc_scr, *, tq):
    i = pl.program_id(2)
    for e in range(2):
        _flash_init(m_scr.at[e], acc_scr.at[e])

    def diag_bias(c):
        row = c * FLASH_ROWS + lax.broadcasted_iota(I32, (FLASH_ROWS, tq), 0)
        col = lax.broadcasted_iota(I32, (FLASH_ROWS, tq), 1)
        return jnp.where(col <= row, 0.0, -jnp.inf).astype(F32)

    def tile(j, bias_fn):
        k0 = pl.multiple_of(j * tq, tq)
        for e in range(2):
            _flash_scores(q_ref[:, e * LANES:(e + 1) * LANES], k_ref[pl.ds(k0, tq), e * LANES:(e + 1) * LANES],
                          s_scr.at[e])
        for e in range(2):
            _flash_max(bias_fn, s_scr.at[e], m_scr.at[e], a_scr.at[e])
        for e in range(2):
            _flash_weights(v_ref[pl.ds(k0, tq), e * LANES:(e + 1) * LANES], s_scr.at[e], p_scr.at[e],
                           m_scr.at[e], a_scr.at[e], acc_scr.at[e])

    def body(j, carry):
        tile(j, None)
        return carry

    lax.fori_loop(0, i, body, 0)
    tile(i, diag_bias)
    o_ref[...] = _pair_slab(_flash_finish(acc_scr[0]), _flash_finish(acc_scr[1])).astype(o_ref.dtype)


def _mla_attention(q, k, v, batch, seq, tq):
    n = q.shape[0]
    nq = seq // tq
    kern = functools.partial(_mla_attn_kernel, tq=tq)
    return pl.pallas_call(
        kern,
        grid=(batch, MLA_HEADS // 2, nq),
        in_specs=[pl.BlockSpec((tq, 2 * LANES), lambda b, p, i: (b * nq + i, p)),
                  pl.BlockSpec((seq, 2 * LANES), lambda b, p, i: (b, p)),
                  pl.BlockSpec((seq, 2 * LANES), lambda b, p, i: (b, p))],
        out_specs=pl.BlockSpec((tq, LANES), lambda b, p, i: (b * nq + i, p)),
        out_shape=jax.ShapeDtypeStruct((n, MLA_HEADS * MLA_V), BF16),
        scratch_shapes=[pltpu.VMEM((2, tq, tq), F32), pltpu.VMEM((2, tq, tq), BF16)]
                       + [pltpu.VMEM((2, tq, LANES), F32)] * 3,
        compiler_params=_params("parallel", "parallel", "arbitrary"),
        name="mla_attn",
    )(q, k, v)


def _router_kernel(x_ref, rw_ref, rb_ref, eidx_ref, gate_ref, pos_ref, cnt_ref, carry_ref, *, steps_per_chunk):
    i = pl.program_id(0)
    tm = x_ref.shape[0]
    per_group = N_EXPERTS // MOE_GROUPS

    @pl.when(i % steps_per_chunk == 0)
    def _():
        carry_ref[...] = jnp.zeros_like(carry_ref)

    x = x_ref[...]
    x_top = pltpu.bitcast(pltpu.bitcast(x, U32) & jnp.uint32(0xFFFF0000), F32)
    x_hi = x_top.astype(BF16)
    x_lo = (x - x_top).astype(BF16)
    logits = (jnp.dot(x_hi, rw_ref[0], preferred_element_type=F32)
              + jnp.dot(x_hi, rw_ref[1], preferred_element_type=F32)
              + jnp.dot(x_lo, rw_ref[0], preferred_element_type=F32))
    lt = jnp.transpose(logits)[:N_EXPERTS]
    s = _sigmoid(lt)
    sb = s + rb_ref[...]

    g3 = sb.reshape(MOE_GROUPS, per_group, tm)
    idx3 = lax.broadcasted_iota(I32, g3.shape, 1).astype(F32)
    m1 = jnp.max(g3, axis=1, keepdims=True)
    first = jnp.min(jnp.where(g3 == m1, idx3, float(per_group)), axis=1, keepdims=True)
    m2 = jnp.max(jnp.where(idx3 == first, -jnp.inf, g3), axis=1, keepdims=True)
    gscore = (m1 + m2).reshape(MOE_GROUPS, tm)
    gsel = _rank_desc(gscore, MOE_GROUPS, 0) < MOE_TOPK_GROUPS
    gsel3 = jnp.where(gsel, 1.0, 0.0).astype(F32).reshape(MOE_GROUPS, 1, tm)
    masked = jnp.where(gsel3 > 0.5, g3, -jnp.inf).reshape(N_EXPERTS, tm)
    e_iota = lax.broadcasted_iota(I32, (N_EXPERTS, tm), 0).astype(F32)
    work = masked
    picks, rows_g = [], []
    sel_f = jnp.zeros((N_EXPERTS, tm), F32)
    for r in range(MOE_TOP_K):
        top = jnp.max(work, axis=0, keepdims=True)
        first = jnp.min(jnp.where(work == top, e_iota, float(N_EXPERTS)), axis=0, keepdims=True)
        hit = e_iota == first
        picks.append(first)
        rows_g.append(jnp.sum(jnp.where(hit, s, 0.0), axis=0, keepdims=True))
        work = jnp.where(hit, -jnp.inf, work)
        sel_f = jnp.where(hit, 1.0, sel_f)
    gate_sum = rows_g[0]
    for g_row in rows_g[1:]:
        gate_sum = gate_sum + g_row
    rows_g = [g_row / gate_sum * MOE_ROUTED_SCALE for g_row in rows_g]

    sel_b = sel_f.astype(BF16)
    r_i = lax.broadcasted_iota(I32, (tm, tm), 0)
    c_i = lax.broadcasted_iota(I32, (tm, tm), 1)
    tri = jnp.where(r_i < c_i, 1.0, 0.0).astype(BF16)
    carry = carry_ref[:, 0:1]
    before = jnp.dot(sel_b, tri, preferred_element_type=F32) + carry
    carry_new = carry + jnp.sum(sel_b.astype(F32), axis=1, keepdims=True)
    carry_ref[...] = jnp.broadcast_to(carry_new, carry_ref.shape)
    cnt_ref[0] = jnp.broadcast_to(carry_new, carry_ref.shape).astype(I32)

    rows_p = [jnp.sum(jnp.where(e_iota == first, before, 0.0), axis=0, keepdims=True) for first in picks]
    pad = 8 - MOE_TOP_K
    eidx_ref[...] = jnp.concatenate(picks + [jnp.zeros((pad, tm), F32)], axis=0).astype(I32)
    gate_ref[...] = jnp.concatenate(rows_g + [jnp.zeros((pad, tm), F32)], axis=0)
    pos_ref[...] = jnp.concatenate(rows_p + [jnp.zeros((pad, tm), F32)], axis=0).astype(I32)


def _router(x2, rw, rb, n_chunk, tm):
    n, d = x2.shape
    steps = n // tm
    spc = steps // n_chunk
    kern = functools.partial(_router_kernel, steps_per_chunk=spc)
    row8 = pl.BlockSpec((8, tm), lambda i: (0, i))
    return pl.pallas_call(
        kern,
        grid=(steps,),
        in_specs=[pl.BlockSpec((tm, d), lambda i: (i, 0)), _full(rw.shape), _full(rb.shape)],
        out_specs=[row8, row8, row8, pl.BlockSpec((1, N_EXPERTS, LANES), lambda i: (i // spc, 0, 0))],
        out_shape=[jax.ShapeDtypeStruct((8, n), I32), jax.ShapeDtypeStruct((8, n), F32),
                   jax.ShapeDtypeStruct((8, n), I32),
                   jax.ShapeDtypeStruct((n_chunk, N_EXPERTS, LANES), I32)],
        scratch_shapes=[pltpu.VMEM((N_EXPERTS, LANES), F32)],
        compiler_params=_params("arbitrary"),
        name="moe_router",
    )(x2, rw, rb)


def _dest_kernel(tab_ref, eidx_ref, pos_ref, dest_ref, *, steps_per_chunk):
    chunk = pl.program_id(0) // steps_per_chunk
    eidx = eidx_ref[...]
    dest = pos_ref[...]
    for e in range(N_EXPERTS):
        dest = dest + jnp.where(eidx == e, tab_ref[chunk * N_EXPERTS + e], 0)
    dest_ref[...] = dest


def _dest_rows(pad_starts, eidx, pos, n_chunk, tm):
    n = eidx.shape[1]
    steps = n // tm
    row8 = lambda i, tab: (0, i)
    return pl.pallas_call(
        functools.partial(_dest_kernel, steps_per_chunk=steps // n_chunk),
        grid_spec=pltpu.PrefetchScalarGridSpec(
            num_scalar_prefetch=1, grid=(steps,),
            in_specs=[pl.BlockSpec((8, tm), row8), pl.BlockSpec((8, tm), row8)],
            out_specs=pl.BlockSpec((8, tm), row8)),
        out_shape=jax.ShapeDtypeStruct((8, n), I32),
        compiler_params=_params("parallel"),
        name="moe_dest",
    )(pad_starts, eidx, pos)


SC_ROWS = 64
SC_WORKERS = 32


def _sc_mesh():
    return plsc.VectorSubcoreMesh(core_axis_name="c", subcore_axis_name="s")


def _sc_worker_base(per_worker):
    return (lax.axis_index("s") * 2 + lax.axis_index("c")) * per_worker


def _sc_scatter_rows(x, dest, n_out):
    n, d = x.shape
    slots = dest.shape[0] // n
    per_w = n // SC_WORKERS

    def body(x_hbm, i_hbm, o_hbm, idx_v, rows_v, sem):
        base = _sc_worker_base(per_w)

        @pl.loop(0, per_w // SC_ROWS)
        def _(c):
            off = pl.multiple_of(base + c * SC_ROWS, 8)
            pltpu.sync_copy(x_hbm.at[pl.ds(off, SC_ROWS)], rows_v)
            for k in range(slots):
                pltpu.sync_copy(i_hbm.at[pl.ds(k * n + off, SC_ROWS)], idx_v)
                pltpu.async_copy(rows_v, o_hbm.at[idx_v], sem).wait()

    return pl.kernel(
        body, out_type=jax.ShapeDtypeStruct((n_out, d), x.dtype), mesh=_sc_mesh(),
        scratch_types=[pltpu.VMEM((SC_ROWS,), I32), pltpu.VMEM((SC_ROWS, d), x.dtype), pltpu.SemaphoreType.DMA],
        name="moe_dispatch_sc")(x, dest)


def _sc_gather_rows(table, idx):
    d = table.shape[1]
    r = idx.shape[0]
    per_w = r // SC_WORKERS

    def body(t_hbm, i_hbm, o_hbm, idx_v, rows_v, sem):
        base = _sc_worker_base(per_w)

        @pl.loop(0, per_w // SC_ROWS)
        def _(c):
            off = pl.multiple_of(base + c * SC_ROWS, 8)
            pltpu.sync_copy(i_hbm.at[pl.ds(off, SC_ROWS)], idx_v)
            pltpu.async_copy(t_hbm.at[idx_v], rows_v, sem).wait()
            pltpu.sync_copy(rows_v, o_hbm.at[pl.ds(off, SC_ROWS)])

    return pl.kernel(
        body, out_type=jax.ShapeDtypeStruct((r, d), table.dtype), mesh=_sc_mesh(),
        scratch_types=[pltpu.VMEM((SC_ROWS,), I32), pltpu.VMEM((SC_ROWS, d), table.dtype), pltpu.SemaphoreType.DMA],
        name="moe_combine_sc")(table, idx)


def _experts_kernel(be_ref, nv_ref, xs_ref, wg_ref, wu_ref, wd_ref, y_ref, wg_b, wu_b, wd_b):
    i = pl.program_id(0)
    prev = be_ref[jnp.maximum(i - 1, 0)]

    @pl.when(jnp.logical_or(i == 0, be_ref[i] != prev))
    def _():
        wg_b[...] = wg_ref[0, 0].astype(BF16)
        wu_b[...] = wu_ref[0, 0].astype(BF16)
        wd_b[...] = wd_ref[0, 0].astype(BF16)

    @pl.when(i < nv_ref[0])
    def _():
        x_lo, x_hi = _unpack_halves(xs_ref[...])
        x_lo, x_hi = x_lo.astype(BF16), x_hi.astype(BF16)
        half = x_lo.shape[1]

        def proj(w):
            return (jnp.dot(x_lo, w[:half, :], preferred_element_type=F32)
                    + jnp.dot(x_hi, w[half:, :], preferred_element_type=F32))

        h = _silu(proj(wg_b)) * proj(wu_b)
        y_ref[...] = _pack_halves(jnp.dot(h.astype(BF16), wd_b[...], preferred_element_type=F32))

    @pl.when(i >= nv_ref[0])
    def _():
        y_ref[...] = jnp.zeros_like(y_ref)


def _experts(blk_exp, n_valid, xs, wg, wu, wd, layer, tb):
    rows, dp = xs.shape
    d = 2 * dp
    grid_spec = pltpu.PrefetchScalarGridSpec(
        num_scalar_prefetch=2,
        grid=(rows // tb,),
        in_specs=[pl.BlockSpec((tb, dp), lambda i, be, nv: (i, 0)),
                  pl.BlockSpec((1, 1, d, D_EXPERT), lambda i, be, nv: (layer, be[i], 0, 0)),
                  pl.BlockSpec((1, 1, d, D_EXPERT), lambda i, be, nv: (layer, be[i], 0, 0)),
                  pl.BlockSpec((1, 1, D_EXPERT, d), lambda i, be, nv: (layer, be[i], 0, 0))],
        out_specs=pl.BlockSpec((tb, dp), lambda i, be, nv: (i, 0)),
        scratch_shapes=[pltpu.VMEM((d, D_EXPERT), BF16), pltpu.VMEM((d, D_EXPERT), BF16),
                        pltpu.VMEM((D_EXPERT, d), BF16)],
    )
    return pl.pallas_call(
        _experts_kernel,
        grid_spec=grid_spec,
        out_shape=jax.ShapeDtypeStruct((rows, dp), U32),
        compiler_params=_params("arbitrary"),
        name="moe_experts",
    )(blk_exp, n_valid, xs, wg, wu, wd)


def _combine_ln_kernel(x_ref, yg_ref, gt_ref, wgu_ref, wd_ref, g_ref, b_ref, o_ref, *, alpha):
    x = x_ref[...]
    gu = jnp.dot(x.astype(BF16), wgu_ref[...], preferred_element_type=F32)
    dsh = gu.shape[1] // 2
    h = _silu(gu[:, :dsh]) * gu[:, dsh:]
    f = jnp.dot(h.astype(BF16), wd_ref[...], preferred_element_type=F32)
    r_lo = r_hi = None
    for k in range(MOE_TOP_K):
        y_lo, y_hi = _unpack_halves(yg_ref[k])
        gate = gt_ref[:, k:k + 1]
        r_lo = gate * y_lo if r_lo is None else r_lo + gate * y_lo
        r_hi = gate * y_hi if r_hi is None else r_hi + gate * y_hi
    f = f + jnp.concatenate([r_lo, r_hi], axis=1)
    o_ref[...] = _layer_norm_rows(alpha * x + f, g_ref[...], b_ref[...])


def _combine_ln(x2, yg, gates_t, wgu, wd, g, b, alpha, tm):
    n, d = x2.shape
    kern = functools.partial(_combine_ln_kernel, alpha=alpha)
    return pl.pallas_call(
        kern,
        grid=(n // tm,),
        in_specs=[pl.BlockSpec((tm, d), lambda i: (i, 0)),
                  pl.BlockSpec((MOE_TOP_K, tm, d // 2), lambda i: (0, i, 0)),
                  pl.BlockSpec((tm, 8), lambda i: (i, 0)),
                  _full(wgu.shape), _full(wd.shape), _full((1, d)), _full((1, d))],
        out_specs=pl.BlockSpec((tm, d), lambda i: (i, 0)),
        out_shape=jax.ShapeDtypeStruct((n, d), F32),
        compiler_params=_params("parallel"),
        name="moe_combine_ln",
    )(x2, yg, gates_t, wgu, wd, g, b)


def _rope_tables(positions, dim, lane_offset, period):
    half = dim // 2
    inv_freq = ROPE_THETA ** (-2.0 * jnp.arange(half, dtype=jnp.float32) / dim)
    ang = positions.astype(jnp.float32)[:, None] * inv_freq[None, :]
    cos_h, sin_h = jnp.cos(ang), jnp.sin(ang)
    lanes = np.arange(LANES)
    rel = (lanes - lane_offset) % period
    active = (lanes >= lane_offset) & (rel < dim)
    fidx = np.where(active, rel % half, 0)
    sign = np.where(rel < half, -1.0, 1.0)
    cos = jnp.where(active[None, :], cos_h[:, fidx], 1.0)
    sin = jnp.where(active[None, :], sin_h[:, fidx] * sign[None, :], 0.0)
    return cos.astype(F32), sin.astype(F32)


def _gate_slabs(w):
    k = w.shape[0]
    g = w.reshape(k, 3, NSA_KV_HEADS, GQA_REP).transpose(0, 2, 1, 3).reshape(k, NSA_KV_HEADS, 3 * GQA_REP)
    return jnp.pad(g, ((0, 0), (0, 0), (0, LANES - 3 * GQA_REP))).reshape(k, NSA_KV_HEADS * LANES)


def _dup_heads(w):
    a, b = w[:, :HEAD_DIM], w[:, HEAD_DIM:]
    return jnp.concatenate([a, a, b, b], axis=1)


def _pad_heads(w):
    a, b = w[:, :HEAD_DIM], w[:, HEAD_DIM:]
    z = jnp.zeros_like(a)
    return jnp.concatenate([a, z, b, z], axis=1)


def _moe_layer(x1_and_packed, rw, rb, wg, wu, wd, layer, shg, shu, shd, ln_g, ln_b, alpha, tm):
    x1, x1_pk = x1_and_packed
    n, d = x1.shape
    n_chunk = 1
    tb = MOE_ROW_BLOCK
    rw_p = jnp.pad(rw, ((0, 0), (0, LANES - N_EXPERTS)))
    rw_top = lax.bitcast_convert_type(lax.bitcast_convert_type(rw_p, U32) & jnp.uint32(0xFFFF0000), F32)
    rw_split = jnp.stack([rw_top.astype(BF16), (rw_p - rw_top).astype(BF16)])
    eidx, gate, pos, cnt = _router(x1, rw_split, rb.reshape(N_EXPERTS, 1), n_chunk, tm)
    counts = cnt[:, :, 0].reshape(-1)
    padded = (counts + tb - 1) // tb * tb
    pad_ends = jnp.cumsum(padded)
    pad_starts = (pad_ends - padded).astype(I32)
    dest = _dest_rows(pad_starts, eidx, pos, n_chunk, tm)[:MOE_TOP_K].reshape(-1)
    n_rows = n * MOE_TOP_K
    n_blk = -(-(n_rows + n_chunk * N_EXPERTS * (tb - 1)) // tb)
    blk_first_row = jnp.arange(n_blk, dtype=I32) * tb
    owner = jnp.sum((pad_ends[None, :] <= blk_first_row[:, None]).astype(I32), axis=1)
    blk_exp = jnp.minimum(owner, n_chunk * N_EXPERTS - 1).astype(I32) % N_EXPERTS
    n_valid = (pad_ends[-1] // tb).astype(I32).reshape(1)

    xs = _sc_scatter_rows(x1_pk, dest, n_blk * tb)
    y = _experts(blk_exp, n_valid, xs, wg, wu, wd, layer, tb)
    yg = _sc_gather_rows(y, dest).reshape(MOE_TOP_K, n, d // 2)
    gates_t = gate.T
    sh_gu = jnp.concatenate([shg, shu], axis=-1).astype(BF16)
    return _combine_ln(x1, yg, gates_t, sh_gu, shd.astype(BF16), ln_g.reshape(1, d), ln_b.reshape(1, d), alpha, tm)


def _swa_nsa_layer(x2, batch, seq, w_in, sinks, pe_k, w_ck1, w_ck2, pe_v, w_cv1, w_cv2, w_out,
                   ln_g, ln_b, alpha, tm):
    n, d = x2.shape
    splits = np.cumsum([512, 128, 128, 512, 128, 128, 128, 128, 128, 128, 24])[:-1]
    qa, ka, va, qb, kc, vc, ksl, vsl, kw, vw, gts = jnp.split(w_in, [int(c) for c in splits], axis=1)
    w_all = jnp.concatenate(
        [qa, _dup_heads(ka), _pad_heads(va), qb, kc, vc, _dup_heads(ksl), _pad_heads(vsl),
         _dup_heads(kw), _pad_heads(vw), _gate_slabs(gts)], axis=1).astype(BF16)
    cos, sin = _rope_tables(jnp.arange(seq), HEAD_DIM, 0, HEAD_DIM)
    (q_a, k_a, v_a, q_b, k_c, v_c, k_s, v_s, k_w, v_w, gates) = _mix0_proj(x2, w_all, cos, sin, seq, tm)

    o_swa = _banded_attention(q_a, k_a, v_a, batch, seq, SWA_WINDOW, 128, sinks=sinks, name="swa")
    o_win = _banded_attention(q_b, k_w, v_w, batch, seq, NSA_WINDOW, 128, gates=gates,
                              gate_col=2 * GQA_REP, out_dtype=F32, name="nsa_win")

    n_chunk = seq // NSA_CMP_STRIDE
    cw = NSA_CMP_STRIDE * NSA_KV_HEADS * HEAD_DIM
    hid = NSA_CMP_HIDDEN

    def expand_w1(w1):
        halves = w1.reshape(2, NSA_CMP_STRIDE, HEAD_DIM, hid)
        out = jnp.zeros((NSA_KV_HEADS, NSA_CMP_STRIDE, NSA_KV_HEADS, HEAD_DIM, 2 * hid), w1.dtype)
        for g in range(NSA_KV_HEADS):
            out = out.at[g, :, g, :, :hid].set(halves[0]).at[g, :, g, :, hid:].set(halves[1])
        return out.reshape(NSA_KV_HEADS, cw, 2 * hid).astype(BF16)

    def pe_rows(pe):
        return jnp.pad(pe.reshape(1, NSA_CMP_LEN * HEAD_DIM), ((0, 15), (0, 0))).astype(BF16)

    cos_c, sin_c = _rope_tables(jnp.arange(n_chunk) * NSA_CMP_STRIDE + NSA_CMP_LEN - 1, HEAD_DIM, 0, HEAD_DIM)
    k_cmp, v_cmp = _compress(
        k_c.reshape(batch * n_chunk, cw), v_c.reshape(batch * n_chunk, cw),
        expand_w1(w_ck1), expand_w1(w_cv1), pe_rows(pe_k), pe_rows(pe_v),
        w_ck1.astype(BF16), w_cv1.astype(BF16),
        jnp.concatenate([w_ck2, w_ck2], axis=1).astype(BF16),
        jnp.concatenate([w_cv2, w_cv2], axis=1).astype(BF16),
        cos_c, sin_c, batch, n_chunk)

    n_sb = seq // NSA_SEL_LEN
    n_sel = min(NSA_N_SEL, n_sb)
    cs = np.arange(n_chunk) * NSA_CMP_STRIDE
    bs = np.arange(n_sb) * NSA_SEL_LEN
    ov = np.clip(np.minimum(cs[:, None] + NSA_CMP_LEN, bs[None, :] + NSA_SEL_LEN)
                 - np.maximum(cs[:, None], bs[None, :]), 0, None) / NSA_CMP_LEN
    overlap = jnp.asarray(np.pad(ov, ((0, 0), (0, LANES - n_sb))).T, F32)
    expand = jnp.asarray(np.arange(LANES)[:, None] == (np.arange(seq)[None, :] // NSA_SEL_LEN), BF16)
    tk_sel = min(512, seq)
    o_nsa = _nsa_cmp_sel(q_b, k_cmp, v_cmp, k_s, v_s, gates, o_win, overlap, expand,
                         batch, seq, 128, n_sb, n_sel, tk_sel)
    w_o = w_out.astype(BF16)
    half = SWA_Q_HEADS * HEAD_DIM
    return _oproj_ln([o_swa, o_nsa], [w_o[:half], w_o[half:]], x2, ln_g.reshape(1, d), ln_b.reshape(1, d),
                     alpha, tm)


def _mla_layer(x2, batch, seq, w_in, q_norm, kv_norm, w_uq, w_ukv, w_out, ln_g, ln_b, alpha, tm):
    n, d = x2.shape
    dq = MLA_NOPE + MLA_ROPE
    w_kr = jnp.zeros((d, LANES), w_in.dtype).at[:, MLA_NOPE:dq].set(w_in[:, MLA_Q_LORA + MLA_KV_LORA:])
    w_in_p = jnp.concatenate([w_in[:, :MLA_Q_LORA + MLA_KV_LORA], w_kr], axis=1).astype(BF16)
    wuq = jnp.pad(w_uq.reshape(MLA_Q_LORA, MLA_HEADS, dq), ((0, 0), (0, 0), (0, LANES - dq)))
    wuq = wuq.reshape(MLA_Q_LORA, MLA_HEADS * LANES).astype(BF16)
    wukv = w_ukv.reshape(MLA_KV_LORA, MLA_HEADS, MLA_NOPE + MLA_V)
    wuk = jnp.pad(wukv[:, :, :MLA_NOPE], ((0, 0), (0, 0), (0, LANES - MLA_NOPE)))
    wuk = wuk.reshape(MLA_KV_LORA, MLA_HEADS * LANES).astype(BF16)
    wuv = jnp.pad(wukv[:, :, MLA_NOPE:], ((0, 0), (0, 0), (0, LANES - MLA_V)))
    wuv = wuv.reshape(MLA_KV_LORA, MLA_HEADS * LANES).astype(BF16)
    cos, sin = _rope_tables(jnp.arange(seq), MLA_ROPE, MLA_NOPE, LANES)
    q, k, v = _mla_proj(x2, w_in_p, q_norm.reshape(1, -1), kv_norm.reshape(1, -1), wuq, wuk, wuv, cos, sin, seq, tm)
    o = _mla_attention(q, k, v, batch, seq, min(512, seq))
    return _oproj_ln([o], [w_out.astype(BF16)], x2, ln_g.reshape(1, d), ln_b.reshape(1, d), alpha, tm)


def kernel(x, swa_nsa_w_in, swa_sinks, nsa_cmp_pe_k, nsa_cmp_k_w1, nsa_cmp_k_w2, nsa_cmp_pe_v, nsa_cmp_v_w1, nsa_cmp_v_w2, swa_nsa_w_out, mla_w_in, mla_q_norm, mla_kv_norm, mla_w_uq, mla_w_ukv, mla_w_out, ln_mix_g, ln_mix_b, ln_ffn_g, ln_ffn_b, router_w, router_bias, expert_w_gate, expert_w_up, expert_w_down, shared_w_gate, shared_w_up, shared_w_down):
    batch, seq, d = x.shape
    depth = ln_mix_g.shape[0]
    alpha = (2 * depth) ** 0.25
    tm = min(512, seq)
    groups = BATCH_GROUPS if batch % BATCH_GROUPS == 0 else 1
    gb = batch // groups
    outs = []
    for grp in range(groups):
        x2 = x[grp * gb:(grp + 1) * gb].reshape(gb * seq, d)
        for layer in range(depth):
            j = layer // 2
            if layer % 2 == 0:
                x2 = _swa_nsa_layer(x2, gb, seq, swa_nsa_w_in[j], swa_sinks[j], nsa_cmp_pe_k[j],
                                    nsa_cmp_k_w1[j], nsa_cmp_k_w2[j], nsa_cmp_pe_v[j], nsa_cmp_v_w1[j],
                                    nsa_cmp_v_w2[j], swa_nsa_w_out[j], ln_mix_g[layer], ln_mix_b[layer], alpha, tm)
            else:
                x2 = _mla_layer(x2, gb, seq, mla_w_in[j], mla_q_norm[j], mla_kv_norm[j], mla_w_uq[j],
                                mla_w_ukv[j], mla_w_out[j], ln_mix_g[layer], ln_mix_b[layer], alpha, tm)
            x2 = _moe_layer(x2, router_w[layer], router_bias[layer], expert_w_gate, expert_w_up, expert_w_down,
                            layer, shared_w_gate[layer], shared_w_up[layer], shared_w_down[layer],
                            ln_ffn_g[layer], ln_ffn_b[layer], alpha, tm)
        outs.append(x2.reshape(gb, seq, d))
    return outs[0] if groups == 1 else jnp.concatenate(outs, axis=0)
```

```python
import functools
import math

import numpy as np
import jax
import jax.numpy as jnp
from jax import lax
from jax.experimental import pallas as pl
from jax.experimental.pallas import tpu as pltpu
from jax.experimental.pallas import tpu_sc as plsc

F32 = jnp.float32
BF16 = jnp.bfloat16
I32 = jnp.int32

LANES = 128
VMEM_LIMIT = 48 * 1024 * 1024

ROPE_THETA = 10000.0
LN_EPS = 1e-5
RMS_EPS = 1e-6
NEG_BIG = -1e30
FORCE_SCORE = 1e4

HEAD_DIM = 64
SWA_Q_HEADS = 8
SWA_KV_HEADS = 2
SWA_WINDOW = 128
NSA_Q_HEADS = 8
NSA_KV_HEADS = 2
NSA_CMP_LEN = 32
NSA_CMP_STRIDE = 16
NSA_CMP_HIDDEN = 128
NSA_SEL_LEN = 64
NSA_N_SEL = 16
NSA_WINDOW = 512
GQA_REP = 4

MLA_HEADS = 16
MLA_NOPE = 64
MLA_ROPE = 32
MLA_V = 64
MLA_Q_LORA = 384
MLA_KV_LORA = 256

N_EXPERTS = 64
MOE_GROUPS = 8
MOE_TOPK_GROUPS = 4
MOE_TOP_K = 6
D_EXPERT = 256
MOE_ROUTED_SCALE = 2.5
MOE_ROW_BLOCK = 512

NT_DIMS = (((1,), (1,)), ((), ()))


def _params(*sem):
    return pltpu.CompilerParams(dimension_semantics=sem, vmem_limit_bytes=VMEM_LIMIT)


def _full(shape):
    nd = len(shape)
    return pl.BlockSpec(shape, lambda *_: (0,) * nd)


def _rope_slab(y, cos, sin, half):
    lane = lax.broadcasted_iota(I32, y.shape, 1)
    first = (lane % (2 * half)) < half
    rot = jnp.where(first, pltpu.roll(y, LANES - half, 1), pltpu.roll(y, half, 1))
    return y * cos + rot * sin


def _layer_norm_rows(v, g, b):
    mu = jnp.mean(v, axis=-1, keepdims=True)
    vc = v - mu
    var = jnp.mean(vc * vc, axis=-1, keepdims=True)
    return vc * lax.rsqrt(var + LN_EPS) * g + b


def _silu(v):
    return v * (1.0 / (1.0 + jnp.exp(-v)))


def _sigmoid(v):
    return 1.0 / (1.0 + jnp.exp(-v))


U32 = jnp.uint32


def _pack_halves(v):
    w = v.shape[1] // 2
    lo = pltpu.bitcast(v[:, :w].astype(BF16).astype(F32), U32)
    hi = pltpu.bitcast(v[:, w:].astype(BF16).astype(F32), U32)
    return (lo >> 16) | (hi & jnp.uint32(0xFFFF0000))


def _unpack_halves(u):
    lo = pltpu.bitcast(u << 16, F32)
    hi = pltpu.bitcast(u & jnp.uint32(0xFFFF0000), F32)
    return lo, hi


def _ones_half(shape):
    lane = lax.broadcasted_iota(I32, shape, 1)
    return jnp.where(lane < HEAD_DIM, 0.0, 1.0).astype(F32)


def _stack_group_queries(q_ref, group, rep):
    lane = lax.broadcasted_iota(I32, (q_ref.shape[0], LANES), 1)
    keep_lo = jnp.where(lane < HEAD_DIM, 1.0, 0.0).astype(q_ref.dtype)
    keep_hi = jnp.where(lane < HEAD_DIM, 0.0, 1.0).astype(q_ref.dtype)
    parts = []
    for r in range(rep):
        h = group * rep + r
        slab = q_ref[:, (h // 2) * LANES:(h // 2 + 1) * LANES]
        parts.append(slab * (keep_lo if h % 2 == 0 else keep_hi))
    return jnp.concatenate(parts, axis=0)


def _rank_desc(score, n_valid, axis):
    idx = lax.broadcasted_iota(I32, score.shape, axis)
    rank = jnp.zeros(score.shape, I32)
    for j in range(n_valid):
        other = lax.slice_in_dim(score, j, j + 1, axis=axis)
        ahead = jnp.where(other > score, 1, jnp.where(other == score, jnp.where(idx > j, 1, 0), 0))
        rank = rank + ahead
    return rank


_MIX0_OUTS = (
    ("qa", 512, "rope_q", BF16), ("ka", 256, "rope", BF16), ("va", 256, "v_ones", BF16),
    ("qb", 512, "rope_q", BF16), ("kc", 128, "plain", BF16), ("vc", 128, "plain", BF16),
    ("ks", 256, "rope", BF16), ("vs", 256, "v_ones", BF16),
    ("kw", 256, "rope", BF16), ("vw", 256, "v_ones", BF16),
    ("gates", 256, "sigmoid", F32),
)


def _mix0_proj_kernel(x_ref, w_ref, cos_ref, sin_ref, *out_refs):
    xb = x_ref[...].astype(BF16)
    cos = cos_ref[...]
    sin = sin_ref[...]
    off = 0
    for (_, width, kind, _), o_ref in zip(_MIX0_OUTS, out_refs):
        y = jnp.dot(xb, w_ref[:, off:off + width], preferred_element_type=F32)
        for c in range(width // LANES):
            yc = y[:, c * LANES:(c + 1) * LANES]
            if kind in ("rope", "rope_q"):
                yc = _rope_slab(yc, cos, sin, HEAD_DIM // 2)
            if kind == "rope_q":
                yc = yc * (HEAD_DIM ** -0.5 * LOG2_E)
            if kind == "v_ones":
                yc = yc + _ones_half(yc.shape)
            if kind == "sigmoid":
                yc = _sigmoid(yc)
            o_ref[:, c * LANES:(c + 1) * LANES] = yc.astype(o_ref.dtype)
        off += width


def _mix0_proj(x2, w_all, cos, sin, seq, tm):
    n = x2.shape[0]
    d = x2.shape[1]
    wtot = w_all.shape[1]
    sblk = seq // tm
    return pl.pallas_call(
        _mix0_proj_kernel,
        grid=(n // tm,),
        in_specs=[pl.BlockSpec((tm, d), lambda i: (i, 0)),
                  _full((d, wtot)),
                  pl.BlockSpec((tm, LANES), lambda i: (i % sblk, 0)),
                  pl.BlockSpec((tm, LANES), lambda i: (i % sblk, 0))],
        out_specs=[pl.BlockSpec((tm, w), lambda i: (i, 0)) for _, w, _, _ in _MIX0_OUTS],
        out_shape=[jax.ShapeDtypeStruct((n, w), dt) for _, w, _, dt in _MIX0_OUTS],
        compiler_params=_params("parallel"),
        name="mix0_proj",
    )(x2, w_all, cos, sin)


def _banded_kernel(*refs, window, blk, seq, has_sink, gate_col):
    if has_sink:
        sink_ref, refs = refs[0], refs[1:]
    q_ref, k_ref, v_ref = refs[:3]
    g_ref = refs[3] if gate_col is not None else None
    o_ref, bias_scr, s_scr, p_scr, m_scr, a_scr, acc_scr = refs[-7:]
    i = pl.program_id(1)
    start = i * blk
    span = min(window + blk, seq)
    k0 = pl.multiple_of(jnp.maximum(start - window, 0), LANES)
    tq = start + lax.broadcasted_iota(I32, (blk, span), 0)
    tk = k0 + lax.broadcasted_iota(I32, (blk, span), 1)
    diff = tq - tk
    bias_scr[...] = jnp.where(diff >= 0, jnp.where(diff < window, 0.0, -jnp.inf), -jnp.inf).astype(F32)
    chunks_per_head = blk // FLASH_ROWS

    def bias_fn(c):
        r0 = (c % chunks_per_head) * FLASH_ROWS
        return bias_scr[r0:r0 + FLASH_ROWS, :]

    lane = lax.broadcasted_iota(I32, (blk, LANES), 1)
    lo = lane < HEAD_DIM
    n_groups = q_ref.shape[1] // (GQA_REP * HEAD_DIM)
    for g in range(n_groups):
        _flash_init(m_scr.at[g], acc_scr.at[g])
        _flash_scores(_stack_group_queries(q_ref, g, GQA_REP), k_ref[pl.ds(k0, span), g * LANES:(g + 1) * LANES],
                      s_scr.at[g])
    for g in range(n_groups):
        _flash_max(bias_fn, s_scr.at[g], m_scr.at[g], a_scr.at[g])
    for g in range(n_groups):
        _flash_weights(v_ref[pl.ds(k0, span), g * LANES:(g + 1) * LANES], s_scr.at[g], p_scr.at[g],
                       m_scr.at[g], a_scr.at[g], acc_scr.at[g])
    for g in range(n_groups):
        acc = acc_scr[g]
        if has_sink:
            m = m_scr[g]
            sink = jnp.concatenate(
                [jnp.full((blk, LANES), sink_ref[g * GQA_REP + r] * LOG2_E, F32) for r in range(GQA_REP)], axis=0)
            m_all = jnp.maximum(m, sink)
            acc = acc * jnp.exp2(m - m_all) + _ones_half(acc.shape) * jnp.exp2(sink - m_all)
        o = _flash_finish(acc)
        for j in range(GQA_REP // 2):
            slab = _pair_slab(o[(2 * j) * blk:(2 * j + 1) * blk], o[(2 * j + 1) * blk:(2 * j + 2) * blk])
            if gate_col is not None:
                c = g * LANES + gate_col + 2 * j
                gate = jnp.where(lo, g_ref[:, c:c + 1], g_ref[:, c + 1:c + 2])
                slab = slab * gate
            col = (g * (GQA_REP // 2) + j) * LANES
            o_ref[:, col:col + LANES] = slab.astype(o_ref.dtype)


def _banded_attention(q, k, v, batch, seq, window, blk, sinks=None, gates=None, gate_col=None,
                      out_dtype=None, name="banded"):
    n, qw = q.shape
    kw = k.shape[1]
    nblk = seq // blk
    span = min(window + blk, seq)
    groups = qw // (GQA_REP * HEAD_DIM)
    rows = GQA_REP * blk
    kern = functools.partial(_banded_kernel, window=window, blk=blk, seq=seq,
                             has_sink=sinks is not None, gate_col=gate_col)
    in_specs = []
    args = []
    if sinks is not None:
        in_specs.append(pl.BlockSpec(memory_space=pltpu.SMEM))
        args.append(sinks)
    in_specs += [pl.BlockSpec((blk, qw), lambda b, i: (b * nblk + i, 0)),
                 pl.BlockSpec((seq, kw), lambda b, i: (b, 0)),
                 pl.BlockSpec((seq, kw), lambda b, i: (b, 0))]
    args += [q, k, v]
    if gates is not None:
        in_specs.append(pl.BlockSpec((blk, gates.shape[1]), lambda b, i: (b * nblk + i, 0)))
        args.append(gates)
    return pl.pallas_call(
        kern,
        grid=(batch, nblk),
        in_specs=in_specs,
        out_specs=pl.BlockSpec((blk, qw), lambda b, i: (b * nblk + i, 0)),
        out_shape=jax.ShapeDtypeStruct((n, qw), BF16 if out_dtype is None else out_dtype),
        scratch_shapes=[pltpu.VMEM((blk, span), F32),
                        pltpu.VMEM((groups, rows, span), F32), pltpu.VMEM((groups, rows, span), BF16)]
                       + [pltpu.VMEM((groups, rows, LANES), F32)] * 3,
        compiler_params=_params("parallel", "arbitrary"),
        name=name,
    )(*args)


def _gelu_tanh(v):
    return 0.5 * v * (1.0 + jnp.tanh(math.sqrt(2.0 / math.pi) * (v + 0.044715 * (v * v * v))))


def _compress_kernel(tk_ref, tv_ref, w1k_ref, w1v_ref, pek_ref, pev_ref, w1ko_ref, w1vo_ref,
                     w2k_ref, w2v_ref, cos_ref, sin_ref, kc_ref, vc_ref):
    n_chunk = tk_ref.shape[0]
    hid = NSA_CMP_HIDDEN
    for t_ref, w1_ref, pe_ref, w1o_ref, w2_ref, o_ref, rope in (
            (tk_ref, w1k_ref, pek_ref, w1ko_ref, w2k_ref, kc_ref, True),
            (tv_ref, w1v_ref, pev_ref, w1vo_ref, w2v_ref, vc_ref, False)):
        pe_term = jnp.dot(pe_ref[...], w1o_ref[...], preferred_element_type=F32)[0:1, :]
        t = t_ref[...]
        for g in range(NSA_KV_HEADS):
            uv = jnp.dot(t, w1_ref[g], preferred_element_type=F32)
            nxt = pltpu.roll(uv[:, hid:], n_chunk - 1, 0)
            h = _gelu_tanh(uv[:, :hid] + nxt + pe_term)
            c = jnp.dot(h.astype(BF16), w2_ref[...], preferred_element_type=F32)
            if rope:
                c = _rope_slab(c, cos_ref[...], sin_ref[...], HEAD_DIM // 2)
            o_ref[0, g] = c.astype(o_ref.dtype)


def _compress(tk, tv, w1k, w1v, pek, pev, w1ko, w1vo, w2k, w2v, cos, sin, batch, n_chunk):
    width = tk.shape[1]
    out = jax.ShapeDtypeStruct((batch, NSA_KV_HEADS, n_chunk, LANES), BF16)
    ospec = pl.BlockSpec((1, NSA_KV_HEADS, n_chunk, LANES), lambda b: (b, 0, 0, 0))
    return pl.pallas_call(
        _compress_kernel,
        grid=(batch,),
        in_specs=[pl.BlockSpec((n_chunk, width), lambda b: (b, 0)),
                  pl.BlockSpec((n_chunk, width), lambda b: (b, 0)),
                  _full(w1k.shape), _full(w1v.shape), _full(pek.shape), _full(pev.shape),
                  _full(w1ko.shape), _full(w1vo.shape), _full(w2k.shape), _full(w2v.shape),
                  _full(cos.shape), _full(sin.shape)],
        out_specs=[ospec, ospec],
        out_shape=[out, out],
        compiler_params=_params("parallel"),
        name="nsa_compress",
    )(tk, tv, w1k, w1v, pek, pev, w1ko, w1vo, w2k, w2v, cos, sin)


def _nsa_kernel(q_ref, kc_ref, vc_ref, ks_ref, vs_ref, g_ref, win_ref, ov_ref, e_ref, o_ref,
                bias_scr, s_scr, p_scr, m_scr, a_scr, acc_scr, *, qblk, seq, n_sb, n_sel, tk_sel):
    i = pl.program_id(2)
    start = i * qblk
    n_cmp = kc_ref.shape[2]
    lane = lax.broadcasted_iota(I32, (qblk, LANES), 1)
    lo = lane < HEAD_DIM
    q = _stack_group_queries(q_ref, 0, GQA_REP)
    tq = start + lax.broadcasted_iota(I32, (qblk, 1), 0)

    kc = kc_ref[0, 0]
    vc = vc_ref[0, 0]
    cmp_end = lax.broadcasted_iota(I32, (qblk, n_cmp), 1) * NSA_CMP_STRIDE + (NSA_CMP_LEN - 1)
    vis = jnp.where(cmp_end <= tq, 1.0, 0.0).astype(F32)
    vis4 = jnp.concatenate([vis] * GQA_REP, axis=0)
    s_c = lax.dot_general(q, kc, NT_DIMS, preferred_element_type=F32)
    s_c = jnp.where(vis4 > 0.5, s_c, NEG_BIG)
    p_c = jnp.exp2(s_c - jnp.max(s_c, axis=-1, keepdims=True)) * vis4
    p_c = p_c / jnp.maximum(jnp.sum(p_c, axis=-1, keepdims=True), 1e-30)
    o_c = jnp.dot(p_c.astype(BF16), vc, preferred_element_type=F32)

    p_sum = p_c[0:qblk]
    for r in range(1, GQA_REP):
        p_sum = p_sum + p_c[r * qblk:(r + 1) * qblk]
    imp_t = lax.dot_general(ov_ref[...], p_sum, NT_DIMS, preferred_element_type=F32,
                            precision=lax.Precision.HIGHEST)[:n_sb]
    blk_t = lax.broadcasted_iota(I32, (n_sb, qblk), 0)
    tq_t = start + lax.broadcasted_iota(I32, (n_sb, qblk), 1)
    cur_t = tq_t // NSA_SEL_LEN
    forced = jnp.where(blk_t == 0, 1, jnp.where(blk_t == cur_t, 1, jnp.where(blk_t == cur_t - 1, 1, 0)))
    score_t = jnp.where(forced == 1, FORCE_SCORE, jnp.where(blk_t * NSA_SEL_LEN > tq_t, -1.0, imp_t))
    sel_t = jnp.where(_rank_desc(score_t, n_sb, 0) < n_sel, 1.0, 0.0).astype(F32)
    if n_sb < LANES:
        sel_t = jnp.concatenate([sel_t, jnp.zeros((LANES - n_sb, qblk), F32)], axis=0)
    sel = jnp.transpose(sel_t).astype(BF16)

    _flash_init(m_scr, acc_scr)
    chunks_per_head = qblk // FLASH_ROWS

    def body(j, carry):
        k0 = pl.multiple_of(j * tk_sel, tk_sel)
        picked = jnp.dot(sel, e_ref[:, pl.ds(k0, tk_sel)], preferred_element_type=F32)
        tk = k0 + lax.broadcasted_iota(I32, (qblk, tk_sel), 1)
        bias_scr[...] = jnp.where(picked > 0.5, jnp.where(tk <= tq, 0.0, -jnp.inf), -jnp.inf).astype(F32)

        def bias_fn(c):
            r0 = (c % chunks_per_head) * FLASH_ROWS
            return bias_scr[r0:r0 + FLASH_ROWS, :]

        _flash_scores(q, ks_ref[pl.ds(k0, tk_sel), :], s_scr)
        _flash_max(bias_fn, s_scr, m_scr, a_scr)
        _flash_weights(vs_ref[pl.ds(k0, tk_sel), :], s_scr, p_scr, m_scr, a_scr, acc_scr)
        return carry

    n_tiles = (start + qblk + tk_sel - 1) // tk_sel
    lax.fori_loop(0, n_tiles, body, 0)
    o_s = _flash_finish(acc_scr[...])

    for j in range(GQA_REP // 2):

        def gate(branch, pair=j):
            c = branch * GQA_REP + 2 * pair
            return jnp.where(lo, g_ref[:, c:c + 1], g_ref[:, c + 1:c + 2])

        rows_e = slice((2 * j) * qblk, (2 * j + 1) * qblk)
        rows_o = slice((2 * j + 1) * qblk, (2 * j + 2) * qblk)
        oc = jnp.where(lo, o_c[rows_e], o_c[rows_o])
        os_ = _pair_slab(o_s[rows_e], o_s[rows_o])
        out = gate(0) * oc + gate(1) * os_ + win_ref[:, j * LANES:(j + 1) * LANES]
        o_ref[:, j * LANES:(j + 1) * LANES] = out.astype(o_ref.dtype)


def _nsa_cmp_sel(qb, kcmp, vcmp, ks, vs, gates, win, overlap, expand, batch, seq, qblk, n_sb, n_sel, tk_sel):
    n = qb.shape[0]
    nq = seq // qblk
    n_cmp = kcmp.shape[2]
    gw = GQA_REP * HEAD_DIM
    rows = GQA_REP * qblk
    kern = functools.partial(_nsa_kernel, qblk=qblk, seq=seq, n_sb=n_sb, n_sel=n_sel, tk_sel=tk_sel)
    return pl.pallas_call(
        kern,
        grid=(batch, NSA_KV_HEADS, nq),
        in_specs=[pl.BlockSpec((qblk, gw), lambda b, g, i: (b * nq + i, g)),
                  pl.BlockSpec((1, 1, n_cmp, LANES), lambda b, g, i: (b, g, 0, 0)),
                  pl.BlockSpec((1, 1, n_cmp, LANES), lambda b, g, i: (b, g, 0, 0)),
                  pl.BlockSpec((seq, LANES), lambda b, g, i: (b, g)),
                  pl.BlockSpec((seq, LANES), lambda b, g, i: (b, g)),
                  pl.BlockSpec((qblk, LANES), lambda b, g, i: (b * nq + i, g)),
                  pl.BlockSpec((qblk, gw), lambda b, g, i: (b * nq + i, g)),
                  _full(overlap.shape), _full(expand.shape)],
        out_specs=pl.BlockSpec((qblk, gw), lambda b, g, i: (b * nq + i, g)),
        out_shape=jax.ShapeDtypeStruct((n, NSA_Q_HEADS * HEAD_DIM), BF16),
        scratch_shapes=[pltpu.VMEM((qblk, tk_sel), F32),
                        pltpu.VMEM((rows, tk_sel), F32), pltpu.VMEM((rows, tk_sel), BF16),
                        pltpu.VMEM((rows, LANES), F32), pltpu.VMEM((rows, LANES), F32),
                        pltpu.VMEM((rows, LANES), F32)],
        compiler_params=_params("parallel", "parallel", "arbitrary"),
        name="nsa_cmp_sel",
    )(qb, kcmp, vcmp, ks, vs, gates, win, overlap, expand)


def _oproj_ln_kernel(*refs, n_parts, alpha):
    o_refs = refs[:n_parts]
    w_refs = refs[n_parts:2 * n_parts]
    x_ref, g_ref, b_ref, y_ref, ypk_ref = refs[2 * n_parts:]
    h = jnp.dot(o_refs[0][...], w_refs[0][...], preferred_element_type=F32)
    for o_ref, w_ref in zip(o_refs[1:], w_refs[1:]):
        h = h + jnp.dot(o_ref[...], w_ref[...], preferred_element_type=F32)
    y = _layer_norm_rows(alpha * x_ref[...] + h, g_ref[...], b_ref[...])
    y_ref[...] = y
    ypk_ref[...] = _pack_halves(y)


def _oproj_ln(o_parts, w_parts, x2, g, b, alpha, tm):
    n, d = x2.shape
    kern = functools.partial(_oproj_ln_kernel, n_parts=len(o_parts), alpha=alpha)
    in_specs = ([pl.BlockSpec((tm, o.shape[1]), lambda i: (i, 0)) for o in o_parts]
                + [_full(w.shape) for w in w_parts]
                + [pl.BlockSpec((tm, d), lambda i: (i, 0)), _full((1, d)), _full((1, d))])
    return pl.pallas_call(
        kern,
        grid=(n // tm,),
        in_specs=in_specs,
        out_specs=[pl.BlockSpec((tm, d), lambda i: (i, 0)), pl.BlockSpec((tm, d // 2), lambda i: (i, 0))],
        out_shape=[jax.ShapeDtypeStruct((n, d), F32), jax.ShapeDtypeStruct((n, d // 2), U32)],
        compiler_params=_params("parallel"),
        name="oproj_ln",
    )(*o_parts, *w_parts, x2, g, b)


def _rms_rows(v, g):
    return v * lax.rsqrt(jnp.mean(v * v, axis=-1, keepdims=True) + RMS_EPS) * g


def _mla_proj_kernel(x_ref, win_ref, qn_ref, kvn_ref, wuq_ref, wuk_ref, wuv_ref, cos_ref, sin_ref,
                     q_ref, k_ref, v_ref):
    xb = x_ref[...].astype(BF16)
    cos = cos_ref[...]
    sin = sin_ref[...]
    lat = jnp.dot(xb, win_ref[...], preferred_element_type=F32)
    cq = _rms_rows(lat[:, :MLA_Q_LORA], qn_ref[...]).astype(BF16)
    ckv = _rms_rows(lat[:, MLA_Q_LORA:MLA_Q_LORA + MLA_KV_LORA], kvn_ref[...]).astype(BF16)
    kr = _rope_slab(lat[:, MLA_Q_LORA + MLA_KV_LORA:], cos, sin, MLA_ROPE // 2)
    q = jnp.dot(cq, wuq_ref[...], preferred_element_type=F32)
    k = jnp.dot(ckv, wuk_ref[...], preferred_element_type=F32)
    v = jnp.dot(ckv, wuv_ref[...], preferred_element_type=F32)
    ones_half = _ones_half((x_ref.shape[0], LANES))
    q_scale = (MLA_NOPE + MLA_ROPE) ** -0.5 * LOG2_E
    for h in range(MLA_HEADS):
        sl = slice(h * LANES, (h + 1) * LANES)
        q_ref[:, sl] = (_rope_slab(q[:, sl], cos, sin, MLA_ROPE // 2) * q_scale).astype(BF16)
        k_ref[:, sl] = (k[:, sl] + kr).astype(BF16)
        v_ref[:, sl] = (v[:, sl] + ones_half).astype(BF16)


def _mla_proj(x2, w_in, qn, kvn, wuq, wuk, wuv, cos, sin, seq, tm):
    n, d = x2.shape
    sblk = seq // tm
    hw = MLA_HEADS * LANES
    vw = hw
    return pl.pallas_call(
        _mla_proj_kernel,
        grid=(n // tm,),
        in_specs=[pl.BlockSpec((tm, d), lambda i: (i, 0)),
                  _full(w_in.shape), _full(qn.shape), _full(kvn.shape),
                  _full(wuq.shape), _full(wuk.shape), _full(wuv.shape),
                  pl.BlockSpec((tm, LANES), lambda i: (i % sblk, 0)),
                  pl.BlockSpec((tm, LANES), lambda i: (i % sblk, 0))],
        out_specs=[pl.BlockSpec((tm, hw), lambda i: (i, 0)),
                   pl.BlockSpec((tm, hw), lambda i: (i, 0)),
                   pl.BlockSpec((tm, vw), lambda i: (i, 0))],
        out_shape=[jax.ShapeDtypeStruct((n, hw), BF16), jax.ShapeDtypeStruct((n, hw), BF16),
                   jax.ShapeDtypeStruct((n, vw), BF16)],
        compiler_params=_params("parallel"),
        name="mla_proj",
    )(x2, w_in, qn, kvn, wuq, wuk, wuv, cos, sin)


FLASH_ROWS = 32
LOG2_E = 1.4426950408889634


def _flash_init(m_scr, acc_scr):
    m_scr[...] = jnp.full(m_scr.shape, -jnp.inf, F32)
    acc_scr[...] = jnp.zeros(acc_scr.shape, F32)


def _flash_scores(q, k, s_scr):
    s_scr[...] = lax.dot_general(q, k, NT_DIMS, preferred_element_type=F32)


def _flash_max(bias_fn, s_scr, m_scr, a_scr):
    rows, keys = s_scr.shape
    n_chunks = rows // FLASH_ROWS
    slabs = keys // LANES
    for c in range(n_chunks):
        r = slice(c * FLASH_ROWS, (c + 1) * FLASH_ROWS)
        s = s_scr[r, :]
        if bias_fn is not None:
            s = s + bias_fn(c)
            s_scr[r, :] = s
        mx = s[:, 0:LANES]
        for j in range(1, slabs):
            mx = jnp.maximum(mx, s[:, j * LANES:(j + 1) * LANES])
        a_scr[r, :] = mx
    m_old = m_scr[...]
    m_new = jnp.maximum(m_old, jnp.broadcast_to(jnp.max(a_scr[...], axis=-1, keepdims=True), m_old.shape))
    m_scr[...] = m_new
    a_scr[...] = jnp.exp2(m_old - m_new)


def _flash_weights(v, s_scr, p_scr, m_scr, a_scr, acc_scr):
    rows, keys = s_scr.shape
    n_chunks = rows // FLASH_ROWS
    slabs = keys // LANES
    for c in range(n_chunks):
        r = slice(c * FLASH_ROWS, (c + 1) * FLASH_ROWS)
        m_rows = m_scr[r, :]
        for j in range(slabs):
            cols = slice(j * LANES, (j + 1) * LANES)
            p_scr[r, cols] = jnp.exp2(s_scr[r, cols] - m_rows).astype(p_scr.dtype)
    acc_scr[...] = a_scr[...] * acc_scr[...] + jnp.dot(p_scr[...], v, preferred_element_type=F32)


def _flash_finish(acc):
    return acc / acc[:, HEAD_DIM:HEAD_DIM + 1]


def _pair_slab(even, odd):
    lane = lax.broadcasted_iota(I32, even.shape, 1)
    return jnp.where(lane < HEAD_DIM, even, pltpu.roll(odd, HEAD_DIM, 1))


def _mla_attn_kernel(q_ref, k_ref, v_ref, o_ref, s_scr, p_scr, m_scr, a_scr, acc_scr, *, tq):
    i = pl.program_id(2)
    for e in range(2):
        _flash_init(m_scr.at[e], acc_scr.at[e])

    def diag_bias(c):
        row = c * FLASH_ROWS + lax.broadcasted_iota(I32, (FLASH_ROWS, tq), 0)
        col = lax.broadcasted_iota(I32, (FLASH_ROWS, tq), 1)
        return jnp.where(col <= row, 0.0, -jnp.inf).astype(F32)

    def tile(j, bias_fn):
        k0 = pl.multiple_of(j * tq, tq)
        for e in range(2):
            _flash_scores(q_ref[:, e * LANES:(e + 1) * LANES], k_ref[pl.ds(k0, tq), e * LANES:(e + 1) * LANES],
                          s_scr.at[e])
        for e in range(2):
            _flash_max(bias_fn, s_scr.at[e], m_scr.at[e], a_scr.at[e])
        for e in range(2):
            _flash_weights(v_ref[pl.ds(k0, tq), e * LANES:(e + 1) * LANES], s_scr.at[e], p_scr.at[e],
                           m_scr.at[e], a_scr.at[e], acc_scr.at[e])

    def body(j, carry):
        tile(j, None)
        return carry

    lax.fori_loop(0, i, body, 0)
    tile(i, diag_bias)
    o_ref[...] = _pair_slab(_flash_finish(acc_scr[0]), _flash_finish(acc_scr[1])).astype(o_ref.dtype)


def _mla_attention(q, k, v, batch, seq, tq):
    n = q.shape[0]
    nq = seq // tq
    kern = functools.partial(_mla_attn_kernel, tq=tq)
    return pl.pallas_call(
        kern,
        grid=(batch, MLA_HEADS // 2, nq),
        in_specs=[pl.BlockSpec((tq, 2 * LANES), lambda b, p, i: (b * nq + i, p)),
                  pl.BlockSpec((seq, 2 * LANES), lambda b, p, i: (b, p)),
                  pl.BlockSpec((seq, 2 * LANES), lambda b, p, i: (b, p))],
        out_specs=pl.BlockSpec((tq, LANES), lambda b, p, i: (b * nq + i, p)),
        out_shape=jax.ShapeDtypeStruct((n, MLA_HEADS * MLA_V), BF16),
        scratch_shapes=[pltpu.VMEM((2, tq, tq), F32), pltpu.VMEM((2, tq, tq), BF16)]
                       + [pltpu.VMEM((2, tq, LANES), F32)] * 3,
        compiler_params=_params("parallel", "parallel", "arbitrary"),
        name="mla_attn",
    )(q, k, v)


def _router_kernel(x_ref, rw_ref, rb_ref, eidx_ref, gate_ref, pos_ref, cnt_ref, carry_ref, *, steps_per_chunk):
    i = pl.program_id(0)
    tm = x_ref.shape[0]
    per_group = N_EXPERTS // MOE_GROUPS

    @pl.when(i % steps_per_chunk == 0)
    def _():
        carry_ref[...] = jnp.zeros_like(carry_ref)

    x = x_ref[...]
    x_top = pltpu.bitcast(pltpu.bitcast(x, U32) & jnp.uint32(0xFFFF0000), F32)
    x_hi = x_top.astype(BF16)
    x_lo = (x - x_top).astype(BF16)
    logits = (jnp.dot(x_hi, rw_ref[0], preferred_element_type=F32)
              + jnp.dot(x_hi, rw_ref[1], preferred_element_type=F32)
              + jnp.dot(x_lo, rw_ref[0], preferred_element_type=F32))
    lt = jnp.transpose(logits)[:N_EXPERTS]
    s = _sigmoid(lt)
    sb = s + rb_ref[...]

    g3 = sb.reshape(MOE_GROUPS, per_group, tm)
    idx3 = lax.broadcasted_iota(I32, g3.shape, 1).astype(F32)
    m1 = jnp.max(g3, axis=1, keepdims=True)
    first = jnp.min(jnp.where(g3 == m1, idx3, float(per_group)), axis=1, keepdims=True)
    m2 = jnp.max(jnp.where(idx3 == first, -jnp.inf, g3), axis=1, keepdims=True)
    gscore = (m1 + m2).reshape(MOE_GROUPS, tm)
    gsel = _rank_desc(gscore, MOE_GROUPS, 0) < MOE_TOPK_GROUPS
    gsel3 = jnp.where(gsel, 1.0, 0.0).astype(F32).reshape(MOE_GROUPS, 1, tm)
    masked = jnp.where(gsel3 > 0.5, g3, -jnp.inf).reshape(N_EXPERTS, tm)
    e_iota = lax.broadcasted_iota(I32, (N_EXPERTS, tm), 0).astype(F32)
    work = masked
    picks, rows_g = [], []
    sel_f = jnp.zeros((N_EXPERTS, tm), F32)
    for r in range(MOE_TOP_K):
        top = jnp.max(work, axis=0, keepdims=True)
        first = jnp.min(jnp.where(work == top, e_iota, float(N_EXPERTS)), axis=0, keepdims=True)
        hit = e_iota == first
        picks.append(first)
        rows_g.append(jnp.sum(jnp.where(hit, s, 0.0), axis=0, keepdims=True))
        work = jnp.where(hit, -jnp.inf, work)
        sel_f = jnp.where(hit, 1.0, sel_f)
    gate_sum = rows_g[0]
    for g_row in rows_g[1:]:
        gate_sum = gate_sum + g_row
    rows_g = [g_row / gate_sum * MOE_ROUTED_SCALE for g_row in rows_g]

    sel_b = sel_f.astype(BF16)
    r_i = lax.broadcasted_iota(I32, (tm, tm), 0)
    c_i = lax.broadcasted_iota(I32, (tm, tm), 1)
    tri = jnp.where(r_i < c_i, 1.0, 0.0).astype(BF16)
    carry = carry_ref[:, 0:1]
    before = jnp.dot(sel_b, tri, preferred_element_type=F32) + carry
    carry_new = carry + jnp.sum(sel_b.astype(F32), axis=1, keepdims=True)
    carry_ref[...] = jnp.broadcast_to(carry_new, carry_ref.shape)
    cnt_ref[0] = jnp.broadcast_to(carry_new, carry_ref.shape).astype(I32)

    rows_p = [jnp.sum(jnp.where(e_iota == first, before, 0.0), axis=0, keepdims=True) for first in picks]
    pad = 8 - MOE_TOP_K
    eidx_ref[...] = jnp.concatenate(picks + [jnp.zeros((pad, tm), F32)], axis=0).astype(I32)
    gate_ref[...] = jnp.concatenate(rows_g + [jnp.zeros((pad, tm), F32)], axis=0)
    pos_ref[...] = jnp.concatenate(rows_p + [jnp.zeros((pad, tm), F32)], axis=0).astype(I32)


def _router(x2, rw, rb, n_chunk, tm):
    n, d = x2.shape
    steps = n // tm
    spc = steps // n_chunk
    kern = functools.partial(_router_kernel, steps_per_chunk=spc)
    row8 = pl.BlockSpec((8, tm), lambda i: (0, i))
    return pl.pallas_call(
        kern,
        grid=(steps,),
        in_specs=[pl.BlockSpec((tm, d), lambda i: (i, 0)), _full(rw.shape), _full(rb.shape)],
        out_specs=[row8, row8, row8, pl.BlockSpec((1, N_EXPERTS, LANES), lambda i: (i // spc, 0, 0))],
        out_shape=[jax.ShapeDtypeStruct((8, n), I32), jax.ShapeDtypeStruct((8, n), F32),
                   jax.ShapeDtypeStruct((8, n), I32),
                   jax.ShapeDtypeStruct((n_chunk, N_EXPERTS, LANES), I32)],
        scratch_shapes=[pltpu.VMEM((N_EXPERTS, LANES), F32)],
        compiler_params=_params("arbitrary"),
        name="moe_router",
    )(x2, rw, rb)


def _dest_kernel(tab_ref, eidx_ref, pos_ref, dest_ref, *, steps_per_chunk):
    chunk = pl.program_id(0) // steps_per_chunk
    eidx = eidx_ref[...]
    dest = pos_ref[...]
    for e in range(N_EXPERTS):
        dest = dest + jnp.where(eidx == e, tab_ref[chunk * N_EXPERTS + e], 0)
    dest_ref[...] = dest


def _dest_rows(pad_starts, eidx, pos, n_chunk, tm):
    n = eidx.shape[1]
    steps = n // tm
    row8 = lambda i, tab: (0, i)
    return pl.pallas_call(
        functools.partial(_dest_kernel, steps_per_chunk=steps // n_chunk),
        grid_spec=pltpu.PrefetchScalarGridSpec(
            num_scalar_prefetch=1, grid=(steps,),
            in_specs=[pl.BlockSpec((8, tm), row8), pl.BlockSpec((8, tm), row8)],
            out_specs=pl.BlockSpec((8, tm), row8)),
        out_shape=jax.ShapeDtypeStruct((8, n), I32),
        compiler_params=_params("parallel"),
        name="moe_dest",
    )(pad_starts, eidx, pos)


SC_ROWS = 64
SC_WORKERS = 32


def _sc_mesh():
    return plsc.VectorSubcoreMesh(core_axis_name="c", subcore_axis_name="s")


def _sc_worker_base(per_worker):
    return (lax.axis_index("s") * 2 + lax.axis_index("c")) * per_worker


def _sc_scatter_rows(x, dest_chunks, n_out):
    n, d = x.shape
    slots = dest_chunks.shape[0] * SC_ROWS // n
    per_w = n // SC_WORKERS
    chunks = per_w // SC_ROWS
    assert chunks * SC_ROWS * SC_WORKERS == n and chunks % 2 == 0, (n, chunks)

    def body(x_hbm, i_hbm, o_hbm, idx_v, rows0, rows1, sem_in0, sem_in1, sem_out):
        worker = lax.axis_index("s") * 2 + lax.axis_index("c")
        base = worker * per_w
        bufs = ((rows0, sem_in0), (rows1, sem_in1))
        pltpu.sync_copy(i_hbm.at[pl.ds(pl.multiple_of(worker * (chunks * slots), 8), chunks * slots)], idx_v)

        def load(c, b):
            off = pl.multiple_of(base + c * SC_ROWS, 8)
            return pltpu.make_async_copy(x_hbm.at[pl.ds(off, SC_ROWS)], bufs[b][0], bufs[b][1])

        load(0, 0).start()

        @pl.loop(0, chunks, step=2)
        def _(c):
            for b in range(2):
                cc = c + b
                load(cc, b).wait()

                @pl.when(cc + 1 < chunks)
                def _():
                    load(cc + 1, 1 - b).start()

                copies = [pltpu.make_async_copy(bufs[b][0], o_hbm.at[idx_v.at[cc * slots + k]], sem_out)
                          for k in range(slots)]
                for cp in copies:
                    cp.start()
                for cp in copies:
                    cp.wait()

    return pl.kernel(
        body, out_type=jax.ShapeDtypeStruct((n_out, d), x.dtype), mesh=_sc_mesh(),
        scratch_types=[pltpu.VMEM((chunks * slots, SC_ROWS), I32),
                       pltpu.VMEM((SC_ROWS, d), x.dtype), pltpu.VMEM((SC_ROWS, d), x.dtype),
                       pltpu.SemaphoreType.DMA, pltpu.SemaphoreType.DMA, pltpu.SemaphoreType.DMA],
        name="moe_dispatch_sc")(x, dest_chunks)


def _sc_gather_rows(table, idx):
    d = table.shape[1]
    r = idx.shape[0]
    per_w = r // SC_WORKERS
    chunks = per_w // SC_ROWS
    assert chunks * SC_ROWS * SC_WORKERS == r and chunks % 2 == 0, (r, chunks)

    def body(t_hbm, i_hbm, o_hbm, idx_v, rows0, rows1, sem_g0, sem_g1, sem_w0, sem_w1):
        base = _sc_worker_base(per_w)
        bufs = ((rows0, sem_g0, sem_w0), (rows1, sem_g1, sem_w1))
        pltpu.sync_copy(i_hbm.at[pl.ds(pl.multiple_of(base, 8), per_w)], idx_v)

        def gather(c, b):
            ids = idx_v.at[pl.ds(pl.multiple_of(c * SC_ROWS, 8), SC_ROWS)]
            return pltpu.make_async_copy(t_hbm.at[ids], bufs[b][0], bufs[b][1])

        def write(c, b):
            off = pl.multiple_of(base + c * SC_ROWS, 8)
            return pltpu.make_async_copy(bufs[b][0], o_hbm.at[pl.ds(off, SC_ROWS)], bufs[b][2])

        gather(0, 0).start()

        @pl.loop(0, chunks, step=2)
        def _(c):
            for b in range(2):
                cc = c + b

                @pl.when(cc >= 1)
                def _():
                    write(cc - 1, 1 - b).wait()

                @pl.when(cc + 1 < chunks)
                def _():
                    gather(cc + 1, 1 - b).start()

                gather(cc, b).wait()
                write(cc, b).start()

        write(chunks - 1, (chunks - 1) % 2).wait()

    return pl.kernel(
        body, out_type=jax.ShapeDtypeStruct((r, d), table.dtype), mesh=_sc_mesh(),
        scratch_types=[pltpu.VMEM((per_w,), I32),
                       pltpu.VMEM((SC_ROWS, d), table.dtype), pltpu.VMEM((SC_ROWS, d), table.dtype),
                       pltpu.SemaphoreType.DMA, pltpu.SemaphoreType.DMA,
                       pltpu.SemaphoreType.DMA, pltpu.SemaphoreType.DMA],
        name="moe_combine_sc")(table, idx)


def _experts_kernel(be_ref, nv_ref, xs_ref, wg_ref, wu_ref, wd_ref, y_ref, wg_b, wu_b, wd_b):
    i = pl.program_id(0)
    prev = be_ref[jnp.maximum(i - 1, 0)]

    @pl.when(jnp.logical_or(i == 0, be_ref[i] != prev))
    def _():
        wg_b[...] = wg_ref[0, 0].astype(BF16)
        wu_b[...] = wu_ref[0, 0].astype(BF16)
        wd_b[...] = wd_ref[0, 0].astype(BF16)

    @pl.when(i < nv_ref[0])
    def _():
        x_lo, x_hi = _unpack_halves(xs_ref[...])
        x_lo, x_hi = x_lo.astype(BF16), x_hi.astype(BF16)
        half = x_lo.shape[1]

        def proj(w):
            return (jnp.dot(x_lo, w[:half, :], preferred_element_type=F32)
                    + jnp.dot(x_hi, w[half:, :], preferred_element_type=F32))

        h = _silu(proj(wg_b)) * proj(wu_b)
        y_ref[...] = _pack_halves(jnp.dot(h.astype(BF16), wd_b[...], preferred_element_type=F32))

    @pl.when(i >= nv_ref[0])
    def _():
        y_ref[...] = jnp.zeros_like(y_ref)


def _experts(blk_exp, n_valid, xs, wg, wu, wd, layer, tb):
    rows, dp = xs.shape
    d = 2 * dp
    grid_spec = pltpu.PrefetchScalarGridSpec(
        num_scalar_prefetch=2,
        grid=(rows // tb,),
        in_specs=[pl.BlockSpec((tb, dp), lambda i, be, nv: (i, 0)),
                  pl.BlockSpec((1, 1, d, D_EXPERT), lambda i, be, nv: (layer, be[i], 0, 0)),
                  pl.BlockSpec((1, 1, d, D_EXPERT), lambda i, be, nv: (layer, be[i], 0, 0)),
                  pl.BlockSpec((1, 1, D_EXPERT, d), lambda i, be, nv: (layer, be[i], 0, 0))],
        out_specs=pl.BlockSpec((tb, dp), lambda i, be, nv: (i, 0)),
        scratch_shapes=[pltpu.VMEM((d, D_EXPERT), BF16), pltpu.VMEM((d, D_EXPERT), BF16),
                        pltpu.VMEM((D_EXPERT, d), BF16)],
    )
    return pl.pallas_call(
        _experts_kernel,
        grid_spec=grid_spec,
        out_shape=jax.ShapeDtypeStruct((rows, dp), U32),
        compiler_params=_params("arbitrary"),
        name="moe_experts",
    )(blk_exp, n_valid, xs, wg, wu, wd)


def _combine_ln_kernel(x_ref, yg_ref, gt_ref, wgu_ref, wd_ref, g_ref, b_ref, o_ref, *, alpha):
    x = x_ref[...]
    gu = jnp.dot(x.astype(BF16), wgu_ref[...], preferred_element_type=F32)
    dsh = gu.shape[1] // 2
    h = _silu(gu[:, :dsh]) * gu[:, dsh:]
    f = jnp.dot(h.astype(BF16), wd_ref[...], preferred_element_type=F32)
    r_lo = r_hi = None
    for k in range(MOE_TOP_K):
        y_lo, y_hi = _unpack_halves(yg_ref[k])
        gate = gt_ref[:, k:k + 1]
        r_lo = gate * y_lo if r_lo is None else r_lo + gate * y_lo
        r_hi = gate * y_hi if r_hi is None else r_hi + gate * y_hi
    f = f + jnp.concatenate([r_lo, r_hi], axis=1)
    o_ref[...] = _layer_norm_rows(alpha * x + f, g_ref[...], b_ref[...])


def _combine_ln(x2, yg, gates_t, wgu, wd, g, b, alpha, tm):
    n, d = x2.shape
    kern = functools.partial(_combine_ln_kernel, alpha=alpha)
    return pl.pallas_call(
        kern,
        grid=(n // tm,),
        in_specs=[pl.BlockSpec((tm, d), lambda i: (i, 0)),
                  pl.BlockSpec((MOE_TOP_K, tm, d // 2), lambda i: (0, i, 0)),
                  pl.BlockSpec((tm, 8), lambda i: (i, 0)),
                  _full(wgu.shape), _full(wd.shape), _full((1, d)), _full((1, d))],
        out_specs=pl.BlockSpec((tm, d), lambda i: (i, 0)),
        out_shape=jax.ShapeDtypeStruct((n, d), F32),
        compiler_params=_params("parallel"),
        name="moe_combine_ln",
    )(x2, yg, gates_t, wgu, wd, g, b)


def _rope_tables(positions, dim, lane_offset, period):
    half = dim // 2
    inv_freq = ROPE_THETA ** (-2.0 * jnp.arange(half, dtype=jnp.float32) / dim)
    ang = positions.astype(jnp.float32)[:, None] * inv_freq[None, :]
    cos_h, sin_h = jnp.cos(ang), jnp.sin(ang)
    lanes = np.arange(LANES)
    rel = (lanes - lane_offset) % period
    active = (lanes >= lane_offset) & (rel < dim)
    fidx = np.where(active, rel % half, 0)
    sign = np.where(rel < half, -1.0, 1.0)
    cos = jnp.where(active[None, :], cos_h[:, fidx], 1.0)
    sin = jnp.where(active[None, :], sin_h[:, fidx] * sign[None, :], 0.0)
    return cos.astype(F32), sin.astype(F32)


def _gate_slabs(w):
    k = w.shape[0]
    g = w.reshape(k, 3, NSA_KV_HEADS, GQA_REP).transpose(0, 2, 1, 3).reshape(k, NSA_KV_HEADS, 3 * GQA_REP)
    return jnp.pad(g, ((0, 0), (0, 0), (0, LANES - 3 * GQA_REP))).reshape(k, NSA_KV_HEADS * LANES)


def _dup_heads(w):
    a, b = w[:, :HEAD_DIM], w[:, HEAD_DIM:]
    return jnp.concatenate([a, a, b, b], axis=1)


def _pad_heads(w):
    a, b = w[:, :HEAD_DIM], w[:, HEAD_DIM:]
    z = jnp.zeros_like(a)
    return jnp.concatenate([a, z, b, z], axis=1)


def _moe_layer(x1_and_packed, rw, rb, wg, wu, wd, layer, shg, shu, shd, ln_g, ln_b, alpha, tm):
    x1, x1_pk = x1_and_packed
    n, d = x1.shape
    n_chunk = 1
    tb = MOE_ROW_BLOCK
    rw_p = jnp.pad(rw, ((0, 0), (0, LANES - N_EXPERTS)))
    rw_top = lax.bitcast_convert_type(lax.bitcast_convert_type(rw_p, U32) & jnp.uint32(0xFFFF0000), F32)
    rw_split = jnp.stack([rw_top.astype(BF16), (rw_p - rw_top).astype(BF16)])
    eidx, gate, pos, cnt = _router(x1, rw_split, rb.reshape(N_EXPERTS, 1), n_chunk, tm)
    counts = cnt[:, :, 0].reshape(-1)
    padded = (counts + tb - 1) // tb * tb
    pad_ends = jnp.cumsum(padded)
    pad_starts = (pad_ends - padded).astype(I32)
    dest2 = _dest_rows(pad_starts, eidx, pos, n_chunk, tm)[:MOE_TOP_K]
    dest = dest2.reshape(-1)
    dest_chunks = dest2.reshape(MOE_TOP_K, n // SC_ROWS, SC_ROWS).transpose(1, 0, 2).reshape(-1, SC_ROWS)
    n_rows = n * MOE_TOP_K
    n_blk = -(-(n_rows + n_chunk * N_EXPERTS * (tb - 1)) // tb)
    blk_first_row = jnp.arange(n_blk, dtype=I32) * tb
    owner = jnp.sum((pad_ends[None, :] <= blk_first_row[:, None]).astype(I32), axis=1)
    blk_exp = jnp.minimum(owner, n_chunk * N_EXPERTS - 1).astype(I32) % N_EXPERTS
    n_valid = (pad_ends[-1] // tb).astype(I32).reshape(1)

    xs = _sc_scatter_rows(x1_pk, dest_chunks, n_blk * tb)
    y = _experts(blk_exp, n_valid, xs, wg, wu, wd, layer, tb)
    yg = _sc_gather_rows(y, dest).reshape(MOE_TOP_K, n, d // 2)
    gates_t = gate.T
    sh_gu = jnp.concatenate([shg, shu], axis=-1).astype(BF16)
    return _combine_ln(x1, yg, gates_t, sh_gu, shd.astype(BF16), ln_g.reshape(1, d), ln_b.reshape(1, d), alpha, tm)


def _swa_nsa_layer(x2, batch, seq, w_in, sinks, pe_k, w_ck1, w_ck2, pe_v, w_cv1, w_cv2, w_out,
                   ln_g, ln_b, alpha, tm):
    n, d = x2.shape
    splits = np.cumsum([512, 128, 128, 512, 128, 128, 128, 128, 128, 128, 24])[:-1]
    qa, ka, va, qb, kc, vc, ksl, vsl, kw, vw, gts = jnp.split(w_in, [int(c) for c in splits], axis=1)
    w_all = jnp.concatenate(
        [qa, _dup_heads(ka), _pad_heads(va), qb, kc, vc, _dup_heads(ksl), _pad_heads(vsl),
         _dup_heads(kw), _pad_heads(vw), _gate_slabs(gts)], axis=1).astype(BF16)
    cos, sin = _rope_tables(jnp.arange(seq), HEAD_DIM, 0, HEAD_DIM)
    (q_a, k_a, v_a, q_b, k_c, v_c, k_s, v_s, k_w, v_w, gates) = _mix0_proj(x2, w_all, cos, sin, seq, tm)

    o_swa = _banded_attention(q_a, k_a, v_a, batch, seq, SWA_WINDOW, 128, sinks=sinks, name="swa")
    o_win = _banded_attention(q_b, k_w, v_w, batch, seq, NSA_WINDOW, 128, gates=gates,
                              gate_col=2 * GQA_REP, out_dtype=F32, name="nsa_win")

    n_chunk = seq // NSA_CMP_STRIDE
    cw = NSA_CMP_STRIDE * NSA_KV_HEADS * HEAD_DIM
    hid = NSA_CMP_HIDDEN

    def expand_w1(w1):
        halves = w1.reshape(2, NSA_CMP_STRIDE, HEAD_DIM, hid)
        out = jnp.zeros((NSA_KV_HEADS, NSA_CMP_STRIDE, NSA_KV_HEADS, HEAD_DIM, 2 * hid), w1.dtype)
        for g in range(NSA_KV_HEADS):
            out = out.at[g, :, g, :, :hid].set(halves[0]).at[g, :, g, :, hid:].set(halves[1])
        return out.reshape(NSA_KV_HEADS, cw, 2 * hid).astype(BF16)

    def pe_rows(pe):
        return jnp.pad(pe.reshape(1, NSA_CMP_LEN * HEAD_DIM), ((0, 15), (0, 0))).astype(BF16)

    cos_c, sin_c = _rope_tables(jnp.arange(n_chunk) * NSA_CMP_STRIDE + NSA_CMP_LEN - 1, HEAD_DIM, 0, HEAD_DIM)
    k_cmp, v_cmp = _compress(
        k_c.reshape(batch * n_chunk, cw), v_c.reshape(batch * n_chunk, cw),
        expand_w1(w_ck1), expand_w1(w_cv1), pe_rows(pe_k), pe_rows(pe_v),
        w_ck1.astype(BF16), w_cv1.astype(BF16),
        jnp.concatenate([w_ck2, w_ck2], axis=1).astype(BF16),
        jnp.concatenate([w_cv2, w_cv2], axis=1).astype(BF16),
        cos_c, sin_c, batch, n_chunk)

    n_sb = seq // NSA_SEL_LEN
    n_sel = min(NSA_N_SEL, n_sb)
    cs = np.arange(n_chunk) * NSA_CMP_STRIDE
    bs = np.arange(n_sb) * NSA_SEL_LEN
    ov = np.clip(np.minimum(cs[:, None] + NSA_CMP_LEN, bs[None, :] + NSA_SEL_LEN)
                 - np.maximum(cs[:, None], bs[None, :]), 0, None) / NSA_CMP_LEN
    overlap = jnp.asarray(np.pad(ov, ((0, 0), (0, LANES - n_sb))).T, F32)
    expand = jnp.asarray(np.arange(LANES)[:, None] == (np.arange(seq)[None, :] // NSA_SEL_LEN), BF16)
    tk_sel = min(512, seq)
    o_nsa = _nsa_cmp_sel(q_b, k_cmp, v_cmp, k_s, v_s, gates, o_win, overlap, expand,
                         batch, seq, 128, n_sb, n_sel, tk_sel)
    w_o = w_out.astype(BF16)
    half = SWA_Q_HEADS * HEAD_DIM
    return _oproj_ln([o_swa, o_nsa], [w_o[:half], w_o[half:]], x2, ln_g.reshape(1, d), ln_b.reshape(1, d),
                     alpha, tm)


def _mla_layer(x2, batch, seq, w_in, q_norm, kv_norm, w_uq, w_ukv, w_out, ln_g, ln_b, alpha, tm):
    n, d = x2.shape
    dq = MLA_NOPE + MLA_ROPE
    w_kr = jnp.zeros((d, LANES), w_in.dtype).at[:, MLA_NOPE:dq].set(w_in[:, MLA_Q_LORA + MLA_KV_LORA:])
    w_in_p = jnp.concatenate([w_in[:, :MLA_Q_LORA + MLA_KV_LORA], w_kr], axis=1).astype(BF16)
    wuq = jnp.pad(w_uq.reshape(MLA_Q_LORA, MLA_HEADS, dq), ((0, 0), (0, 0), (0, LANES - dq)))
    wuq = wuq.reshape(MLA_Q_LORA, MLA_HEADS * LANES).astype(BF16)
    wukv = w_ukv.reshape(MLA_KV_LORA, MLA_HEADS, MLA_NOPE + MLA_V)
    wuk = jnp.pad(wukv[:, :, :MLA_NOPE], ((0, 0), (0, 0), (0, LANES - MLA_NOPE)))
    wuk = wuk.reshape(MLA_KV_LORA, MLA_HEADS * LANES).astype(BF16)
    wuv = jnp.pad(wukv[:, :, MLA_NOPE:], ((0, 0), (0, 0), (0, LANES - MLA_V)))
    wuv = wuv.reshape(MLA_KV_LORA, MLA_HEADS * LANES).astype(BF16)
    cos, sin = _rope_tables(jnp.arange(seq), MLA_ROPE, MLA_NOPE, LANES)
    q, k, v = _mla_proj(x2, w_in_p, q_norm.reshape(1, -1), kv_norm.reshape(1, -1), wuq, wuk, wuv, cos, sin, seq, tm)
    o = _mla_attention(q, k, v, batch, seq, min(512, seq))
    return _oproj_ln([o], [w_out.astype(BF16)], x2, ln_g.reshape(1, d), ln_b.reshape(1, d), alpha, tm)


def kernel(x, swa_nsa_w_in, swa_sinks, nsa_cmp_pe_k, nsa_cmp_k_w1, nsa_cmp_k_w2, nsa_cmp_pe_v, nsa_cmp_v_w1, nsa_cmp_v_w2, swa_nsa_w_out, mla_w_in, mla_q_norm, mla_kv_norm, mla_w_uq, mla_w_ukv, mla_w_out, ln_mix_g, ln_mix_b, ln_ffn_g, ln_ffn_b, router_w, router_bias, expert_w_gate, expert_w_up, expert_w_down, shared_w_gate, shared_w_up, shared_w_down):
    batch, seq, d = x.shape
    depth = ln_mix_g.shape[0]
    alpha = (2 * depth) ** 0.25
    tm = min(512, seq)
    x2 = x.reshape(batch * seq, d)
    for layer in range(depth):
        j = layer // 2
        if layer % 2 == 0:
            x2 = _swa_nsa_layer(x2, batch, seq, swa_nsa_w_in[j], swa_sinks[j], nsa_cmp_pe_k[j],
                                nsa_cmp_k_w1[j], nsa_cmp_k_w2[j], nsa_cmp_pe_v[j], nsa_cmp_v_w1[j],
                                nsa_cmp_v_w2[j], swa_nsa_w_out[j], ln_mix_g[layer], ln_mix_b[layer], alpha, tm)
        else:
            x2 = _mla_layer(x2, batch, seq, mla_w_in[j], mla_q_norm[j], mla_kv_norm[j], mla_w_uq[j],
                            mla_w_ukv[j], mla_w_out[j], ln_mix_g[layer], ln_mix_b[layer], alpha, tm)
        x2 = _moe_layer(x2, router_w[layer], router_bias[layer], expert_w_gate, expert_w_up, expert_w_down, layer,
                        shared_w_gate[layer], shared_w_up[layer], shared_w_down[layer],
                        ln_ffn_g[layer], ln_ffn_b[layer], alpha, tm)
    return x2.reshape(batch, seq, d)
```

```python
import functools
import math

import numpy as np
import jax
import jax.numpy as jnp
from jax import lax
from jax.experimental import pallas as pl
from jax.experimental.pallas import tpu as pltpu
from jax.experimental.pallas import tpu_sc as plsc

F32 = jnp.float32
BF16 = jnp.bfloat16
I32 = jnp.int32

LANES = 128
VMEM_LIMIT = 48 * 1024 * 1024

ROPE_THETA = 10000.0
LN_EPS = 1e-5
RMS_EPS = 1e-6
NEG_BIG = -1e30
FORCE_SCORE = 1e4

HEAD_DIM = 64
SWA_Q_HEADS = 8
SWA_KV_HEADS = 2
SWA_WINDOW = 128
NSA_Q_HEADS = 8
NSA_KV_HEADS = 2
NSA_CMP_LEN = 32
NSA_CMP_STRIDE = 16
NSA_CMP_HIDDEN = 128
NSA_SEL_LEN = 64
NSA_N_SEL = 16
NSA_WINDOW = 512
GQA_REP = 4

MLA_HEADS = 16
MLA_NOPE = 64
MLA_ROPE = 32
MLA_V = 64
MLA_Q_LORA = 384
MLA_KV_LORA = 256

N_EXPERTS = 64
MOE_GROUPS = 8
MOE_TOPK_GROUPS = 4
MOE_TOP_K = 6
D_EXPERT = 256
MOE_ROUTED_SCALE = 2.5
MOE_ROW_BLOCK = 512

NT_DIMS = (((1,), (1,)), ((), ()))


def _params(*sem):
    return pltpu.CompilerParams(dimension_semantics=sem, vmem_limit_bytes=VMEM_LIMIT)


def _full(shape):
    nd = len(shape)
    return pl.BlockSpec(shape, lambda *_: (0,) * nd)


def _rope_slab(y, cos, sin, half):
    lane = lax.broadcasted_iota(I32, y.shape, 1)
    first = (lane % (2 * half)) < half
    rot = jnp.where(first, pltpu.roll(y, LANES - half, 1), pltpu.roll(y, half, 1))
    return y * cos + rot * sin


def _layer_norm_rows(v, g, b):
    mu = jnp.mean(v, axis=-1, keepdims=True)
    vc = v - mu
    var = jnp.mean(vc * vc, axis=-1, keepdims=True)
    return vc * lax.rsqrt(var + LN_EPS) * g + b


def _silu(v):
    return v * (1.0 / (1.0 + jnp.exp(-v)))


def _sigmoid(v):
    return 1.0 / (1.0 + jnp.exp(-v))


U32 = jnp.uint32


def _pack_halves(v):
    w = v.shape[1] // 2
    lo = pltpu.bitcast(v[:, :w].astype(BF16).astype(F32), U32)
    hi = pltpu.bitcast(v[:, w:].astype(BF16).astype(F32), U32)
    return (lo >> 16) | (hi & jnp.uint32(0xFFFF0000))


def _unpack_halves(u):
    lo = pltpu.bitcast(u << 16, F32)
    hi = pltpu.bitcast(u & jnp.uint32(0xFFFF0000), F32)
    return lo, hi


def _ones_half(shape):
    lane = lax.broadcasted_iota(I32, shape, 1)
    return jnp.where(lane < HEAD_DIM, 0.0, 1.0).astype(F32)


def _stack_group_queries(q_ref, group, rep):
    lane = lax.broadcasted_iota(I32, (q_ref.shape[0], LANES), 1)
    keep_lo = jnp.where(lane < HEAD_DIM, 1.0, 0.0).astype(q_ref.dtype)
    keep_hi = jnp.where(lane < HEAD_DIM, 0.0, 1.0).astype(q_ref.dtype)
    parts = []
    for r in range(rep):
        h = group * rep + r
        slab = q_ref[:, (h // 2) * LANES:(h // 2 + 1) * LANES]
        parts.append(slab * (keep_lo if h % 2 == 0 else keep_hi))
    return jnp.concatenate(parts, axis=0)


def _rank_desc(score, n_valid, axis):
    idx = lax.broadcasted_iota(I32, score.shape, axis)
    rank = jnp.zeros(score.shape, I32)
    for j in range(n_valid):
        other = lax.slice_in_dim(score, j, j + 1, axis=axis)
        ahead = jnp.where(other > score, 1, jnp.where(other == score, jnp.where(idx > j, 1, 0), 0))
        rank = rank + ahead
    return rank


_MIX0_OUTS = (
    ("qa", 512, "rope_q", BF16), ("ka", 256, "rope", BF16), ("va", 256, "v_ones", BF16),
    ("qb", 512, "rope_q", BF16), ("kc", 128, "plain", BF16), ("vc", 128, "plain", BF16),
    ("ks", 256, "rope", BF16), ("vs", 256, "v_ones", BF16),
    ("kw", 256, "rope", BF16), ("vw", 256, "v_ones", BF16),
    ("gates", 256, "sigmoid", F32),
)


def _mix0_proj_kernel(x_ref, w_ref, cos_ref, sin_ref, *out_refs):
    xb = x_ref[...].astype(BF16)
    cos = cos_ref[...]
    sin = sin_ref[...]
    off = 0
    for (_, width, kind, _), o_ref in zip(_MIX0_OUTS, out_refs):
        y = jnp.dot(xb, w_ref[:, off:off + width], preferred_element_type=F32)
        for c in range(width // LANES):
            yc = y[:, c * LANES:(c + 1) * LANES]
            if kind in ("rope", "rope_q"):
                yc = _rope_slab(yc, cos, sin, HEAD_DIM // 2)
            if kind == "rope_q":
                yc = yc * (HEAD_DIM ** -0.5 * LOG2_E)
            if kind == "v_ones":
                yc = yc + _ones_half(yc.shape)
            if kind == "sigmoid":
                yc = _sigmoid(yc)
            o_ref[:, c * LANES:(c + 1) * LANES] = yc.astype(o_ref.dtype)
        off += width


def _mix0_proj(x2, w_all, cos, sin, seq, tm):
    n = x2.shape[0]
    d = x2.shape[1]
    wtot = w_all.shape[1]
    sblk = seq // tm
    return pl.pallas_call(
        _mix0_proj_kernel,
        grid=(n // tm,),
        in_specs=[pl.BlockSpec((tm, d), lambda i: (i, 0)),
                  _full((d, wtot)),
                  pl.BlockSpec((tm, LANES), lambda i: (i % sblk, 0)),
                  pl.BlockSpec((tm, LANES), lambda i: (i % sblk, 0))],
        out_specs=[pl.BlockSpec((tm, w), lambda i: (i, 0)) for _, w, _, _ in _MIX0_OUTS],
        out_shape=[jax.ShapeDtypeStruct((n, w), dt) for _, w, _, dt in _MIX0_OUTS],
        compiler_params=_params("parallel"),
        name="mix0_proj",
    )(x2, w_all, cos, sin)


def _banded_kernel(*refs, window, blk, seq, has_sink, gate_col):
    if has_sink:
        sink_ref, refs = refs[0], refs[1:]
    q_ref, k_ref, v_ref = refs[:3]
    g_ref = refs[3] if gate_col is not None else None
    o_ref, bias_scr, s_scr, p_scr, m_scr, a_scr, acc_scr = refs[-7:]
    i = pl.program_id(1)
    start = i * blk
    span = min(window + blk, seq)
    k0 = pl.multiple_of(jnp.maximum(start - window, 0), LANES)
    tq = start + lax.broadcasted_iota(I32, (blk, span), 0)
    tk = k0 + lax.broadcasted_iota(I32, (blk, span), 1)
    diff = tq - tk
    bias_scr[...] = jnp.where(diff >= 0, jnp.where(diff < window, 0.0, -jnp.inf), -jnp.inf).astype(F32)
    chunks_per_head = blk // FLASH_ROWS

    def bias_fn(c):
        r0 = (c % chunks_per_head) * FLASH_ROWS
        return bias_scr[r0:r0 + FLASH_ROWS, :]

    lane = lax.broadcasted_iota(I32, (blk, LANES), 1)
    lo = lane < HEAD_DIM
    n_groups = q_ref.shape[1] // (GQA_REP * HEAD_DIM)
    for g in range(n_groups):
        _flash_init(m_scr.at[g], acc_scr.at[g])
        _flash_scores(_stack_group_queries(q_ref, g, GQA_REP), k_ref[pl.ds(k0, span), g * LANES:(g + 1) * LANES],
                      s_scr.at[g])
    for g in range(n_groups):
        _flash_max(bias_fn, s_scr.at[g], m_scr.at[g], a_scr.at[g])
    for g in range(n_groups):
        _flash_weights(v_ref[pl.ds(k0, span), g * LANES:(g + 1) * LANES], s_scr.at[g], p_scr.at[g],
                       m_scr.at[g], a_scr.at[g], acc_scr.at[g])
    for g in range(n_groups):
        acc = acc_scr[g]
        if has_sink:
            m = m_scr[g]
            sink = jnp.concatenate(
                [jnp.full((blk, LANES), sink_ref[g * GQA_REP + r] * LOG2_E, F32) for r in range(GQA_REP)], axis=0)
            m_all = jnp.maximum(m, sink)
            acc = acc * jnp.exp2(m - m_all) + _ones_half(acc.shape) * jnp.exp2(sink - m_all)
        o = _flash_finish(acc)
        for j in range(GQA_REP // 2):
            slab = _pair_slab(o[(2 * j) * blk:(2 * j + 1) * blk], o[(2 * j + 1) * blk:(2 * j + 2) * blk])
            if gate_col is not None:
                c = g * LANES + gate_col + 2 * j
                gate = jnp.where(lo, g_ref[:, c:c + 1], g_ref[:, c + 1:c + 2])
                slab = slab * gate
            col = (g * (GQA_REP // 2) + j) * LANES
            o_ref[:, col:col + LANES] = slab.astype(o_ref.dtype)


def _banded_attention(q, k, v, batch, seq, window, blk, sinks=None, gates=None, gate_col=None,
                      out_dtype=None, name="banded"):
    n, qw = q.shape
    kw = k.shape[1]
    nblk = seq // blk
    span = min(window + blk, seq)
    groups = qw // (GQA_REP * HEAD_DIM)
    rows = GQA_REP * blk
    kern = functools.partial(_banded_kernel, window=window, blk=blk, seq=seq,
                             has_sink=sinks is not None, gate_col=gate_col)
    in_specs = []
    args = []
    if sinks is not None:
        in_specs.append(pl.BlockSpec(memory_space=pltpu.SMEM))
        args.append(sinks)
    in_specs += [pl.BlockSpec((blk, qw), lambda b, i: (b * nblk + i, 0)),
                 pl.BlockSpec((seq, kw), lambda b, i: (b, 0)),
                 pl.BlockSpec((seq, kw), lambda b, i: (b, 0))]
    args += [q, k, v]
    if gates is not None:
        in_specs.append(pl.BlockSpec((blk, gates.shape[1]), lambda b, i: (b * nblk + i, 0)))
        args.append(gates)
    return pl.pallas_call(
        kern,
        grid=(batch, nblk),
        in_specs=in_specs,
        out_specs=pl.BlockSpec((blk, qw), lambda b, i: (b * nblk + i, 0)),
        out_shape=jax.ShapeDtypeStruct((n, qw), BF16 if out_dtype is None else out_dtype),
        scratch_shapes=[pltpu.VMEM((blk, span), F32),
                        pltpu.VMEM((groups, rows, span), F32), pltpu.VMEM((groups, rows, span), BF16)]
                       + [pltpu.VMEM((groups, rows, LANES), F32)] * 3,
        compiler_params=_params("parallel", "arbitrary"),
        name=name,
    )(*args)


def _gelu_tanh(v):
    return 0.5 * v * (1.0 + jnp.tanh(math.sqrt(2.0 / math.pi) * (v + 0.044715 * (v * v * v))))


def _compress_kernel(tk_ref, tv_ref, w1k_ref, w1v_ref, pek_ref, pev_ref, w1ko_ref, w1vo_ref,
                     w2k_ref, w2v_ref, cos_ref, sin_ref, kc_ref, vc_ref):
    n_chunk = tk_ref.shape[0]
    hid = NSA_CMP_HIDDEN
    for t_ref, w1_ref, pe_ref, w1o_ref, w2_ref, o_ref, rope in (
            (tk_ref, w1k_ref, pek_ref, w1ko_ref, w2k_ref, kc_ref, True),
            (tv_ref, w1v_ref, pev_ref, w1vo_ref, w2v_ref, vc_ref, False)):
        pe_term = jnp.dot(pe_ref[...], w1o_ref[...], preferred_element_type=F32)[0:1, :]
        t = t_ref[...]
        for g in range(NSA_KV_HEADS):
            uv = jnp.dot(t, w1_ref[g], preferred_element_type=F32)
            nxt = pltpu.roll(uv[:, hid:], n_chunk - 1, 0)
            h = _gelu_tanh(uv[:, :hid] + nxt + pe_term)
            c = jnp.dot(h.astype(BF16), w2_ref[...], preferred_element_type=F32)
            if rope:
                c = _rope_slab(c, cos_ref[...], sin_ref[...], HEAD_DIM // 2)
            o_ref[0, g] = c.astype(o_ref.dtype)


def _compress(tk, tv, w1k, w1v, pek, pev, w1ko, w1vo, w2k, w2v, cos, sin, batch, n_chunk):
    width = tk.shape[1]
    out = jax.ShapeDtypeStruct((batch, NSA_KV_HEADS, n_chunk, LANES), BF16)
    ospec = pl.BlockSpec((1, NSA_KV_HEADS, n_chunk, LANES), lambda b: (b, 0, 0, 0))
    return pl.pallas_call(
        _compress_kernel,
        grid=(batch,),
        in_specs=[pl.BlockSpec((n_chunk, width), lambda b: (b, 0)),
                  pl.BlockSpec((n_chunk, width), lambda b: (b, 0)),
                  _full(w1k.shape), _full(w1v.shape), _full(pek.shape), _full(pev.shape),
                  _full(w1ko.shape), _full(w1vo.shape), _full(w2k.shape), _full(w2v.shape),
                  _full(cos.shape), _full(sin.shape)],
        out_specs=[ospec, ospec],
        out_shape=[out, out],
        compiler_params=_params("parallel"),
        name="nsa_compress",
    )(tk, tv, w1k, w1v, pek, pev, w1ko, w1vo, w2k, w2v, cos, sin)


def _nsa_kernel(q_ref, kc_ref, vc_ref, ks_ref, vs_ref, g_ref, win_ref, ov_ref, e_ref, o_ref,
                bias_scr, s_scr, p_scr, m_scr, a_scr, acc_scr, *, qblk, seq, n_sb, n_sel, tk_sel):
    i = pl.program_id(2)
    start = i * qblk
    n_cmp = kc_ref.shape[2]
    lane = lax.broadcasted_iota(I32, (qblk, LANES), 1)
    lo = lane < HEAD_DIM
    q = _stack_group_queries(q_ref, 0, GQA_REP)
    tq = start + lax.broadcasted_iota(I32, (qblk, 1), 0)

    kc = kc_ref[0, 0]
    vc = vc_ref[0, 0]
    cmp_end = lax.broadcasted_iota(I32, (qblk, n_cmp), 1) * NSA_CMP_STRIDE + (NSA_CMP_LEN - 1)
    vis = jnp.where(cmp_end <= tq, 1.0, 0.0).astype(F32)
    vis4 = jnp.concatenate([vis] * GQA_REP, axis=0)
    s_c = lax.dot_general(q, kc, NT_DIMS, preferred_element_type=F32)
    s_c = jnp.where(vis4 > 0.5, s_c, NEG_BIG)
    p_c = jnp.exp2(s_c - jnp.max(s_c, axis=-1, keepdims=True)) * vis4
    p_c = p_c / jnp.maximum(jnp.sum(p_c, axis=-1, keepdims=True), 1e-30)
    o_c = jnp.dot(p_c.astype(BF16), vc, preferred_element_type=F32)

    p_sum = p_c[0:qblk]
    for r in range(1, GQA_REP):
        p_sum = p_sum + p_c[r * qblk:(r + 1) * qblk]
    p_top = pltpu.bitcast(pltpu.bitcast(p_sum, U32) & jnp.uint32(0xFFFF0000), F32)
    imp_t = (lax.dot_general(ov_ref[...], p_top.astype(BF16), NT_DIMS, preferred_element_type=F32)
             + lax.dot_general(ov_ref[...], (p_sum - p_top).astype(BF16), NT_DIMS,
                               preferred_element_type=F32))[:n_sb]
    blk_t = lax.broadcasted_iota(I32, (n_sb, qblk), 0)
    tq_t = start + lax.broadcasted_iota(I32, (n_sb, qblk), 1)
    cur_t = tq_t // NSA_SEL_LEN
    forced = jnp.where(blk_t == 0, 1, jnp.where(blk_t == cur_t, 1, jnp.where(blk_t == cur_t - 1, 1, 0)))
    score_t = jnp.where(forced == 1, FORCE_SCORE, jnp.where(blk_t * NSA_SEL_LEN > tq_t, -1.0, imp_t))
    sel_t = jnp.where(_rank_desc(score_t, n_sb, 0) < n_sel, 1.0, 0.0).astype(F32)
    if n_sb < LANES:
        sel_t = jnp.concatenate([sel_t, jnp.zeros((LANES - n_sb, qblk), F32)], axis=0)
    sel = jnp.transpose(sel_t).astype(BF16)

    _flash_init(m_scr, acc_scr)
    chunks_per_head = qblk // FLASH_ROWS

    def body(j, carry):
        k0 = pl.multiple_of(j * tk_sel, tk_sel)
        picked = jnp.dot(sel, e_ref[:, pl.ds(k0, tk_sel)], preferred_element_type=F32)
        tk = k0 + lax.broadcasted_iota(I32, (qblk, tk_sel), 1)
        bias_scr[...] = jnp.where(picked > 0.5, jnp.where(tk <= tq, 0.0, -jnp.inf), -jnp.inf).astype(F32)

        def bias_fn(c):
            r0 = (c % chunks_per_head) * FLASH_ROWS
            return bias_scr[r0:r0 + FLASH_ROWS, :]

        _flash_scores(q, ks_ref[pl.ds(k0, tk_sel), :], s_scr)
        _flash_max(bias_fn, s_scr, m_scr, a_scr)
        _flash_weights(vs_ref[pl.ds(k0, tk_sel), :], s_scr, p_scr, m_scr, a_scr, acc_scr)
        return carry

    n_tiles = (start + qblk + tk_sel - 1) // tk_sel
    lax.fori_loop(0, n_tiles, body, 0)
    o_s = _flash_finish(acc_scr[...])

    for j in range(GQA_REP // 2):

        def gate(branch, pair=j):
            c = branch * GQA_REP + 2 * pair
            return jnp.where(lo, g_ref[:, c:c + 1], g_ref[:, c + 1:c + 2])

        rows_e = slice((2 * j) * qblk, (2 * j + 1) * qblk)
        rows_o = slice((2 * j + 1) * qblk, (2 * j + 2) * qblk)
        oc = jnp.where(lo, o_c[rows_e], o_c[rows_o])
        os_ = _pair_slab(o_s[rows_e], o_s[rows_o])
        out = gate(0) * oc + gate(1) * os_ + win_ref[:, j * LANES:(j + 1) * LANES]
        o_ref[:, j * LANES:(j + 1) * LANES] = out.astype(o_ref.dtype)


def _nsa_cmp_sel(qb, kcmp, vcmp, ks, vs, gates, win, overlap, expand, batch, seq, qblk, n_sb, n_sel, tk_sel):
    n = qb.shape[0]
    nq = seq // qblk
    n_cmp = kcmp.shape[2]
    gw = GQA_REP * HEAD_DIM
    rows = GQA_REP * qblk
    kern = functools.partial(_nsa_kernel, qblk=qblk, seq=seq, n_sb=n_sb, n_sel=n_sel, tk_sel=tk_sel)
    return pl.pallas_call(
        kern,
        grid=(batch, NSA_KV_HEADS, nq),
        in_specs=[pl.BlockSpec((qblk, gw), lambda b, g, i: (b * nq + i, g)),
                  pl.BlockSpec((1, 1, n_cmp, LANES), lambda b, g, i: (b, g, 0, 0)),
                  pl.BlockSpec((1, 1, n_cmp, LANES), lambda b, g, i: (b, g, 0, 0)),
                  pl.BlockSpec((seq, LANES), lambda b, g, i: (b, g)),
                  pl.BlockSpec((seq, LANES), lambda b, g, i: (b, g)),
                  pl.BlockSpec((qblk, LANES), lambda b, g, i: (b * nq + i, g)),
                  pl.BlockSpec((qblk, gw), lambda b, g, i: (b * nq + i, g)),
                  _full(overlap.shape), _full(expand.shape)],
        out_specs=pl.BlockSpec((qblk, gw), lambda b, g, i: (b * nq + i, g)),
        out_shape=jax.ShapeDtypeStruct((n, NSA_Q_HEADS * HEAD_DIM), BF16),
        scratch_shapes=[pltpu.VMEM((qblk, tk_sel), F32),
                        pltpu.VMEM((rows, tk_sel), F32), pltpu.VMEM((rows, tk_sel), BF16),
                        pltpu.VMEM((rows, LANES), F32), pltpu.VMEM((rows, LANES), F32),
                        pltpu.VMEM((rows, LANES), F32)],
        compiler_params=_params("parallel", "parallel", "arbitrary"),
        name="nsa_cmp_sel",
    )(qb, kcmp, vcmp, ks, vs, gates, win, overlap, expand)


def _oproj_ln_kernel(*refs, n_parts, alpha):
    o_refs = refs[:n_parts]
    w_refs = refs[n_parts:2 * n_parts]
    x_ref, g_ref, b_ref, y_ref, ypk_ref = refs[2 * n_parts:]
    h = jnp.dot(o_refs[0][...], w_refs[0][...], preferred_element_type=F32)
    for o_ref, w_ref in zip(o_refs[1:], w_refs[1:]):
        h = h + jnp.dot(o_ref[...], w_ref[...], preferred_element_type=F32)
    y = _layer_norm_rows(alpha * x_ref[...] + h, g_ref[...], b_ref[...])
    y_ref[...] = y
    ypk_ref[...] = _pack_halves(y)


def _oproj_ln(o_parts, w_parts, x2, g, b, alpha, tm):
    n, d = x2.shape
    kern = functools.partial(_oproj_ln_kernel, n_parts=len(o_parts), alpha=alpha)
    in_specs = ([pl.BlockSpec((tm, o.shape[1]), lambda i: (i, 0)) for o in o_parts]
                + [_full(w.shape) for w in w_parts]
                + [pl.BlockSpec((tm, d), lambda i: (i, 0)), _full((1, d)), _full((1, d))])
    return pl.pallas_call(
        kern,
        grid=(n // tm,),
        in_specs=in_specs,
        out_specs=[pl.BlockSpec((tm, d), lambda i: (i, 0)), pl.BlockSpec((tm, d // 2), lambda i: (i, 0))],
        out_shape=[jax.ShapeDtypeStruct((n, d), F32), jax.ShapeDtypeStruct((n, d // 2), U32)],
        compiler_params=_params("parallel"),
        name="oproj_ln",
    )(*o_parts, *w_parts, x2, g, b)


def _rms_rows(v, g):
    return v * lax.rsqrt(jnp.mean(v * v, axis=-1, keepdims=True) + RMS_EPS) * g


def _mla_proj_kernel(x_ref, win_ref, qn_ref, kvn_ref, wuq_ref, wuk_ref, wuv_ref, cos_ref, sin_ref,
                     q_ref, k_ref, v_ref):
    xb = x_ref[...].astype(BF16)
    cos = cos_ref[...]
    sin = sin_ref[...]
    lat = jnp.dot(xb, win_ref[...], preferred_element_type=F32)
    cq = _rms_rows(lat[:, :MLA_Q_LORA], qn_ref[...]).astype(BF16)
    ckv = _rms_rows(lat[:, MLA_Q_LORA:MLA_Q_LORA + MLA_KV_LORA], kvn_ref[...]).astype(BF16)
    kr = _rope_slab(lat[:, MLA_Q_LORA + MLA_KV_LORA:], cos, sin, MLA_ROPE // 2)
    q = jnp.dot(cq, wuq_ref[...], preferred_element_type=F32)
    k = jnp.dot(ckv, wuk_ref[...], preferred_element_type=F32)
    v = jnp.dot(ckv, wuv_ref[...], preferred_element_type=F32)
    ones_half = _ones_half((x_ref.shape[0], LANES))
    q_scale = (MLA_NOPE + MLA_ROPE) ** -0.5 * LOG2_E
    for h in range(MLA_HEADS):
        sl = slice(h * LANES, (h + 1) * LANES)
        q_ref[:, sl] = (_rope_slab(q[:, sl], cos, sin, MLA_ROPE // 2) * q_scale).astype(BF16)
        k_ref[:, sl] = (k[:, sl] + kr).astype(BF16)
        v_ref[:, sl] = (v[:, sl] + ones_half).astype(BF16)


def _mla_proj(x2, w_in, qn, kvn, wuq, wuk, wuv, cos, sin, seq, tm):
    n, d = x2.shape
    sblk = seq // tm
    hw = MLA_HEADS * LANES
    vw = hw
    return pl.pallas_call(
        _mla_proj_kernel,
        grid=(n // tm,),
        in_specs=[pl.BlockSpec((tm, d), lambda i: (i, 0)),
                  _full(w_in.shape), _full(qn.shape), _full(kvn.shape),
                  _full(wuq.shape), _full(wuk.shape), _full(wuv.shape),
                  pl.BlockSpec((tm, LANES), lambda i: (i % sblk, 0)),
                  pl.BlockSpec((tm, LANES), lambda i: (i % sblk, 0))],
        out_specs=[pl.BlockSpec((tm, hw), lambda i: (i, 0)),
                   pl.BlockSpec((tm, hw), lambda i: (i, 0)),
                   pl.BlockSpec((tm, vw), lambda i: (i, 0))],
        out_shape=[jax.ShapeDtypeStruct((n, hw), BF16), jax.ShapeDtypeStruct((n, hw), BF16),
                   jax.ShapeDtypeStruct((n, vw), BF16)],
        compiler_params=_params("parallel"),
        name="mla_proj",
    )(x2, w_in, qn, kvn, wuq, wuk, wuv, cos, sin)


FLASH_ROWS = 32
LOG2_E = 1.4426950408889634


def _flash_init(m_scr, acc_scr):
    m_scr[...] = jnp.full(m_scr.shape, -jnp.inf, F32)
    acc_scr[...] = jnp.zeros(acc_scr.shape, F32)


def _flash_scores(q, k, s_scr):
    s_scr[...] = lax.dot_general(q, k, NT_DIMS, preferred_element_type=F32)


def _flash_max(bias_fn, s_scr, m_scr, a_scr):
    rows, keys = s_scr.shape
    n_chunks = rows // FLASH_ROWS
    slabs = keys // LANES
    for c in range(n_chunks):
        r = slice(c * FLASH_ROWS, (c + 1) * FLASH_ROWS)
        s = s_scr[r, :]
        if bias_fn is not None:
            s = s + bias_fn(c)
            s_scr[r, :] = s
        mx = s[:, 0:LANES]
        for j in range(1, slabs):
            mx = jnp.maximum(mx, s[:, j * LANES:(j + 1) * LANES])
        a_scr[r, :] = mx
    m_old = m_scr[...]
    m_new = jnp.maximum(m_old, jnp.broadcast_to(jnp.max(a_scr[...], axis=-1, keepdims=True), m_old.shape))
    m_scr[...] = m_new
    a_scr[...] = jnp.exp2(m_old - m_new)


def _flash_weights(v, s_scr, p_scr, m_scr, a_scr, acc_scr):
    rows, keys = s_scr.shape
    n_chunks = rows // FLASH_ROWS
    slabs = keys // LANES
    for c in range(n_chunks):
        r = slice(c * FLASH_ROWS, (c + 1) * FLASH_ROWS)
        m_rows = m_scr[r, :]
        for j in range(slabs):
            cols = slice(j * LANES, (j + 1) * LANES)
            p_scr[r, cols] = jnp.exp2(s_scr[r, cols] - m_rows).astype(p_scr.dtype)
    acc_scr[...] = a_scr[...] * acc_scr[...] + jnp.dot(p_scr[...], v, preferred_element_type=F32)


def _flash_finish(acc):
    return acc / acc[:, HEAD_DIM:HEAD_DIM + 1]


def _pair_slab(even, odd):
    lane = lax.broadcasted_iota(I32, even.shape, 1)
    return jnp.where(lane < HEAD_DIM, even, pltpu.roll(odd, HEAD_DIM, 1))


def _mla_attn_kernel(q_ref, k_ref, v_ref, o_ref, s_scr, p_scr, m_scr, a_scr, acc_scr, *, tq, tk):
    i = pl.program_id(2)
    for e in range(2):
        _flash_init(m_scr.at[e], acc_scr.at[e])

    def tile(k0, bias_fn):
        for e in range(2):
            _flash_scores(q_ref[:, e * LANES:(e + 1) * LANES], k_ref[pl.ds(k0, tk), e * LANES:(e + 1) * LANES],
                          s_scr.at[e])
        for e in range(2):
            _flash_max(bias_fn, s_scr.at[e], m_scr.at[e], a_scr.at[e])
        for e in range(2):
            _flash_weights(v_ref[pl.ds(k0, tk), e * LANES:(e + 1) * LANES], s_scr.at[e], p_scr.at[e],
                           m_scr.at[e], a_scr.at[e], acc_scr.at[e])

    def body(j, carry):
        tile(pl.multiple_of(j * tk, tk), None)
        return carry

    lax.fori_loop(0, i * (tq // tk), body, 0)
    for jd in range(tq // tk):

        def diag_bias(c, first_key=jd * tk):
            row = c * FLASH_ROWS + lax.broadcasted_iota(I32, (FLASH_ROWS, tk), 0)
            col = first_key + lax.broadcasted_iota(I32, (FLASH_ROWS, tk), 1)
            return jnp.where(col <= row, 0.0, -jnp.inf).astype(F32)

        tile(pl.multiple_of(i * tq + jd * tk, tk), diag_bias)
    o_ref[...] = _pair_slab(_flash_finish(acc_scr[0]), _flash_finish(acc_scr[1])).astype(o_ref.dtype)


def _mla_attention(q, k, v, batch, seq, tq, tk):
    n = q.shape[0]
    nq = seq // tq
    kern = functools.partial(_mla_attn_kernel, tq=tq, tk=tk)
    return pl.pallas_call(
        kern,
        grid=(batch, MLA_HEADS // 2, nq),
        in_specs=[pl.BlockSpec((tq, 2 * LANES), lambda b, p, i: (b * nq + i, p)),
                  pl.BlockSpec((seq, 2 * LANES), lambda b, p, i: (b, p)),
                  pl.BlockSpec((seq, 2 * LANES), lambda b, p, i: (b, p))],
        out_specs=pl.BlockSpec((tq, LANES), lambda b, p, i: (b * nq + i, p)),
        out_shape=jax.ShapeDtypeStruct((n, MLA_HEADS * MLA_V), BF16),
        scratch_shapes=[pltpu.VMEM((2, tq, tk), F32), pltpu.VMEM((2, tq, tk), BF16)]
                       + [pltpu.VMEM((2, tq, LANES), F32)] * 3,
        compiler_params=_params("parallel", "parallel", "arbitrary"),
        name="mla_attn",
    )(q, k, v)


def _router_kernel(x_ref, rw_ref, rb_ref, eidx_ref, gate_ref, pos_ref, cnt_ref, carry_ref, *, steps_per_chunk):
    i = pl.program_id(0)
    tm = x_ref.shape[0]
    per_group = N_EXPERTS // MOE_GROUPS

    @pl.when(i % steps_per_chunk == 0)
    def _():
        carry_ref[...] = jnp.zeros_like(carry_ref)

    x = x_ref[...]
    x_top = pltpu.bitcast(pltpu.bitcast(x, U32) & jnp.uint32(0xFFFF0000), F32)
    x_hi = x_top.astype(BF16)
    x_lo = (x - x_top).astype(BF16)
    logits = (jnp.dot(x_hi, rw_ref[0], preferred_element_type=F32)
              + jnp.dot(x_hi, rw_ref[1], preferred_element_type=F32)
              + jnp.dot(x_lo, rw_ref[0], preferred_element_type=F32))
    lt = jnp.transpose(logits)[:N_EXPERTS]
    s = _sigmoid(lt)
    sb = s + rb_ref[...]

    g3 = sb.reshape(MOE_GROUPS, per_group, tm)
    idx3 = lax.broadcasted_iota(I32, g3.shape, 1).astype(F32)
    m1 = jnp.max(g3, axis=1, keepdims=True)
    first = jnp.min(jnp.where(g3 == m1, idx3, float(per_group)), axis=1, keepdims=True)
    m2 = jnp.max(jnp.where(idx3 == first, -jnp.inf, g3), axis=1, keepdims=True)
    gscore = (m1 + m2).reshape(MOE_GROUPS, tm)
    gsel = _rank_desc(gscore, MOE_GROUPS, 0) < MOE_TOPK_GROUPS
    gsel3 = jnp.where(gsel, 1.0, 0.0).astype(F32).reshape(MOE_GROUPS, 1, tm)
    masked = jnp.where(gsel3 > 0.5, g3, -jnp.inf).reshape(N_EXPERTS, tm)
    e_iota = lax.broadcasted_iota(I32, (N_EXPERTS, tm), 0).astype(F32)
    work = masked
    picks, rows_g = [], []
    sel_f = jnp.zeros((N_EXPERTS, tm), F32)
    for r in range(MOE_TOP_K):
        top = jnp.max(work, axis=0, keepdims=True)
        first = jnp.min(jnp.where(work == top, e_iota, float(N_EXPERTS)), axis=0, keepdims=True)
        hit = e_iota == first
        picks.append(first)
        rows_g.append(jnp.sum(jnp.where(hit, s, 0.0), axis=0, keepdims=True))
        work = jnp.where(hit, -jnp.inf, work)
        sel_f = jnp.where(hit, 1.0, sel_f)
    gate_sum = rows_g[0]
    for g_row in rows_g[1:]:
        gate_sum = gate_sum + g_row
    rows_g = [g_row / gate_sum * MOE_ROUTED_SCALE for g_row in rows_g]

    sel_b = sel_f.astype(BF16)
    r_i = lax.broadcasted_iota(I32, (tm, tm), 0)
    c_i = lax.broadcasted_iota(I32, (tm, tm), 1)
    tri = jnp.where(r_i < c_i, 1.0, 0.0).astype(BF16)
    carry = carry_ref[:, 0:1]
    before = jnp.dot(sel_b, tri, preferred_element_type=F32) + carry
    carry_new = carry + jnp.sum(sel_b.astype(F32), axis=1, keepdims=True)
    carry_ref[...] = jnp.broadcast_to(carry_new, carry_ref.shape)
    cnt_ref[0] = jnp.broadcast_to(carry_new, carry_ref.shape).astype(I32)

    rows_p = [jnp.sum(jnp.where(e_iota == first, before, 0.0), axis=0, keepdims=True) for first in picks]
    pad = 8 - MOE_TOP_K
    eidx_ref[...] = jnp.concatenate(picks + [jnp.zeros((pad, tm), F32)], axis=0).astype(I32)
    gate_ref[...] = jnp.concatenate(rows_g + [jnp.zeros((pad, tm), F32)], axis=0)
    pos_ref[...] = jnp.concatenate(rows_p + [jnp.zeros((pad, tm), F32)], axis=0).astype(I32)


def _router(x2, rw, rb, n_chunk, tm):
    n, d = x2.shape
    steps = n // tm
    spc = steps // n_chunk
    kern = functools.partial(_router_kernel, steps_per_chunk=spc)
    row8 = pl.BlockSpec((8, tm), lambda i: (0, i))
    return pl.pallas_call(
        kern,
        grid=(steps,),
        in_specs=[pl.BlockSpec((tm, d), lambda i: (i, 0)), _full(rw.shape), _full(rb.shape)],
        out_specs=[row8, row8, row8, pl.BlockSpec((1, N_EXPERTS, LANES), lambda i: (i // spc, 0, 0))],
        out_shape=[jax.ShapeDtypeStruct((8, n), I32), jax.ShapeDtypeStruct((8, n), F32),
                   jax.ShapeDtypeStruct((8, n), I32),
                   jax.ShapeDtypeStruct((n_chunk, N_EXPERTS, LANES), I32)],
        scratch_shapes=[pltpu.VMEM((N_EXPERTS, LANES), F32)],
        compiler_params=_params("arbitrary"),
        name="moe_router",
    )(x2, rw, rb)


def _dest_kernel(tab_ref, eidx_ref, pos_ref, dest_ref, *, steps_per_chunk):
    chunk = pl.program_id(0) // steps_per_chunk
    eidx = eidx_ref[...]
    dest = pos_ref[...]
    for e in range(N_EXPERTS):
        dest = dest + jnp.where(eidx == e, tab_ref[chunk * N_EXPERTS + e], 0)
    dest_ref[...] = dest


def _dest_rows(pad_starts, eidx, pos, n_chunk, tm):
    n = eidx.shape[1]
    steps = n // tm
    row8 = lambda i, tab: (0, i)
    return pl.pallas_call(
        functools.partial(_dest_kernel, steps_per_chunk=steps // n_chunk),
        grid_spec=pltpu.PrefetchScalarGridSpec(
            num_scalar_prefetch=1, grid=(steps,),
            in_specs=[pl.BlockSpec((8, tm), row8), pl.BlockSpec((8, tm), row8)],
            out_specs=pl.BlockSpec((8, tm), row8)),
        out_shape=jax.ShapeDtypeStruct((8, n), I32),
        compiler_params=_params("parallel"),
        name="moe_dest",
    )(pad_starts, eidx, pos)


SC_ROWS = 64
SC_WORKERS = 32


def _sc_mesh():
    return plsc.VectorSubcoreMesh(core_axis_name="c", subcore_axis_name="s")


def _sc_worker_base(per_worker):
    return (lax.axis_index("s") * 2 + lax.axis_index("c")) * per_worker


def _sc_scatter_rows(x, dest_chunks, n_out):
    n, d = x.shape
    slots = dest_chunks.shape[0] * SC_ROWS // n
    per_w = n // SC_WORKERS
    chunks = per_w // SC_ROWS
    assert chunks * SC_ROWS * SC_WORKERS == n and chunks % 2 == 0, (n, chunks)

    def body(x_hbm, i_hbm, o_hbm, idx_v, rows0, rows1, sem_in0, sem_in1, sem_out):
        worker = lax.axis_index("s") * 2 + lax.axis_index("c")
        base = worker * per_w
        bufs = ((rows0, sem_in0), (rows1, sem_in1))
        pltpu.sync_copy(i_hbm.at[pl.ds(pl.multiple_of(worker * (chunks * slots), 8), chunks * slots)], idx_v)

        def load(c, b):
            off = pl.multiple_of(base + c * SC_ROWS, 8)
            return pltpu.make_async_copy(x_hbm.at[pl.ds(off, SC_ROWS)], bufs[b][0], bufs[b][1])

        load(0, 0).start()

        @pl.loop(0, chunks, step=2)
        def _(c):
            for b in range(2):
                cc = c + b
                load(cc, b).wait()

                @pl.when(cc + 1 < chunks)
                def _():
                    load(cc + 1, 1 - b).start()

                copies = [pltpu.make_async_copy(bufs[b][0], o_hbm.at[idx_v.at[cc * slots + k]], sem_out)
                          for k in range(slots)]
                for cp in copies:
                    cp.start()
                for cp in copies:
                    cp.wait()

    return pl.kernel(
        body, out_type=jax.ShapeDtypeStruct((n_out, d), x.dtype), mesh=_sc_mesh(),
        scratch_types=[pltpu.VMEM((chunks * slots, SC_ROWS), I32),
                       pltpu.VMEM((SC_ROWS, d), x.dtype), pltpu.VMEM((SC_ROWS, d), x.dtype),
                       pltpu.SemaphoreType.DMA, pltpu.SemaphoreType.DMA, pltpu.SemaphoreType.DMA],
        name="moe_dispatch_sc")(x, dest_chunks)


def _sc_gather_rows(table, idx):
    d = table.shape[1]
    r = idx.shape[0]
    per_w = r // SC_WORKERS
    chunks = per_w // SC_ROWS
    assert chunks * SC_ROWS * SC_WORKERS == r and chunks % 2 == 0, (r, chunks)

    def body(t_hbm, i_hbm, o_hbm, idx_v, rows0, rows1, sem_g0, sem_g1, sem_w0, sem_w1):
        base = _sc_worker_base(per_w)
        bufs = ((rows0, sem_g0, sem_w0), (rows1, sem_g1, sem_w1))
        pltpu.sync_copy(i_hbm.at[pl.ds(pl.multiple_of(base, 8), per_w)], idx_v)

        def gather(c, b):
            ids = idx_v.at[pl.ds(pl.multiple_of(c * SC_ROWS, 8), SC_ROWS)]
            return pltpu.make_async_copy(t_hbm.at[ids], bufs[b][0], bufs[b][1])

        def write(c, b):
            off = pl.multiple_of(base + c * SC_ROWS, 8)
            return pltpu.make_async_copy(bufs[b][0], o_hbm.at[pl.ds(off, SC_ROWS)], bufs[b][2])

        gather(0, 0).start()

        @pl.loop(0, chunks, step=2)
        def _(c):
            for b in range(2):
                cc = c + b

                @pl.when(cc >= 1)
                def _():
                    write(cc - 1, 1 - b).wait()

                @pl.when(cc + 1 < chunks)
                def _():
                    gather(cc + 1, 1 - b).start()

                gather(cc, b).wait()
                write(cc, b).start()

        write(chunks - 1, (chunks - 1) % 2).wait()

    return pl.kernel(
        body, out_type=jax.ShapeDtypeStruct((r, d), table.dtype), mesh=_sc_mesh(),
        scratch_types=[pltpu.VMEM((per_w,), I32),
                       pltpu.VMEM((SC_ROWS, d), table.dtype), pltpu.VMEM((SC_ROWS, d), table.dtype),
                       pltpu.SemaphoreType.DMA, pltpu.SemaphoreType.DMA,
                       pltpu.SemaphoreType.DMA, pltpu.SemaphoreType.DMA],
        name="moe_combine_sc")(table, idx)


def _experts_kernel(be_ref, nv_ref, xs_ref, wg_ref, wu_ref, wd_ref, y_ref, wg_b, wu_b, wd_b):
    i = pl.program_id(0)
    prev = be_ref[jnp.maximum(i - 1, 0)]

    @pl.when(jnp.logical_or(i == 0, be_ref[i] != prev))
    def _():
        wg_b[...] = wg_ref[0, 0].astype(BF16)
        wu_b[...] = wu_ref[0, 0].astype(BF16)
        wd_b[...] = wd_ref[0, 0].astype(BF16)

    @pl.when(i < nv_ref[0])
    def _():
        x_lo, x_hi = _unpack_halves(xs_ref[...])
        x_lo, x_hi = x_lo.astype(BF16), x_hi.astype(BF16)
        half = x_lo.shape[1]

        def proj(w):
            return (jnp.dot(x_lo, w[:half, :], preferred_element_type=F32)
                    + jnp.dot(x_hi, w[half:, :], preferred_element_type=F32))

        h = _silu(proj(wg_b)) * proj(wu_b)
        y_ref[...] = _pack_halves(jnp.dot(h.astype(BF16), wd_b[...], preferred_element_type=F32))

    @pl.when(i >= nv_ref[0])
    def _():
        y_ref[...] = jnp.zeros_like(y_ref)


def _experts(blk_exp, n_valid, xs, wg, wu, wd, layer, tb):
    rows, dp = xs.shape
    d = 2 * dp
    grid_spec = pltpu.PrefetchScalarGridSpec(
        num_scalar_prefetch=2,
        grid=(rows // tb,),
        in_specs=[pl.BlockSpec((tb, dp), lambda i, be, nv: (i, 0)),
                  pl.BlockSpec((1, 1, d, D_EXPERT), lambda i, be, nv: (layer, be[i], 0, 0)),
                  pl.BlockSpec((1, 1, d, D_EXPERT), lambda i, be, nv: (layer, be[i], 0, 0)),
                  pl.BlockSpec((1, 1, D_EXPERT, d), lambda i, be, nv: (layer, be[i], 0, 0))],
        out_specs=pl.BlockSpec((tb, dp), lambda i, be, nv: (i, 0)),
        scratch_shapes=[pltpu.VMEM((d, D_EXPERT), BF16), pltpu.VMEM((d, D_EXPERT), BF16),
                        pltpu.VMEM((D_EXPERT, d), BF16)],
    )
    return pl.pallas_call(
        _experts_kernel,
        grid_spec=grid_spec,
        out_shape=jax.ShapeDtypeStruct((rows, dp), U32),
        compiler_params=_params("arbitrary"),
        name="moe_experts",
    )(blk_exp, n_valid, xs, wg, wu, wd)


def _combine_ln_kernel(x_ref, yg_ref, gt_ref, wgu_ref, wd_ref, g_ref, b_ref, o_ref, *, alpha):
    x = x_ref[...]
    gu = jnp.dot(x.astype(BF16), wgu_ref[...], preferred_element_type=F32)
    dsh = gu.shape[1] // 2
    h = _silu(gu[:, :dsh]) * gu[:, dsh:]
    f = jnp.dot(h.astype(BF16), wd_ref[...], preferred_element_type=F32)
    r_lo = r_hi = None
    for k in range(MOE_TOP_K):
        y_lo, y_hi = _unpack_halves(yg_ref[k])
        gate = gt_ref[:, k:k + 1]
        r_lo = gate * y_lo if r_lo is None else r_lo + gate * y_lo
        r_hi = gate * y_hi if r_hi is None else r_hi + gate * y_hi
    f = f + jnp.concatenate([r_lo, r_hi], axis=1)
    o_ref[...] = _layer_norm_rows(alpha * x + f, g_ref[...], b_ref[...])


def _combine_ln(x2, yg, gates_t, wgu, wd, g, b, alpha, tm):
    n, d = x2.shape
    kern = functools.partial(_combine_ln_kernel, alpha=alpha)
    return pl.pallas_call(
        kern,
        grid=(n // tm,),
        in_specs=[pl.BlockSpec((tm, d), lambda i: (i, 0)),
                  pl.BlockSpec((MOE_TOP_K, tm, d // 2), lambda i: (0, i, 0)),
                  pl.BlockSpec((tm, 8), lambda i: (i, 0)),
                  _full(wgu.shape), _full(wd.shape), _full((1, d)), _full((1, d))],
        out_specs=pl.BlockSpec((tm, d), lambda i: (i, 0)),
        out_shape=jax.ShapeDtypeStruct((n, d), F32),
        compiler_params=_params("parallel"),
        name="moe_combine_ln",
    )(x2, yg, gates_t, wgu, wd, g, b)


def _rope_tables(positions, dim, lane_offset, period):
    half = dim // 2
    inv_freq = ROPE_THETA ** (-2.0 * jnp.arange(half, dtype=jnp.float32) / dim)
    ang = positions.astype(jnp.float32)[:, None] * inv_freq[None, :]
    cos_h, sin_h = jnp.cos(ang), jnp.sin(ang)
    lanes = np.arange(LANES)
    rel = (lanes - lane_offset) % period
    active = (lanes >= lane_offset) & (rel < dim)
    fidx = np.where(active, rel % half, 0)
    sign = np.where(rel < half, -1.0, 1.0)
    cos = jnp.where(active[None, :], cos_h[:, fidx], 1.0)
    sin = jnp.where(active[None, :], sin_h[:, fidx] * sign[None, :], 0.0)
    return cos.astype(F32), sin.astype(F32)


def _gate_slabs(w):
    k = w.shape[0]
    g = w.reshape(k, 3, NSA_KV_HEADS, GQA_REP).transpose(0, 2, 1, 3).reshape(k, NSA_KV_HEADS, 3 * GQA_REP)
    return jnp.pad(g, ((0, 0), (0, 0), (0, LANES - 3 * GQA_REP))).reshape(k, NSA_KV_HEADS * LANES)


def _dup_heads(w):
    a, b = w[:, :HEAD_DIM], w[:, HEAD_DIM:]
    return jnp.concatenate([a, a, b, b], axis=1)


def _pad_heads(w):
    a, b = w[:, :HEAD_DIM], w[:, HEAD_DIM:]
    z = jnp.zeros_like(a)
    return jnp.concatenate([a, z, b, z], axis=1)


def _moe_layer(x1_and_packed, rw, rb, wg, wu, wd, layer, shg, shu, shd, ln_g, ln_b, alpha, tm):
    x1, x1_pk = x1_and_packed
    n, d = x1.shape
    n_chunk = 1
    tb = MOE_ROW_BLOCK
    rw_p = jnp.pad(rw, ((0, 0), (0, LANES - N_EXPERTS)))
    rw_top = lax.bitcast_convert_type(lax.bitcast_convert_type(rw_p, U32) & jnp.uint32(0xFFFF0000), F32)
    rw_split = jnp.stack([rw_top.astype(BF16), (rw_p - rw_top).astype(BF16)])
    eidx, gate, pos, cnt = _router(x1, rw_split, rb.reshape(N_EXPERTS, 1), n_chunk, tm)
    counts = cnt[:, :, 0].reshape(-1)
    padded = (counts + tb - 1) // tb * tb
    pad_ends = jnp.cumsum(padded)
    pad_starts = (pad_ends - padded).astype(I32)
    dest2 = _dest_rows(pad_starts, eidx, pos, n_chunk, tm)[:MOE_TOP_K]
    dest = dest2.reshape(-1)
    dest_chunks = dest2.reshape(MOE_TOP_K, n // SC_ROWS, SC_ROWS).transpose(1, 0, 2).reshape(-1, SC_ROWS)
    n_rows = n * MOE_TOP_K
    n_blk = -(-(n_rows + n_chunk * N_EXPERTS * (tb - 1)) // tb)
    blk_first_row = jnp.arange(n_blk, dtype=I32) * tb
    owner = jnp.sum((pad_ends[None, :] <= blk_first_row[:, None]).astype(I32), axis=1)
    blk_exp = jnp.minimum(owner, n_chunk * N_EXPERTS - 1).astype(I32) % N_EXPERTS
    n_valid = (pad_ends[-1] // tb).astype(I32).reshape(1)

    xs = _sc_scatter_rows(x1_pk, dest_chunks, n_blk * tb)
    y = _experts(blk_exp, n_valid, xs, wg, wu, wd, layer, tb)
    yg = _sc_gather_rows(y, dest).reshape(MOE_TOP_K, n, d // 2)
    gates_t = gate.T
    sh_gu = jnp.concatenate([shg, shu], axis=-1).astype(BF16)
    return _combine_ln(x1, yg, gates_t, sh_gu, shd.astype(BF16), ln_g.reshape(1, d), ln_b.reshape(1, d), alpha, tm)


def _swa_nsa_layer(x2, batch, seq, w_in, sinks, pe_k, w_ck1, w_ck2, pe_v, w_cv1, w_cv2, w_out,
                   ln_g, ln_b, alpha, tm):
    n, d = x2.shape
    splits = np.cumsum([512, 128, 128, 512, 128, 128, 128, 128, 128, 128, 24])[:-1]
    qa, ka, va, qb, kc, vc, ksl, vsl, kw, vw, gts = jnp.split(w_in, [int(c) for c in splits], axis=1)
    w_all = jnp.concatenate(
        [qa, _dup_heads(ka), _pad_heads(va), qb, kc, vc, _dup_heads(ksl), _pad_heads(vsl),
         _dup_heads(kw), _pad_heads(vw), _gate_slabs(gts)], axis=1).astype(BF16)
    cos, sin = _rope_tables(jnp.arange(seq), HEAD_DIM, 0, HEAD_DIM)
    (q_a, k_a, v_a, q_b, k_c, v_c, k_s, v_s, k_w, v_w, gates) = _mix0_proj(x2, w_all, cos, sin, seq, tm)

    o_swa = _banded_attention(q_a, k_a, v_a, batch, seq, SWA_WINDOW, 128, sinks=sinks, name="swa")
    o_win = _banded_attention(q_b, k_w, v_w, batch, seq, NSA_WINDOW, 128, gates=gates,
                              gate_col=2 * GQA_REP, out_dtype=F32, name="nsa_win")

    n_chunk = seq // NSA_CMP_STRIDE
    cw = NSA_CMP_STRIDE * NSA_KV_HEADS * HEAD_DIM
    hid = NSA_CMP_HIDDEN

    def expand_w1(w1):
        halves = w1.reshape(2, NSA_CMP_STRIDE, HEAD_DIM, hid)
        out = jnp.zeros((NSA_KV_HEADS, NSA_CMP_STRIDE, NSA_KV_HEADS, HEAD_DIM, 2 * hid), w1.dtype)
        for g in range(NSA_KV_HEADS):
            out = out.at[g, :, g, :, :hid].set(halves[0]).at[g, :, g, :, hid:].set(halves[1])
        return out.reshape(NSA_KV_HEADS, cw, 2 * hid).astype(BF16)

    def pe_rows(pe):
        return jnp.pad(pe.reshape(1, NSA_CMP_LEN * HEAD_DIM), ((0, 15), (0, 0))).astype(BF16)

    cos_c, sin_c = _rope_tables(jnp.arange(n_chunk) * NSA_CMP_STRIDE + NSA_CMP_LEN - 1, HEAD_DIM, 0, HEAD_DIM)
    k_cmp, v_cmp = _compress(
        k_c.reshape(batch * n_chunk, cw), v_c.reshape(batch * n_chunk, cw),
        expand_w1(w_ck1), expand_w1(w_cv1), pe_rows(pe_k), pe_rows(pe_v),
        w_ck1.astype(BF16), w_cv1.astype(BF16),
        jnp.concatenate([w_ck2, w_ck2], axis=1).astype(BF16),
        jnp.concatenate([w_cv2, w_cv2], axis=1).astype(BF16),
        cos_c, sin_c, batch, n_chunk)

    n_sb = seq // NSA_SEL_LEN
    n_sel = min(NSA_N_SEL, n_sb)
    cs = np.arange(n_chunk) * NSA_CMP_STRIDE
    bs = np.arange(n_sb) * NSA_SEL_LEN
    ov = np.clip(np.minimum(cs[:, None] + NSA_CMP_LEN, bs[None, :] + NSA_SEL_LEN)
                 - np.maximum(cs[:, None], bs[None, :]), 0, None) / NSA_CMP_LEN
    overlap = jnp.asarray(np.pad(ov, ((0, 0), (0, LANES - n_sb))).T, BF16)
    expand = jnp.asarray(np.arange(LANES)[:, None] == (np.arange(seq)[None, :] // NSA_SEL_LEN), BF16)
    tk_sel = min(512, seq)
    o_nsa = _nsa_cmp_sel(q_b, k_cmp, v_cmp, k_s, v_s, gates, o_win, overlap, expand,
                         batch, seq, 256, n_sb, n_sel, tk_sel)
    w_o = w_out.astype(BF16)
    half = SWA_Q_HEADS * HEAD_DIM
    return _oproj_ln([o_swa, o_nsa], [w_o[:half], w_o[half:]], x2, ln_g.reshape(1, d), ln_b.reshape(1, d),
                     alpha, tm)


def _mla_layer(x2, batch, seq, w_in, q_norm, kv_norm, w_uq, w_ukv, w_out, ln_g, ln_b, alpha, tm):
    n, d = x2.shape
    dq = MLA_NOPE + MLA_ROPE
    w_kr = jnp.zeros((d, LANES), w_in.dtype).at[:, MLA_NOPE:dq].set(w_in[:, MLA_Q_LORA + MLA_KV_LORA:])
    w_in_p = jnp.concatenate([w_in[:, :MLA_Q_LORA + MLA_KV_LORA], w_kr], axis=1).astype(BF16)
    wuq = jnp.pad(w_uq.reshape(MLA_Q_LORA, MLA_HEADS, dq), ((0, 0), (0, 0), (0, LANES - dq)))
    wuq = wuq.reshape(MLA_Q_LORA, MLA_HEADS * LANES).astype(BF16)
    wukv = w_ukv.reshape(MLA_KV_LORA, MLA_HEADS, MLA_NOPE + MLA_V)
    wuk = jnp.pad(wukv[:, :, :MLA_NOPE], ((0, 0), (0, 0), (0, LANES - MLA_NOPE)))
    wuk = wuk.reshape(MLA_KV_LORA, MLA_HEADS * LANES).astype(BF16)
    wuv = jnp.pad(wukv[:, :, MLA_NOPE:], ((0, 0), (0, 0), (0, LANES - MLA_V)))
    wuv = wuv.reshape(MLA_KV_LORA, MLA_HEADS * LANES).astype(BF16)
    cos, sin = _rope_tables(jnp.arange(seq), MLA_ROPE, MLA_NOPE, LANES)
    q, k, v = _mla_proj(x2, w_in_p, q_norm.reshape(1, -1), kv_norm.reshape(1, -1), wuq, wuk, wuv, cos, sin, seq, tm)
    o = _mla_attention(q, k, v, batch, seq, min(512, seq), min(512, seq))
    return _oproj_ln([o], [w_out.astype(BF16)], x2, ln_g.reshape(1, d), ln_b.reshape(1, d), alpha, tm)


def kernel(x, swa_nsa_w_in, swa_sinks, nsa_cmp_pe_k, nsa_cmp_k_w1, nsa_cmp_k_w2, nsa_cmp_pe_v, nsa_cmp_v_w1, nsa_cmp_v_w2, swa_nsa_w_out, mla_w_in, mla_q_norm, mla_kv_norm, mla_w_uq, mla_w_ukv, mla_w_out, ln_mix_g, ln_mix_b, ln_ffn_g, ln_ffn_b, router_w, router_bias, expert_w_gate, expert_w_up, expert_w_down, shared_w_gate, shared_w_up, shared_w_down):
    batch, seq, d = x.shape
    depth = ln_mix_g.shape[0]
    alpha = (2 * depth) ** 0.25
    tm = min(512, seq)
    x2 = x.reshape(batch * seq, d)
    for layer in range(depth):
        j = layer // 2
        if layer % 2 == 0:
            x2 = _swa_nsa_layer(x2, batch, seq, swa_nsa_w_in[j], swa_sinks[j], nsa_cmp_pe_k[j],
                                nsa_cmp_k_w1[j], nsa_cmp_k_w2[j], nsa_cmp_pe_v[j], nsa_cmp_v_w1[j],
                                nsa_cmp_v_w2[j], swa_nsa_w_out[j], ln_mix_g[layer], ln_mix_b[layer], alpha, tm)
        else:
            x2 = _mla_layer(x2, batch, seq, mla_w_in[j], mla_q_norm[j], mla_kv_norm[j], mla_w_uq[j],
                            mla_w_ukv[j], mla_w_out[j], ln_mix_g[layer], ln_mix_b[layer], alpha, tm)
        x2 = _moe_layer(x2, router_w[layer], router_bias[layer], expert_w_gate, expert_w_up, expert_w_down, layer,
                        shared_w_gate[layer], shared_w_up[layer], shared_w_down[layer],
                        ln_ffn_g[layer], ln_ffn_b[layer], alpha, tm)
    return x2.reshape(batch, seq, d)
```

```python
import functools
import math

import numpy as np
import jax
import jax.numpy as jnp
from jax import lax
from jax.experimental import pallas as pl
from jax.experimental.pallas import tpu as pltpu
from jax.experimental.pallas import tpu_sc as plsc

F32 = jnp.float32
BF16 = jnp.bfloat16
I32 = jnp.int32

LANES = 128
VMEM_LIMIT = 48 * 1024 * 1024

ROPE_THETA = 10000.0
LN_EPS = 1e-5
RMS_EPS = 1e-6
NEG_BIG = -1e30
FORCE_SCORE = 1e4

HEAD_DIM = 64
SWA_Q_HEADS = 8
SWA_KV_HEADS = 2
SWA_WINDOW = 128
NSA_Q_HEADS = 8
NSA_KV_HEADS = 2
NSA_CMP_LEN = 32
NSA_CMP_STRIDE = 16
NSA_CMP_HIDDEN = 128
NSA_SEL_LEN = 64
NSA_N_SEL = 16
NSA_WINDOW = 512
GQA_REP = 4

MLA_HEADS = 16
MLA_NOPE = 64
MLA_ROPE = 32
MLA_V = 64
MLA_Q_LORA = 384
MLA_KV_LORA = 256

N_EXPERTS = 64
MOE_GROUPS = 8
MOE_TOPK_GROUPS = 4
MOE_TOP_K = 6
D_EXPERT = 256
MOE_ROUTED_SCALE = 2.5
MOE_ROW_BLOCK = 512

NT_DIMS = (((1,), (1,)), ((), ()))


def _params(*sem):
    return pltpu.CompilerParams(dimension_semantics=sem, vmem_limit_bytes=VMEM_LIMIT)


def _full(shape):
    nd = len(shape)
    return pl.BlockSpec(shape, lambda *_: (0,) * nd)


def _rope_slab(y, cos, sin, half):
    lane = lax.broadcasted_iota(I32, y.shape, 1)
    first = (lane % (2 * half)) < half
    rot = jnp.where(first, pltpu.roll(y, LANES - half, 1), pltpu.roll(y, half, 1))
    return y * cos + rot * sin


def _layer_norm_rows(v, g, b):
    mu = jnp.mean(v, axis=-1, keepdims=True)
    vc = v - mu
    var = jnp.mean(vc * vc, axis=-1, keepdims=True)
    return vc * lax.rsqrt(var + LN_EPS) * g + b


def _silu(v):
    return v * (1.0 / (1.0 + jnp.exp(-v)))


def _sigmoid(v):
    return 1.0 / (1.0 + jnp.exp(-v))


U32 = jnp.uint32


def _pack_halves(v):
    w = v.shape[1] // 2
    lo = pltpu.bitcast(v[:, :w].astype(BF16).astype(F32), U32)
    hi = pltpu.bitcast(v[:, w:].astype(BF16).astype(F32), U32)
    return (lo >> 16) | (hi & jnp.uint32(0xFFFF0000))


def _unpack_halves(u):
    lo = pltpu.bitcast(u << 16, F32)
    hi = pltpu.bitcast(u & jnp.uint32(0xFFFF0000), F32)
    return lo, hi


def _ones_half(shape):
    lane = lax.broadcasted_iota(I32, shape, 1)
    return jnp.where(lane < HEAD_DIM, 0.0, 1.0).astype(F32)


def _stack_group_queries(q_ref, group, rep):
    lane = lax.broadcasted_iota(I32, (q_ref.shape[0], LANES), 1)
    keep_lo = jnp.where(lane < HEAD_DIM, 1.0, 0.0).astype(q_ref.dtype)
    keep_hi = jnp.where(lane < HEAD_DIM, 0.0, 1.0).astype(q_ref.dtype)
    parts = []
    for r in range(rep):
        h = group * rep + r
        slab = q_ref[:, (h // 2) * LANES:(h // 2 + 1) * LANES]
        parts.append(slab * (keep_lo if h % 2 == 0 else keep_hi))
    return jnp.concatenate(parts, axis=0)


def _rank_desc(score, n_valid, axis):
    idx = lax.broadcasted_iota(I32, score.shape, axis)
    rank = jnp.zeros(score.shape, I32)
    for j in range(n_valid):
        other = lax.slice_in_dim(score, j, j + 1, axis=axis)
        ahead = jnp.where(other > score, 1, jnp.where(other == score, jnp.where(idx > j, 1, 0), 0))
        rank = rank + ahead
    return rank


_MIX0_OUTS = (
    ("qa", 512, "rope_q", BF16), ("ka", 256, "rope", BF16), ("va", 256, "v_ones", BF16),
    ("qb", 512, "rope_q", BF16), ("kc", 128, "plain", BF16), ("vc", 128, "plain", BF16),
    ("ks", 256, "rope", BF16), ("vs", 256, "v_ones", BF16),
    ("kw", 256, "rope", BF16), ("vw", 256, "v_ones", BF16),
    ("gates", 256, "sigmoid", F32),
)


def _mix0_proj_kernel(x_ref, w_ref, cos_ref, sin_ref, *out_refs):
    xb = x_ref[...].astype(BF16)
    cos = cos_ref[...]
    sin = sin_ref[...]
    off = 0
    for (_, width, kind, _), o_ref in zip(_MIX0_OUTS, out_refs):
        y = jnp.dot(xb, w_ref[:, off:off + width], preferred_element_type=F32)
        for c in range(width // LANES):
            yc = y[:, c * LANES:(c + 1) * LANES]
            if kind in ("rope", "rope_q"):
                yc = _rope_slab(yc, cos, sin, HEAD_DIM // 2)
            if kind == "rope_q":
                yc = yc * (HEAD_DIM ** -0.5 * LOG2_E)
            if kind == "v_ones":
                yc = yc + _ones_half(yc.shape)
            if kind == "sigmoid":
                yc = _sigmoid(yc)
            o_ref[:, c * LANES:(c + 1) * LANES] = yc.astype(o_ref.dtype)
        off += width


def _mix0_proj(x2, w_all, cos, sin, seq, tm):
    n = x2.shape[0]
    d = x2.shape[1]
    wtot = w_all.shape[1]
    sblk = seq // tm
    return pl.pallas_call(
        _mix0_proj_kernel,
        grid=(n // tm,),
        in_specs=[pl.BlockSpec((tm, d), lambda i: (i, 0)),
                  _full((d, wtot)),
                  pl.BlockSpec((tm, LANES), lambda i: (i % sblk, 0)),
                  pl.BlockSpec((tm, LANES), lambda i: (i % sblk, 0))],
        out_specs=[pl.BlockSpec((tm, w), lambda i: (i, 0)) for _, w, _, _ in _MIX0_OUTS],
        out_shape=[jax.ShapeDtypeStruct((n, w), dt) for _, w, _, dt in _MIX0_OUTS],
        compiler_params=_params("parallel"),
        name="mix0_proj",
    )(x2, w_all, cos, sin)


def _banded_kernel(*refs, window, blk, seq, has_sink, gate_col):
    if has_sink:
        sink_ref, refs = refs[0], refs[1:]
    q_ref, k_ref, v_ref = refs[:3]
    g_ref = refs[3] if gate_col is not None else None
    o_ref, bias_scr, s_scr, p_scr, m_scr, a_scr, acc_scr = refs[-7:]
    i = pl.program_id(1)
    start = i * blk
    span = min(window + blk, seq)
    k0 = pl.multiple_of(jnp.maximum(start - window, 0), LANES)
    tq = start + lax.broadcasted_iota(I32, (blk, span), 0)
    tk = k0 + lax.broadcasted_iota(I32, (blk, span), 1)
    diff = tq - tk
    bias_scr[...] = jnp.where(diff >= 0, jnp.where(diff < window, 0.0, -jnp.inf), -jnp.inf).astype(F32)
    chunks_per_head = blk // FLASH_ROWS

    def bias_fn(c):
        r0 = (c % chunks_per_head) * FLASH_ROWS
        return bias_scr[r0:r0 + FLASH_ROWS, :]

    lane = lax.broadcasted_iota(I32, (blk, LANES), 1)
    lo = lane < HEAD_DIM
    n_groups = q_ref.shape[1] // (GQA_REP * HEAD_DIM)
    for g in range(n_groups):
        _flash_init(m_scr.at[g], acc_scr.at[g])
        _flash_scores(_stack_group_queries(q_ref, g, GQA_REP), k_ref[pl.ds(k0, span), g * LANES:(g + 1) * LANES],
                      s_scr.at[g])
    for g in range(n_groups):
        _flash_max(bias_fn, s_scr.at[g], m_scr.at[g], a_scr.at[g])
    for g in range(n_groups):
        _flash_weights(v_ref[pl.ds(k0, span), g * LANES:(g + 1) * LANES], s_scr.at[g], p_scr.at[g],
                       m_scr.at[g], a_scr.at[g], acc_scr.at[g])
    for g in range(n_groups):
        acc = acc_scr[g]
        if has_sink:
            m = m_scr[g]
            sink = jnp.concatenate(
                [jnp.full((blk, LANES), sink_ref[g * GQA_REP + r] * LOG2_E, F32) for r in range(GQA_REP)], axis=0)
            m_all = jnp.maximum(m, sink)
            acc = acc * jnp.exp2(m - m_all) + _ones_half(acc.shape) * jnp.exp2(sink - m_all)
        o = _flash_finish(acc)
        for j in range(GQA_REP // 2):
            slab = _pair_slab(o[(2 * j) * blk:(2 * j + 1) * blk], o[(2 * j + 1) * blk:(2 * j + 2) * blk])
            if gate_col is not None:
                c = g * LANES + gate_col + 2 * j
                gate = jnp.where(lo, g_ref[:, c:c + 1], g_ref[:, c + 1:c + 2])
                slab = slab * gate
            col = (g * (GQA_REP // 2) + j) * LANES
            o_ref[:, col:col + LANES] = slab.astype(o_ref.dtype)


def _banded_attention(q, k, v, batch, seq, window, blk, sinks=None, gates=None, gate_col=None,
                      out_dtype=None, name="banded"):
    n, qw = q.shape
    kw = k.shape[1]
    nblk = seq // blk
    span = min(window + blk, seq)
    groups = qw // (GQA_REP * HEAD_DIM)
    rows = GQA_REP * blk
    kern = functools.partial(_banded_kernel, window=window, blk=blk, seq=seq,
                             has_sink=sinks is not None, gate_col=gate_col)
    in_specs = []
    args = []
    if sinks is not None:
        in_specs.append(pl.BlockSpec(memory_space=pltpu.SMEM))
        args.append(sinks)
    in_specs += [pl.BlockSpec((blk, qw), lambda b, i: (b * nblk + i, 0)),
                 pl.BlockSpec((seq, kw), lambda b, i: (b, 0)),
                 pl.BlockSpec((seq, kw), lambda b, i: (b, 0))]
    args += [q, k, v]
    if gates is not None:
        in_specs.append(pl.BlockSpec((blk, gates.shape[1]), lambda b, i: (b * nblk + i, 0)))
        args.append(gates)
    return pl.pallas_call(
        kern,
        grid=(batch, nblk),
        in_specs=in_specs,
        out_specs=pl.BlockSpec((blk, qw), lambda b, i: (b * nblk + i, 0)),
        out_shape=jax.ShapeDtypeStruct((n, qw), BF16 if out_dtype is None else out_dtype),
        scratch_shapes=[pltpu.VMEM((blk, span), F32),
                        pltpu.VMEM((groups, rows, span), F32), pltpu.VMEM((groups, rows, span), BF16)]
                       + [pltpu.VMEM((groups, rows, LANES), F32)] * 3,
        compiler_params=_params("parallel", "arbitrary"),
        name=name,
    )(*args)


def _gelu_tanh(v):
    return 0.5 * v * (1.0 + jnp.tanh(math.sqrt(2.0 / math.pi) * (v + 0.044715 * (v * v * v))))


def _compress_kernel(tk_ref, tv_ref, w1k_ref, w1v_ref, pek_ref, pev_ref, w1ko_ref, w1vo_ref,
                     w2k_ref, w2v_ref, cos_ref, sin_ref, kc_ref, vc_ref):
    n_chunk = tk_ref.shape[0]
    hid = NSA_CMP_HIDDEN
    for t_ref, w1_ref, pe_ref, w1o_ref, w2_ref, o_ref, rope in (
            (tk_ref, w1k_ref, pek_ref, w1ko_ref, w2k_ref, kc_ref, True),
            (tv_ref, w1v_ref, pev_ref, w1vo_ref, w2v_ref, vc_ref, False)):
        pe_term = jnp.dot(pe_ref[...], w1o_ref[...], preferred_element_type=F32)[0:1, :]
        t = t_ref[...]
        for g in range(NSA_KV_HEADS):
            uv = jnp.dot(t, w1_ref[g], preferred_element_type=F32)
            nxt = pltpu.roll(uv[:, hid:], n_chunk - 1, 0)
            h = _gelu_tanh(uv[:, :hid] + nxt + pe_term)
            c = jnp.dot(h.astype(BF16), w2_ref[...], preferred_element_type=F32)
            if rope:
                c = _rope_slab(c, cos_ref[...], sin_ref[...], HEAD_DIM // 2)
            o_ref[0, g] = c.astype(o_ref.dtype)


def _compress(tk, tv, w1k, w1v, pek, pev, w1ko, w1vo, w2k, w2v, cos, sin, batch, n_chunk):
    width = tk.shape[1]
    out = jax.ShapeDtypeStruct((batch, NSA_KV_HEADS, n_chunk, LANES), BF16)
    ospec = pl.BlockSpec((1, NSA_KV_HEADS, n_chunk, LANES), lambda b: (b, 0, 0, 0))
    return pl.pallas_call(
        _compress_kernel,
        grid=(batch,),
        in_specs=[pl.BlockSpec((n_chunk, width), lambda b: (b, 0)),
                  pl.BlockSpec((n_chunk, width), lambda b: (b, 0)),
                  _full(w1k.shape), _full(w1v.shape), _full(pek.shape), _full(pev.shape),
                  _full(w1ko.shape), _full(w1vo.shape), _full(w2k.shape), _full(w2v.shape),
                  _full(cos.shape), _full(sin.shape)],
        out_specs=[ospec, ospec],
        out_shape=[out, out],
        compiler_params=_params("parallel"),
        name="nsa_compress",
    )(tk, tv, w1k, w1v, pek, pev, w1ko, w1vo, w2k, w2v, cos, sin)


def _nsa_kernel(q_ref, kc_ref, vc_ref, ks_ref, vs_ref, g_ref, win_ref, ov_ref, e_ref, o_ref,
                bias_scr, s_scr, p_scr, m_scr, a_scr, acc_scr, *, qblk, seq, n_sb, n_sel, tk_sel):
    i = pl.program_id(2)
    start = i * qblk
    n_cmp = kc_ref.shape[2]
    lane = lax.broadcasted_iota(I32, (qblk, LANES), 1)
    lo = lane < HEAD_DIM
    q = _stack_group_queries(q_ref, 0, GQA_REP)
    tq = start + lax.broadcasted_iota(I32, (qblk, 1), 0)

    kc = kc_ref[0, 0]
    vc = vc_ref[0, 0]
    cmp_end = lax.broadcasted_iota(I32, (qblk, n_cmp), 1) * NSA_CMP_STRIDE + (NSA_CMP_LEN - 1)
    vis = jnp.where(cmp_end <= tq, 1.0, 0.0).astype(F32)
    vis4 = jnp.concatenate([vis] * GQA_REP, axis=0)
    s_c = lax.dot_general(q, kc, NT_DIMS, preferred_element_type=F32)
    s_c = jnp.where(vis4 > 0.5, s_c, NEG_BIG)
    p_c = jnp.exp2(s_c - jnp.max(s_c, axis=-1, keepdims=True)) * vis4
    p_c = p_c / jnp.maximum(jnp.sum(p_c, axis=-1, keepdims=True), 1e-30)
    o_c = jnp.dot(p_c.astype(BF16), vc, preferred_element_type=F32)

    p_sum = p_c[0:qblk]
    for r in range(1, GQA_REP):
        p_sum = p_sum + p_c[r * qblk:(r + 1) * qblk]
    p_top = pltpu.bitcast(pltpu.bitcast(p_sum, U32) & jnp.uint32(0xFFFF0000), F32)
    imp_t = (lax.dot_general(ov_ref[...], p_top.astype(BF16), NT_DIMS, preferred_element_type=F32)
             + lax.dot_general(ov_ref[...], (p_sum - p_top).astype(BF16), NT_DIMS,
                               preferred_element_type=F32))[:n_sb]
    blk_t = lax.broadcasted_iota(I32, (n_sb, qblk), 0)
    tq_t = start + lax.broadcasted_iota(I32, (n_sb, qblk), 1)
    cur_t = tq_t // NSA_SEL_LEN
    forced = jnp.where(blk_t == 0, 1, jnp.where(blk_t == cur_t, 1, jnp.where(blk_t == cur_t - 1, 1, 0)))
    score_t = jnp.where(forced == 1, FORCE_SCORE, jnp.where(blk_t * NSA_SEL_LEN > tq_t, -1.0, imp_t))
    sel_t = jnp.where(_rank_desc(score_t, n_sb, 0) < n_sel, 1.0, 0.0).astype(F32)
    if n_sb < LANES:
        sel_t = jnp.concatenate([sel_t, jnp.zeros((LANES - n_sb, qblk), F32)], axis=0)
    sel = jnp.transpose(sel_t).astype(BF16)

    _flash_init(m_scr, acc_scr)
    chunks_per_head = qblk // FLASH_ROWS

    def body(j, carry):
        k0 = pl.multiple_of(j * tk_sel, tk_sel)
        picked = jnp.dot(sel, e_ref[:, pl.ds(k0, tk_sel)], preferred_element_type=F32)
        tk = k0 + lax.broadcasted_iota(I32, (qblk, tk_sel), 1)
        bias_scr[...] = jnp.where(picked > 0.5, jnp.where(tk <= tq, 0.0, -jnp.inf), -jnp.inf).astype(F32)

        def bias_fn(c):
            r0 = (c % chunks_per_head) * FLASH_ROWS
            return bias_scr[r0:r0 + FLASH_ROWS, :]

        _flash_scores(q, ks_ref[pl.ds(k0, tk_sel), :], s_scr)
        _flash_max(bias_fn, s_scr, m_scr, a_scr)
        _flash_weights(vs_ref[pl.ds(k0, tk_sel), :], s_scr, p_scr, m_scr, a_scr, acc_scr)
        return carry

    n_tiles = (start + qblk + tk_sel - 1) // tk_sel
    lax.fori_loop(0, n_tiles, body, 0)
    o_s = _flash_finish(acc_scr[...])

    for j in range(GQA_REP // 2):

        def gate(branch, pair=j):
            c = branch * GQA_REP + 2 * pair
            return jnp.where(lo, g_ref[:, c:c + 1], g_ref[:, c + 1:c + 2])

        rows_e = slice((2 * j) * qblk, (2 * j + 1) * qblk)
        rows_o = slice((2 * j + 1) * qblk, (2 * j + 2) * qblk)
        oc = jnp.where(lo, o_c[rows_e], o_c[rows_o])
        os_ = _pair_slab(o_s[rows_e], o_s[rows_o])
        out = gate(0) * oc + gate(1) * os_ + win_ref[:, j * LANES:(j + 1) * LANES]
        o_ref[:, j * LANES:(j + 1) * LANES] = out.astype(o_ref.dtype)


def _nsa_cmp_sel(qb, kcmp, vcmp, ks, vs, gates, win, overlap, expand, batch, seq, qblk, n_sb, n_sel, tk_sel):
    n = qb.shape[0]
    nq = seq // qblk
    n_cmp = kcmp.shape[2]
    gw = GQA_REP * HEAD_DIM
    rows = GQA_REP * qblk
    kern = functools.partial(_nsa_kernel, qblk=qblk, seq=seq, n_sb=n_sb, n_sel=n_sel, tk_sel=tk_sel)
    return pl.pallas_call(
        kern,
        grid=(batch, NSA_KV_HEADS, nq),
        in_specs=[pl.BlockSpec((qblk, gw), lambda b, g, i: (b * nq + i, g)),
                  pl.BlockSpec((1, 1, n_cmp, LANES), lambda b, g, i: (b, g, 0, 0)),
                  pl.BlockSpec((1, 1, n_cmp, LANES), lambda b, g, i: (b, g, 0, 0)),
                  pl.BlockSpec((seq, LANES), lambda b, g, i: (b, g)),
                  pl.BlockSpec((seq, LANES), lambda b, g, i: (b, g)),
                  pl.BlockSpec((qblk, LANES), lambda b, g, i: (b * nq + i, g)),
                  pl.BlockSpec((qblk, gw), lambda b, g, i: (b * nq + i, g)),
                  _full(overlap.shape), _full(expand.shape)],
        out_specs=pl.BlockSpec((qblk, gw), lambda b, g, i: (b * nq + i, g)),
        out_shape=jax.ShapeDtypeStruct((n, NSA_Q_HEADS * HEAD_DIM), BF16),
        scratch_shapes=[pltpu.VMEM((qblk, tk_sel), F32),
                        pltpu.VMEM((rows, tk_sel), F32), pltpu.VMEM((rows, tk_sel), BF16),
                        pltpu.VMEM((rows, LANES), F32), pltpu.VMEM((rows, LANES), F32),
                        pltpu.VMEM((rows, LANES), F32)],
        compiler_params=_params("parallel", "parallel", "arbitrary"),
        name="nsa_cmp_sel",
    )(qb, kcmp, vcmp, ks, vs, gates, win, overlap, expand)


def _oproj_ln_kernel(*refs, n_parts, alpha):
    o_refs = refs[:n_parts]
    w_refs = refs[n_parts:2 * n_parts]
    x_ref, g_ref, b_ref, y_ref, ypk_ref = refs[2 * n_parts:]
    h = jnp.dot(o_refs[0][...], w_refs[0][...], preferred_element_type=F32)
    for o_ref, w_ref in zip(o_refs[1:], w_refs[1:]):
        h = h + jnp.dot(o_ref[...], w_ref[...], preferred_element_type=F32)
    y = _layer_norm_rows(alpha * x_ref[...] + h, g_ref[...], b_ref[...])
    y_ref[...] = y
    ypk_ref[...] = _pack_halves(y)


def _oproj_ln(o_parts, w_parts, x2, g, b, alpha, tm):
    n, d = x2.shape
    kern = functools.partial(_oproj_ln_kernel, n_parts=len(o_parts), alpha=alpha)
    in_specs = ([pl.BlockSpec((tm, o.shape[1]), lambda i: (i, 0)) for o in o_parts]
                + [_full(w.shape) for w in w_parts]
                + [pl.BlockSpec((tm, d), lambda i: (i, 0)), _full((1, d)), _full((1, d))])
    return pl.pallas_call(
        kern,
        grid=(n // tm,),
        in_specs=in_specs,
        out_specs=[pl.BlockSpec((tm, d), lambda i: (i, 0)), pl.BlockSpec((tm, d // 2), lambda i: (i, 0))],
        out_shape=[jax.ShapeDtypeStruct((n, d), F32), jax.ShapeDtypeStruct((n, d // 2), U32)],
        compiler_params=_params("parallel"),
        name="oproj_ln",
    )(*o_parts, *w_parts, x2, g, b)


def _rms_rows(v, g):
    return v * lax.rsqrt(jnp.mean(v * v, axis=-1, keepdims=True) + RMS_EPS) * g


def _mla_proj_kernel(x_ref, win_ref, qn_ref, kvn_ref, wuq_ref, wuk_ref, wuv_ref, cos_ref, sin_ref,
                     q_ref, k_ref, v_ref):
    xb = x_ref[...].astype(BF16)
    cos = cos_ref[...]
    sin = sin_ref[...]
    lat = jnp.dot(xb, win_ref[...], preferred_element_type=F32)
    cq = _rms_rows(lat[:, :MLA_Q_LORA], qn_ref[...]).astype(BF16)
    ckv = _rms_rows(lat[:, MLA_Q_LORA:MLA_Q_LORA + MLA_KV_LORA], kvn_ref[...]).astype(BF16)
    kr = _rope_slab(lat[:, MLA_Q_LORA + MLA_KV_LORA:], cos, sin, MLA_ROPE // 2)
    q = jnp.dot(cq, wuq_ref[...], preferred_element_type=F32)
    k = jnp.dot(ckv, wuk_ref[...], preferred_element_type=F32)
    v = jnp.dot(ckv, wuv_ref[...], preferred_element_type=F32)
    ones_half = _ones_half((x_ref.shape[0], LANES))
    q_scale = (MLA_NOPE + MLA_ROPE) ** -0.5 * LOG2_E
    for h in range(MLA_HEADS):
        sl = slice(h * LANES, (h + 1) * LANES)
        q_ref[:, sl] = (_rope_slab(q[:, sl], cos, sin, MLA_ROPE // 2) * q_scale).astype(BF16)
        k_ref[:, sl] = (k[:, sl] + kr).astype(BF16)
        v_ref[:, sl] = (v[:, sl] + ones_half).astype(BF16)


def _mla_proj(x2, w_in, qn, kvn, wuq, wuk, wuv, cos, sin, seq, tm):
    n, d = x2.shape
    sblk = seq // tm
    hw = MLA_HEADS * LANES
    vw = hw
    return pl.pallas_call(
        _mla_proj_kernel,
        grid=(n // tm,),
        in_specs=[pl.BlockSpec((tm, d), lambda i: (i, 0)),
                  _full(w_in.shape), _full(qn.shape), _full(kvn.shape),
                  _full(wuq.shape), _full(wuk.shape), _full(wuv.shape),
                  pl.BlockSpec((tm, LANES), lambda i: (i % sblk, 0)),
                  pl.BlockSpec((tm, LANES), lambda i: (i % sblk, 0))],
        out_specs=[pl.BlockSpec((tm, hw), lambda i: (i, 0)),
                   pl.BlockSpec((tm, hw), lambda i: (i, 0)),
                   pl.BlockSpec((tm, vw), lambda i: (i, 0))],
        out_shape=[jax.ShapeDtypeStruct((n, hw), BF16), jax.ShapeDtypeStruct((n, hw), BF16),
                   jax.ShapeDtypeStruct((n, vw), BF16)],
        compiler_params=_params("parallel"),
        name="mla_proj",
    )(x2, w_in, qn, kvn, wuq, wuk, wuv, cos, sin)


FLASH_ROWS = 32
LOG2_E = 1.4426950408889634


def _flash_init(m_scr, acc_scr):
    m_scr[...] = jnp.full(m_scr.shape, -jnp.inf, F32)
    acc_scr[...] = jnp.zeros(acc_scr.shape, F32)


def _flash_scores(q, k, s_scr):
    s_scr[...] = lax.dot_general(q, k, NT_DIMS, preferred_element_type=F32)


def _flash_max(bias_fn, s_scr, m_scr, a_scr):
    rows, keys = s_scr.shape
    n_chunks = rows // FLASH_ROWS
    slabs = keys // LANES
    for c in range(n_chunks):
        r = slice(c * FLASH_ROWS, (c + 1) * FLASH_ROWS)
        s = s_scr[r, :]
        if bias_fn is not None:
            s = s + bias_fn(c)
            s_scr[r, :] = s
        mx = s[:, 0:LANES]
        for j in range(1, slabs):
            mx = jnp.maximum(mx, s[:, j * LANES:(j + 1) * LANES])
        a_scr[r, :] = mx
    m_old = m_scr[...]
    m_new = jnp.maximum(m_old, jnp.broadcast_to(jnp.max(a_scr[...], axis=-1, keepdims=True), m_old.shape))
    m_scr[...] = m_new
    a_scr[...] = jnp.exp2(m_old - m_new)


def _flash_weights(v, s_scr, p_scr, m_scr, a_scr, acc_scr):
    rows, keys = s_scr.shape
    n_chunks = rows // FLASH_ROWS
    slabs = keys // LANES
    for c in range(n_chunks):
        r = slice(c * FLASH_ROWS, (c + 1) * FLASH_ROWS)
        m_rows = m_scr[r, :]
        for j in range(slabs):
            cols = slice(j * LANES, (j + 1) * LANES)
            p_scr[r, cols] = jnp.exp2(s_scr[r, cols] - m_rows).astype(p_scr.dtype)
    acc_scr[...] = a_scr[...] * acc_scr[...] + jnp.dot(p_scr[...], v, preferred_element_type=F32)


def _flash_finish(acc):
    return acc / acc[:, HEAD_DIM:HEAD_DIM + 1]


def _pair_slab(even, odd):
    lane = lax.broadcasted_iota(I32, even.shape, 1)
    return jnp.where(lane < HEAD_DIM, even, pltpu.roll(odd, HEAD_DIM, 1))


def _mla_attn_kernel(q_ref, k_ref, v_ref, o_ref, s_scr, p_scr, m_scr, a_scr, acc_scr, *, tq, tk):
    i = pl.program_id(2)
    for e in range(2):
        _flash_init(m_scr.at[e], acc_scr.at[e])

    def tile(k0, bias_fn):
        for e in range(2):
            _flash_scores(q_ref[:, e * LANES:(e + 1) * LANES], k_ref[pl.ds(k0, tk), e * LANES:(e + 1) * LANES],
                          s_scr.at[e])
        for e in range(2):
            _flash_max(bias_fn, s_scr.at[e], m_scr.at[e], a_scr.at[e])
        for e in range(2):
            _flash_weights(v_ref[pl.ds(k0, tk), e * LANES:(e + 1) * LANES], s_scr.at[e], p_scr.at[e],
                           m_scr.at[e], a_scr.at[e], acc_scr.at[e])

    def body(j, carry):
        tile(pl.multiple_of(j * tk, tk), None)
        return carry

    lax.fori_loop(0, i * (tq // tk), body, 0)
    for jd in range(tq // tk):

        def diag_bias(c, first_key=jd * tk):
            row = c * FLASH_ROWS + lax.broadcasted_iota(I32, (FLASH_ROWS, tk), 0)
            col = first_key + lax.broadcasted_iota(I32, (FLASH_ROWS, tk), 1)
            return jnp.where(col <= row, 0.0, -jnp.inf).astype(F32)

        tile(pl.multiple_of(i * tq + jd * tk, tk), diag_bias)
    o_ref[...] = _pair_slab(_flash_finish(acc_scr[0]), _flash_finish(acc_scr[1])).astype(o_ref.dtype)


def _mla_attention(q, k, v, batch, seq, tq, tk):
    n = q.shape[0]
    nq = seq // tq
    kern = functools.partial(_mla_attn_kernel, tq=tq, tk=tk)
    return pl.pallas_call(
        kern,
        grid=(batch, MLA_HEADS // 2, nq),
        in_specs=[pl.BlockSpec((tq, 2 * LANES), lambda b, p, i: (b * nq + i, p)),
                  pl.BlockSpec((seq, 2 * LANES), lambda b, p, i: (b, p)),
                  pl.BlockSpec((seq, 2 * LANES), lambda b, p, i: (b, p))],
        out_specs=pl.BlockSpec((tq, LANES), lambda b, p, i: (b * nq + i, p)),
        out_shape=jax.ShapeDtypeStruct((n, MLA_HEADS * MLA_V), BF16),
        scratch_shapes=[pltpu.VMEM((2, tq, tk), F32), pltpu.VMEM((2, tq, tk), BF16)]
                       + [pltpu.VMEM((2, tq, LANES), F32)] * 3,
        compiler_params=_params("parallel", "parallel", "arbitrary"),
        name="mla_attn",
    )(q, k, v)


def _router_kernel(x_ref, rw_ref, rb_ref, eidx_ref, gate_ref, pos_ref, cnt_ref, carry_ref, *, steps_per_chunk):
    i = pl.program_id(0)
    tm = x_ref.shape[0]
    per_group = N_EXPERTS // MOE_GROUPS

    @pl.when(i % steps_per_chunk == 0)
    def _():
        carry_ref[...] = jnp.zeros_like(carry_ref)

    x = x_ref[...]
    x_top = pltpu.bitcast(pltpu.bitcast(x, U32) & jnp.uint32(0xFFFF0000), F32)
    x_hi = x_top.astype(BF16)
    x_lo = (x - x_top).astype(BF16)
    logits = (jnp.dot(x_hi, rw_ref[0], preferred_element_type=F32)
              + jnp.dot(x_hi, rw_ref[1], preferred_element_type=F32)
              + jnp.dot(x_lo, rw_ref[0], preferred_element_type=F32))
    lt = jnp.transpose(logits)[:N_EXPERTS]
    s = _sigmoid(lt)
    sb = s + rb_ref[...]

    g3 = sb.reshape(MOE_GROUPS, per_group, tm)
    idx3 = lax.broadcasted_iota(I32, g3.shape, 1).astype(F32)
    m1 = jnp.max(g3, axis=1, keepdims=True)
    first = jnp.min(jnp.where(g3 == m1, idx3, float(per_group)), axis=1, keepdims=True)
    m2 = jnp.max(jnp.where(idx3 == first, -jnp.inf, g3), axis=1, keepdims=True)
    gscore = (m1 + m2).reshape(MOE_GROUPS, tm)
    gsel = _rank_desc(gscore, MOE_GROUPS, 0) < MOE_TOPK_GROUPS
    gsel3 = jnp.where(gsel, 1.0, 0.0).astype(F32).reshape(MOE_GROUPS, 1, tm)
    masked = jnp.where(gsel3 > 0.5, g3, -jnp.inf).reshape(N_EXPERTS, tm)
    e_iota = lax.broadcasted_iota(I32, (N_EXPERTS, tm), 0).astype(F32)
    work = masked
    picks, rows_g = [], []
    sel_f = jnp.zeros((N_EXPERTS, tm), F32)
    for r in range(MOE_TOP_K):
        top = jnp.max(work, axis=0, keepdims=True)
        first = jnp.min(jnp.where(work == top, e_iota, float(N_EXPERTS)), axis=0, keepdims=True)
        hit = e_iota == first
        picks.append(first)
        rows_g.append(jnp.sum(jnp.where(hit, s, 0.0), axis=0, keepdims=True))
        work = jnp.where(hit, -jnp.inf, work)
        sel_f = jnp.where(hit, 1.0, sel_f)
    gate_sum = rows_g[0]
    for g_row in rows_g[1:]:
        gate_sum = gate_sum + g_row
    rows_g = [g_row / gate_sum * MOE_ROUTED_SCALE for g_row in rows_g]

    sel_b = sel_f.astype(BF16)
    r_i = lax.broadcasted_iota(I32, (tm, tm), 0)
    c_i = lax.broadcasted_iota(I32, (tm, tm), 1)
    tri = jnp.where(r_i < c_i, 1.0, 0.0).astype(BF16)
    carry = carry_ref[:, 0:1]
    before = jnp.dot(sel_b, tri, preferred_element_type=F32) + carry
    carry_new = carry + jnp.sum(sel_b.astype(F32), axis=1, keepdims=True)
    carry_ref[...] = jnp.broadcast_to(carry_new, carry_ref.shape)
    cnt_ref[0] = jnp.broadcast_to(carry_new, carry_ref.shape).astype(I32)

    rows_p = [jnp.sum(jnp.where(e_iota == first, before, 0.0), axis=0, keepdims=True) for first in picks]
    pad = 8 - MOE_TOP_K
    eidx_ref[...] = jnp.concatenate(picks + [jnp.zeros((pad, tm), F32)], axis=0).astype(I32)
    gate_ref[...] = jnp.concatenate(rows_g + [jnp.zeros((pad, tm), F32)], axis=0)
    pos_ref[...] = jnp.concatenate(rows_p + [jnp.zeros((pad, tm), F32)], axis=0).astype(I32)


def _router(x2, rw, rb, n_chunk, tm):
    n, d = x2.shape
    steps = n // tm
    spc = steps // n_chunk
    kern = functools.partial(_router_kernel, steps_per_chunk=spc)
    row8 = pl.BlockSpec((8, tm), lambda i: (0, i))
    return pl.pallas_call(
        kern,
        grid=(steps,),
        in_specs=[pl.BlockSpec((tm, d), lambda i: (i, 0)), _full(rw.shape), _full(rb.shape)],
        out_specs=[row8, row8, row8, pl.BlockSpec((1, N_EXPERTS, LANES), lambda i: (i // spc, 0, 0))],
        out_shape=[jax.ShapeDtypeStruct((8, n), I32), jax.ShapeDtypeStruct((8, n), F32),
                   jax.ShapeDtypeStruct((8, n), I32),
                   jax.ShapeDtypeStruct((n_chunk, N_EXPERTS, LANES), I32)],
        scratch_shapes=[pltpu.VMEM((N_EXPERTS, LANES), F32)],
        compiler_params=_params("arbitrary"),
        name="moe_router",
    )(x2, rw, rb)


def _dest_kernel(tab_ref, eidx_ref, pos_ref, dest_ref, *, steps_per_chunk):
    chunk = pl.program_id(0) // steps_per_chunk
    eidx = eidx_ref[...]
    dest = pos_ref[...]
    for e in range(N_EXPERTS):
        dest = dest + jnp.where(eidx == e, tab_ref[chunk * N_EXPERTS + e], 0)
    dest_ref[...] = dest


def _dest_rows(pad_starts, eidx, pos, n_chunk, tm):
    n = eidx.shape[1]
    steps = n // tm
    row8 = lambda i, tab: (0, i)
    return pl.pallas_call(
        functools.partial(_dest_kernel, steps_per_chunk=steps // n_chunk),
        grid_spec=pltpu.PrefetchScalarGridSpec(
            num_scalar_prefetch=1, grid=(steps,),
            in_specs=[pl.BlockSpec((8, tm), row8), pl.BlockSpec((8, tm), row8)],
            out_specs=pl.BlockSpec((8, tm), row8)),
        out_shape=jax.ShapeDtypeStruct((8, n), I32),
        compiler_params=_params("parallel"),
        name="moe_dest",
    )(pad_starts, eidx, pos)


SC_ROWS = 64
SC_WORKERS = 32


def _sc_mesh():
    return plsc.VectorSubcoreMesh(core_axis_name="c", subcore_axis_name="s")


def _sc_worker_base(per_worker):
    return (lax.axis_index("s") * 2 + lax.axis_index("c")) * per_worker


def _sc_scatter_rows(x, dest_chunks, n_out):
    n, d = x.shape
    slots = dest_chunks.shape[0] * SC_ROWS // n
    per_w = n // SC_WORKERS
    chunks = per_w // SC_ROWS
    assert chunks * SC_ROWS * SC_WORKERS == n and chunks % 2 == 0, (n, chunks)

    def body(x_hbm, i_hbm, o_hbm, idx_v, rows0, rows1, sem_in0, sem_in1, sem_out):
        worker = lax.axis_index("s") * 2 + lax.axis_index("c")
        base = worker * per_w
        bufs = ((rows0, sem_in0), (rows1, sem_in1))
        pltpu.sync_copy(i_hbm.at[pl.ds(pl.multiple_of(worker * (chunks * slots), 8), chunks * slots)], idx_v)

        def load(c, b):
            off = pl.multiple_of(base + c * SC_ROWS, 8)
            return pltpu.make_async_copy(x_hbm.at[pl.ds(off, SC_ROWS)], bufs[b][0], bufs[b][1])

        load(0, 0).start()

        @pl.loop(0, chunks, step=2)
        def _(c):
            for b in range(2):
                cc = c + b
                load(cc, b).wait()

                @pl.when(cc + 1 < chunks)
                def _():
                    load(cc + 1, 1 - b).start()

                copies = [pltpu.make_async_copy(bufs[b][0], o_hbm.at[idx_v.at[cc * slots + k]], sem_out)
                          for k in range(slots)]
                for cp in copies:
                    cp.start()
                for cp in copies:
                    cp.wait()

    return pl.kernel(
        body, out_type=jax.ShapeDtypeStruct((n_out, d), x.dtype), mesh=_sc_mesh(),
        scratch_types=[pltpu.VMEM((chunks * slots, SC_ROWS), I32),
                       pltpu.VMEM((SC_ROWS, d), x.dtype), pltpu.VMEM((SC_ROWS, d), x.dtype),
                       pltpu.SemaphoreType.DMA, pltpu.SemaphoreType.DMA, pltpu.SemaphoreType.DMA],
        name="moe_dispatch_sc")(x, dest_chunks)


def _sc_gather_rows(table, idx):
    d = table.shape[1]
    r = idx.shape[0]
    per_w = r // SC_WORKERS
    chunks = per_w // SC_ROWS
    assert chunks * SC_ROWS * SC_WORKERS == r and chunks % 2 == 0, (r, chunks)

    def body(t_hbm, i_hbm, o_hbm, idx_v, rows0, rows1, sem_g0, sem_g1, sem_w0, sem_w1):
        base = _sc_worker_base(per_w)
        bufs = ((rows0, sem_g0, sem_w0), (rows1, sem_g1, sem_w1))
        pltpu.sync_copy(i_hbm.at[pl.ds(pl.multiple_of(base, 8), per_w)], idx_v)

        def gather(c, b):
            ids = idx_v.at[pl.ds(pl.multiple_of(c * SC_ROWS, 8), SC_ROWS)]
            return pltpu.make_async_copy(t_hbm.at[ids], bufs[b][0], bufs[b][1])

        def write(c, b):
            off = pl.multiple_of(base + c * SC_ROWS, 8)
            return pltpu.make_async_copy(bufs[b][0], o_hbm.at[pl.ds(off, SC_ROWS)], bufs[b][2])

        gather(0, 0).start()

        @pl.loop(0, chunks, step=2)
        def _(c):
            for b in range(2):
                cc = c + b

                @pl.when(cc >= 1)
                def _():
                    write(cc - 1, 1 - b).wait()

                @pl.when(cc + 1 < chunks)
                def _():
                    gather(cc + 1, 1 - b).start()

                gather(cc, b).wait()
                write(cc, b).start()

        write(chunks - 1, (chunks - 1) % 2).wait()

    return pl.kernel(
        body, out_type=jax.ShapeDtypeStruct((r, d), table.dtype), mesh=_sc_mesh(),
        scratch_types=[pltpu.VMEM((per_w,), I32),
                       pltpu.VMEM((SC_ROWS, d), table.dtype), pltpu.VMEM((SC_ROWS, d), table.dtype),
                       pltpu.SemaphoreType.DMA, pltpu.SemaphoreType.DMA,
                       pltpu.SemaphoreType.DMA, pltpu.SemaphoreType.DMA],
        name="moe_combine_sc")(table, idx)


def _experts_kernel(be_ref, nv_ref, xs_ref, wg_ref, wu_ref, wd_ref, y_ref, wg_b, wu_b, wd_b):
    i = pl.program_id(0)
    prev = be_ref[jnp.maximum(i - 1, 0)]

    @pl.when(jnp.logical_or(i == 0, be_ref[i] != prev))
    def _():
        wg_b[...] = wg_ref[0, 0].astype(BF16)
        wu_b[...] = wu_ref[0, 0].astype(BF16)
        wd_b[...] = wd_ref[0, 0].astype(BF16)

    @pl.when(i < nv_ref[0])
    def _():
        x_lo, x_hi = _unpack_halves(xs_ref[...])
        x_lo, x_hi = x_lo.astype(BF16), x_hi.astype(BF16)
        half = x_lo.shape[1]

        def proj(w):
            return (jnp.dot(x_lo, w[:half, :], preferred_element_type=F32)
                    + jnp.dot(x_hi, w[half:, :], preferred_element_type=F32))

        h = _silu(proj(wg_b)) * proj(wu_b)
        y_ref[...] = _pack_halves(jnp.dot(h.astype(BF16), wd_b[...], preferred_element_type=F32))

    @pl.when(i >= nv_ref[0])
    def _():
        y_ref[...] = jnp.zeros_like(y_ref)


def _experts(blk_exp, n_valid, xs, wg, wu, wd, layer, tb):
    rows, dp = xs.shape
    d = 2 * dp
    grid_spec = pltpu.PrefetchScalarGridSpec(
        num_scalar_prefetch=2,
        grid=(rows // tb,),
        in_specs=[pl.BlockSpec((tb, dp), lambda i, be, nv: (i, 0)),
                  pl.BlockSpec((1, 1, d, D_EXPERT), lambda i, be, nv: (layer, be[i], 0, 0)),
                  pl.BlockSpec((1, 1, d, D_EXPERT), lambda i, be, nv: (layer, be[i], 0, 0)),
                  pl.BlockSpec((1, 1, D_EXPERT, d), lambda i, be, nv: (layer, be[i], 0, 0))],
        out_specs=pl.BlockSpec((tb, dp), lambda i, be, nv: (i, 0)),
        scratch_shapes=[pltpu.VMEM((d, D_EXPERT), BF16), pltpu.VMEM((d, D_EXPERT), BF16),
                        pltpu.VMEM((D_EXPERT, d), BF16)],
    )
    return pl.pallas_call(
        _experts_kernel,
        grid_spec=grid_spec,
        out_shape=jax.ShapeDtypeStruct((rows, dp), U32),
        compiler_params=_params("arbitrary"),
        name="moe_experts",
    )(blk_exp, n_valid, xs, wg, wu, wd)


def _combine_ln_kernel(x_ref, yg_ref, gt_ref, wgu_ref, wd_ref, g_ref, b_ref, o_ref, *, alpha):
    x = x_ref[...]
    gu = jnp.dot(x.astype(BF16), wgu_ref[...], preferred_element_type=F32)
    dsh = gu.shape[1] // 2
    h = _silu(gu[:, :dsh]) * gu[:, dsh:]
    f = jnp.dot(h.astype(BF16), wd_ref[...], preferred_element_type=F32)
    r_lo = r_hi = None
    for k in range(MOE_TOP_K):
        y_lo, y_hi = _unpack_halves(yg_ref[k])
        gate = gt_ref[:, k:k + 1]
        r_lo = gate * y_lo if r_lo is None else r_lo + gate * y_lo
        r_hi = gate * y_hi if r_hi is None else r_hi + gate * y_hi
    f = f + jnp.concatenate([r_lo, r_hi], axis=1)
    o_ref[...] = _layer_norm_rows(alpha * x + f, g_ref[...], b_ref[...])


def _combine_ln(x2, yg, gates_t, wgu, wd, g, b, alpha, tm):
    n, d = x2.shape
    kern = functools.partial(_combine_ln_kernel, alpha=alpha)
    return pl.pallas_call(
        kern,
        grid=(n // tm,),
        in_specs=[pl.BlockSpec((tm, d), lambda i: (i, 0)),
                  pl.BlockSpec((MOE_TOP_K, tm, d // 2), lambda i: (0, i, 0)),
                  pl.BlockSpec((tm, 8), lambda i: (i, 0)),
                  _full(wgu.shape), _full(wd.shape), _full((1, d)), _full((1, d))],
        out_specs=pl.BlockSpec((tm, d), lambda i: (i, 0)),
        out_shape=jax.ShapeDtypeStruct((n, d), F32),
        compiler_params=_params("parallel"),
        name="moe_combine_ln",
    )(x2, yg, gates_t, wgu, wd, g, b)


def _rope_tables(positions, dim, lane_offset, period):
    half = dim // 2
    inv_freq = ROPE_THETA ** (-2.0 * jnp.arange(half, dtype=jnp.float32) / dim)
    ang = positions.astype(jnp.float32)[:, None] * inv_freq[None, :]
    cos_h, sin_h = jnp.cos(ang), jnp.sin(ang)
    lanes = np.arange(LANES)
    rel = (lanes - lane_offset) % period
    active = (lanes >= lane_offset) & (rel < dim)
    fidx = np.where(active, rel % half, 0)
    sign = np.where(rel < half, -1.0, 1.0)
    cos = jnp.where(active[None, :], cos_h[:, fidx], 1.0)
    sin = jnp.where(active[None, :], sin_h[:, fidx] * sign[None, :], 0.0)
    return cos.astype(F32), sin.astype(F32)


def _gate_slabs(w):
    k = w.shape[0]
    g = w.reshape(k, 3, NSA_KV_HEADS, GQA_REP).transpose(0, 2, 1, 3).reshape(k, NSA_KV_HEADS, 3 * GQA_REP)
    return jnp.pad(g, ((0, 0), (0, 0), (0, LANES - 3 * GQA_REP))).reshape(k, NSA_KV_HEADS * LANES)


def _dup_heads(w):
    a, b = w[:, :HEAD_DIM], w[:, HEAD_DIM:]
    return jnp.concatenate([a, a, b, b], axis=1)


def _pad_heads(w):
    a, b = w[:, :HEAD_DIM], w[:, HEAD_DIM:]
    z = jnp.zeros_like(a)
    return jnp.concatenate([a, z, b, z], axis=1)


def _moe_layer(x1_and_packed, rw, rb, wg, wu, wd, layer, shg, shu, shd, ln_g, ln_b, alpha, tm):
    x1, x1_pk = x1_and_packed
    n, d = x1.shape
    n_chunk = 1
    tb = MOE_ROW_BLOCK
    rw_p = jnp.pad(rw, ((0, 0), (0, LANES - N_EXPERTS)))
    rw_top = lax.bitcast_convert_type(lax.bitcast_convert_type(rw_p, U32) & jnp.uint32(0xFFFF0000), F32)
    rw_split = jnp.stack([rw_top.astype(BF16), (rw_p - rw_top).astype(BF16)])
    eidx, gate, pos, cnt = _router(x1, rw_split, rb.reshape(N_EXPERTS, 1), n_chunk, tm)
    counts = cnt[:, :, 0].reshape(-1)
    padded = (counts + tb - 1) // tb * tb
    pad_ends = jnp.cumsum(padded)
    pad_starts = (pad_ends - padded).astype(I32)
    dest2 = _dest_rows(pad_starts, eidx, pos, n_chunk, tm)[:MOE_TOP_K]
    dest = dest2.reshape(-1)
    dest_chunks = dest2.reshape(MOE_TOP_K, n // SC_ROWS, SC_ROWS).transpose(1, 0, 2).reshape(-1, SC_ROWS)
    n_rows = n * MOE_TOP_K
    n_blk = -(-(n_rows + n_chunk * N_EXPERTS * (tb - 1)) // tb)
    blk_first_row = jnp.arange(n_blk, dtype=I32) * tb
    owner = jnp.sum((pad_ends[None, :] <= blk_first_row[:, None]).astype(I32), axis=1)
    blk_exp = jnp.minimum(owner, n_chunk * N_EXPERTS - 1).astype(I32) % N_EXPERTS
    n_valid = (pad_ends[-1] // tb).astype(I32).reshape(1)

    xs = _sc_scatter_rows(x1_pk, dest_chunks, n_blk * tb)
    y = _experts(blk_exp, n_valid, xs, wg, wu, wd, layer, tb)
    yg = _sc_gather_rows(y, dest).reshape(MOE_TOP_K, n, d // 2)
    gates_t = gate.T
    sh_gu = jnp.concatenate([shg, shu], axis=-1).astype(BF16)
    return _combine_ln(x1, yg, gates_t, sh_gu, shd.astype(BF16), ln_g.reshape(1, d), ln_b.reshape(1, d), alpha, tm)


def _swa_nsa_layer(x2, batch, seq, w_in, sinks, pe_k, w_ck1, w_ck2, pe_v, w_cv1, w_cv2, w_out,
                   ln_g, ln_b, alpha, tm):
    n, d = x2.shape
    splits = np.cumsum([512, 128, 128, 512, 128, 128, 128, 128, 128, 128, 24])[:-1]
    qa, ka, va, qb, kc, vc, ksl, vsl, kw, vw, gts = jnp.split(w_in, [int(c) for c in splits], axis=1)
    w_all = jnp.concatenate(
        [qa, _dup_heads(ka), _pad_heads(va), qb, kc, vc, _dup_heads(ksl), _pad_heads(vsl),
         _dup_heads(kw), _pad_heads(vw), _gate_slabs(gts)], axis=1).astype(BF16)
    cos, sin = _rope_tables(jnp.arange(seq), HEAD_DIM, 0, HEAD_DIM)
    (q_a, k_a, v_a, q_b, k_c, v_c, k_s, v_s, k_w, v_w, gates) = _mix0_proj(x2, w_all, cos, sin, seq, tm)

    o_swa = _banded_attention(q_a, k_a, v_a, batch, seq, SWA_WINDOW, 128, sinks=sinks, name="swa")
    o_win = _banded_attention(q_b, k_w, v_w, batch, seq, NSA_WINDOW, 128, gates=gates,
                              gate_col=2 * GQA_REP, out_dtype=F32, name="nsa_win")

    n_chunk = seq // NSA_CMP_STRIDE
    cw = NSA_CMP_STRIDE * NSA_KV_HEADS * HEAD_DIM
    hid = NSA_CMP_HIDDEN

    def expand_w1(w1):
        halves = w1.reshape(2, NSA_CMP_STRIDE, HEAD_DIM, hid)
        out = jnp.zeros((NSA_KV_HEADS, NSA_CMP_STRIDE, NSA_KV_HEADS, HEAD_DIM, 2 * hid), w1.dtype)
        for g in range(NSA_KV_HEADS):
            out = out.at[g, :, g, :, :hid].set(halves[0]).at[g, :, g, :, hid:].set(halves[1])
        return out.reshape(NSA_KV_HEADS, cw, 2 * hid).astype(BF16)

    def pe_rows(pe):
        return jnp.pad(pe.reshape(1, NSA_CMP_LEN * HEAD_DIM), ((0, 15), (0, 0))).astype(BF16)

    cos_c, sin_c = _rope_tables(jnp.arange(n_chunk) * NSA_CMP_STRIDE + NSA_CMP_LEN - 1, HEAD_DIM, 0, HEAD_DIM)
    k_cmp, v_cmp = _compress(
        k_c.reshape(batch * n_chunk, cw), v_c.reshape(batch * n_chunk, cw),
        expand_w1(w_ck1), expand_w1(w_cv1), pe_rows(pe_k), pe_rows(pe_v),
        w_ck1.astype(BF16), w_cv1.astype(BF16),
        jnp.concatenate([w_ck2, w_ck2], axis=1).astype(BF16),
        jnp.concatenate([w_cv2, w_cv2], axis=1).astype(BF16),
        cos_c, sin_c, batch, n_chunk)

    n_sb = seq // NSA_SEL_LEN
    n_sel = min(NSA_N_SEL, n_sb)
    cs = np.arange(n_chunk) * NSA_CMP_STRIDE
    bs = np.arange(n_sb) * NSA_SEL_LEN
    ov = np.clip(np.minimum(cs[:, None] + NSA_CMP_LEN, bs[None, :] + NSA_SEL_LEN)
                 - np.maximum(cs[:, None], bs[None, :]), 0, None) / NSA_CMP_LEN
    overlap = jnp.asarray(np.pad(ov, ((0, 0), (0, LANES - n_sb))).T, BF16)
    expand = jnp.asarray(np.arange(LANES)[:, None] == (np.arange(seq)[None, :] // NSA_SEL_LEN), BF16)
    tk_sel = min(512, seq)
    o_nsa = _nsa_cmp_sel(q_b, k_cmp, v_cmp, k_s, v_s, gates, o_win, overlap, expand,
                         batch, seq, 512, n_sb, n_sel, tk_sel)
    w_o = w_out.astype(BF16)
    half = SWA_Q_HEADS * HEAD_DIM
    return _oproj_ln([o_swa, o_nsa], [w_o[:half], w_o[half:]], x2, ln_g.reshape(1, d), ln_b.reshape(1, d),
                     alpha, tm)


def _mla_layer(x2, batch, seq, w_in, q_norm, kv_norm, w_uq, w_ukv, w_out, ln_g, ln_b, alpha, tm):
    n, d = x2.shape
    dq = MLA_NOPE + MLA_ROPE
    w_kr = jnp.zeros((d, LANES), w_in.dtype).at[:, MLA_NOPE:dq].set(w_in[:, MLA_Q_LORA + MLA_KV_LORA:])
    w_in_p = jnp.concatenate([w_in[:, :MLA_Q_LORA + MLA_KV_LORA], w_kr], axis=1).astype(BF16)
    wuq = jnp.pad(w_uq.reshape(MLA_Q_LORA, MLA_HEADS, dq), ((0, 0), (0, 0), (0, LANES - dq)))
    wuq = wuq.reshape(MLA_Q_LORA, MLA_HEADS * LANES).astype(BF16)
    wukv = w_ukv.reshape(MLA_KV_LORA, MLA_HEADS, MLA_NOPE + MLA_V)
    wuk = jnp.pad(wukv[:, :, :MLA_NOPE], ((0, 0), (0, 0), (0, LANES - MLA_NOPE)))
    wuk = wuk.reshape(MLA_KV_LORA, MLA_HEADS * LANES).astype(BF16)
    wuv = jnp.pad(wukv[:, :, MLA_NOPE:], ((0, 0), (0, 0), (0, LANES - MLA_V)))
    wuv = wuv.reshape(MLA_KV_LORA, MLA_HEADS * LANES).astype(BF16)
    cos, sin = _rope_tables(jnp.arange(seq), MLA_ROPE, MLA_NOPE, LANES)
    q, k, v = _mla_proj(x2, w_in_p, q_norm.reshape(1, -1), kv_norm.reshape(1, -1), wuq, wuk, wuv, cos, sin, seq, tm)
    o = _mla_attention(q, k, v, batch, seq, min(512, seq), min(512, seq))
    return _oproj_ln([o], [w_out.astype(BF16)], x2, ln_g.reshape(1, d), ln_b.reshape(1, d), alpha, tm)


def kernel(x, swa_nsa_w_in, swa_sinks, nsa_cmp_pe_k, nsa_cmp_k_w1, nsa_cmp_k_w2, nsa_cmp_pe_v, nsa_cmp_v_w1, nsa_cmp_v_w2, swa_nsa_w_out, mla_w_in, mla_q_norm, mla_kv_norm, mla_w_uq, mla_w_ukv, mla_w_out, ln_mix_g, ln_mix_b, ln_ffn_g, ln_ffn_b, router_w, router_bias, expert_w_gate, expert_w_up, expert_w_down, shared_w_gate, shared_w_up, shared_w_down):
    batch, seq, d = x.shape
    depth = ln_mix_g.shape[0]
    alpha = (2 * depth) ** 0.25
    tm = min(512, seq)
    x2 = x.reshape(batch * seq, d)
    for layer in range(depth):
        j = layer // 2
        if layer % 2 == 0:
            x2 = _swa_nsa_layer(x2, batch, seq, swa_nsa_w_in[j], swa_sinks[j], nsa_cmp_pe_k[j],
                                nsa_cmp_k_w1[j], nsa_cmp_k_w2[j], nsa_cmp_pe_v[j], nsa_cmp_v_w1[j],
                                nsa_cmp_v_w2[j], swa_nsa_w_out[j], ln_mix_g[layer], ln_mix_b[layer], alpha, tm)
        else:
            x2 = _mla_layer(x2, batch, seq, mla_w_in[j], mla_q_norm[j], mla_kv_norm[j], mla_w_uq[j],
                            mla_w_ukv[j], mla_w_out[j], ln_mix_g[layer], ln_mix_b[layer], alpha, tm)
        x2 = _moe_layer(x2, router_w[layer], router_bias[layer], expert_w_gate, expert_w_up, expert_w_down, layer,
                        shared_w_gate[layer], shared_w_up[layer], shared_w_down[layer],
                        ln_ffn_g[layer], ln_ffn_b[layer], alpha, tm)
    return x2.reshape(batch, seq, d)
```
